```python
import math
import jax
import jax.numpy as jnp
from jax import lax
import numpy as np

D_MODEL = 1024
BATCH = 16
SEQ = 256
DEPTH = 2
DEC_BATCH = 2
DEC_SEQ = 2048
PAST_LEN = 256

GRID_W = 64
GROUP_W = D_MODEL // 4
HEAD_DIM = 64
DA_HEADS = GROUP_W // HEAD_DIM
DA_QK = HEAD_DIM // 2
DA_V = HEAD_DIM
MLA_HEADS = GROUP_W // HEAD_DIM
MLA_NOPE = HEAD_DIM
MLA_ROPE = HEAD_DIM // 2
MLA_V = HEAD_DIM
MLA_Q_LORA = GROUP_W
MLA_KV_LORA = GROUP_W // 2
NA_HEADS = GROUP_W // HEAD_DIM
NA_DIM = HEAD_DIM
NA_KR = 8
NA_KC = 16
CONV_CH = GROUP_W
CONV_W = 31
D_FF = 4 * D_MODEL
ROPE_BASE = 10000.0
QBLK = 128
EPS = 1e-6
N_MOD = 6

IN_SIZES = (2 * DA_HEADS * DA_QK, 2 * DA_HEADS * DA_QK, DA_HEADS * DA_V,
            MLA_Q_LORA, MLA_KV_LORA, MLA_ROPE,
            NA_HEADS * NA_DIM, NA_HEADS * NA_DIM, NA_HEADS * NA_DIM,
            2 * CONV_CH)
IN_COLS = sum(IN_SIZES)
IN_OFFSETS = tuple(int(o) for o in np.cumsum(IN_SIZES)[:-1])
MIX_OUT = DA_HEADS * DA_V + MLA_HEADS * MLA_V + NA_HEADS * NA_DIM + CONV_CH

kernel_name = 'hybrid_dit_prefix_ctx_step'


def rmsnorm(x, g):
    x32 = x.astype(jnp.float32)
    y = x32 * lax.rsqrt(jnp.mean(x32 * x32, axis=-1, keepdims=True) + EPS)
    return y.astype(x.dtype) * g


def layernorm(x, g, b):
    x32 = x.astype(jnp.float32)
    mu = jnp.mean(x32, axis=-1, keepdims=True)
    var = jnp.mean(jnp.square(x32 - mu), axis=-1, keepdims=True)
    return ((x32 - mu) * lax.rsqrt(var + EPS)).astype(x.dtype) * g + b


def heads(t, n):
    B, S, _ = t.shape
    return t.reshape(B, S, n, -1).transpose(0, 2, 1, 3)


def merge_heads(t):
    B, n, S, d = t.shape
    return t.transpose(0, 2, 1, 3).reshape(B, S, n * d)


def rope1d(x, pos):
    d = x.shape[-1]
    half = d // 2
    freqs = ROPE_BASE ** (-jnp.arange(half, dtype=jnp.float32) * 2.0 / d)
    ang = pos[:, None] * freqs[None, :]
    cos = jnp.cos(ang).astype(x.dtype)
    sin = jnp.sin(ang).astype(x.dtype)
    x1, x2 = x[..., :half], x[..., half:]
    return jnp.concatenate([x1 * cos - x2 * sin, x1 * sin + x2 * cos], axis=-1)


def rope2d(x, rows, cols):
    h = x.shape[-1] // 2
    return jnp.concatenate([rope1d(x[..., :h], rows), rope1d(x[..., h:], cols)], axis=-1)


def sweep_query_blocks(fn, qs):
    B, H, S = qs[0].shape[:3]
    blk = math.gcd(S, QBLK)
    nb = S // blk
    blocks = tuple(jnp.moveaxis(q.reshape(B, H, nb, blk, q.shape[-1]), 2, 0) for q in qs)
    out = lax.map(fn, blocks)
    return jnp.moveaxis(out, 0, 2).reshape(B, H, S, out.shape[-1])


def attn_core(q, k, v):
    scale = q.shape[-1] ** -0.5
    def blk(qb):
        s = jnp.einsum('bhqd,bhkd->bhqk', qb[0], k).astype(jnp.float32) * scale
        p = jax.nn.softmax(s, axis=-1).astype(v.dtype)
        return jnp.einsum('bhqk,bhkd->bhqd', p, v)
    return sweep_query_blocks(blk, (q,))


def diff_lambda(lq1, lk1, lq2, lk2, layer):
    lam_init = 0.8 - 0.6 * math.exp(-0.3 * layer)
    f = lambda a, b: jnp.exp(jnp.sum(a.astype(jnp.float32) * b.astype(jnp.float32)))
    return f(lq1, lk1) - f(lq2, lk2) + lam_init, lam_init


def diff_attn_core(q1, q2, k1, k2, v, lam):
    scale = DA_QK ** -0.5
    def blk(qb):
        a, b = qb
        s1 = jnp.einsum('bhqd,bhkd->bhqk', a, k1).astype(jnp.float32) * scale
        s2 = jnp.einsum('bhqd,bhkd->bhqk', b, k2).astype(jnp.float32) * scale
        p = jax.nn.softmax(s1, axis=-1) - lam * jax.nn.softmax(s2, axis=-1)
        return jnp.einsum('bhqk,bhkd->bhqd', p.astype(v.dtype), v)
    return sweep_query_blocks(blk, (q1, q2))


def neighbourhood_attn_latent(q, k, v, kc, vc, rpb):
    B, H, S, d = q.shape
    rows = S // GRID_W
    kr = min(NA_KR, rows)
    scale = d ** -0.5
    qg = q.reshape(B, H, rows, GRID_W, d)
    kg = k.reshape(B, H, rows, GRID_W, d)
    vg = v.reshape(B, H, rows, GRID_W, d)
    r = np.arange(rows)
    row_idx = np.clip(r - kr // 2, 0, rows - kr)[:, None] + np.arange(kr)[None, :]
    kb = jnp.take(kg, row_idx, axis=2)
    vb = jnp.take(vg, row_idx, axis=2)
    w = np.arange(GRID_W)
    col_start = np.clip(w - NA_KC // 2, 0, GRID_W - NA_KC)
    col_ok = (w[None, :] >= col_start[:, None]) & (w[None, :] < col_start[:, None] + NA_KC)
    dr = row_idx - r[:, None] + NA_KR - 1
    dc = np.clip(w[None, :] - w[:, None], -(NA_KC - 1), NA_KC - 1) + NA_KC - 1
    bias = rpb[:, dr[:, None, :, None], dc[None, :, None, :]]
    s_loc = jnp.einsum('bhrqd,bhrjwd->bhrqjw', qg, kb).astype(jnp.float32) * scale + bias.astype(jnp.float32)
    s_loc = jnp.where(col_ok[:, None, :], s_loc, -jnp.inf)
    n_loc = kr * GRID_W
    s_loc = s_loc.reshape(B, H, rows, GRID_W, n_loc)
    s_ctx = jnp.einsum('bhrqd,bhkd->bhrqk', qg, kc).astype(jnp.float32) * scale
    p = jax.nn.softmax(jnp.concatenate([s_loc, s_ctx], axis=-1), axis=-1).astype(v.dtype)
    p_loc = p[..., :n_loc].reshape(B, H, rows, GRID_W, kr, GRID_W)
    o = (jnp.einsum('bhrqjw,bhrjwd->bhrqd', p_loc, vb)
         + jnp.einsum('bhrqk,bhkd->bhrqd', p[..., n_loc:], vc))
    return o.reshape(B, H, S, d)


def conformer_conv(u, dw, db, ln_g, ln_b):
    a, b = jnp.split(u, 2, axis=-1)
    g = a * jax.nn.sigmoid(b)
    y = lax.conv_general_dilated(g, dw[:, None, :], window_strides=(1,),
                                 padding=[(CONV_W // 2, CONV_W // 2)],
                                 dimension_numbers=('NWC', 'WIO', 'NWC'),
                                 feature_group_count=CONV_CH) + db
    return jax.nn.silu(layernorm(y, ln_g, ln_b))


def token_mixers(h, lw, layer, ctx):
    B, S, _ = h.shape
    latent = ctx is not None
    (da_q, da_k, da_v, mla_qd, mla_kvd, mla_kr,
     na_q, na_k, na_v, conv_u) = jnp.split(h @ lw['w_in'], IN_OFFSETS, axis=-1)
    if latent:
        c_dak, c_dav, c_ckv, c_kr, c_nak, c_nav = ctx
        t = jnp.arange(S)
        rows = (t // GRID_W).astype(jnp.float32)
        cols = (t % GRID_W).astype(jnp.float32)
        rot = lambda z: rope2d(z, rows, cols)
    qa, ka, va = heads(da_q, DA_HEADS), heads(da_k, DA_HEADS), heads(da_v, DA_HEADS)
    q1, q2 = qa[..., :DA_QK], qa[..., DA_QK:]
    k1, k2 = ka[..., :DA_QK], ka[..., DA_QK:]
    if latent:
        q1, q2 = rot(q1), rot(q2)
        k1 = jnp.concatenate([rot(k1), c_dak[..., :DA_QK]], axis=2)
        k2 = jnp.concatenate([rot(k2), c_dak[..., DA_QK:]], axis=2)
        va_all = jnp.concatenate([va, c_dav], axis=2)
    else:
        va_all = va
    lam, lam_init = diff_lambda(lw['lq1'], lw['lk1'], lw['lq2'], lw['lk2'], layer)
    o_da = rmsnorm(diff_attn_core(q1, q2, k1, k2, va_all, lam), lw['g_da_subln']) * (1.0 - lam_init)
    qm = heads(rmsnorm(mla_qd, lw['g_mla_q']) @ lw['w_mla_qup'], MLA_HEADS)
    q_nope, q_rope = qm[..., :MLA_NOPE], qm[..., MLA_NOPE:]
    ckv = rmsnorm(mla_kvd, lw['g_mla_kv'])
    if latent:
        q_rope = rot(q_rope)
        ckv_all = jnp.concatenate([ckv, c_ckv], axis=1)
        kr_all = jnp.concatenate([rot(mla_kr), c_kr], axis=1)
    else:
        ckv_all, kr_all = ckv, mla_kr
    kvm = heads(ckv_all @ lw['w_mla_kvup'], MLA_HEADS)
    k_nope, v_m = kvm[..., :MLA_NOPE], kvm[..., MLA_NOPE:]
    Sk = ckv_all.shape[1]
    k_m = jnp.concatenate([k_nope, jnp.broadcast_to(kr_all[:, None], (B, MLA_HEADS, Sk, MLA_ROPE))], axis=-1)
    o_mla = attn_core(jnp.concatenate([q_nope, q_rope], axis=-1), k_m, v_m)
    qn, kn, vn = heads(na_q, NA_HEADS), heads(na_k, NA_HEADS), heads(na_v, NA_HEADS)
    if latent:
        o_na = neighbourhood_attn_latent(qn, kn, vn, c_nak, c_nav, lw['na_rpb'])
    else:
        o_na = attn_core(qn, kn, vn)
    o_conv = conformer_conv(conv_u, lw['conv_dw'], lw['conv_b'], lw['conv_ln_g'], lw['conv_ln_b'])
    mixed = jnp.concatenate([merge_heads(o_da), merge_heads(o_mla), merge_heads(o_na), o_conv], axis=-1) @ lw['w_out']
    new_ctx = None if latent else (ka, va, ckv, mla_kr, kn, vn)
    return mixed, new_ctx


def modulation(cvec, w_mod, b_mod):
    m = jax.nn.silu(cvec) @ w_mod + b_mod
    return jnp.split(m[:, None, :], N_MOD, axis=-1)


def trunk_layer(x, cvec, lw, layer, ctx):
    sh1, sc1, g1, sh2, sc2, g2 = modulation(cvec, lw['w_mod'], lw['b_mod'])
    h = rmsnorm(x, lw['g_mix']) * (1.0 + sc1) + sh1
    mixed, new_ctx = token_mixers(h, lw, layer, ctx)
    x = x + g1 * mixed
    h = rmsnorm(x, lw['g_ff']) * (1.0 + sc2) + sh2
    x = x + g2 * (jnp.square(jax.nn.relu(h @ lw['w_ff1'])) @ lw['w_ff2'])
    return x, new_ctx


def setup_inputs(seed: int = 0) -> dict:
    key = jax.random.key(seed)
    keys = jax.random.split(key, 40)
    it = iter(range(40))
    nrm = lambda shape, s: jax.random.normal(keys[next(it)], shape, jnp.float32) * s
    gain = lambda shape: 1.0 + nrm(shape, 0.02)
    L = DEPTH
    return {
        'x_prompt': nrm((BATCH, SEQ, D_MODEL), 1.0),
        'x_sample': nrm((DEC_BATCH, DEC_SEQ, D_MODEL), 1.0),
        'c': nrm((DEC_BATCH, D_MODEL), 1.0),
        'cache_da_k': nrm((DEC_BATCH, L, DA_HEADS, PAST_LEN, 2 * DA_QK), 1.0),
        'cache_da_v': nrm((DEC_BATCH, L, DA_HEADS, PAST_LEN, DA_V), 1.0),
        'cache_mla_ckv': nrm((DEC_BATCH, L, PAST_LEN, MLA_KV_LORA), 1.0),
        'cache_mla_krope': nrm((DEC_BATCH, L, PAST_LEN, MLA_ROPE), 1.0),
        'cache_na_k': nrm((DEC_BATCH, L, NA_HEADS, PAST_LEN, NA_DIM), 1.0),
        'cache_na_v': nrm((DEC_BATCH, L, NA_HEADS, PAST_LEN, NA_DIM), 1.0),
        'c_ctx': nrm((D_MODEL,), 1.0),
        'w_mod': nrm((L, D_MODEL, N_MOD * D_MODEL), 0.5 * D_MODEL ** -0.5),
        'b_mod': nrm((L, N_MOD * D_MODEL), 0.01),
        'g_norm_mix': gain((L, D_MODEL)),
        'g_norm_ff': gain((L, D_MODEL)),
        'w_in': nrm((L, D_MODEL, IN_COLS), D_MODEL ** -0.5),
        'da_lambda_q1': nrm((L, DA_QK), 0.1),
        'da_lambda_k1': nrm((L, DA_QK), 0.1),
        'da_lambda_q2': nrm((L, DA_QK), 0.1),
        'da_lambda_k2': nrm((L, DA_QK), 0.1),
        'g_da_subln': gain((L, DA_V)),
        'g_mla_q': gain((L, MLA_Q_LORA)),
        'w_mla_qup': nrm((L, MLA_Q_LORA, MLA_HEADS * (MLA_NOPE + MLA_ROPE)), MLA_Q_LORA ** -0.5),
        'g_mla_kv': gain((L, MLA_KV_LORA)),
        'w_mla_kvup': nrm((L, MLA_KV_LORA, MLA_HEADS * (MLA_NOPE + MLA_V)), MLA_KV_LORA ** -0.5),
        'na_rpb': nrm((L, NA_HEADS, 2 * NA_KR - 1, 2 * NA_KC - 1), 0.1),
        'conv_dw': nrm((L, CONV_W, CONV_CH), CONV_W ** -0.5),
        'conv_b': nrm((L, CONV_CH), 0.01),
        'conv_ln_g': gain((L, CONV_CH)),
        'conv_ln_b': nrm((L, CONV_CH), 0.01),
        'w_out': nrm((L, MIX_OUT, D_MODEL), MIX_OUT ** -0.5),
        'w_ff1': nrm((L, D_MODEL, D_FF), D_MODEL ** -0.5),
        'w_ff2': nrm((L, D_FF, D_MODEL), D_FF ** -0.5),
        'g_final': gain((D_MODEL,)),
    }


def reference(x_prompt, x_sample, c, cache_da_k, cache_da_v, cache_mla_ckv, cache_mla_krope,
              cache_na_k, cache_na_v, c_ctx, w_mod, b_mod, g_norm_mix, g_norm_ff, w_in,
              da_lambda_q1, da_lambda_k1, da_lambda_q2, da_lambda_k2, g_da_subln,
              g_mla_q, w_mla_qup, g_mla_kv, w_mla_kvup, na_rpb,
              conv_dw, conv_b, conv_ln_g, conv_ln_b, w_out, w_ff1, w_ff2, g_final):
    xp, xs = x_prompt, x_sample
    ctx_lists = ([], [], [], [], [], [])
    for l in range(DEPTH):
        lw = {'w_mod': w_mod[l], 'b_mod': b_mod[l], 'g_mix': g_norm_mix[l], 'g_ff': g_norm_ff[l],
              'w_in': w_in[l], 'lq1': da_lambda_q1[l], 'lk1': da_lambda_k1[l],
              'lq2': da_lambda_q2[l], 'lk2': da_lambda_k2[l], 'g_da_subln': g_da_subln[l],
              'g_mla_q': g_mla_q[l], 'w_mla_qup': w_mla_qup[l], 'g_mla_kv': g_mla_kv[l],
              'w_mla_kvup': w_mla_kvup[l], 'na_rpb': na_rpb[l], 'conv_dw': conv_dw[l],
              'conv_b': conv_b[l], 'conv_ln_g': conv_ln_g[l], 'conv_ln_b': conv_ln_b[l],
              'w_out': w_out[l], 'w_ff1': w_ff1[l], 'w_ff2': w_ff2[l]}
        xp, new_ctx = trunk_layer(xp, c_ctx[None, :], lw, l, None)
        for lst, t in zip(ctx_lists, new_ctx):
            lst.append(t)
        cached = (cache_da_k[:, l], cache_da_v[:, l], cache_mla_ckv[:, l], cache_mla_krope[:, l],
                  cache_na_k[:, l], cache_na_v[:, l])
        xs, _ = trunk_layer(xs, c, lw, l, cached)
    y_prompt = rmsnorm(xp, g_final)
    y_sample = rmsnorm(xs, g_final)
    new_da_k = jnp.stack(ctx_lists[0], axis=1)
    new_da_v = jnp.stack(ctx_lists[1], axis=1)
    new_mla_ckv = jnp.stack(ctx_lists[2], axis=1)
    new_mla_krope = jnp.stack(ctx_lists[3], axis=1)
    new_na_k = jnp.stack(ctx_lists[4], axis=1)
    new_na_v = jnp.stack(ctx_lists[5], axis=1)
    return (y_prompt, y_sample, new_da_k, new_da_v, new_mla_ckv, new_mla_krope, new_na_k, new_na_v)
```

```python
import functools
import math

import numpy as np
import jax
import jax.numpy as jnp
from jax import lax
from jax.experimental import pallas as pl
from jax.experimental.pallas import tpu as pltpu

F32 = jnp.float32
BF = jnp.bfloat16

D_MODEL = 1024
BATCH = 16
SEQ = 256
DEPTH = 2
DEC_BATCH = 2
DEC_SEQ = 2048
PAST_LEN = 256
GRID_W = 64
GROUP_W = 256
HEAD_DIM = 64
N_HEADS = 4
DA_QK = 32
MLA_NOPE = 64
MLA_ROPE = 32
MLA_KV_LORA = 128
NA_KR = 8
NA_KC = 16
CONV_W = 31
D_FF = 4096
ROPE_BASE = 10000.0
EPS = 1e-6
N_MOD = 6
IN_COLS = 2464

P_DAQ, P_DAK, P_DAV, P_QD, P_KVD, P_KR = 0, 256, 512, 768, 1024, 1152
P_NAQ, P_NAK, P_NAV, P_CONV, P_TOT = 1280, 1536, 1792, 2048, 2560
KR_ORIG_END = 1184
KR_PAD = 128 - MLA_ROPE

DA_SCALE = DA_QK ** -0.5
MLA_SCALE = (MLA_NOPE + MLA_ROPE) ** -0.5
NA_SCALE = HEAD_DIM ** -0.5

TQ = 256
N_QT = DEC_SEQ // TQ
KEYS = DEC_SEQ + PAST_LEN
ROWS_PER_TILE = TQ // GRID_W
N_ROWS = DEC_SEQ // GRID_W
NA_LOCAL = NA_KR * GRID_W
TM_FFN = 512
VMEM_LIMIT = 58 * 1024 * 1024

NT_DIMS = (((1,), (1,)), ((), ()))


def _cparams(*sem):
    return pltpu.CompilerParams(dimension_semantics=sem, vmem_limit_bytes=VMEM_LIMIT)


def _const_spec(shape):
    nd = len(shape)
    return pl.BlockSpec(shape, lambda *_: (0,) * nd, pipeline_mode=pl.Buffered(1))


def _rms(x):
    return x * lax.rsqrt(jnp.mean(x * x, axis=-1, keepdims=True) + EPS)


def _dot(a, b):
    return jnp.dot(a, b, preferred_element_type=F32)


def _qk(q, k):
    return lax.dot_general(q, k, NT_DIMS, preferred_element_type=F32)


def _softmax_parts(s):
    m = jnp.max(s, axis=-1, keepdims=True)
    e = jnp.exp(s - m)
    return e, 1.0 / jnp.sum(e, axis=-1, keepdims=True)


def _lane_ids(width):
    return lax.broadcasted_iota(jnp.int32, (1, width), 1)


def _diff_lambda(lamv_ref, lam_init):
    v = lamv_ref[...]
    a = jnp.exp(jnp.sum(v[0:1] * v[1:2], axis=-1, keepdims=True))
    b = jnp.exp(jnp.sum(v[2:3] * v[3:4], axis=-1, keepdims=True))
    return a - b + lam_init


def _head_rms(x):
    r = _group_of(lax.broadcasted_iota(jnp.int32, (GROUP_W, GROUP_W), 0), HEAD_DIM)
    c = _group_of(lax.broadcasted_iota(jnp.int32, (GROUP_W, GROUP_W), 1), HEAD_DIM)
    ones_bd = jnp.where(r == c, 1.0, 0.0).astype(BF)
    sq = x * x
    hi = sq.astype(BF)
    lo = (sq - hi.astype(F32)).astype(BF)
    ss = _dot(hi, ones_bd) + _dot(lo, ones_bd)
    return x * lax.rsqrt(ss * (1.0 / HEAD_DIM) + EPS)


def _rope(z, cos, sin):
    lane = _lane_ids(z.shape[1])
    swapped = jnp.where((lane & 15) < 8, pltpu.roll(z, 120, 1), pltpu.roll(z, 8, 1))
    return z * cos + swapped * sin


def _group_of(lane, group):
    return lane >> (group.bit_length() - 1)


def _mask_heads(qf, lane, group, idx):
    return jnp.where(_group_of(lane, group) == idx, qf, 0.0).astype(BF)


def _conv_ln_silu(gpad_ref, n, dw_ref, cb_ref, lng_ref, lnb_ref):
    y = jnp.zeros((n, GROUP_W), F32) + cb_ref[...]
    for t in range(CONV_W):
        y = y + gpad_ref[pl.ds(1 + t, n), :] * dw_ref[t:t + 1, :]
    mu = jnp.mean(y, axis=-1, keepdims=True)
    yc = y - mu
    var = jnp.mean(yc * yc, axis=-1, keepdims=True)
    z = yc * lax.rsqrt(var + EPS) * lng_ref[...] + lnb_ref[...]
    return z * jax.nn.sigmoid(z)


def _mod_kernel(cv_ref, w_ref, b_ref, o_ref):
    c = cv_ref[...]
    a = (c * jax.nn.sigmoid(c)).astype(BF)
    o_ref[0] = _dot(a, w_ref[0].astype(BF)) + b_ref[0]


def _modulation(cv, w_mod, b_mod):
    nblk = 4
    wblk = N_MOD * D_MODEL // nblk
    return pl.pallas_call(
        _mod_kernel,
        grid=(DEPTH, nblk),
        in_specs=[pl.BlockSpec((16, D_MODEL), lambda l, j: (0, 0)),
                  pl.BlockSpec((1, D_MODEL, wblk), lambda l, j: (l, 0, j)),
                  pl.BlockSpec((1, 1, wblk), lambda l, j: (l, 0, j))],
        out_specs=pl.BlockSpec((1, 16, wblk), lambda l, j: (l, 0, j)),
        out_shape=jax.ShapeDtypeStruct((DEPTH, 16, N_MOD * D_MODEL), F32),
        compiler_params=_cparams("arbitrary", "arbitrary"),
        name="modulation",
    )(cv, w_mod, b_mod.reshape(DEPTH, 1, N_MOD * D_MODEL))


def _mla_cache_kernel(ckv_ref, kr_ref, w_ref, ka_ref, kb_ref, v_ref):
    kvm = _dot(ckv_ref[0, 0].astype(BF), w_ref[0])
    kr = kr_ref[0, 0].astype(BF)
    kn = kvm[:, :GROUP_W].astype(BF)
    ka_ref[0, 0] = jnp.concatenate([kn[:, :128], kr], axis=1)
    kb_ref[0, 0] = jnp.concatenate([kn[:, 128:], kr], axis=1)
    v_ref[0, 0] = kvm[:, GROUP_W:].astype(BF)


def _mla_cache(c_ckv, c_kr_pad, w_kvup_p):
    blk = pl.BlockSpec((1, 1, PAST_LEN, GROUP_W), lambda b, l: (b, l, 0, 0))
    shp = jax.ShapeDtypeStruct((DEC_BATCH, DEPTH, PAST_LEN, GROUP_W), BF)
    return pl.pallas_call(
        _mla_cache_kernel,
        grid=(DEC_BATCH, DEPTH),
        in_specs=[pl.BlockSpec((1, 1, PAST_LEN, MLA_KV_LORA), lambda b, l: (b, l, 0, 0)),
                  pl.BlockSpec((1, 1, PAST_LEN, 128), lambda b, l: (b, l, 0, 0)),
                  pl.BlockSpec((1, MLA_KV_LORA, 2 * GROUP_W), lambda b, l: (l, 0, 0))],
        out_specs=[blk, blk, blk],
        out_shape=[shp, shp, shp],
        compiler_params=_cparams("arbitrary", "arbitrary"),
        name="mla_cache",
    )(c_ckv, c_kr_pad, w_kvup_p)


N_DR = 2 * NA_KR - 1
N_DC = 2 * NA_KC - 1


def _na_bias_kernel(rpb_ref, o_ref, tp_ref):
    base = (pl.program_id(0) * N_HEADS + pl.program_id(1)) * (N_DR * N_DC)
    cq = lax.broadcasted_iota(jnp.int32, (GRID_W, 128), 0)
    lane = lax.broadcasted_iota(jnp.int32, (GRID_W, 128), 1)
    ck = lane & (GRID_W - 1)
    upper = lane >= GRID_W
    dc = jnp.clip(ck - cq, -(NA_KC - 1), NA_KC - 1) + NA_KC - 1
    cstart = jnp.clip(cq - NA_KC // 2, 0, GRID_W - NA_KC)
    ok = (ck >= cstart) & (ck < cstart + NA_KC)

    def pair(dr, carry):
        val = jnp.zeros((GRID_W, 128), F32)
        for d in range(N_DC):
            s0 = rpb_ref[base + dr * N_DC + d]
            s1 = rpb_ref[base + (dr + 1) * N_DC + d]
            val = jnp.where(dc == d, jnp.where(upper, s1, s0), val)
        tp_ref[dr] = jnp.where(ok, val, -jnp.inf)
        return carry

    lax.fori_loop(0, N_DR - 1, pair, 0)
    for dr0 in range(NA_KR):
        o_ref[0, 0, dr0] = jnp.concatenate([tp_ref[dr0 + 2 * m] for m in range(NA_KR // 2)], axis=1)


def _na_bias(rpb):
    return pl.pallas_call(
        _na_bias_kernel,
        grid=(DEPTH, N_HEADS),
        in_specs=[pl.BlockSpec(memory_space=pltpu.SMEM)],
        out_specs=pl.BlockSpec((1, 1, NA_KR, GRID_W, NA_LOCAL), lambda l, h: (l, h, 0, 0, 0)),
        out_shape=jax.ShapeDtypeStruct((DEPTH, N_HEADS, NA_KR, GRID_W, NA_LOCAL), F32),
        scratch_shapes=[pltpu.VMEM((N_DR - 1, GRID_W, 128), F32)],
        compiler_params=_cparams("arbitrary", "arbitrary"),
        name="na_bias",
    )(rpb.reshape(-1))


def _ctx_kernel(lam_init, x_ref, mod_ref, gmix_ref, win_ref, gq_ref, wqup_ref, gkv_ref, wkv_ref,
                lamv_ref, gsub_ref, dw_ref, cb_ref, lng_ref, lnb_ref,
                mixed_ref, dak_ref, dav_ref, ckv_ref, kr_ref, nak_ref, nav_ref,
                proj_ref, gpad_ref):
    x = x_ref[...]
    sh1 = mod_ref[0:1, 0:D_MODEL]
    sc1 = mod_ref[0:1, D_MODEL:2 * D_MODEL]
    h = (_rms(x) * gmix_ref[...]) * (1.0 + sc1) + sh1
    proj_ref[...] = _dot(h.astype(BF), win_ref[...])

    for hh in range(N_HEADS):
        lo, hi = hh * HEAD_DIM, (hh + 1) * HEAD_DIM
        dak_ref[0, hh] = proj_ref[:, P_DAK + lo:P_DAK + hi]
        dav_ref[0, hh] = proj_ref[:, P_DAV + lo:P_DAV + hi]
        nak_ref[0, hh] = proj_ref[:, P_NAK + lo:P_NAK + hi]
        nav_ref[0, hh] = proj_ref[:, P_NAV + lo:P_NAV + hi]

    lane = _lane_ids(GROUP_W)

    lam = _diff_lambda(lamv_ref, lam_init)
    qa = proj_ref[:, P_DAQ:P_DAQ + GROUP_W] * DA_SCALE
    ka = proj_ref[:, P_DAK:P_DAK + GROUP_W].astype(BF)
    va = proj_ref[:, P_DAV:P_DAV + GROUP_W].astype(BF)
    o_da = jnp.zeros((SEQ, GROUP_W), F32)
    for hh in range(N_HEADS):
        e1, r1 = _softmax_parts(_qk(_mask_heads(qa, lane, DA_QK, 2 * hh), ka))
        e2, r2 = _softmax_parts(_qk(_mask_heads(qa, lane, DA_QK, 2 * hh + 1), ka))
        p = (e1 * r1 - lam * (e2 * r2)).astype(BF)
        o_da = jnp.where(_group_of(lane, HEAD_DIM) == hh, _dot(p, va), o_da)
    o_da = _head_rms(o_da) * gsub_ref[...] * (1.0 - lam_init)

    qd = _rms(proj_ref[:, P_QD:P_QD + GROUP_W]) * gq_ref[...]
    qm = _dot(qd.astype(BF), wqup_ref[...]) * MLA_SCALE
    ckv = _rms(proj_ref[:, P_KVD:P_KVD + MLA_KV_LORA]) * gkv_ref[...]
    ckv_ref[0] = ckv
    kr_pad = proj_ref[:, P_KR:P_KR + 128]
    kr_ref[0] = kr_pad[:, 0:MLA_ROPE]
    kvm = _dot(ckv.astype(BF), wkv_ref[...])
    kn = kvm[:, :GROUP_W].astype(BF)
    vm = kvm[:, GROUP_W:].astype(BF)
    krb = kr_pad.astype(BF)
    kslab = (jnp.concatenate([kn[:, :128], krb], axis=1), jnp.concatenate([kn[:, 128:], krb], axis=1))
    o_mla = jnp.zeros((SEQ, GROUP_W), F32)
    for hh in range(N_HEADS):
        qh = qm[:, hh * GROUP_W:(hh + 1) * GROUP_W].astype(BF)
        e, r = _softmax_parts(_qk(qh, kslab[hh // 2]))
        o_mla = jnp.where(_group_of(lane, HEAD_DIM) == hh, _dot(e.astype(BF), vm) * r, o_mla)

    qn = proj_ref[:, P_NAQ:P_NAQ + GROUP_W] * NA_SCALE
    kn2 = proj_ref[:, P_NAK:P_NAK + GROUP_W].astype(BF)
    vn2 = proj_ref[:, P_NAV:P_NAV + GROUP_W].astype(BF)
    o_na = jnp.zeros((SEQ, GROUP_W), F32)
    for hh in range(N_HEADS):
        e, r = _softmax_parts(_qk(_mask_heads(qn, lane, HEAD_DIM, hh), kn2))
        o_na = jnp.where(_group_of(lane, HEAD_DIM) == hh, _dot(e.astype(BF), vn2) * r, o_na)

    g = proj_ref[:, P_CONV:P_CONV + GROUP_W] * jax.nn.sigmoid(proj_ref[:, P_CONV + GROUP_W:P_TOT])
    gpad_ref[0:16] = jnp.zeros((16, GROUP_W), F32)
    gpad_ref[16 + SEQ:32 + SEQ] = jnp.zeros((16, GROUP_W), F32)
    gpad_ref[16:16 + SEQ] = g
    o_conv = _conv_ln_silu(gpad_ref, SEQ, dw_ref, cb_ref, lng_ref, lnb_ref)

    mixed_ref[...] = jnp.concatenate([o_da, o_mla, o_na, o_conv], axis=1).astype(BF)


def _ctx_layer(layer, x, mod_l, wts):
    lam_init = 0.8 - 0.6 * math.exp(-0.3 * layer)
    head_blk = pl.BlockSpec((1, N_HEADS, SEQ, HEAD_DIM), lambda b: (b, 0, 0, 0))
    head_shp = jax.ShapeDtypeStruct((BATCH, N_HEADS, SEQ, HEAD_DIM), F32)
    in_specs = [pl.BlockSpec((SEQ, D_MODEL), lambda b: (b, 0)),
                _const_spec((16, N_MOD * D_MODEL)),
                _const_spec((1, D_MODEL)),
                _const_spec((D_MODEL, P_TOT)),
                _const_spec((1, GROUP_W)),
                _const_spec((GROUP_W, N_HEADS * GROUP_W)),
                _const_spec((1, MLA_KV_LORA)),
                _const_spec((MLA_KV_LORA, 2 * GROUP_W)),
                _const_spec((4, DA_QK)),
                _const_spec((1, GROUP_W)),
                _const_spec((32, GROUP_W)),
                _const_spec((1, GROUP_W)),
                _const_spec((1, GROUP_W)),
                _const_spec((1, GROUP_W))]
    out_specs = [pl.BlockSpec((SEQ, D_MODEL), lambda b: (b, 0)),
                 head_blk, head_blk,
                 pl.BlockSpec((1, SEQ, MLA_KV_LORA), lambda b: (b, 0, 0)),
                 pl.BlockSpec((1, SEQ, MLA_ROPE), lambda b: (b, 0, 0)),
                 head_blk, head_blk]
    out_shape = [jax.ShapeDtypeStruct((BATCH * SEQ, D_MODEL), BF),
                 head_shp, head_shp,
                 jax.ShapeDtypeStruct((BATCH, SEQ, MLA_KV_LORA), F32),
                 jax.ShapeDtypeStruct((BATCH, SEQ, MLA_ROPE), F32),
                 head_shp, head_shp]
    return pl.pallas_call(
        functools.partial(_ctx_kernel, lam_init),
        grid=(BATCH,),
        in_specs=in_specs,
        out_specs=out_specs,
        out_shape=out_shape,
        scratch_shapes=[pltpu.VMEM((SEQ, P_TOT), F32), pltpu.VMEM((SEQ + 32, GROUP_W), F32)],
        compiler_params=_cparams("arbitrary"),
        name=f"ctx_layer{layer}",
    )(x, mod_l, wts["g_mix"], wts["w_in"], wts["g_q"], wts["w_qup"], wts["g_kv"], wts["w_kvup"],
      wts["lamv"], wts["g_sub"], wts["dw"], wts["cb"], wts["ln_g"], wts["ln_b"])


def _lat_proj_kernel(x_ref, mod_ref, gmix_ref, win_ref, gq_ref, wqup_ref, gkv_ref, wkv_ref,
                     cos_ref, sin_ref, cdak_ref, cdav_ref, cka_ref, ckb_ref, cmv_ref, cnak_ref, cnav_ref,
                     daq_ref, mq_ref, naq_ref, g_ref,
                     dak_ref, dav_ref, ka_ref, kb_ref, mv_ref, nak_ref, nav_ref):
    b = pl.program_id(0)
    j = pl.program_id(1)

    @pl.when(j < N_QT)
    def _():
        x = x_ref[...]
        sh1 = mod_ref[pl.ds(1 + b, 1), 0:D_MODEL]
        sc1 = mod_ref[pl.ds(1 + b, 1), D_MODEL:2 * D_MODEL]
        h = (_rms(x) * gmix_ref[...]) * (1.0 + sc1) + sh1
        proj = _dot(h.astype(BF), win_ref[...])
        cos = cos_ref[...]
        sin = sin_ref[...]

        def rope2(z):
            return jnp.concatenate([_rope(z[:, :128], cos[:, :128], sin[:, :128]),
                                    _rope(z[:, 128:], cos[:, 128:], sin[:, 128:])], axis=1)

        daq_ref[0] = (rope2(proj[:, P_DAQ:P_DAQ + GROUP_W]) * DA_SCALE).astype(BF)
        dak_ref[0] = rope2(proj[:, P_DAK:P_DAK + GROUP_W]).astype(BF)
        dav_ref[0] = proj[:, P_DAV:P_DAV + GROUP_W].astype(BF)

        qd = _rms(proj[:, P_QD:P_QD + GROUP_W]) * gq_ref[...]
        qm = _dot(qd.astype(BF), wqup_ref[...])
        for hh in range(N_HEADS):
            nope = qm[:, hh * GROUP_W:hh * GROUP_W + 128]
            rope = _rope(qm[:, hh * GROUP_W + 128:(hh + 1) * GROUP_W], cos[:, :128], sin[:, :128])
            mq_ref[0, hh] = (jnp.concatenate([nope, rope], axis=1) * MLA_SCALE).astype(BF)
        ckv = _rms(proj[:, P_KVD:P_KVD + MLA_KV_LORA]) * gkv_ref[...]
        kvm = _dot(ckv.astype(BF), wkv_ref[...])
        kn = kvm[:, :GROUP_W].astype(BF)
        krb = _rope(proj[:, P_KR:P_KR + 128], cos[:, :128], sin[:, :128]).astype(BF)
        ka_ref[0] = jnp.concatenate([kn[:, :128], krb], axis=1)
        kb_ref[0] = jnp.concatenate([kn[:, 128:], krb], axis=1)
        mv_ref[0] = kvm[:, GROUP_W:].astype(BF)

        naq_ref[0] = (proj[:, P_NAQ:P_NAQ + GROUP_W] * NA_SCALE).astype(BF)
        nak_ref[0] = proj[:, P_NAK:P_NAK + GROUP_W].astype(BF)
        nav_ref[0] = proj[:, P_NAV:P_NAV + GROUP_W].astype(BF)
        g_ref[0] = proj[:, P_CONV:P_CONV + GROUP_W] * jax.nn.sigmoid(proj[:, P_CONV + GROUP_W:P_TOT])

    @pl.when(j == N_QT)
    def _():
        dak_ref[0] = cdak_ref[0]
        dav_ref[0] = cdav_ref[0]
        ka_ref[0] = cka_ref[0]
        kb_ref[0] = ckb_ref[0]
        mv_ref[0] = cmv_ref[0]
        nak_ref[0] = cnak_ref[0]
        nav_ref[0] = cnav_ref[0]


def _lat_proj(layer, x, mod_l, wts, cos_t, sin_t, caches):
    jq = lambda j: jnp.minimum(j, N_QT - 1)
    cache_spec = pl.BlockSpec((1, PAST_LEN, GROUP_W), lambda b, j: (b, 0, 0))
    q_spec = pl.BlockSpec((1, TQ, GROUP_W), lambda b, j: (b, jq(j), 0))
    k_spec = pl.BlockSpec((1, TQ, GROUP_W), lambda b, j: (b, j, 0))
    in_specs = [pl.BlockSpec((TQ, D_MODEL), lambda b, j: (b * N_QT + jq(j), 0)),
                _const_spec((16, N_MOD * D_MODEL)),
                _const_spec((1, D_MODEL)),
                _const_spec((D_MODEL, P_TOT)),
                _const_spec((1, GROUP_W)),
                _const_spec((GROUP_W, N_HEADS * GROUP_W)),
                _const_spec((1, MLA_KV_LORA)),
                _const_spec((MLA_KV_LORA, 2 * GROUP_W)),
                pl.BlockSpec((TQ, GROUP_W), lambda b, j: (jq(j), 0)),
                pl.BlockSpec((TQ, GROUP_W), lambda b, j: (jq(j), 0))] + [cache_spec] * 7
    q_shp = jax.ShapeDtypeStruct((DEC_BATCH, DEC_SEQ, GROUP_W), BF)
    k_shp = jax.ShapeDtypeStruct((DEC_BATCH, KEYS, GROUP_W), BF)
    out_specs = [q_spec,
                 pl.BlockSpec((1, N_HEADS, TQ, GROUP_W), lambda b, j: (b, 0, jq(j), 0)),
                 q_spec, q_spec] + [k_spec] * 7
    out_shape = [q_shp,
                 jax.ShapeDtypeStruct((DEC_BATCH, N_HEADS, DEC_SEQ, GROUP_W), BF),
                 q_shp,
                 jax.ShapeDtypeStruct((DEC_BATCH, DEC_SEQ, GROUP_W), F32)] + [k_shp] * 7
    return pl.pallas_call(
        _lat_proj_kernel,
        grid=(DEC_BATCH, N_QT + 1),
        in_specs=in_specs,
        out_specs=out_specs,
        out_shape=out_shape,
        compiler_params=_cparams("arbitrary", "arbitrary"),
        name=f"lat_proj{layer}",
    )(x, mod_l, wts["g_mix"], wts["w_in"], wts["g_q"], wts["w_qup"], wts["g_kv"], wts["w_kvup"],
      cos_t, sin_t, *caches)


def _lat_attn_kernel(lam_init, daq_ref, mq_ref, naq_ref, g_ref,
                     dak_ref, dav_ref, ka_ref, kb_ref, mv_ref, nak_ref, nav_ref,
                     nab_ref, lamv_ref, gsub_ref, dw_ref, cb_ref, lng_ref, lnb_ref,
                     mixed_ref, gpad_ref):
    t = pl.program_id(1)
    lane = _lane_ids(GROUP_W)

    lam = _diff_lambda(lamv_ref, lam_init)
    qa = daq_ref[0].astype(F32)
    ka = dak_ref[0]
    va = dav_ref[0]

    def da_head(hh, o_acc):
        e1, r1 = _softmax_parts(_qk(_mask_heads(qa, lane, DA_QK, 2 * hh), ka))
        e2, r2 = _softmax_parts(_qk(_mask_heads(qa, lane, DA_QK, 2 * hh + 1), ka))
        p = (e1 * r1 - lam * (e2 * r2)).astype(BF)
        return jnp.where(_group_of(lane, HEAD_DIM) == hh, _dot(p, va), o_acc)

    o_da = lax.fori_loop(0, N_HEADS, da_head, jnp.zeros((TQ, GROUP_W), F32))
    o_da = _head_rms(o_da) * gsub_ref[...] * (1.0 - lam_init)

    vm = mv_ref[0]

    o_mla = jnp.zeros((TQ, GROUP_W), F32)
    for hh in range(N_HEADS):
        kslab = (ka_ref, kb_ref)[hh // 2][0]
        e, r = _softmax_parts(_qk(mq_ref[0, hh], kslab))
        o_mla = jnp.where(_group_of(lane, HEAD_DIM) == hh, _dot(e.astype(BF), vm) * r, o_mla)

    kc = nak_ref[0, DEC_SEQ:KEYS, :]
    vc = nav_ref[0, DEC_SEQ:KEYS, :]
    rows = []
    for i in range(ROWS_PER_TILE):
        r = t * ROWS_PER_TILE + i
        start = jnp.clip(r - NA_KR // 2, 0, N_ROWS - NA_KR)
        dr0 = start - r + NA_KR - 1
        koff = pl.multiple_of(start * GRID_W, GRID_W)
        kl = nak_ref[0, pl.ds(koff, NA_LOCAL), :]
        vl = nav_ref[0, pl.ds(koff, NA_LOCAL), :]
        qrow = naq_ref[0, i * GRID_W:(i + 1) * GRID_W, :].astype(F32)
        o_row = jnp.zeros((GRID_W, GROUP_W), F32)
        for hh in range(N_HEADS):
            qh = _mask_heads(qrow, lane, HEAD_DIM, hh)
            s_loc = _qk(qh, kl) + nab_ref[hh, dr0]
            s_ctx = _qk(qh, kc)
            m = jnp.maximum(jnp.max(s_loc, axis=-1, keepdims=True), jnp.max(s_ctx, axis=-1, keepdims=True))
            e_loc = jnp.exp(s_loc - m)
            e_ctx = jnp.exp(s_ctx - m)
            den = jnp.sum(e_loc, axis=-1, keepdims=True) + jnp.sum(e_ctx, axis=-1, keepdims=True)
            o = (_dot(e_loc.astype(BF), vl) + _dot(e_ctx.astype(BF), vc)) * (1.0 / den)
            o_row = jnp.where(_group_of(lane, HEAD_DIM) == hh, o, o_row)
        rows.append(o_row)
    o_na = jnp.concatenate(rows, axis=0)

    base = pl.multiple_of(t * TQ, TQ)
    gpad_ref[16:16 + TQ] = g_ref[0, pl.ds(base, TQ), :]
    lo = g_ref[0, pl.ds(pl.multiple_of(jnp.maximum(base - 16, 0), 16), 16), :]
    hi = g_ref[0, pl.ds(pl.multiple_of(jnp.minimum(base + TQ, DEC_SEQ - 16), 16), 16), :]
    gpad_ref[0:16] = jnp.where(t > 0, lo, 0.0)
    gpad_ref[16 + TQ:32 + TQ] = jnp.where(t < N_QT - 1, hi, 0.0)
    o_conv = _conv_ln_silu(gpad_ref, TQ, dw_ref, cb_ref, lng_ref, lnb_ref)

    mixed_ref[...] = jnp.concatenate([o_da, o_mla, o_na, o_conv], axis=1).astype(BF)


def _lat_attn(layer, proj_outs, nab_l, wts):
    lam_init = 0.8 - 0.6 * math.exp(-0.3 * layer)
    q_spec = pl.BlockSpec((1, TQ, GROUP_W), lambda b, t: (b, t, 0))
    full_k = pl.BlockSpec((1, KEYS, GROUP_W), lambda b, t: (b, 0, 0))
    in_specs = [q_spec,
                pl.BlockSpec((1, N_HEADS, TQ, GROUP_W), lambda b, t: (b, 0, t, 0)),
                q_spec,
                pl.BlockSpec((1, DEC_SEQ, GROUP_W), lambda b, t: (b, 0, 0))] + [full_k] * 7 + [
                _const_spec((N_HEADS, NA_KR, GRID_W, NA_LOCAL)),
                _const_spec((4, DA_QK)),
                _const_spec((1, GROUP_W)),
                _const_spec((32, GROUP_W)),
                _const_spec((1, GROUP_W)),
                _const_spec((1, GROUP_W)),
                _const_spec((1, GROUP_W))]
    return pl.pallas_call(
        functools.partial(_lat_attn_kernel, lam_init),
        grid=(DEC_BATCH, N_QT),
        in_specs=in_specs,
        out_specs=pl.BlockSpec((TQ, D_MODEL), lambda b, t: (b * N_QT + t, 0)),
        out_shape=jax.ShapeDtypeStruct((DEC_BATCH * DEC_SEQ, D_MODEL), BF),
        scratch_shapes=[pltpu.VMEM((TQ + 32, GROUP_W), F32)],
        compiler_params=_cparams("arbitrary", "arbitrary"),
        name=f"lat_attn{layer}",
    )(*proj_outs, nab_l, wts["lamv"], wts["g_sub"], wts["dw"], wts["cb"], wts["ln_g"], wts["ln_b"])


def _out_ffn_kernel(final, tiles_per_mod, mod_base, x_ref, mx_ref, mod_ref, wout_ref, gff_ref,
                    w1_ref, w2_ref, gfin_ref, o_ref):
    row = mod_base + pl.program_id(0) // tiles_per_mod
    g1 = mod_ref[pl.ds(row, 1), 2 * D_MODEL:3 * D_MODEL]
    sh2 = mod_ref[pl.ds(row, 1), 3 * D_MODEL:4 * D_MODEL]
    sc2 = mod_ref[pl.ds(row, 1), 4 * D_MODEL:5 * D_MODEL]
    g2 = mod_ref[pl.ds(row, 1), 5 * D_MODEL:6 * D_MODEL]
    x1 = x_ref[...] + g1 * _dot(mx_ref[...], wout_ref[...])
    h2 = ((_rms(x1) * gff_ref[...]) * (1.0 + sc2) + sh2).astype(BF)
    acc = jnp.zeros((TM_FFN, D_MODEL), F32)
    for c in range(D_FF // D_MODEL):
        a = jnp.maximum(_dot(h2, w1_ref[:, c * D_MODEL:(c + 1) * D_MODEL]), 0.0)
        acc = acc + _dot((a * a).astype(BF), w2_ref[c * D_MODEL:(c + 1) * D_MODEL, :])
    x2 = x1 + g2 * acc
    o_ref[...] = _rms(x2) * gfin_ref[...] if final else x2


def _out_ffn(name, final, tiles_per_mod, mod_base, x, mixed, mod_l, wts, g_final):
    n = x.shape[0]
    tile = pl.BlockSpec((TM_FFN, D_MODEL), lambda i: (i, 0))
    return pl.pallas_call(
        functools.partial(_out_ffn_kernel, final, tiles_per_mod, mod_base),
        grid=(n // TM_FFN,),
        in_specs=[tile, tile,
                  _const_spec((16, N_MOD * D_MODEL)),
                  _const_spec((D_MODEL, D_MODEL)),
                  _const_spec((1, D_MODEL)),
                  _const_spec((D_MODEL, D_FF)),
                  _const_spec((D_FF, D_MODEL)),
                  _const_spec((1, D_MODEL))],
        out_specs=tile,
        out_shape=jax.ShapeDtypeStruct((n, D_MODEL), F32),
        compiler_params=_cparams("arbitrary"),
        name=name,
    )(x, mixed, mod_l, wts["w_out"], wts["g_ff"], wts["w_ff1"], wts["w_ff2"], g_final)


def _rope_tables():
    t = jnp.arange(DEC_SEQ)
    rows = (t // GRID_W).astype(F32)
    cols = (t % GRID_W).astype(F32)
    lane = np.arange(GROUP_W)
    c = lane % 32
    i = (c % 16) % 8
    freqs = ROPE_BASE ** (-jnp.arange(8, dtype=F32) * 2.0 / 16)
    pos = jnp.where(jnp.asarray(c < 16)[None, :], rows[:, None], cols[:, None])
    ang = pos * freqs[i][None, :]
    first = jnp.asarray((c % 16) < 8)[None, :]
    return jnp.cos(ang), jnp.where(first, -jnp.sin(ang), jnp.sin(ang))


def _qup_gather_index():
    idx = np.full((N_HEADS * GROUP_W,), -1, np.int64)
    for h in range(N_HEADS):
        src = h * (MLA_NOPE + MLA_ROPE)
        dst = h * GROUP_W + (h % 2) * MLA_NOPE
        idx[dst:dst + MLA_NOPE] = np.arange(src, src + MLA_NOPE)
        idx[h * GROUP_W + 128:h * GROUP_W + 128 + MLA_ROPE] = np.arange(src + MLA_NOPE, src + MLA_NOPE + MLA_ROPE)
    return idx


def _kvup_perm():
    k = [h * 128 + d for h in range(N_HEADS) for d in range(MLA_NOPE)]
    v = [h * 128 + MLA_NOPE + d for h in range(N_HEADS) for d in range(HEAD_DIM)]
    return np.asarray(k + v)


def _heads_to_lanes(c):
    b, l, h, s, d = c.shape
    return c.transpose(0, 1, 3, 2, 4).reshape(b, l, s, h * d).astype(BF)


def kernel(x_prompt, x_sample, c, cache_da_k, cache_da_v, cache_mla_ckv, cache_mla_krope, cache_na_k, cache_na_v, c_ctx, w_mod, b_mod, g_norm_mix, g_norm_ff, w_in, da_lambda_q1, da_lambda_k1, da_lambda_q2, da_lambda_k2, g_da_subln, g_mla_q, w_mla_qup, g_mla_kv, w_mla_kvup, na_rpb, conv_dw, conv_b, conv_ln_g, conv_ln_b, w_out, w_ff1, w_ff2, g_final):
    w_in_p = jnp.concatenate([w_in[..., :KR_ORIG_END], jnp.zeros((DEPTH, D_MODEL, KR_PAD), F32),
                              w_in[..., KR_ORIG_END:]], axis=-1).astype(BF)
    qidx = _qup_gather_index()
    w_qup_e = jnp.where(jnp.asarray(qidx >= 0)[None, None, :],
                        jnp.take(w_mla_qup, jnp.asarray(np.maximum(qidx, 0)), axis=-1), 0.0).astype(BF)
    w_kvup_p = jnp.take(w_mla_kvup, jnp.asarray(_kvup_perm()), axis=-1).astype(BF)
    w_out_b = w_out.astype(BF)
    w_ff1_b = w_ff1.astype(BF)
    w_ff2_b = w_ff2.astype(BF)
    lamv = jnp.stack([da_lambda_q1, da_lambda_k1, da_lambda_q2, da_lambda_k2], axis=1)
    g_sub = jnp.tile(g_da_subln, (1, N_HEADS))
    dw_p = jnp.concatenate([conv_dw, jnp.zeros((DEPTH, 1, GROUP_W), F32)], axis=1)
    cv = jnp.concatenate([c_ctx[None, :], c, jnp.zeros((16 - 1 - DEC_BATCH, D_MODEL), F32)], axis=0)
    cos_t, sin_t = _rope_tables()
    g_final2 = g_final.reshape(1, D_MODEL)

    mod = _modulation(cv, w_mod, b_mod)
    c_kr_pad = jnp.pad(cache_mla_krope, ((0, 0), (0, 0), (0, 0), (0, KR_PAD)))
    cka, ckb, cmv = _mla_cache(cache_mla_ckv, c_kr_pad, w_kvup_p)
    cdak, cdav = _heads_to_lanes(cache_da_k), _heads_to_lanes(cache_da_v)
    cnak, cnav = _heads_to_lanes(cache_na_k), _heads_to_lanes(cache_na_v)
    nab = _na_bias(na_rpb)

    xp = x_prompt.reshape(BATCH * SEQ, D_MODEL)
    xs = x_sample.reshape(DEC_BATCH * DEC_SEQ, D_MODEL)
    ctx_outs = [[] for _ in range(6)]
    for l in range(DEPTH):
        wts = dict(g_mix=g_norm_mix[l].reshape(1, -1), w_in=w_in_p[l], g_q=g_mla_q[l].reshape(1, -1),
                   w_qup=w_qup_e[l], g_kv=g_mla_kv[l].reshape(1, -1), w_kvup=w_kvup_p[l], lamv=lamv[l],
                   g_sub=g_sub[l].reshape(1, -1), dw=dw_p[l], cb=conv_b[l].reshape(1, -1),
                   ln_g=conv_ln_g[l].reshape(1, -1), ln_b=conv_ln_b[l].reshape(1, -1),
                   w_out=w_out_b[l], g_ff=g_norm_ff[l].reshape(1, -1), w_ff1=w_ff1_b[l], w_ff2=w_ff2_b[l])
        final = l == DEPTH - 1
        mixed_p, *new_ctx = _ctx_layer(l, xp, mod[l], wts)
        for lst, a in zip(ctx_outs, new_ctx):
            lst.append(a)
        xp = _out_ffn(f"ctx_ffn{l}", final, BATCH * SEQ // TM_FFN, 0, xp, mixed_p, mod[l], wts, g_final2)

        caches = (cdak[:, l], cdav[:, l], cka[:, l], ckb[:, l], cmv[:, l], cnak[:, l], cnav[:, l])
        proj_outs = _lat_proj(l, xs, mod[l], wts, cos_t, sin_t, caches)
        mixed_s = _lat_attn(l, proj_outs, nab[l], wts)
        xs = _out_ffn(f"lat_ffn{l}", final, DEC_SEQ // TM_FFN, 1, xs, mixed_s, mod[l], wts, g_final2)

    y_prompt = xp.reshape(BATCH, SEQ, D_MODEL)
    y_sample = xs.reshape(DEC_BATCH, DEC_SEQ, D_MODEL)
    return (y_prompt, y_sample) + tuple(jnp.stack(lst, axis=1) for lst in ctx_outs)
```

```python
import functools
import math

import numpy as np
import jax
import jax.numpy as jnp
from jax import lax
from jax.experimental import pallas as pl
from jax.experimental.pallas import tpu as pltpu

F32 = jnp.float32
BF = jnp.bfloat16

D_MODEL = 1024
BATCH = 16
SEQ = 256
DEPTH = 2
DEC_BATCH = 2
DEC_SEQ = 2048
PAST_LEN = 256
GRID_W = 64
GROUP_W = 256
HEAD_DIM = 64
N_HEADS = 4
DA_QK = 32
MLA_NOPE = 64
MLA_ROPE = 32
MLA_KV_LORA = 128
NA_KR = 8
NA_KC = 16
CONV_W = 31
D_FF = 4096
ROPE_BASE = 10000.0
EPS = 1e-6
N_MOD = 6
IN_COLS = 2464

P_DAQ, P_DAK, P_DAV, P_QD, P_KVD, P_KR = 0, 256, 512, 768, 1024, 1152
P_NAQ, P_NAK, P_NAV, P_CONV, P_TOT = 1280, 1536, 1792, 2048, 2560
KR_ORIG_END = 1184
KR_PAD = 128 - MLA_ROPE

DA_SCALE = DA_QK ** -0.5
MLA_SCALE = (MLA_NOPE + MLA_ROPE) ** -0.5
NA_SCALE = HEAD_DIM ** -0.5
LOG2E = math.log2(math.e)

TQ = 256
N_QT = DEC_SEQ // TQ
KEYS = DEC_SEQ + PAST_LEN
ROWS_PER_TILE = TQ // GRID_W
N_ROWS = DEC_SEQ // GRID_W
NA_LOCAL = NA_KR * GRID_W
TM_FFN = 512
VMEM_LIMIT = 58 * 1024 * 1024

NT_DIMS = (((1,), (1,)), ((), ()))


def _cparams(*sem):
    return pltpu.CompilerParams(dimension_semantics=sem, vmem_limit_bytes=VMEM_LIMIT)


def _const_spec(shape):
    nd = len(shape)
    return pl.BlockSpec(shape, lambda *_: (0,) * nd, pipeline_mode=pl.Buffered(1))


def _layer_spec(layer, shape):
    nd = len(shape)
    return pl.BlockSpec((None,) + tuple(shape), lambda *_: (layer,) + (0,) * nd, pipeline_mode=pl.Buffered(1))


def _rms(x):
    return x * lax.rsqrt(jnp.mean(x * x, axis=-1, keepdims=True) + EPS)


def _dot(a, b):
    return jnp.dot(a, b, preferred_element_type=F32)


def _qk(q, k):
    return lax.dot_general(q, k, NT_DIMS, preferred_element_type=F32)


def _softmax_parts(s):
    m = jnp.max(s, axis=-1, keepdims=True)
    e = jnp.exp(s - m)
    return e, 1.0 / jnp.sum(e, axis=-1, keepdims=True)


def _softmax2_parts(s):
    m = jnp.max(s, axis=-1, keepdims=True)
    e = jnp.exp2(s - m)
    return e, 1.0 / jnp.sum(e, axis=-1, keepdims=True)


def _lane_ids(width):
    return lax.broadcasted_iota(jnp.int32, (1, width), 1)


def _diff_lambda(lamv_ref, lam_init):
    v = lamv_ref[...]
    a = jnp.exp(jnp.sum(v[0:1] * v[1:2], axis=-1, keepdims=True))
    b = jnp.exp(jnp.sum(v[2:3] * v[3:4], axis=-1, keepdims=True))
    return a - b + lam_init


def _group_of(lane, group):
    return lane >> (group.bit_length() - 1)


def _head_rms(x):
    r = _group_of(lax.broadcasted_iota(jnp.int32, (GROUP_W, GROUP_W), 0), HEAD_DIM)
    c = _group_of(lax.broadcasted_iota(jnp.int32, (GROUP_W, GROUP_W), 1), HEAD_DIM)
    ones_bd = jnp.where(r == c, 1.0, 0.0).astype(BF)
    sq = x * x
    hi = sq.astype(BF)
    lo = (sq - hi.astype(F32)).astype(BF)
    ss = _dot(hi, ones_bd) + _dot(lo, ones_bd)
    return x * lax.rsqrt(ss * (1.0 / HEAD_DIM) + EPS)


def _rope(z, cos, sin):
    lane = _lane_ids(z.shape[1])
    swapped = jnp.where((lane & 15) < 8, pltpu.roll(z, 120, 1), pltpu.roll(z, 8, 1))
    return z * cos + swapped * sin


def _mask_heads(qf, lane, group, idx):
    return jnp.where(_group_of(lane, group) == idx, qf, 0.0).astype(BF)


def _conv_ln_silu(gpad_ref, n, dw_ref, cb_ref, lng_ref, lnb_ref):
    y = jnp.zeros((n, GROUP_W), F32) + cb_ref[...]
    for t in range(CONV_W):
        y = y + gpad_ref[pl.ds(1 + t, n), :] * dw_ref[t:t + 1, :]
    mu = jnp.mean(y, axis=-1, keepdims=True)
    yc = y - mu
    var = jnp.mean(yc * yc, axis=-1, keepdims=True)
    z = yc * lax.rsqrt(var + EPS) * lng_ref[...] + lnb_ref[...]
    return z * jax.nn.sigmoid(z)


def _mod_kernel(cv_ref, w_ref, b_ref, o_ref):
    c = cv_ref[...]
    a = (c * jax.nn.sigmoid(c)).astype(BF)
    o_ref[0] = _dot(a, w_ref[0].astype(BF)) + b_ref[0]


def _modulation(cv, w_mod, b_mod):
    nblk = 4
    wblk = N_MOD * D_MODEL // nblk
    return pl.pallas_call(
        _mod_kernel,
        grid=(DEPTH, nblk),
        in_specs=[pl.BlockSpec((16, D_MODEL), lambda l, j: (0, 0)),
                  pl.BlockSpec((1, D_MODEL, wblk), lambda l, j: (l, 0, j)),
                  pl.BlockSpec((1, 1, wblk), lambda l, j: (l, 0, j))],
        out_specs=pl.BlockSpec((1, 16, wblk), lambda l, j: (l, 0, j)),
        out_shape=jax.ShapeDtypeStruct((DEPTH, 16, N_MOD * D_MODEL), F32),
        compiler_params=_cparams("arbitrary", "arbitrary"),
        name="modulation",
    )(cv, w_mod, b_mod.reshape(DEPTH, 1, N_MOD * D_MODEL))


def _mla_cache_kernel(ckv_ref, kr_ref, w_ref, ka_ref, kb_ref, v_ref):
    kvm = _dot(ckv_ref[0, 0].astype(BF), w_ref[0])
    kr = kr_ref[0, 0].astype(BF)
    kn = kvm[:, :GROUP_W].astype(BF)
    ka_ref[0, 0] = jnp.concatenate([kn[:, :128], kr], axis=1)
    kb_ref[0, 0] = jnp.concatenate([kn[:, 128:], kr], axis=1)
    v_ref[0, 0] = kvm[:, GROUP_W:].astype(BF)


def _mla_cache(c_ckv, c_kr_pad, w_kvup_p):
    blk = pl.BlockSpec((1, 1, PAST_LEN, GROUP_W), lambda b, l: (b, l, 0, 0))
    shp = jax.ShapeDtypeStruct((DEC_BATCH, DEPTH, PAST_LEN, GROUP_W), BF)
    return pl.pallas_call(
        _mla_cache_kernel,
        grid=(DEC_BATCH, DEPTH),
        in_specs=[pl.BlockSpec((1, 1, PAST_LEN, MLA_KV_LORA), lambda b, l: (b, l, 0, 0)),
                  pl.BlockSpec((1, 1, PAST_LEN, 128), lambda b, l: (b, l, 0, 0)),
                  pl.BlockSpec((1, MLA_KV_LORA, 2 * GROUP_W), lambda b, l: (l, 0, 0))],
        out_specs=[blk, blk, blk],
        out_shape=[shp, shp, shp],
        compiler_params=_cparams("arbitrary", "arbitrary"),
        name="mla_cache",
    )(c_ckv, c_kr_pad, w_kvup_p)


N_DR = 2 * NA_KR - 1
N_DC = 2 * NA_KC - 1


def _na_bias_kernel(rpb_ref, o_ref, tp_ref):
    base = (pl.program_id(0) * N_HEADS + pl.program_id(1)) * (N_DR * N_DC)
    cq = lax.broadcasted_iota(jnp.int32, (GRID_W, 128), 0)
    lane = lax.broadcasted_iota(jnp.int32, (GRID_W, 128), 1)
    ck = lane & (GRID_W - 1)
    upper = lane >= GRID_W
    dc = jnp.clip(ck - cq, -(NA_KC - 1), NA_KC - 1) + NA_KC - 1
    cstart = jnp.clip(cq - NA_KC // 2, 0, GRID_W - NA_KC)
    ok = (ck >= cstart) & (ck < cstart + NA_KC)

    def pair(dr, carry):
        val = jnp.zeros((GRID_W, 128), F32)
        for d in range(N_DC):
            s0 = rpb_ref[base + dr * N_DC + d]
            s1 = rpb_ref[base + (dr + 1) * N_DC + d]
            val = jnp.where(dc == d, jnp.where(upper, s1, s0), val)
        tp_ref[dr] = jnp.where(ok, val, -jnp.inf)
        return carry

    lax.fori_loop(0, N_DR - 1, pair, 0)
    for dr0 in range(NA_KR):
        o_ref[0, dr0] = jnp.concatenate([tp_ref[dr0 + 2 * m] for m in range(NA_KR // 2)], axis=1)


def _na_bias(rpb):
    return pl.pallas_call(
        _na_bias_kernel,
        grid=(DEPTH, N_HEADS),
        in_specs=[pl.BlockSpec(memory_space=pltpu.SMEM)],
        out_specs=pl.BlockSpec((1, NA_KR, GRID_W, NA_LOCAL), lambda l, h: (l, 0, h, 0)),
        out_shape=jax.ShapeDtypeStruct((DEPTH, NA_KR, N_HEADS * GRID_W, NA_LOCAL), F32),
        scratch_shapes=[pltpu.VMEM((N_DR - 1, GRID_W, 128), F32)],
        compiler_params=_cparams("arbitrary", "arbitrary"),
        name="na_bias",
    )(rpb.reshape(-1))


def _ctx_kernel(lam_init, n_prev, *refs):
    (x_ref, mod_ref, gmix_ref, win_ref, gq_ref, wqup_ref, gkv_ref, wkv_ref,
     lamv_ref, gsub_ref, dw_ref, cb_ref, lng_ref, lnb_ref) = refs[:14]
    (mixed_ref, dak_ref, dav_ref, ckv_ref, kr_ref, nak_ref, nav_ref,
     proj_ref, gpad_ref) = refs[14 + n_prev:]
    x = x_ref[...]
    sh1 = mod_ref[0:1, 0:D_MODEL]
    sc1 = mod_ref[0:1, D_MODEL:2 * D_MODEL]
    h = (_rms(x) * gmix_ref[...]) * (1.0 + sc1) + sh1
    proj_ref[...] = _dot(h.astype(BF), win_ref[...])

    for hh in range(N_HEADS):
        lo, hi = hh * HEAD_DIM, (hh + 1) * HEAD_DIM
        dak_ref[0, hh] = proj_ref[:, P_DAK + lo:P_DAK + hi]
        dav_ref[0, hh] = proj_ref[:, P_DAV + lo:P_DAV + hi]
        nak_ref[0, hh] = proj_ref[:, P_NAK + lo:P_NAK + hi]
        nav_ref[0, hh] = proj_ref[:, P_NAV + lo:P_NAV + hi]

    lane = _lane_ids(GROUP_W)

    lam = _diff_lambda(lamv_ref, lam_init)
    qa = proj_ref[:, P_DAQ:P_DAQ + GROUP_W] * DA_SCALE
    ka = proj_ref[:, P_DAK:P_DAK + GROUP_W].astype(BF)
    va = proj_ref[:, P_DAV:P_DAV + GROUP_W].astype(BF)
    o_da = jnp.zeros((SEQ, GROUP_W), F32)
    for hh in range(N_HEADS):
        e1, r1 = _softmax_parts(_qk(_mask_heads(qa, lane, DA_QK, 2 * hh), ka))
        e2, r2 = _softmax_parts(_qk(_mask_heads(qa, lane, DA_QK, 2 * hh + 1), ka))
        p = (e1 * r1 - lam * (e2 * r2)).astype(BF)
        o_da = jnp.where(_group_of(lane, HEAD_DIM) == hh, _dot(p, va), o_da)
    o_da = _head_rms(o_da) * gsub_ref[...] * (1.0 - lam_init)

    qd = _rms(proj_ref[:, P_QD:P_QD + GROUP_W]) * gq_ref[...]
    qm = _dot(qd.astype(BF), wqup_ref[...]) * MLA_SCALE
    ckv = _rms(proj_ref[:, P_KVD:P_KVD + MLA_KV_LORA]) * gkv_ref[...]
    ckv_ref[0] = ckv
    kr_pad = proj_ref[:, P_KR:P_KR + 128]
    kr_ref[0] = kr_pad[:, 0:MLA_ROPE]
    kvm = _dot(ckv.astype(BF), wkv_ref[...])
    kn = kvm[:, :GROUP_W].astype(BF)
    vm = kvm[:, GROUP_W:].astype(BF)
    krb = kr_pad.astype(BF)
    kslab = (jnp.concatenate([kn[:, :128], krb], axis=1), jnp.concatenate([kn[:, 128:], krb], axis=1))
    o_mla = jnp.zeros((SEQ, GROUP_W), F32)
    for hh in range(N_HEADS):
        qh = qm[:, hh * GROUP_W:(hh + 1) * GROUP_W].astype(BF)
        e, r = _softmax_parts(_qk(qh, kslab[hh // 2]))
        o_mla = jnp.where(_group_of(lane, HEAD_DIM) == hh, _dot(e.astype(BF), vm) * r, o_mla)

    qn = proj_ref[:, P_NAQ:P_NAQ + GROUP_W] * NA_SCALE
    kn2 = proj_ref[:, P_NAK:P_NAK + GROUP_W].astype(BF)
    vn2 = proj_ref[:, P_NAV:P_NAV + GROUP_W].astype(BF)
    o_na = jnp.zeros((SEQ, GROUP_W), F32)
    for hh in range(N_HEADS):
        e, r = _softmax_parts(_qk(_mask_heads(qn, lane, HEAD_DIM, hh), kn2))
        o_na = jnp.where(_group_of(lane, HEAD_DIM) == hh, _dot(e.astype(BF), vn2) * r, o_na)

    g = proj_ref[:, P_CONV:P_CONV + GROUP_W] * jax.nn.sigmoid(proj_ref[:, P_CONV + GROUP_W:P_TOT])
    gpad_ref[0:16] = jnp.zeros((16, GROUP_W), F32)
    gpad_ref[16 + SEQ:32 + SEQ] = jnp.zeros((16, GROUP_W), F32)
    gpad_ref[16:16 + SEQ] = g
    o_conv = _conv_ln_silu(gpad_ref, SEQ, dw_ref, cb_ref, lng_ref, lnb_ref)

    mixed_ref[...] = jnp.concatenate([o_da, o_mla, o_na, o_conv], axis=1).astype(BF)


def _ctx_layer(layer, x, prm, prev):
    lam_init = 0.8 - 0.6 * math.exp(-0.3 * layer)
    head_blk = pl.BlockSpec((1, None, N_HEADS, SEQ, HEAD_DIM), lambda b: (b, layer, 0, 0, 0))
    head_shp = jax.ShapeDtypeStruct((BATCH, DEPTH, N_HEADS, SEQ, HEAD_DIM), F32)
    in_specs = [pl.BlockSpec((SEQ, D_MODEL), lambda b: (b, 0)),
                _layer_spec(layer, (16, N_MOD * D_MODEL)),
                _layer_spec(layer, (1, D_MODEL)),
                _layer_spec(layer, (D_MODEL, P_TOT)),
                _layer_spec(layer, (1, GROUP_W)),
                _layer_spec(layer, (GROUP_W, N_HEADS * GROUP_W)),
                _layer_spec(layer, (1, MLA_KV_LORA)),
                _layer_spec(layer, (MLA_KV_LORA, 2 * GROUP_W)),
                _layer_spec(layer, (4, DA_QK)),
                _layer_spec(layer, (1, GROUP_W)),
                _layer_spec(layer, (32, GROUP_W)),
                _layer_spec(layer, (1, GROUP_W)),
                _layer_spec(layer, (1, GROUP_W)),
                _layer_spec(layer, (1, GROUP_W))] + [pl.BlockSpec(memory_space=pl.ANY)] * len(prev)
    out_specs = [pl.BlockSpec((SEQ, D_MODEL), lambda b: (b, 0)),
                 head_blk, head_blk,
                 pl.BlockSpec((1, None, SEQ, MLA_KV_LORA), lambda b: (b, layer, 0, 0)),
                 pl.BlockSpec((1, None, SEQ, MLA_ROPE), lambda b: (b, layer, 0, 0)),
                 head_blk, head_blk]
    out_shape = [jax.ShapeDtypeStruct((BATCH * SEQ, D_MODEL), BF),
                 head_shp, head_shp,
                 jax.ShapeDtypeStruct((BATCH, DEPTH, SEQ, MLA_KV_LORA), F32),
                 jax.ShapeDtypeStruct((BATCH, DEPTH, SEQ, MLA_ROPE), F32),
                 head_shp, head_shp]
    n_in = len(in_specs) - len(prev)
    return pl.pallas_call(
        functools.partial(_ctx_kernel, lam_init, len(prev)),
        grid=(BATCH,),
        in_specs=in_specs,
        out_specs=out_specs,
        out_shape=out_shape,
        input_output_aliases={n_in + k: 1 + k for k in range(len(prev))},
        scratch_shapes=[pltpu.VMEM((SEQ, P_TOT), F32), pltpu.VMEM((SEQ + 32, GROUP_W), F32)],
        compiler_params=_cparams("arbitrary"),
        name=f"ctx_layer{layer}",
    )(x, prm["mod"], prm["g_mix"], prm["w_in"], prm["g_q"], prm["w_qup"], prm["g_kv"], prm["w_kvup"],
      prm["lamv"], prm["g_sub"], prm["dw"], prm["cb"], prm["ln_g"], prm["ln_b"], *prev)


def _lat_proj_kernel(x_ref, mod_ref, gmix_ref, win_ref, gq_ref, wqup_ref, gkv_ref, wkv_ref,
                     cos_ref, sin_ref, cdak_ref, cdav_ref, cka_ref, ckb_ref, cmv_ref, cnak_ref, cnav_ref,
                     daq_ref, mq_ref, naq_ref, g_ref,
                     dak_ref, dav_ref, ka_ref, kb_ref, mv_ref, nak_ref, nav_ref):
    b = pl.program_id(0)
    j = pl.program_id(1)

    @pl.when(j < N_QT)
    def _():
        x = x_ref[...]
        sh1 = mod_ref[pl.ds(1 + b, 1), 0:D_MODEL]
        sc1 = mod_ref[pl.ds(1 + b, 1), D_MODEL:2 * D_MODEL]
        h = (_rms(x) * gmix_ref[...]) * (1.0 + sc1) + sh1
        proj = _dot(h.astype(BF), win_ref[...])
        cos = cos_ref[...]
        sin = sin_ref[...]

        def rope2(z):
            return jnp.concatenate([_rope(z[:, :128], cos[:, :128], sin[:, :128]),
                                    _rope(z[:, 128:], cos[:, 128:], sin[:, 128:])], axis=1)

        daq_ref[0] = (rope2(proj[:, P_DAQ:P_DAQ + GROUP_W]) * (DA_SCALE * LOG2E)).astype(BF)
        dak_ref[0] = rope2(proj[:, P_DAK:P_DAK + GROUP_W]).astype(BF)
        dav_ref[0] = proj[:, P_DAV:P_DAV + GROUP_W].astype(BF)

        qd = _rms(proj[:, P_QD:P_QD + GROUP_W]) * gq_ref[...]
        qm = _dot(qd.astype(BF), wqup_ref[...])
        for hh in range(N_HEADS):
            nope = qm[:, hh * GROUP_W:hh * GROUP_W + 128]
            rope = _rope(qm[:, hh * GROUP_W + 128:(hh + 1) * GROUP_W], cos[:, :128], sin[:, :128])
            mq_ref[0, hh] = (jnp.concatenate([nope, rope], axis=1) * (MLA_SCALE * LOG2E)).astype(BF)
        ckv = _rms(proj[:, P_KVD:P_KVD + MLA_KV_LORA]) * gkv_ref[...]
        kvm = _dot(ckv.astype(BF), wkv_ref[...])
        kn = kvm[:, :GROUP_W].astype(BF)
        krb = _rope(proj[:, P_KR:P_KR + 128], cos[:, :128], sin[:, :128]).astype(BF)
        ka_ref[0] = jnp.concatenate([kn[:, :128], krb], axis=1)
        kb_ref[0] = jnp.concatenate([kn[:, 128:], krb], axis=1)
        mv_ref[0] = kvm[:, GROUP_W:].astype(BF)

        naq_ref[0] = (proj[:, P_NAQ:P_NAQ + GROUP_W] * NA_SCALE).astype(BF)
        nak_ref[0] = proj[:, P_NAK:P_NAK + GROUP_W].astype(BF)
        nav_ref[0] = proj[:, P_NAV:P_NAV + GROUP_W].astype(BF)
        g_ref[0] = proj[:, P_CONV:P_CONV + GROUP_W] * jax.nn.sigmoid(proj[:, P_CONV + GROUP_W:P_TOT])

    @pl.when(j == N_QT)
    def _():
        dak_ref[0] = cdak_ref[0]
        dav_ref[0] = cdav_ref[0]
        ka_ref[0] = cka_ref[0]
        kb_ref[0] = ckb_ref[0]
        mv_ref[0] = cmv_ref[0]
        nak_ref[0] = cnak_ref[0]
        nav_ref[0] = cnav_ref[0]


def _lat_proj(layer, x, prm, cos_t, sin_t, caches):
    jq = lambda j: jnp.minimum(j, N_QT - 1)
    cache_spec = pl.BlockSpec((1, None, PAST_LEN, GROUP_W), lambda b, j: (b, layer, 0, 0))
    q_spec = pl.BlockSpec((1, TQ, GROUP_W), lambda b, j: (b, jq(j), 0))
    k_spec = pl.BlockSpec((1, TQ, GROUP_W), lambda b, j: (b, j, 0))
    in_specs = [pl.BlockSpec((TQ, D_MODEL), lambda b, j: (b * N_QT + jq(j), 0)),
                _layer_spec(layer, (16, N_MOD * D_MODEL)),
                _layer_spec(layer, (1, D_MODEL)),
                _layer_spec(layer, (D_MODEL, P_TOT)),
                _layer_spec(layer, (1, GROUP_W)),
                _layer_spec(layer, (GROUP_W, N_HEADS * GROUP_W)),
                _layer_spec(layer, (1, MLA_KV_LORA)),
                _layer_spec(layer, (MLA_KV_LORA, 2 * GROUP_W)),
                pl.BlockSpec((TQ, GROUP_W), lambda b, j: (jq(j), 0)),
                pl.BlockSpec((TQ, GROUP_W), lambda b, j: (jq(j), 0))] + [cache_spec] * 7
    q_shp = jax.ShapeDtypeStruct((DEC_BATCH, DEC_SEQ, GROUP_W), BF)
    k_shp = jax.ShapeDtypeStruct((DEC_BATCH, KEYS, GROUP_W), BF)
    out_specs = [q_spec,
                 pl.BlockSpec((1, N_HEADS, TQ, GROUP_W), lambda b, j: (b, 0, jq(j), 0)),
                 q_spec, q_spec] + [k_spec] * 7
    out_shape = [q_shp,
                 jax.ShapeDtypeStruct((DEC_BATCH, N_HEADS, DEC_SEQ, GROUP_W), BF),
                 q_shp,
                 jax.ShapeDtypeStruct((DEC_BATCH, DEC_SEQ, GROUP_W), F32)] + [k_shp] * 7
    return pl.pallas_call(
        _lat_proj_kernel,
        grid=(DEC_BATCH, N_QT + 1),
        in_specs=in_specs,
        out_specs=out_specs,
        out_shape=out_shape,
        compiler_params=_cparams("arbitrary", "arbitrary"),
        name=f"lat_proj{layer}",
    )(x, prm["mod"], prm["g_mix"], prm["w_in"], prm["g_q"], prm["w_qup"], prm["g_kv"], prm["w_kvup"],
      cos_t, sin_t, *caches)


def _lat_attn_kernel(lam_init, daq_ref, mq_ref, naq_ref, g_ref,
                     dak_ref, dav_ref, ka_ref, kb_ref, mv_ref, nak_ref, nav_ref,
                     nab_ref, lamv_ref, gsub_ref, dw_ref, cb_ref, lng_ref, lnb_ref,
                     mixed_ref, gpad_ref):
    t = pl.program_id(1)
    lane = _lane_ids(GROUP_W)

    lam = _diff_lambda(lamv_ref, lam_init)
    qa = daq_ref[0].astype(F32)
    ka = dak_ref[0]
    va = dav_ref[0]
    o_da = jnp.zeros((TQ, GROUP_W), F32)
    for hh in range(N_HEADS):
        e1, r1 = _softmax2_parts(_qk(_mask_heads(qa, lane, DA_QK, 2 * hh), ka))
        e2, r2 = _softmax2_parts(_qk(_mask_heads(qa, lane, DA_QK, 2 * hh + 1), ka))
        p = (e1 * r1 - e2 * (lam * r2)).astype(BF)
        o_da = jnp.where(_group_of(lane, HEAD_DIM) == hh, _dot(p, va), o_da)
    o_da = _head_rms(o_da) * gsub_ref[...] * (1.0 - lam_init)

    vm = mv_ref[0]
    o_mla = jnp.zeros((TQ, GROUP_W), F32)
    for hh in range(N_HEADS):
        kslab = (ka_ref, kb_ref)[hh // 2][0]
        e, r = _softmax2_parts(_qk(mq_ref[0, hh], kslab))
        o_mla = jnp.where(_group_of(lane, HEAD_DIM) == hh, _dot(e.astype(BF), vm) * r, o_mla)

    kc = nak_ref[0, DEC_SEQ:KEYS, :]
    vc = nav_ref[0, DEC_SEQ:KEYS, :]
    rows = []
    for i in range(ROWS_PER_TILE):
        r = t * ROWS_PER_TILE + i
        start = jnp.clip(r - NA_KR // 2, 0, N_ROWS - NA_KR)
        dr0 = start - r + NA_KR - 1
        koff = pl.multiple_of(start * GRID_W, GRID_W)
        kl = nak_ref[0, pl.ds(koff, NA_LOCAL), :]
        vl = nav_ref[0, pl.ds(koff, NA_LOCAL), :]
        qrow = naq_ref[0, i * GRID_W:(i + 1) * GRID_W, :].astype(F32)
        q4 = jnp.concatenate([_mask_heads(qrow, lane, HEAD_DIM, hh) for hh in range(N_HEADS)], axis=0)
        s_loc = _qk(q4, kl) + nab_ref[dr0]
        s_ctx = _qk(q4, kc)
        m = jnp.maximum(jnp.max(s_loc, axis=-1, keepdims=True), jnp.max(s_ctx, axis=-1, keepdims=True))
        e_loc = jnp.exp(s_loc - m)
        e_ctx = jnp.exp(s_ctx - m)
        den = jnp.sum(e_loc, axis=-1, keepdims=True) + jnp.sum(e_ctx, axis=-1, keepdims=True)
        o4 = (_dot(e_loc.astype(BF), vl) + _dot(e_ctx.astype(BF), vc)) * (1.0 / den)
        o_row = o4[0:GRID_W]
        for hh in range(1, N_HEADS):
            o_row = jnp.where(_group_of(lane, HEAD_DIM) == hh, o4[hh * GRID_W:(hh + 1) * GRID_W], o_row)
        rows.append(o_row)
    o_na = jnp.concatenate(rows, axis=0)

    base = pl.multiple_of(t * TQ, TQ)
    gpad_ref[16:16 + TQ] = g_ref[0, pl.ds(base, TQ), :]
    lo = g_ref[0, pl.ds(pl.multiple_of(jnp.maximum(base - 16, 0), 16), 16), :]
    hi = g_ref[0, pl.ds(pl.multiple_of(jnp.minimum(base + TQ, DEC_SEQ - 16), 16), 16), :]
    gpad_ref[0:16] = jnp.where(t > 0, lo, 0.0)
    gpad_ref[16 + TQ:32 + TQ] = jnp.where(t < N_QT - 1, hi, 0.0)
    o_conv = _conv_ln_silu(gpad_ref, TQ, dw_ref, cb_ref, lng_ref, lnb_ref)

    mixed_ref[...] = jnp.concatenate([o_da, o_mla, o_na, o_conv], axis=1).astype(BF)


def _lat_attn(layer, proj_outs, nab, prm):
    lam_init = 0.8 - 0.6 * math.exp(-0.3 * layer)
    q_spec = pl.BlockSpec((1, TQ, GROUP_W), lambda b, t: (b, t, 0))
    full_k = pl.BlockSpec((1, KEYS, GROUP_W), lambda b, t: (b, 0, 0))
    in_specs = [q_spec,
                pl.BlockSpec((1, N_HEADS, TQ, GROUP_W), lambda b, t: (b, 0, t, 0)),
                q_spec,
                pl.BlockSpec((1, DEC_SEQ, GROUP_W), lambda b, t: (b, 0, 0))] + [full_k] * 7 + [
                _layer_spec(layer, (NA_KR, N_HEADS * GRID_W, NA_LOCAL)),
                _layer_spec(layer, (4, DA_QK)),
                _layer_spec(layer, (1, GROUP_W)),
                _layer_spec(layer, (32, GROUP_W)),
                _layer_spec(layer, (1, GROUP_W)),
                _layer_spec(layer, (1, GROUP_W)),
                _layer_spec(layer, (1, GROUP_W))]
    return pl.pallas_call(
        functools.partial(_lat_attn_kernel, lam_init),
        grid=(DEC_BATCH, N_QT),
        in_specs=in_specs,
        out_specs=pl.BlockSpec((TQ, D_MODEL), lambda b, t: (b * N_QT + t, 0)),
        out_shape=jax.ShapeDtypeStruct((DEC_BATCH * DEC_SEQ, D_MODEL), BF),
        scratch_shapes=[pltpu.VMEM((TQ + 32, GROUP_W), F32)],
        compiler_params=_cparams("arbitrary", "arbitrary"),
        name=f"lat_attn{layer}",
    )(*proj_outs, nab, prm["lamv"], prm["g_sub"], prm["dw"], prm["cb"], prm["ln_g"], prm["ln_b"])


def _out_ffn_kernel(final, tiles_per_mod, mod_base, x_ref, mx_ref, mod_ref, wout_ref, gff_ref,
                    w1_ref, w2_ref, gfin_ref, o_ref):
    row = mod_base + pl.program_id(0) // tiles_per_mod
    g1 = mod_ref[pl.ds(row, 1), 2 * D_MODEL:3 * D_MODEL]
    sh2 = mod_ref[pl.ds(row, 1), 3 * D_MODEL:4 * D_MODEL]
    sc2 = mod_ref[pl.ds(row, 1), 4 * D_MODEL:5 * D_MODEL]
    g2 = mod_ref[pl.ds(row, 1), 5 * D_MODEL:6 * D_MODEL]
    x1 = x_ref[...] + g1 * _dot(mx_ref[...], wout_ref[...])
    h2 = ((_rms(x1) * gff_ref[...]) * (1.0 + sc2) + sh2).astype(BF)
    acc = jnp.zeros((TM_FFN, D_MODEL), F32)
    for c in range(D_FF // D_MODEL):
        a = jnp.maximum(_dot(h2, w1_ref[:, c * D_MODEL:(c + 1) * D_MODEL]), 0.0)
        acc = acc + _dot((a * a).astype(BF), w2_ref[c * D_MODEL:(c + 1) * D_MODEL, :])
    x2 = x1 + g2 * acc
    o_ref[...] = _rms(x2) * gfin_ref[...] if final else x2


def _out_ffn(name, layer, tiles_per_mod, mod_base, x, mixed, prm, g_final):
    n = x.shape[0]
    tile = pl.BlockSpec((TM_FFN, D_MODEL), lambda i: (i, 0))
    return pl.pallas_call(
        functools.partial(_out_ffn_kernel, layer == DEPTH - 1, tiles_per_mod, mod_base),
        grid=(n // TM_FFN,),
        in_specs=[tile, tile,
                  _layer_spec(layer, (16, N_MOD * D_MODEL)),
                  _layer_spec(layer, (D_MODEL, D_MODEL)),
                  _layer_spec(layer, (1, D_MODEL)),
                  _layer_spec(layer, (D_MODEL, D_FF)),
                  _layer_spec(layer, (D_FF, D_MODEL)),
                  _const_spec((1, D_MODEL))],
        out_specs=tile,
        out_shape=jax.ShapeDtypeStruct((n, D_MODEL), F32),
        compiler_params=_cparams("arbitrary"),
        name=name,
    )(x, mixed, prm["mod"], prm["w_out"], prm["g_ff"], prm["w_ff1"], prm["w_ff2"], g_final)


def _rope_tables():
    t = np.arange(DEC_SEQ)
    rows = (t // GRID_W).astype(np.float32)
    cols = (t % GRID_W).astype(np.float32)
    c = np.arange(GROUP_W) % 32
    freqs = np.float32(ROPE_BASE) ** (-np.arange(8, dtype=np.float32) * np.float32(2.0) / np.float32(16))
    pos = np.where((c < 16)[None, :], rows[:, None], cols[:, None]).astype(np.float32)
    ang = (pos * freqs[(c % 16) % 8][None, :]).astype(np.float32)
    first = ((c % 16) < 8)[None, :]
    cos = np.cos(ang).astype(np.float32)
    sin = np.sin(ang).astype(np.float32)
    return jnp.asarray(cos), jnp.asarray(np.where(first, -sin, sin))


def _qup_gather_index():
    idx = np.full((N_HEADS * GROUP_W,), -1, np.int64)
    for h in range(N_HEADS):
        src = h * (MLA_NOPE + MLA_ROPE)
        dst = h * GROUP_W + (h % 2) * MLA_NOPE
        idx[dst:dst + MLA_NOPE] = np.arange(src, src + MLA_NOPE)
        idx[h * GROUP_W + 128:h * GROUP_W + 128 + MLA_ROPE] = np.arange(src + MLA_NOPE, src + MLA_NOPE + MLA_ROPE)
    return idx


def _kvup_perm():
    k = [h * 128 + d for h in range(N_HEADS) for d in range(MLA_NOPE)]
    v = [h * 128 + MLA_NOPE + d for h in range(N_HEADS) for d in range(HEAD_DIM)]
    return np.asarray(k + v)


def _heads_to_lanes(c):
    b, l, h, s, d = c.shape
    return c.transpose(0, 1, 3, 2, 4).reshape(b, l, s, h * d).astype(BF)


def kernel(x_prompt, x_sample, c, cache_da_k, cache_da_v, cache_mla_ckv, cache_mla_krope, cache_na_k, cache_na_v, c_ctx, w_mod, b_mod, g_norm_mix, g_norm_ff, w_in, da_lambda_q1, da_lambda_k1, da_lambda_q2, da_lambda_k2, g_da_subln, g_mla_q, w_mla_qup, g_mla_kv, w_mla_kvup, na_rpb, conv_dw, conv_b, conv_ln_g, conv_ln_b, w_out, w_ff1, w_ff2, g_final):
    w_in_p = jnp.concatenate([w_in[..., :KR_ORIG_END], jnp.zeros((DEPTH, D_MODEL, KR_PAD), F32),
                              w_in[..., KR_ORIG_END:]], axis=-1).astype(BF)
    qidx = _qup_gather_index()
    w_qup_e = jnp.where(jnp.asarray(qidx >= 0)[None, None, :],
                        jnp.take(w_mla_qup, jnp.asarray(np.maximum(qidx, 0)), axis=-1), 0.0).astype(BF)
    w_kvup_p = jnp.take(w_mla_kvup, jnp.asarray(_kvup_perm()), axis=-1).astype(BF)
    r3 = lambda a: a.reshape(DEPTH, 1, -1)
    prm = dict(
        g_mix=r3(g_norm_mix), w_in=w_in_p, g_q=r3(g_mla_q), w_qup=w_qup_e, g_kv=r3(g_mla_kv), w_kvup=w_kvup_p,
        lamv=jnp.stack([da_lambda_q1, da_lambda_k1, da_lambda_q2, da_lambda_k2], axis=1),
        g_sub=r3(jnp.tile(g_da_subln, (1, N_HEADS))),
        dw=jnp.concatenate([conv_dw, jnp.zeros((DEPTH, 1, GROUP_W), F32)], axis=1),
        cb=r3(conv_b), ln_g=r3(conv_ln_g), ln_b=r3(conv_ln_b),
        w_out=w_out.astype(BF), g_ff=r3(g_norm_ff), w_ff1=w_ff1.astype(BF), w_ff2=w_ff2.astype(BF))
    cv = jnp.concatenate([c_ctx[None, :], c, jnp.zeros((16 - 1 - DEC_BATCH, D_MODEL), F32)], axis=0)
    cos_t, sin_t = _rope_tables()
    g_final2 = g_final.reshape(1, D_MODEL)

    prm["mod"] = _modulation(cv, w_mod, b_mod)
    c_kr_pad = jnp.pad(cache_mla_krope, ((0, 0), (0, 0), (0, 0), (0, KR_PAD)))
    cka, ckb, cmv = _mla_cache(cache_mla_ckv, c_kr_pad, w_kvup_p)
    caches = (_heads_to_lanes(cache_da_k), _heads_to_lanes(cache_da_v), cka, ckb, cmv,
              _heads_to_lanes(cache_na_k), _heads_to_lanes(cache_na_v))
    nab = _na_bias(na_rpb)

    xp = x_prompt.reshape(BATCH * SEQ, D_MODEL)
    xs = x_sample.reshape(DEC_BATCH * DEC_SEQ, D_MODEL)
    new_ctx = ()
    for l in range(DEPTH):
        mixed_p, *new_ctx = _ctx_layer(l, xp, prm, tuple(new_ctx))
        xp = _out_ffn(f"ctx_ffn{l}", l, BATCH * SEQ // TM_FFN, 0, xp, mixed_p, prm, g_final2)
        proj_outs = _lat_proj(l, xs, prm, cos_t, sin_t, caches)
        mixed_s = _lat_attn(l, proj_outs, nab, prm)
        xs = _out_ffn(f"lat_ffn{l}", l, DEC_SEQ // TM_FFN, 1, xs, mixed_s, prm, g_final2)

    y_prompt = xp.reshape(BATCH, SEQ, D_MODEL)
    y_sample = xs.reshape(DEC_BATCH, DEC_SEQ, D_MODEL)
    return (y_prompt, y_sample) + tuple(new_ctx)
```

```python
import functools
import math

import numpy as np
import jax
import jax.numpy as jnp
from jax import lax
from jax.experimental import pallas as pl
from jax.experimental.pallas import tpu as pltpu

F32 = jnp.float32
BF = jnp.bfloat16

D_MODEL = 1024
BATCH = 16
SEQ = 256
DEPTH = 2
DEC_BATCH = 2
DEC_SEQ = 2048
PAST_LEN = 256
GRID_W = 64
GROUP_W = 256
HEAD_DIM = 64
N_HEADS = 4
DA_QK = 32
MLA_NOPE = 64
MLA_ROPE = 32
MLA_KV_LORA = 128
NA_KR = 8
NA_KC = 16
CONV_W = 31
D_FF = 4096
ROPE_BASE = 10000.0
EPS = 1e-6
N_MOD = 6
IN_COLS = 2464

P_DAQ, P_DAK, P_DAV, P_QD, P_KVD, P_KR = 0, 256, 512, 768, 1024, 1152
P_NAQ, P_NAK, P_NAV, P_CONV, P_TOT = 1280, 1536, 1792, 2048, 2560
KR_ORIG_END = 1184
KR_PAD = 128 - MLA_ROPE

DA_SCALE = DA_QK ** -0.5
MLA_SCALE = (MLA_NOPE + MLA_ROPE) ** -0.5
NA_SCALE = HEAD_DIM ** -0.5
LOG2E = math.log2(math.e)

TQ = 256
N_QT = DEC_SEQ // TQ
KEYS = DEC_SEQ + PAST_LEN
ROWS_PER_TILE = TQ // GRID_W
N_ROWS = DEC_SEQ // GRID_W
NA_LOCAL = NA_KR * GRID_W
TM_FFN = 512
VMEM_LIMIT = 58 * 1024 * 1024

NT_DIMS = (((1,), (1,)), ((), ()))


def _cparams(*sem):
    return pltpu.CompilerParams(dimension_semantics=sem, vmem_limit_bytes=VMEM_LIMIT)


def _const_spec(shape):
    nd = len(shape)
    return pl.BlockSpec(shape, lambda *_: (0,) * nd, pipeline_mode=pl.Buffered(1))


def _layer_spec(layer, shape):
    nd = len(shape)
    return pl.BlockSpec((None,) + tuple(shape), lambda *_: (layer,) + (0,) * nd, pipeline_mode=pl.Buffered(1))


def _rms(x):
    return x * lax.rsqrt(jnp.mean(x * x, axis=-1, keepdims=True) + EPS)


def _dot(a, b):
    return jnp.dot(a, b, preferred_element_type=F32)


def _qk(q, k):
    return lax.dot_general(q, k, NT_DIMS, preferred_element_type=F32)


def _softmax_parts(s):
    m = jnp.max(s, axis=-1, keepdims=True)
    e = jnp.exp(s - m)
    return e, 1.0 / jnp.sum(e, axis=-1, keepdims=True)


def _softmax2_parts(s):
    m = jnp.max(s, axis=-1, keepdims=True)
    e = jnp.exp2(s - m)
    return e, 1.0 / jnp.sum(e, axis=-1, keepdims=True)


def _lane_ids(width):
    return lax.broadcasted_iota(jnp.int32, (1, width), 1)


def _diff_lambda(lamv_ref, lam_init):
    v = lamv_ref[...]
    a = jnp.exp(jnp.sum(v[0:1] * v[1:2], axis=-1, keepdims=True))
    b = jnp.exp(jnp.sum(v[2:3] * v[3:4], axis=-1, keepdims=True))
    return a - b + lam_init


def _group_of(lane, group):
    return lane >> (group.bit_length() - 1)


def _head_rms(x):
    r = _group_of(lax.broadcasted_iota(jnp.int32, (GROUP_W, GROUP_W), 0), HEAD_DIM)
    c = _group_of(lax.broadcasted_iota(jnp.int32, (GROUP_W, GROUP_W), 1), HEAD_DIM)
    ones_bd = jnp.where(r == c, 1.0, 0.0).astype(BF)
    sq = x * x
    hi = sq.astype(BF)
    lo = (sq - hi.astype(F32)).astype(BF)
    ss = _dot(hi, ones_bd) + _dot(lo, ones_bd)
    return x * lax.rsqrt(ss * (1.0 / HEAD_DIM) + EPS)


def _rope(z, cos, sin):
    lane = _lane_ids(z.shape[1])
    swapped = jnp.where((lane & 15) < 8, pltpu.roll(z, 120, 1), pltpu.roll(z, 8, 1))
    return z * cos + swapped * sin


def _mask_heads(qf, lane, group, idx):
    return jnp.where(_group_of(lane, group) == idx, qf, 0.0).astype(BF)


def _conv_ln_silu(gpad_ref, zsh_ref, n, dw_ref, cb_ref, lng_ref, lnb_ref):
    y = jnp.zeros((n, GROUP_W), F32) + cb_ref[...]
    for b in range(8):
        z = None
        for a in range(4):
            t = 8 * a + b - 1
            if 0 <= t < CONV_W:
                term = gpad_ref[pl.ds(8 * a, n + 8), :] * dw_ref[t:t + 1, :]
                z = term if z is None else z + term
        if b == 0:
            y = y + z[0:n]
        else:
            zsh_ref[b - 1] = z
            y = y + zsh_ref[b - 1, pl.ds(b, n), :]
    mu = jnp.mean(y, axis=-1, keepdims=True)
    yc = y - mu
    var = jnp.mean(yc * yc, axis=-1, keepdims=True)
    z = yc * lax.rsqrt(var + EPS) * lng_ref[...] + lnb_ref[...]
    return z * jax.nn.sigmoid(z)


def _mod_kernel(cv_ref, w_ref, b_ref, o_ref):
    c = cv_ref[...]
    a = (c * jax.nn.sigmoid(c)).astype(BF)
    o_ref[0] = _dot(a, w_ref[0].astype(BF)) + b_ref[0]


def _modulation(cv, w_mod, b_mod):
    nblk = 4
    wblk = N_MOD * D_MODEL // nblk
    return pl.pallas_call(
        _mod_kernel,
        grid=(DEPTH, nblk),
        in_specs=[pl.BlockSpec((16, D_MODEL), lambda l, j: (0, 0)),
                  pl.BlockSpec((1, D_MODEL, wblk), lambda l, j: (l, 0, j)),
                  pl.BlockSpec((1, 1, wblk), lambda l, j: (l, 0, j))],
        out_specs=pl.BlockSpec((1, 16, wblk), lambda l, j: (l, 0, j)),
        out_shape=jax.ShapeDtypeStruct((DEPTH, 16, N_MOD * D_MODEL), F32),
        compiler_params=_cparams("arbitrary", "arbitrary"),
        name="modulation",
    )(cv, w_mod, b_mod.reshape(DEPTH, 1, N_MOD * D_MODEL))


def _mla_cache_kernel(ckv_ref, kr_ref, w_ref, ka_ref, kb_ref, v_ref):
    kvm = _dot(ckv_ref[0, 0].astype(BF), w_ref[0])
    kr = kr_ref[0, 0].astype(BF)
    kn = kvm[:, :GROUP_W].astype(BF)
    ka_ref[0, 0] = jnp.concatenate([kn[:, :128], kr], axis=1)
    kb_ref[0, 0] = jnp.concatenate([kn[:, 128:], kr], axis=1)
    v_ref[0, 0] = kvm[:, GROUP_W:].astype(BF)


def _mla_cache(c_ckv, c_kr_pad, w_kvup_p):
    blk = pl.BlockSpec((1, 1, PAST_LEN, GROUP_W), lambda b, l: (b, l, 0, 0))
    shp = jax.ShapeDtypeStruct((DEC_BATCH, DEPTH, PAST_LEN, GROUP_W), BF)
    return pl.pallas_call(
        _mla_cache_kernel,
        grid=(DEC_BATCH, DEPTH),
        in_specs=[pl.BlockSpec((1, 1, PAST_LEN, MLA_KV_LORA), lambda b, l: (b, l, 0, 0)),
                  pl.BlockSpec((1, 1, PAST_LEN, 128), lambda b, l: (b, l, 0, 0)),
                  pl.BlockSpec((1, MLA_KV_LORA, 2 * GROUP_W), lambda b, l: (l, 0, 0))],
        out_specs=[blk, blk, blk],
        out_shape=[shp, shp, shp],
        compiler_params=_cparams("arbitrary", "arbitrary"),
        name="mla_cache",
    )(c_ckv, c_kr_pad, w_kvup_p)


N_DR = 2 * NA_KR - 1
N_DC = 2 * NA_KC - 1


def _na_bias_kernel(rpb_ref, o_ref, tp_ref):
    base = (pl.program_id(0) * N_HEADS + pl.program_id(1)) * (N_DR * N_DC)
    cq = lax.broadcasted_iota(jnp.int32, (GRID_W, 128), 0)
    lane = lax.broadcasted_iota(jnp.int32, (GRID_W, 128), 1)
    ck = lane & (GRID_W - 1)
    upper = lane >= GRID_W
    dc = jnp.clip(ck - cq, -(NA_KC - 1), NA_KC - 1) + NA_KC - 1
    cstart = jnp.clip(cq - NA_KC // 2, 0, GRID_W - NA_KC)
    ok = (ck >= cstart) & (ck < cstart + NA_KC)

    def pair(dr, carry):
        val = jnp.zeros((GRID_W, 128), F32)
        for d in range(N_DC):
            s0 = rpb_ref[base + dr * N_DC + d]
            s1 = rpb_ref[base + (dr + 1) * N_DC + d]
            val = jnp.where(dc == d, jnp.where(upper, s1, s0), val)
        tp_ref[dr] = jnp.where(ok, val, -jnp.inf)
        return carry

    lax.fori_loop(0, N_DR - 1, pair, 0)
    for dr0 in range(NA_KR):
        o_ref[0, dr0] = jnp.concatenate([tp_ref[dr0 + 2 * m] for m in range(NA_KR // 2)], axis=1)


def _na_bias(rpb):
    return pl.pallas_call(
        _na_bias_kernel,
        grid=(DEPTH, N_HEADS),
        in_specs=[pl.BlockSpec(memory_space=pltpu.SMEM)],
        out_specs=pl.BlockSpec((1, NA_KR, GRID_W, NA_LOCAL), lambda l, h: (l, 0, h, 0)),
        out_shape=jax.ShapeDtypeStruct((DEPTH, NA_KR, N_HEADS * GRID_W, NA_LOCAL), F32),
        scratch_shapes=[pltpu.VMEM((N_DR - 1, GRID_W, 128), F32)],
        compiler_params=_cparams("arbitrary", "arbitrary"),
        name="na_bias",
    )(rpb.reshape(-1))


def _ctx_kernel(lam_init, n_prev, *refs):
    (x_ref, mod_ref, gmix_ref, win_ref, gq_ref, wqup_ref, gkv_ref, wkv_ref,
     lamv_ref, gsub_ref, dw_ref, cb_ref, lng_ref, lnb_ref) = refs[:14]
    (mixed_ref, dak_ref, dav_ref, ckv_ref, kr_ref, nak_ref, nav_ref,
     proj_ref, gpad_ref, zsh_ref) = refs[14 + n_prev:]
    x = x_ref[...]
    sh1 = mod_ref[0:1, 0:D_MODEL]
    sc1 = mod_ref[0:1, D_MODEL:2 * D_MODEL]
    h = (_rms(x) * gmix_ref[...]) * (1.0 + sc1) + sh1
    proj_ref[...] = _dot(h.astype(BF), win_ref[...])

    for hh in range(N_HEADS):
        lo, hi = hh * HEAD_DIM, (hh + 1) * HEAD_DIM
        dak_ref[0, hh] = proj_ref[:, P_DAK + lo:P_DAK + hi]
        dav_ref[0, hh] = proj_ref[:, P_DAV + lo:P_DAV + hi]
        nak_ref[0, hh] = proj_ref[:, P_NAK + lo:P_NAK + hi]
        nav_ref[0, hh] = proj_ref[:, P_NAV + lo:P_NAV + hi]

    lane = _lane_ids(GROUP_W)

    lam = _diff_lambda(lamv_ref, lam_init)
    qa = proj_ref[:, P_DAQ:P_DAQ + GROUP_W] * DA_SCALE
    ka = proj_ref[:, P_DAK:P_DAK + GROUP_W].astype(BF)
    va = proj_ref[:, P_DAV:P_DAV + GROUP_W].astype(BF)
    o_da = jnp.zeros((SEQ, GROUP_W), F32)
    for hh in range(N_HEADS):
        e1, r1 = _softmax_parts(_qk(_mask_heads(qa, lane, DA_QK, 2 * hh), ka))
        e2, r2 = _softmax_parts(_qk(_mask_heads(qa, lane, DA_QK, 2 * hh + 1), ka))
        p = (e1 * r1 - lam * (e2 * r2)).astype(BF)
        o_da = jnp.where(_group_of(lane, HEAD_DIM) == hh, _dot(p, va), o_da)
    o_da = _head_rms(o_da) * gsub_ref[...] * (1.0 - lam_init)

    qd = _rms(proj_ref[:, P_QD:P_QD + GROUP_W]) * gq_ref[...]
    qm = _dot(qd.astype(BF), wqup_ref[...]) * MLA_SCALE
    ckv = _rms(proj_ref[:, P_KVD:P_KVD + MLA_KV_LORA]) * gkv_ref[...]
    ckv_ref[0] = ckv
    kr_pad = proj_ref[:, P_KR:P_KR + 128]
    kr_ref[0] = kr_pad[:, 0:MLA_ROPE]
    kvm = _dot(ckv.astype(BF), wkv_ref[...])
    kn = kvm[:, :GROUP_W].astype(BF)
    vm = kvm[:, GROUP_W:].astype(BF)
    krb = kr_pad.astype(BF)
    kslab = (jnp.concatenate([kn[:, :128], krb], axis=1), jnp.concatenate([kn[:, 128:], krb], axis=1))
    o_mla = jnp.zeros((SEQ, GROUP_W), F32)
    for hh in range(N_HEADS):
        qh = qm[:, hh * GROUP_W:(hh + 1) * GROUP_W].astype(BF)
        e, r = _softmax_parts(_qk(qh, kslab[hh // 2]))
        o_mla = jnp.where(_group_of(lane, HEAD_DIM) == hh, _dot(e.astype(BF), vm) * r, o_mla)

    qn = proj_ref[:, P_NAQ:P_NAQ + GROUP_W] * NA_SCALE
    kn2 = proj_ref[:, P_NAK:P_NAK + GROUP_W].astype(BF)
    vn2 = proj_ref[:, P_NAV:P_NAV + GROUP_W].astype(BF)
    o_na = jnp.zeros((SEQ, GROUP_W), F32)
    for hh in range(N_HEADS):
        e, r = _softmax_parts(_qk(_mask_heads(qn, lane, HEAD_DIM, hh), kn2))
        o_na = jnp.where(_group_of(lane, HEAD_DIM) == hh, _dot(e.astype(BF), vn2) * r, o_na)

    g = proj_ref[:, P_CONV:P_CONV + GROUP_W] * jax.nn.sigmoid(proj_ref[:, P_CONV + GROUP_W:P_TOT])
    gpad_ref[0:16] = jnp.zeros((16, GROUP_W), F32)
    gpad_ref[16 + SEQ:32 + SEQ] = jnp.zeros((16, GROUP_W), F32)
    gpad_ref[16:16 + SEQ] = g
    o_conv = _conv_ln_silu(gpad_ref, zsh_ref, SEQ, dw_ref, cb_ref, lng_ref, lnb_ref)

    mixed_ref[...] = jnp.concatenate([o_da, o_mla, o_na, o_conv], axis=1).astype(BF)


def _ctx_layer(layer, x, prm, prev):
    lam_init = 0.8 - 0.6 * math.exp(-0.3 * layer)
    head_blk = pl.BlockSpec((1, None, N_HEADS, SEQ, HEAD_DIM), lambda b: (b, layer, 0, 0, 0))
    head_shp = jax.ShapeDtypeStruct((BATCH, DEPTH, N_HEADS, SEQ, HEAD_DIM), F32)
    in_specs = [pl.BlockSpec((SEQ, D_MODEL), lambda b: (b, 0)),
                _layer_spec(layer, (16, N_MOD * D_MODEL)),
                _layer_spec(layer, (1, D_MODEL)),
                _layer_spec(layer, (D_MODEL, P_TOT)),
                _layer_spec(layer, (1, GROUP_W)),
                _layer_spec(layer, (GROUP_W, N_HEADS * GROUP_W)),
                _layer_spec(layer, (1, MLA_KV_LORA)),
                _layer_spec(layer, (MLA_KV_LORA, 2 * GROUP_W)),
                _layer_spec(layer, (4, DA_QK)),
                _layer_spec(layer, (1, GROUP_W)),
                _layer_spec(layer, (32, GROUP_W)),
                _layer_spec(layer, (1, GROUP_W)),
                _layer_spec(layer, (1, GROUP_W)),
                _layer_spec(layer, (1, GROUP_W))] + [pl.BlockSpec(memory_space=pl.ANY)] * len(prev)
    out_specs = [pl.BlockSpec((SEQ, D_MODEL), lambda b: (b, 0)),
                 head_blk, head_blk,
                 pl.BlockSpec((1, None, SEQ, MLA_KV_LORA), lambda b: (b, layer, 0, 0)),
                 pl.BlockSpec((1, None, SEQ, MLA_ROPE), lambda b: (b, layer, 0, 0)),
                 head_blk, head_blk]
    out_shape = [jax.ShapeDtypeStruct((BATCH * SEQ, D_MODEL), BF),
                 head_shp, head_shp,
                 jax.ShapeDtypeStruct((BATCH, DEPTH, SEQ, MLA_KV_LORA), F32),
                 jax.ShapeDtypeStruct((BATCH, DEPTH, SEQ, MLA_ROPE), F32),
                 head_shp, head_shp]
    n_in = len(in_specs) - len(prev)
    return pl.pallas_call(
        functools.partial(_ctx_kernel, lam_init, len(prev)),
        grid=(BATCH,),
        in_specs=in_specs,
        out_specs=out_specs,
        out_shape=out_shape,
        input_output_aliases={n_in + k: 1 + k for k in range(len(prev))},
        scratch_shapes=[pltpu.VMEM((SEQ, P_TOT), F32), pltpu.VMEM((SEQ + 32, GROUP_W), F32),
                        pltpu.VMEM((7, SEQ + 8, GROUP_W), F32)],
        compiler_params=_cparams("arbitrary"),
        name=f"ctx_layer{layer}",
    )(x, prm["mod"], prm["g_mix"], prm["w_in"], prm["g_q"], prm["w_qup"], prm["g_kv"], prm["w_kvup"],
      prm["lamv"], prm["g_sub"], prm["dw"], prm["cb"], prm["ln_g"], prm["ln_b"], *prev)


def _lat_proj_kernel(x_ref, mod_ref, gmix_ref, win_ref, gq_ref, wqup_ref, gkv_ref, wkv_ref,
                     cos_ref, sin_ref, cdak_ref, cdav_ref, cka_ref, ckb_ref, cmv_ref, cnak_ref, cnav_ref,
                     daq_ref, mq_ref, naq_ref, g_ref,
                     dak_ref, dav_ref, ka_ref, kb_ref, mv_ref, nak_ref, nav_ref):
    b = pl.program_id(0)
    j = pl.program_id(1)

    @pl.when(j < N_QT)
    def _():
        x = x_ref[...]
        sh1 = mod_ref[pl.ds(1 + b, 1), 0:D_MODEL]
        sc1 = mod_ref[pl.ds(1 + b, 1), D_MODEL:2 * D_MODEL]
        h = (_rms(x) * gmix_ref[...]) * (1.0 + sc1) + sh1
        proj = _dot(h.astype(BF), win_ref[...])
        cos = cos_ref[...]
        sin = sin_ref[...]

        def rope2(z):
            return jnp.concatenate([_rope(z[:, :128], cos[:, :128], sin[:, :128]),
                                    _rope(z[:, 128:], cos[:, 128:], sin[:, 128:])], axis=1)

        daq_ref[0] = (rope2(proj[:, P_DAQ:P_DAQ + GROUP_W]) * (DA_SCALE * LOG2E)).astype(BF)
        dak_ref[0] = rope2(proj[:, P_DAK:P_DAK + GROUP_W]).astype(BF)
        dav_ref[0] = proj[:, P_DAV:P_DAV + GROUP_W].astype(BF)

        qd = _rms(proj[:, P_QD:P_QD + GROUP_W]) * gq_ref[...]
        qm = _dot(qd.astype(BF), wqup_ref[...])
        for hh in range(N_HEADS):
            nope = qm[:, hh * GROUP_W:hh * GROUP_W + 128]
            rope = _rope(qm[:, hh * GROUP_W + 128:(hh + 1) * GROUP_W], cos[:, :128], sin[:, :128])
            mq_ref[0, hh] = (jnp.concatenate([nope, rope], axis=1) * (MLA_SCALE * LOG2E)).astype(BF)
        ckv = _rms(proj[:, P_KVD:P_KVD + MLA_KV_LORA]) * gkv_ref[...]
        kvm = _dot(ckv.astype(BF), wkv_ref[...])
        kn = kvm[:, :GROUP_W].astype(BF)
        krb = _rope(proj[:, P_KR:P_KR + 128], cos[:, :128], sin[:, :128]).astype(BF)
        ka_ref[0] = jnp.concatenate([kn[:, :128], krb], axis=1)
        kb_ref[0] = jnp.concatenate([kn[:, 128:], krb], axis=1)
        mv_ref[0] = kvm[:, GROUP_W:].astype(BF)

        naq_ref[0] = (proj[:, P_NAQ:P_NAQ + GROUP_W] * NA_SCALE).astype(BF)
        nak_ref[0] = proj[:, P_NAK:P_NAK + GROUP_W].astype(BF)
        nav_ref[0] = proj[:, P_NAV:P_NAV + GROUP_W].astype(BF)
        g_ref[0] = proj[:, P_CONV:P_CONV + GROUP_W] * jax.nn.sigmoid(proj[:, P_CONV + GROUP_W:P_TOT])

    @pl.when(j == N_QT)
    def _():
        dak_ref[0] = cdak_ref[0]
        dav_ref[0] = cdav_ref[0]
        ka_ref[0] = cka_ref[0]
        kb_ref[0] = ckb_ref[0]
        mv_ref[0] = cmv_ref[0]
        nak_ref[0] = cnak_ref[0]
        nav_ref[0] = cnav_ref[0]


def _lat_proj(layer, x, prm, cos_t, sin_t, caches):
    jq = lambda j: jnp.minimum(j, N_QT - 1)
    cache_spec = pl.BlockSpec((1, None, PAST_LEN, GROUP_W), lambda b, j: (b, layer, 0, 0))
    q_spec = pl.BlockSpec((1, TQ, GROUP_W), lambda b, j: (b, jq(j), 0))
    k_spec = pl.BlockSpec((1, TQ, GROUP_W), lambda b, j: (b, j, 0))
    in_specs = [pl.BlockSpec((TQ, D_MODEL), lambda b, j: (b * N_QT + jq(j), 0)),
                _layer_spec(layer, (16, N_MOD * D_MODEL)),
                _layer_spec(layer, (1, D_MODEL)),
                _layer_spec(layer, (D_MODEL, P_TOT)),
                _layer_spec(layer, (1, GROUP_W)),
                _layer_spec(layer, (GROUP_W, N_HEADS * GROUP_W)),
                _layer_spec(layer, (1, MLA_KV_LORA)),
                _layer_spec(layer, (MLA_KV_LORA, 2 * GROUP_W)),
                pl.BlockSpec((TQ, GROUP_W), lambda b, j: (jq(j), 0)),
                pl.BlockSpec((TQ, GROUP_W), lambda b, j: (jq(j), 0))] + [cache_spec] * 7
    q_shp = jax.ShapeDtypeStruct((DEC_BATCH, DEC_SEQ, GROUP_W), BF)
    k_shp = jax.ShapeDtypeStruct((DEC_BATCH, KEYS, GROUP_W), BF)
    out_specs = [q_spec,
                 pl.BlockSpec((1, N_HEADS, TQ, GROUP_W), lambda b, j: (b, 0, jq(j), 0)),
                 q_spec, q_spec] + [k_spec] * 7
    out_shape = [q_shp,
                 jax.ShapeDtypeStruct((DEC_BATCH, N_HEADS, DEC_SEQ, GROUP_W), BF),
                 q_shp,
                 jax.ShapeDtypeStruct((DEC_BATCH, DEC_SEQ, GROUP_W), F32)] + [k_shp] * 7
    return pl.pallas_call(
        _lat_proj_kernel,
        grid=(DEC_BATCH, N_QT + 1),
        in_specs=in_specs,
        out_specs=out_specs,
        out_shape=out_shape,
        compiler_params=_cparams("arbitrary", "arbitrary"),
        name=f"lat_proj{layer}",
    )(x, prm["mod"], prm["g_mix"], prm["w_in"], prm["g_q"], prm["w_qup"], prm["g_kv"], prm["w_kvup"],
      cos_t, sin_t, *caches)


def _lat_attn_kernel(lam_init, daq_ref, mq_ref, naq_ref, g_ref,
                     dak_ref, dav_ref, ka_ref, kb_ref, mv_ref, nak_ref, nav_ref,
                     nab_ref, lamv_ref, gsub_ref, dw_ref, cb_ref, lng_ref, lnb_ref,
                     mixed_ref, gpad_ref, zsh_ref):
    t = pl.program_id(1)
    lane = _lane_ids(GROUP_W)

    lam = _diff_lambda(lamv_ref, lam_init)
    qa = daq_ref[0].astype(F32)
    ka = dak_ref[0]
    va = dav_ref[0]
    o_da = jnp.zeros((TQ, GROUP_W), F32)
    for hh in range(N_HEADS):
        e1, r1 = _softmax2_parts(_qk(_mask_heads(qa, lane, DA_QK, 2 * hh), ka))
        e2, r2 = _softmax2_parts(_qk(_mask_heads(qa, lane, DA_QK, 2 * hh + 1), ka))
        p = (e1 * r1 - e2 * (lam * r2)).astype(BF)
        o_da = jnp.where(_group_of(lane, HEAD_DIM) == hh, _dot(p, va), o_da)
    o_da = _head_rms(o_da) * gsub_ref[...] * (1.0 - lam_init)

    vm = mv_ref[0]
    o_mla = jnp.zeros((TQ, GROUP_W), F32)
    for hh in range(N_HEADS):
        kslab = (ka_ref, kb_ref)[hh // 2][0]
        e, r = _softmax2_parts(_qk(mq_ref[0, hh], kslab))
        o_mla = jnp.where(_group_of(lane, HEAD_DIM) == hh, _dot(e.astype(BF), vm) * r, o_mla)

    kc = nak_ref[0, DEC_SEQ:KEYS, :]
    vc = nav_ref[0, DEC_SEQ:KEYS, :]
    rows = []
    for i in range(ROWS_PER_TILE):
        r = t * ROWS_PER_TILE + i
        start = jnp.clip(r - NA_KR // 2, 0, N_ROWS - NA_KR)
        dr0 = start - r + NA_KR - 1
        koff = pl.multiple_of(start * GRID_W, GRID_W)
        kl = nak_ref[0, pl.ds(koff, NA_LOCAL), :]
        vl = nav_ref[0, pl.ds(koff, NA_LOCAL), :]
        qrow = naq_ref[0, i * GRID_W:(i + 1) * GRID_W, :].astype(F32)
        q4 = jnp.concatenate([_mask_heads(qrow, lane, HEAD_DIM, hh) for hh in range(N_HEADS)], axis=0)
        s_loc = _qk(q4, kl) + nab_ref[dr0]
        s_ctx = _qk(q4, kc)
        m = jnp.maximum(jnp.max(s_loc, axis=-1, keepdims=True), jnp.max(s_ctx, axis=-1, keepdims=True))
        e_loc = jnp.exp(s_loc - m)
        e_ctx = jnp.exp(s_ctx - m)
        den = jnp.sum(e_loc, axis=-1, keepdims=True) + jnp.sum(e_ctx, axis=-1, keepdims=True)
        o4 = (_dot(e_loc.astype(BF), vl) + _dot(e_ctx.astype(BF), vc)) * (1.0 / den)
        o_row = o4[0:GRID_W]
        for hh in range(1, N_HEADS):
            o_row = jnp.where(_group_of(lane, HEAD_DIM) == hh, o4[hh * GRID_W:(hh + 1) * GRID_W], o_row)
        rows.append(o_row)
    o_na = jnp.concatenate(rows, axis=0)

    base = pl.multiple_of(t * TQ, TQ)
    gpad_ref[16:16 + TQ] = g_ref[0, pl.ds(base, TQ), :]
    lo = g_ref[0, pl.ds(pl.multiple_of(jnp.maximum(base - 16, 0), 16), 16), :]
    hi = g_ref[0, pl.ds(pl.multiple_of(jnp.minimum(base + TQ, DEC_SEQ - 16), 16), 16), :]
    gpad_ref[0:16] = jnp.where(t > 0, lo, 0.0)
    gpad_ref[16 + TQ:32 + TQ] = jnp.where(t < N_QT - 1, hi, 0.0)
    o_conv = _conv_ln_silu(gpad_ref, zsh_ref, TQ, dw_ref, cb_ref, lng_ref, lnb_ref)

    mixed_ref[...] = jnp.concatenate([o_da, o_mla, o_na, o_conv], axis=1).astype(BF)


def _lat_attn(layer, proj_outs, nab, prm):
    lam_init = 0.8 - 0.6 * math.exp(-0.3 * layer)
    q_spec = pl.BlockSpec((1, TQ, GROUP_W), lambda b, t: (b, t, 0))
    full_k = pl.BlockSpec((1, KEYS, GROUP_W), lambda b, t: (b, 0, 0))
    in_specs = [q_spec,
                pl.BlockSpec((1, N_HEADS, TQ, GROUP_W), lambda b, t: (b, 0, t, 0)),
                q_spec,
                pl.BlockSpec((1, DEC_SEQ, GROUP_W), lambda b, t: (b, 0, 0))] + [full_k] * 7 + [
                _layer_spec(layer, (NA_KR, N_HEADS * GRID_W, NA_LOCAL)),
                _layer_spec(layer, (4, DA_QK)),
                _layer_spec(layer, (1, GROUP_W)),
                _layer_spec(layer, (32, GROUP_W)),
                _layer_spec(layer, (1, GROUP_W)),
                _layer_spec(layer, (1, GROUP_W)),
                _layer_spec(layer, (1, GROUP_W))]
    return pl.pallas_call(
        functools.partial(_lat_attn_kernel, lam_init),
        grid=(DEC_BATCH, N_QT),
        in_specs=in_specs,
        out_specs=pl.BlockSpec((TQ, D_MODEL), lambda b, t: (b * N_QT + t, 0)),
        out_shape=jax.ShapeDtypeStruct((DEC_BATCH * DEC_SEQ, D_MODEL), BF),
        scratch_shapes=[pltpu.VMEM((TQ + 32, GROUP_W), F32), pltpu.VMEM((7, TQ + 8, GROUP_W), F32)],
        compiler_params=_cparams("arbitrary", "arbitrary"),
        name=f"lat_attn{layer}",
    )(*proj_outs, nab, prm["lamv"], prm["g_sub"], prm["dw"], prm["cb"], prm["ln_g"], prm["ln_b"])


def _out_ffn_kernel(final, tiles_per_mod, mod_base, x_ref, mx_ref, mod_ref, wout_ref, gff_ref,
                    w1_ref, w2_ref, gfin_ref, o_ref):
    row = mod_base + pl.program_id(0) // tiles_per_mod
    g1 = mod_ref[pl.ds(row, 1), 2 * D_MODEL:3 * D_MODEL]
    sh2 = mod_ref[pl.ds(row, 1), 3 * D_MODEL:4 * D_MODEL]
    sc2 = mod_ref[pl.ds(row, 1), 4 * D_MODEL:5 * D_MODEL]
    g2 = mod_ref[pl.ds(row, 1), 5 * D_MODEL:6 * D_MODEL]
    x1 = x_ref[...] + g1 * _dot(mx_ref[...], wout_ref[...])
    h2 = ((_rms(x1) * gff_ref[...]) * (1.0 + sc2) + sh2).astype(BF)
    acc = jnp.zeros((TM_FFN, D_MODEL), F32)
    for c in range(D_FF // D_MODEL):
        a = jnp.maximum(_dot(h2, w1_ref[:, c * D_MODEL:(c + 1) * D_MODEL]), 0.0)
        acc = acc + _dot((a * a).astype(BF), w2_ref[c * D_MODEL:(c + 1) * D_MODEL, :])
    x2 = x1 + g2 * acc
    o_ref[...] = _rms(x2) * gfin_ref[...] if final else x2


def _out_ffn(name, layer, tiles_per_mod, mod_base, x, mixed, prm, g_final):
    n = x.shape[0]
    tile = pl.BlockSpec((TM_FFN, D_MODEL), lambda i: (i, 0))
    return pl.pallas_call(
        functools.partial(_out_ffn_kernel, layer == DEPTH - 1, tiles_per_mod, mod_base),
        grid=(n // TM_FFN,),
        in_specs=[tile, tile,
                  _layer_spec(layer, (16, N_MOD * D_MODEL)),
                  _layer_spec(layer, (D_MODEL, D_MODEL)),
                  _layer_spec(layer, (1, D_MODEL)),
                  _layer_spec(layer, (D_MODEL, D_FF)),
                  _layer_spec(layer, (D_FF, D_MODEL)),
                  _const_spec((1, D_MODEL))],
        out_specs=tile,
        out_shape=jax.ShapeDtypeStruct((n, D_MODEL), F32),
        compiler_params=_cparams("arbitrary"),
        name=name,
    )(x, mixed, prm["mod"], prm["w_out"], prm["g_ff"], prm["w_ff1"], prm["w_ff2"], g_final)


def _rope_tables():
    t = np.arange(DEC_SEQ)
    rows = (t // GRID_W).astype(np.float32)
    cols = (t % GRID_W).astype(np.float32)
    c = np.arange(GROUP_W) % 32
    freqs = np.float32(ROPE_BASE) ** (-np.arange(8, dtype=np.float32) * np.float32(2.0) / np.float32(16))
    pos = np.where((c < 16)[None, :], rows[:, None], cols[:, None]).astype(np.float32)
    ang = (pos * freqs[(c % 16) % 8][None, :]).astype(np.float32)
    first = ((c % 16) < 8)[None, :]
    cos = np.cos(ang).astype(np.float32)
    sin = np.sin(ang).astype(np.float32)
    return jnp.asarray(cos), jnp.asarray(np.where(first, -sin, sin))


def _qup_gather_index():
    idx = np.full((N_HEADS * GROUP_W,), -1, np.int64)
    for h in range(N_HEADS):
        src = h * (MLA_NOPE + MLA_ROPE)
        dst = h * GROUP_W + (h % 2) * MLA_NOPE
        idx[dst:dst + MLA_NOPE] = np.arange(src, src + MLA_NOPE)
        idx[h * GROUP_W + 128:h * GROUP_W + 128 + MLA_ROPE] = np.arange(src + MLA_NOPE, src + MLA_NOPE + MLA_ROPE)
    return idx


def _kvup_perm():
    k = [h * 128 + d for h in range(N_HEADS) for d in range(MLA_NOPE)]
    v = [h * 128 + MLA_NOPE + d for h in range(N_HEADS) for d in range(HEAD_DIM)]
    return np.asarray(k + v)


def _heads_to_lanes(c):
    b, l, h, s, d = c.shape
    return c.transpose(0, 1, 3, 2, 4).reshape(b, l, s, h * d).astype(BF)


def kernel(x_prompt, x_sample, c, cache_da_k, cache_da_v, cache_mla_ckv, cache_mla_krope, cache_na_k, cache_na_v, c_ctx, w_mod, b_mod, g_norm_mix, g_norm_ff, w_in, da_lambda_q1, da_lambda_k1, da_lambda_q2, da_lambda_k2, g_da_subln, g_mla_q, w_mla_qup, g_mla_kv, w_mla_kvup, na_rpb, conv_dw, conv_b, conv_ln_g, conv_ln_b, w_out, w_ff1, w_ff2, g_final):
    w_in_p = lax.dynamic_update_slice(
        jnp.pad(w_in[..., :KR_ORIG_END].astype(BF), ((0, 0), (0, 0), (0, P_TOT - KR_ORIG_END))),
        w_in[..., KR_ORIG_END:].astype(BF), (0, 0, P_NAQ))
    qidx = _qup_gather_index()
    w_qup_e = jnp.where(jnp.asarray(qidx >= 0)[None, None, :],
                        jnp.take(w_mla_qup, jnp.asarray(np.maximum(qidx, 0)), axis=-1), 0.0).astype(BF)
    w_kvup_p = jnp.take(w_mla_kvup, jnp.asarray(_kvup_perm()), axis=-1).astype(BF)
    r3 = lambda a: a.reshape(DEPTH, 1, -1)
    prm = dict(
        g_mix=r3(g_norm_mix), w_in=w_in_p, g_q=r3(g_mla_q), w_qup=w_qup_e, g_kv=r3(g_mla_kv), w_kvup=w_kvup_p,
        lamv=jnp.stack([da_lambda_q1, da_lambda_k1, da_lambda_q2, da_lambda_k2], axis=1),
        g_sub=r3(jnp.tile(g_da_subln, (1, N_HEADS))),
        dw=jnp.concatenate([conv_dw, jnp.zeros((DEPTH, 1, GROUP_W), F32)], axis=1),
        cb=r3(conv_b), ln_g=r3(conv_ln_g), ln_b=r3(conv_ln_b),
        w_out=w_out.astype(BF), g_ff=r3(g_norm_ff), w_ff1=w_ff1.astype(BF), w_ff2=w_ff2.astype(BF))
    cv = jnp.concatenate([c_ctx[None, :], c, jnp.zeros((16 - 1 - DEC_BATCH, D_MODEL), F32)], axis=0)
    cos_t, sin_t = _rope_tables()
    g_final2 = g_final.reshape(1, D_MODEL)

    prm["mod"] = _modulation(cv, w_mod, b_mod)
    c_kr_pad = jnp.pad(cache_mla_krope, ((0, 0), (0, 0), (0, 0), (0, KR_PAD)))
    cka, ckb, cmv = _mla_cache(cache_mla_ckv, c_kr_pad, w_kvup_p)
    caches = (_heads_to_lanes(cache_da_k), _heads_to_lanes(cache_da_v), cka, ckb, cmv,
              _heads_to_lanes(cache_na_k), _heads_to_lanes(cache_na_v))
    nab = _na_bias(na_rpb)

    xp = x_prompt.reshape(BATCH * SEQ, D_MODEL)
    xs = x_sample.reshape(DEC_BATCH * DEC_SEQ, D_MODEL)
    new_ctx = ()
    for l in range(DEPTH):
        mixed_p, *new_ctx = _ctx_layer(l, xp, prm, tuple(new_ctx))
        xp = _out_ffn(f"ctx_ffn{l}", l, BATCH * SEQ // TM_FFN, 0, xp, mixed_p, prm, g_final2)
        proj_outs = _lat_proj(l, xs, prm, cos_t, sin_t, caches)
        mixed_s = _lat_attn(l, proj_outs, nab, prm)
        xs = _out_ffn(f"lat_ffn{l}", l, DEC_SEQ // TM_FFN, 1, xs, mixed_s, prm, g_final2)

    y_prompt = xp.reshape(BATCH, SEQ, D_MODEL)
    y_sample = xs.reshape(DEC_BATCH, DEC_SEQ, D_MODEL)
    return (y_prompt, y_sample) + tuple(new_ctx)
```

```python
import functools
import math

import numpy as np
import jax
import jax.numpy as jnp
from jax import lax
from jax.experimental import pallas as pl
from jax.experimental.pallas import tpu as pltpu

F32 = jnp.float32
BF = jnp.bfloat16

D_MODEL = 1024
BATCH = 16
SEQ = 256
DEPTH = 2
DEC_BATCH = 2
DEC_SEQ = 2048
PAST_LEN = 256
GRID_W = 64
GROUP_W = 256
HEAD_DIM = 64
N_HEADS = 4
DA_QK = 32
MLA_NOPE = 64
MLA_ROPE = 32
MLA_KV_LORA = 128
NA_KR = 8
NA_KC = 16
CONV_W = 31
D_FF = 4096
ROPE_BASE = 10000.0
EPS = 1e-6
N_MOD = 6
IN_COLS = 2464

P_DAQ, P_DAK, P_DAV, P_QD, P_KVD, P_KR = 0, 256, 512, 768, 1024, 1152
P_NAQ, P_NAK, P_NAV, P_CONV, P_TOT = 1280, 1536, 1792, 2048, 2560
KR_ORIG_END = 1184
KR_PAD = 128 - MLA_ROPE

DA_SCALE = DA_QK ** -0.5
MLA_SCALE = (MLA_NOPE + MLA_ROPE) ** -0.5
NA_SCALE = HEAD_DIM ** -0.5
LOG2E = math.log2(math.e)

TQ = 256
N_QT = DEC_SEQ // TQ
KEYS = DEC_SEQ + PAST_LEN
ROWS_PER_TILE = TQ // GRID_W
N_ROWS = DEC_SEQ // GRID_W
NA_LOCAL = NA_KR * GRID_W
TM_FFN = 512
VMEM_LIMIT = 58 * 1024 * 1024

NT_DIMS = (((1,), (1,)), ((), ()))


def _cparams(*sem):
    return pltpu.CompilerParams(dimension_semantics=sem, vmem_limit_bytes=VMEM_LIMIT)


def _const_spec(shape):
    nd = len(shape)
    return pl.BlockSpec(shape, lambda *_: (0,) * nd, pipeline_mode=pl.Buffered(1))


def _layer_spec(layer, shape):
    nd = len(shape)
    return pl.BlockSpec((None,) + tuple(shape), lambda *_: (layer,) + (0,) * nd, pipeline_mode=pl.Buffered(1))


def _rms(x):
    return x * lax.rsqrt(jnp.mean(x * x, axis=-1, keepdims=True) + EPS)


def _dot(a, b):
    return jnp.dot(a, b, preferred_element_type=F32)


def _qk(q, k):
    return lax.dot_general(q, k, NT_DIMS, preferred_element_type=F32)


def _softmax_parts(s):
    m = jnp.max(s, axis=-1, keepdims=True)
    e = jnp.exp(s - m)
    return e, 1.0 / jnp.sum(e, axis=-1, keepdims=True)


def _softmax2_parts(s):
    m = jnp.max(s, axis=-1, keepdims=True)
    e = jnp.exp2(s - m)
    return e, 1.0 / jnp.sum(e, axis=-1, keepdims=True)


def _lane_ids(width):
    return lax.broadcasted_iota(jnp.int32, (1, width), 1)


def _diff_lambda(lamv_ref, lam_init):
    v = lamv_ref[...]
    a = jnp.exp(jnp.sum(v[0:1] * v[1:2], axis=-1, keepdims=True))
    b = jnp.exp(jnp.sum(v[2:3] * v[3:4], axis=-1, keepdims=True))
    return a - b + lam_init


def _group_of(lane, group):
    return lane >> (group.bit_length() - 1)


def _head_rms(x):
    r = _group_of(lax.broadcasted_iota(jnp.int32, (GROUP_W, GROUP_W), 0), HEAD_DIM)
    c = _group_of(lax.broadcasted_iota(jnp.int32, (GROUP_W, GROUP_W), 1), HEAD_DIM)
    ones_bd = jnp.where(r == c, 1.0, 0.0).astype(BF)
    sq = x * x
    hi = sq.astype(BF)
    lo = (sq - hi.astype(F32)).astype(BF)
    ss = _dot(hi, ones_bd) + _dot(lo, ones_bd)
    return x * lax.rsqrt(ss * (1.0 / HEAD_DIM) + EPS)


def _rope(z, cos, sin):
    lane = _lane_ids(z.shape[1])
    swapped = jnp.where((lane & 15) < 8, pltpu.roll(z, 120, 1), pltpu.roll(z, 8, 1))
    return z * cos + swapped * sin


def _mask_heads(qf, lane, group, idx):
    return jnp.where(_group_of(lane, group) == idx, qf, 0.0).astype(BF)


def _conv_ln_silu(gpad_ref, zsh_ref, n, dw_ref, cb_ref, lng_ref, lnb_ref):
    y = jnp.zeros((n, GROUP_W), F32) + cb_ref[...]
    for b in range(8):
        z = None
        for a in range(4):
            t = 8 * a + b - 1
            if 0 <= t < CONV_W:
                term = gpad_ref[pl.ds(8 * a, n + 8), :] * dw_ref[t:t + 1, :]
                z = term if z is None else z + term
        if b == 0:
            y = y + z[0:n]
        else:
            zsh_ref[b - 1] = z
            y = y + zsh_ref[b - 1, pl.ds(b, n), :]
    mu = jnp.mean(y, axis=-1, keepdims=True)
    yc = y - mu
    var = jnp.mean(yc * yc, axis=-1, keepdims=True)
    z = yc * lax.rsqrt(var + EPS) * lng_ref[...] + lnb_ref[...]
    return z * jax.nn.sigmoid(z)


def _mod_kernel(cv_ref, w_ref, b_ref, o_ref):
    c = cv_ref[...]
    a = (c * jax.nn.sigmoid(c)).astype(BF)
    o_ref[0] = _dot(a, w_ref[0].astype(BF)) + b_ref[0]


def _modulation(cv, w_mod, b_mod):
    nblk = 4
    wblk = N_MOD * D_MODEL // nblk
    return pl.pallas_call(
        _mod_kernel,
        grid=(DEPTH, nblk),
        in_specs=[pl.BlockSpec((16, D_MODEL), lambda l, j: (0, 0)),
                  pl.BlockSpec((1, D_MODEL, wblk), lambda l, j: (l, 0, j)),
                  pl.BlockSpec((1, 1, wblk), lambda l, j: (l, 0, j))],
        out_specs=pl.BlockSpec((1, 16, wblk), lambda l, j: (l, 0, j)),
        out_shape=jax.ShapeDtypeStruct((DEPTH, 16, N_MOD * D_MODEL), F32),
        compiler_params=_cparams("arbitrary", "arbitrary"),
        name="modulation",
    )(cv, w_mod, b_mod.reshape(DEPTH, 1, N_MOD * D_MODEL))


def _mla_cache_kernel(ckv_ref, kr_ref, w_ref, ka_ref, kb_ref, v_ref):
    kvm = _dot(ckv_ref[0, 0].astype(BF), w_ref[0])
    kr = kr_ref[0, 0].astype(BF)
    kn = kvm[:, :GROUP_W].astype(BF)
    ka_ref[0, 0] = jnp.concatenate([kn[:, :128], kr], axis=1)
    kb_ref[0, 0] = jnp.concatenate([kn[:, 128:], kr], axis=1)
    v_ref[0, 0] = kvm[:, GROUP_W:].astype(BF)


def _mla_cache(c_ckv, c_kr_pad, w_kvup_p):
    blk = pl.BlockSpec((1, 1, PAST_LEN, GROUP_W), lambda b, l: (b, l, 0, 0))
    shp = jax.ShapeDtypeStruct((DEC_BATCH, DEPTH, PAST_LEN, GROUP_W), BF)
    return pl.pallas_call(
        _mla_cache_kernel,
        grid=(DEC_BATCH, DEPTH),
        in_specs=[pl.BlockSpec((1, 1, PAST_LEN, MLA_KV_LORA), lambda b, l: (b, l, 0, 0)),
                  pl.BlockSpec((1, 1, PAST_LEN, 128), lambda b, l: (b, l, 0, 0)),
                  pl.BlockSpec((1, MLA_KV_LORA, 2 * GROUP_W), lambda b, l: (l, 0, 0))],
        out_specs=[blk, blk, blk],
        out_shape=[shp, shp, shp],
        compiler_params=_cparams("arbitrary", "arbitrary"),
        name="mla_cache",
    )(c_ckv, c_kr_pad, w_kvup_p)


N_DR = 2 * NA_KR - 1
N_DC = 2 * NA_KC - 1


def _na_bias_kernel(rpb_ref, o_ref, tp_ref):
    base = (pl.program_id(0) * N_HEADS + pl.program_id(1)) * (N_DR * N_DC)
    cq = lax.broadcasted_iota(jnp.int32, (GRID_W, 128), 0)
    lane = lax.broadcasted_iota(jnp.int32, (GRID_W, 128), 1)
    ck = lane & (GRID_W - 1)
    upper = lane >= GRID_W
    dc = jnp.clip(ck - cq, -(NA_KC - 1), NA_KC - 1) + NA_KC - 1
    cstart = jnp.clip(cq - NA_KC // 2, 0, GRID_W - NA_KC)
    ok = (ck >= cstart) & (ck < cstart + NA_KC)

    def pair(dr, carry):
        val = jnp.zeros((GRID_W, 128), F32)
        for d in range(N_DC):
            s0 = rpb_ref[base + dr * N_DC + d]
            s1 = rpb_ref[base + (dr + 1) * N_DC + d]
            val = jnp.where(dc == d, jnp.where(upper, s1, s0), val)
        tp_ref[dr] = jnp.where(ok, val, -jnp.inf)
        return carry

    lax.fori_loop(0, N_DR - 1, pair, 0)
    for dr0 in range(NA_KR):
        o_ref[0, dr0] = jnp.concatenate([tp_ref[dr0 + 2 * m] for m in range(NA_KR // 2)], axis=1)


def _na_bias(rpb):
    return pl.pallas_call(
        _na_bias_kernel,
        grid=(DEPTH, N_HEADS),
        in_specs=[pl.BlockSpec(memory_space=pltpu.SMEM)],
        out_specs=pl.BlockSpec((1, NA_KR, GRID_W, NA_LOCAL), lambda l, h: (l, 0, h, 0)),
        out_shape=jax.ShapeDtypeStruct((DEPTH, NA_KR, N_HEADS * GRID_W, NA_LOCAL), F32),
        scratch_shapes=[pltpu.VMEM((N_DR - 1, GRID_W, 128), F32)],
        compiler_params=_cparams("arbitrary", "arbitrary"),
        name="na_bias",
    )(rpb.reshape(-1))


def _ctx_kernel(lam_init, n_prev, *refs):
    (x_ref, mod_ref, gmix_ref, win_ref, gq_ref, wqup_ref, gkv_ref, wkv_ref,
     lamv_ref, gsub_ref, dw_ref, cb_ref, lng_ref, lnb_ref) = refs[:14]
    (mixed_ref, dak_ref, dav_ref, ckv_ref, kr_ref, nak_ref, nav_ref,
     proj_ref, gpad_ref, zsh_ref) = refs[14 + n_prev:]
    x = x_ref[...]
    sh1 = mod_ref[0:1, 0:D_MODEL]
    sc1 = mod_ref[0:1, D_MODEL:2 * D_MODEL]
    h = (_rms(x) * gmix_ref[...]) * (1.0 + sc1) + sh1
    proj_ref[...] = _dot(h.astype(BF), win_ref[...])

    for hh in range(N_HEADS):
        lo, hi = hh * HEAD_DIM, (hh + 1) * HEAD_DIM
        dak_ref[0, hh] = proj_ref[:, P_DAK + lo:P_DAK + hi]
        dav_ref[0, hh] = proj_ref[:, P_DAV + lo:P_DAV + hi]
        nak_ref[0, hh] = proj_ref[:, P_NAK + lo:P_NAK + hi]
        nav_ref[0, hh] = proj_ref[:, P_NAV + lo:P_NAV + hi]

    lane = _lane_ids(GROUP_W)

    lam = _diff_lambda(lamv_ref, lam_init)
    qa = proj_ref[:, P_DAQ:P_DAQ + GROUP_W] * DA_SCALE
    ka = proj_ref[:, P_DAK:P_DAK + GROUP_W].astype(BF)
    va = proj_ref[:, P_DAV:P_DAV + GROUP_W].astype(BF)
    o_da = jnp.zeros((SEQ, GROUP_W), F32)
    for hh in range(N_HEADS):
        e1, r1 = _softmax_parts(_qk(_mask_heads(qa, lane, DA_QK, 2 * hh), ka))
        e2, r2 = _softmax_parts(_qk(_mask_heads(qa, lane, DA_QK, 2 * hh + 1), ka))
        p = (e1 * r1 - lam * (e2 * r2)).astype(BF)
        o_da = jnp.where(_group_of(lane, HEAD_DIM) == hh, _dot(p, va), o_da)
    o_da = _head_rms(o_da) * gsub_ref[...] * (1.0 - lam_init)

    qd = _rms(proj_ref[:, P_QD:P_QD + GROUP_W]) * gq_ref[...]
    qm = _dot(qd.astype(BF), wqup_ref[...]) * MLA_SCALE
    ckv = _rms(proj_ref[:, P_KVD:P_KVD + MLA_KV_LORA]) * gkv_ref[...]
    ckv_ref[0] = ckv
    kr_pad = proj_ref[:, P_KR:P_KR + 128]
    kr_ref[0] = kr_pad[:, 0:MLA_ROPE]
    kvm = _dot(ckv.astype(BF), wkv_ref[...])
    kn = kvm[:, :GROUP_W].astype(BF)
    vm = kvm[:, GROUP_W:].astype(BF)
    krb = kr_pad.astype(BF)
    kslab = (jnp.concatenate([kn[:, :128], krb], axis=1), jnp.concatenate([kn[:, 128:], krb], axis=1))
    o_mla = jnp.zeros((SEQ, GROUP_W), F32)
    for hh in range(N_HEADS):
        qh = qm[:, hh * GROUP_W:(hh + 1) * GROUP_W].astype(BF)
        e, r = _softmax_parts(_qk(qh, kslab[hh // 2]))
        o_mla = jnp.where(_group_of(lane, HEAD_DIM) == hh, _dot(e.astype(BF), vm) * r, o_mla)

    qn = proj_ref[:, P_NAQ:P_NAQ + GROUP_W] * NA_SCALE
    kn2 = proj_ref[:, P_NAK:P_NAK + GROUP_W].astype(BF)
    vn2 = proj_ref[:, P_NAV:P_NAV + GROUP_W].astype(BF)
    o_na = jnp.zeros((SEQ, GROUP_W), F32)
    for hh in range(N_HEADS):
        e, r = _softmax_parts(_qk(_mask_heads(qn, lane, HEAD_DIM, hh), kn2))
        o_na = jnp.where(_group_of(lane, HEAD_DIM) == hh, _dot(e.astype(BF), vn2) * r, o_na)

    g = proj_ref[:, P_CONV:P_CONV + GROUP_W] * jax.nn.sigmoid(proj_ref[:, P_CONV + GROUP_W:P_TOT])
    gpad_ref[0:16] = jnp.zeros((16, GROUP_W), F32)
    gpad_ref[16 + SEQ:32 + SEQ] = jnp.zeros((16, GROUP_W), F32)
    gpad_ref[16:16 + SEQ] = g
    o_conv = _conv_ln_silu(gpad_ref, zsh_ref, SEQ, dw_ref, cb_ref, lng_ref, lnb_ref)

    mixed_ref[...] = jnp.concatenate([o_da, o_mla, o_na, o_conv], axis=1).astype(BF)


def _ctx_layer(layer, x, prm, prev):
    lam_init = 0.8 - 0.6 * math.exp(-0.3 * layer)
    head_blk = pl.BlockSpec((1, None, N_HEADS, SEQ, HEAD_DIM), lambda b: (b, layer, 0, 0, 0))
    head_shp = jax.ShapeDtypeStruct((BATCH, DEPTH, N_HEADS, SEQ, HEAD_DIM), F32)
    in_specs = [pl.BlockSpec((SEQ, D_MODEL), lambda b: (b, 0)),
                _layer_spec(layer, (16, N_MOD * D_MODEL)),
                _layer_spec(layer, (1, D_MODEL)),
                _layer_spec(layer, (D_MODEL, P_TOT)),
                _layer_spec(layer, (1, GROUP_W)),
                _layer_spec(layer, (GROUP_W, N_HEADS * GROUP_W)),
                _layer_spec(layer, (1, MLA_KV_LORA)),
                _layer_spec(layer, (MLA_KV_LORA, 2 * GROUP_W)),
                _layer_spec(layer, (4, DA_QK)),
                _layer_spec(layer, (1, GROUP_W)),
                _layer_spec(layer, (32, GROUP_W)),
                _layer_spec(layer, (1, GROUP_W)),
                _layer_spec(layer, (1, GROUP_W)),
                _layer_spec(layer, (1, GROUP_W))] + [pl.BlockSpec(memory_space=pl.ANY)] * len(prev)
    out_specs = [pl.BlockSpec((SEQ, D_MODEL), lambda b: (b, 0)),
                 head_blk, head_blk,
                 pl.BlockSpec((1, None, SEQ, MLA_KV_LORA), lambda b: (b, layer, 0, 0)),
                 pl.BlockSpec((1, None, SEQ, MLA_ROPE), lambda b: (b, layer, 0, 0)),
                 head_blk, head_blk]
    out_shape = [jax.ShapeDtypeStruct((BATCH * SEQ, D_MODEL), BF),
                 head_shp, head_shp,
                 jax.ShapeDtypeStruct((BATCH, DEPTH, SEQ, MLA_KV_LORA), F32),
                 jax.ShapeDtypeStruct((BATCH, DEPTH, SEQ, MLA_ROPE), F32),
                 head_shp, head_shp]
    n_in = len(in_specs) - len(prev)
    return pl.pallas_call(
        functools.partial(_ctx_kernel, lam_init, len(prev)),
        grid=(BATCH,),
        in_specs=in_specs,
        out_specs=out_specs,
        out_shape=out_shape,
        input_output_aliases={n_in + k: 1 + k for k in range(len(prev))},
        scratch_shapes=[pltpu.VMEM((SEQ, P_TOT), F32), pltpu.VMEM((SEQ + 32, GROUP_W), F32),
                        pltpu.VMEM((7, SEQ + 8, GROUP_W), F32)],
        compiler_params=_cparams("arbitrary"),
        name=f"ctx_layer{layer}",
    )(x, prm["mod"], prm["g_mix"], prm["w_in"], prm["g_q"], prm["w_qup"], prm["g_kv"], prm["w_kvup"],
      prm["lamv"], prm["g_sub"], prm["dw"], prm["cb"], prm["ln_g"], prm["ln_b"], *prev)


def _lat_proj_kernel(x_ref, mod_ref, gmix_ref, win_ref, gq_ref, wqup_ref, gkv_ref, wkv_ref,
                     cos_ref, sin_ref, cdak_ref, cdav_ref, cka_ref, ckb_ref, cmv_ref, cnak_ref, cnav_ref,
                     daq_ref, mq_ref, naq_ref, g_ref,
                     dak_ref, dav_ref, ka_ref, kb_ref, mv_ref, nak_ref, nav_ref):
    b = pl.program_id(0)
    j = pl.program_id(1)

    @pl.when(j < N_QT)
    def _():
        x = x_ref[...]
        sh1 = mod_ref[pl.ds(1 + b, 1), 0:D_MODEL]
        sc1 = mod_ref[pl.ds(1 + b, 1), D_MODEL:2 * D_MODEL]
        h = (_rms(x) * gmix_ref[...]) * (1.0 + sc1) + sh1
        proj = _dot(h.astype(BF), win_ref[...])
        cos = cos_ref[...]
        sin = sin_ref[...]

        def rope2(z):
            return jnp.concatenate([_rope(z[:, :128], cos[:, :128], sin[:, :128]),
                                    _rope(z[:, 128:], cos[:, 128:], sin[:, 128:])], axis=1)

        daq_ref[0] = (rope2(proj[:, P_DAQ:P_DAQ + GROUP_W]) * (DA_SCALE * LOG2E)).astype(BF)
        dak_ref[0] = rope2(proj[:, P_DAK:P_DAK + GROUP_W]).astype(BF)
        dav_ref[0] = proj[:, P_DAV:P_DAV + GROUP_W].astype(BF)

        qd = _rms(proj[:, P_QD:P_QD + GROUP_W]) * gq_ref[...]
        qm = _dot(qd.astype(BF), wqup_ref[...])
        for hh in range(N_HEADS):
            nope = qm[:, hh * GROUP_W:hh * GROUP_W + 128]
            rope = _rope(qm[:, hh * GROUP_W + 128:(hh + 1) * GROUP_W], cos[:, :128], sin[:, :128])
            mq_ref[0, hh] = (jnp.concatenate([nope, rope], axis=1) * (MLA_SCALE * LOG2E)).astype(BF)
        ckv = _rms(proj[:, P_KVD:P_KVD + MLA_KV_LORA]) * gkv_ref[...]
        kvm = _dot(ckv.astype(BF), wkv_ref[...])
        kn = kvm[:, :GROUP_W].astype(BF)
        krb = _rope(proj[:, P_KR:P_KR + 128], cos[:, :128], sin[:, :128]).astype(BF)
        ka_ref[0] = jnp.concatenate([kn[:, :128], krb], axis=1)
        kb_ref[0] = jnp.concatenate([kn[:, 128:], krb], axis=1)
        mv_ref[0] = kvm[:, GROUP_W:].astype(BF)

        naq_ref[0] = (proj[:, P_NAQ:P_NAQ + GROUP_W] * NA_SCALE).astype(BF)
        nak_ref[0] = proj[:, P_NAK:P_NAK + GROUP_W].astype(BF)
        nav_ref[0] = proj[:, P_NAV:P_NAV + GROUP_W].astype(BF)
        g_ref[0] = proj[:, P_CONV:P_CONV + GROUP_W] * jax.nn.sigmoid(proj[:, P_CONV + GROUP_W:P_TOT])

    @pl.when(j == N_QT)
    def _():
        dak_ref[0] = cdak_ref[0]
        dav_ref[0] = cdav_ref[0]
        ka_ref[0] = cka_ref[0]
        kb_ref[0] = ckb_ref[0]
        mv_ref[0] = cmv_ref[0]
        nak_ref[0] = cnak_ref[0]
        nav_ref[0] = cnav_ref[0]


def _lat_proj(layer, x, prm, cos_t, sin_t, caches):
    jq = lambda j: jnp.minimum(j, N_QT - 1)
    cache_spec = pl.BlockSpec((1, None, PAST_LEN, GROUP_W), lambda b, j: (b, layer, 0, 0))
    q_spec = pl.BlockSpec((1, TQ, GROUP_W), lambda b, j: (b, jq(j), 0))
    k_spec = pl.BlockSpec((1, TQ, GROUP_W), lambda b, j: (b, j, 0))
    in_specs = [pl.BlockSpec((TQ, D_MODEL), lambda b, j: (b * N_QT + jq(j), 0)),
                _layer_spec(layer, (16, N_MOD * D_MODEL)),
                _layer_spec(layer, (1, D_MODEL)),
                _layer_spec(layer, (D_MODEL, P_TOT)),
                _layer_spec(layer, (1, GROUP_W)),
                _layer_spec(layer, (GROUP_W, N_HEADS * GROUP_W)),
                _layer_spec(layer, (1, MLA_KV_LORA)),
                _layer_spec(layer, (MLA_KV_LORA, 2 * GROUP_W)),
                pl.BlockSpec((TQ, GROUP_W), lambda b, j: (jq(j), 0)),
                pl.BlockSpec((TQ, GROUP_W), lambda b, j: (jq(j), 0))] + [cache_spec] * 7
    q_shp = jax.ShapeDtypeStruct((DEC_BATCH, DEC_SEQ, GROUP_W), BF)
    k_shp = jax.ShapeDtypeStruct((DEC_BATCH, KEYS, GROUP_W), BF)
    out_specs = [q_spec,
                 pl.BlockSpec((1, N_HEADS, TQ, GROUP_W), lambda b, j: (b, 0, jq(j), 0)),
                 q_spec, q_spec] + [k_spec] * 7
    out_shape = [q_shp,
                 jax.ShapeDtypeStruct((DEC_BATCH, N_HEADS, DEC_SEQ, GROUP_W), BF),
                 q_shp,
                 jax.ShapeDtypeStruct((DEC_BATCH, DEC_SEQ, GROUP_W), F32)] + [k_shp] * 7
    return pl.pallas_call(
        _lat_proj_kernel,
        grid=(DEC_BATCH, N_QT + 1),
        in_specs=in_specs,
        out_specs=out_specs,
        out_shape=out_shape,
        compiler_params=_cparams("arbitrary", "arbitrary"),
        name=f"lat_proj{layer}",
    )(x, prm["mod"], prm["g_mix"], prm["w_in"], prm["g_q"], prm["w_qup"], prm["g_kv"], prm["w_kvup"],
      cos_t, sin_t, *caches)


def _lat_attn_kernel(lam_init, daq_ref, mq_ref, naq_ref, g_ref,
                     dak_ref, dav_ref, ka_ref, kb_ref, mv_ref, nak_ref, nav_ref,
                     nab_ref, lamv_ref, gsub_ref, dw_ref, cb_ref, lng_ref, lnb_ref,
                     mixed_ref, gpad_ref, zsh_ref, kt_ref):
    t = pl.program_id(1)
    lane = _lane_ids(GROUP_W)

    @pl.when(t == 0)
    def _():
        kt_ref[0] = dak_ref[0].T
        kt_ref[1] = ka_ref[0].T
        kt_ref[2] = kb_ref[0].T

    lam = _diff_lambda(lamv_ref, lam_init)
    qa = daq_ref[0].astype(F32)
    va = dav_ref[0]
    ka_t = kt_ref[0]
    vm = mv_ref[0]

    kc = nak_ref[0, DEC_SEQ:KEYS, :]
    vc = nav_ref[0, DEC_SEQ:KEYS, :]
    n_items = 2 * N_HEADS + ROWS_PER_TILE

    def na_window(j):
        r = t * ROWS_PER_TILE + j
        start = jnp.clip(r - NA_KR // 2, 0, N_ROWS - NA_KR)
        return start - r + NA_KR - 1, pl.multiple_of(start * GRID_W, GRID_W)

    def scores(i):
        if i < N_HEADS:
            return (_dot(_mask_heads(qa, lane, DA_QK, 2 * i), ka_t),
                    _dot(_mask_heads(qa, lane, DA_QK, 2 * i + 1), ka_t))
        if i < 2 * N_HEADS:
            hh = i - N_HEADS
            return (_dot(mq_ref[0, hh], kt_ref[1 + hh // 2]),)
        j = i - 2 * N_HEADS
        dr0, koff = na_window(j)
        qrow = naq_ref[0, j * GRID_W:(j + 1) * GRID_W, :].astype(F32)
        q4 = jnp.concatenate([_mask_heads(qrow, lane, HEAD_DIM, hh) for hh in range(N_HEADS)], axis=0)
        return (_qk(q4, nak_ref[0, pl.ds(koff, NA_LOCAL), :]) + nab_ref[dr0], _qk(q4, kc))

    o_da = jnp.zeros((TQ, GROUP_W), F32)
    o_mla = jnp.zeros((TQ, GROUP_W), F32)
    na_rows = []
    s_next = scores(0)

    base = pl.multiple_of(t * TQ, TQ)
    gpad_ref[16:16 + TQ] = g_ref[0, pl.ds(base, TQ), :]
    lo = g_ref[0, pl.ds(pl.multiple_of(jnp.maximum(base - 16, 0), 16), 16), :]
    hi = g_ref[0, pl.ds(pl.multiple_of(jnp.minimum(base + TQ, DEC_SEQ - 16), 16), 16), :]
    gpad_ref[0:16] = jnp.where(t > 0, lo, 0.0)
    gpad_ref[16 + TQ:32 + TQ] = jnp.where(t < N_QT - 1, hi, 0.0)
    o_conv = _conv_ln_silu(gpad_ref, zsh_ref, TQ, dw_ref, cb_ref, lng_ref, lnb_ref)

    for i in range(n_items):
        s_cur = s_next
        if i + 1 < n_items:
            s_next = scores(i + 1)
        if i < N_HEADS:
            e1, r1 = _softmax2_parts(s_cur[0])
            e2, r2 = _softmax2_parts(s_cur[1])
            p = (e1 * r1 - e2 * (lam * r2)).astype(BF)
            o_da = jnp.where(_group_of(lane, HEAD_DIM) == i, _dot(p, va), o_da)
        elif i < 2 * N_HEADS:
            e, r = _softmax2_parts(s_cur[0])
            o_mla = jnp.where(_group_of(lane, HEAD_DIM) == i - N_HEADS, _dot(e.astype(BF), vm) * r, o_mla)
        else:
            _, koff = na_window(i - 2 * N_HEADS)
            s_loc, s_ctx = s_cur
            m = jnp.maximum(jnp.max(s_loc, axis=-1, keepdims=True), jnp.max(s_ctx, axis=-1, keepdims=True))
            e_loc = jnp.exp(s_loc - m)
            e_ctx = jnp.exp(s_ctx - m)
            den = jnp.sum(e_loc, axis=-1, keepdims=True) + jnp.sum(e_ctx, axis=-1, keepdims=True)
            o4 = (_dot(e_loc.astype(BF), nav_ref[0, pl.ds(koff, NA_LOCAL), :])
                  + _dot(e_ctx.astype(BF), vc)) * (1.0 / den)
            o_row = o4[0:GRID_W]
            for hh in range(1, N_HEADS):
                o_row = jnp.where(_group_of(lane, HEAD_DIM) == hh, o4[hh * GRID_W:(hh + 1) * GRID_W], o_row)
            na_rows.append(o_row)
    o_da = _head_rms(o_da) * gsub_ref[...] * (1.0 - lam_init)
    o_na = jnp.concatenate(na_rows, axis=0)

    mixed_ref[...] = jnp.concatenate([o_da, o_mla, o_na, o_conv], axis=1).astype(BF)


def _lat_attn(layer, proj_outs, nab, prm):
    lam_init = 0.8 - 0.6 * math.exp(-0.3 * layer)
    q_spec = pl.BlockSpec((1, TQ, GROUP_W), lambda b, t: (b, t, 0))
    full_k = pl.BlockSpec((1, KEYS, GROUP_W), lambda b, t: (b, 0, 0))
    in_specs = [q_spec,
                pl.BlockSpec((1, N_HEADS, TQ, GROUP_W), lambda b, t: (b, 0, t, 0)),
                q_spec,
                pl.BlockSpec((1, DEC_SEQ, GROUP_W), lambda b, t: (b, 0, 0))] + [full_k] * 7 + [
                _layer_spec(layer, (NA_KR, N_HEADS * GRID_W, NA_LOCAL)),
                _layer_spec(layer, (4, DA_QK)),
                _layer_spec(layer, (1, GROUP_W)),
                _layer_spec(layer, (32, GROUP_W)),
                _layer_spec(layer, (1, GROUP_W)),
                _layer_spec(layer, (1, GROUP_W)),
                _layer_spec(layer, (1, GROUP_W))]
    return pl.pallas_call(
        functools.partial(_lat_attn_kernel, lam_init),
        grid=(DEC_BATCH, N_QT),
        in_specs=in_specs,
        out_specs=pl.BlockSpec((TQ, D_MODEL), lambda b, t: (b * N_QT + t, 0)),
        out_shape=jax.ShapeDtypeStruct((DEC_BATCH * DEC_SEQ, D_MODEL), BF),
        scratch_shapes=[pltpu.VMEM((TQ + 32, GROUP_W), F32), pltpu.VMEM((7, TQ + 8, GROUP_W), F32),
                        pltpu.VMEM((3, GROUP_W, KEYS), BF)],
        compiler_params=_cparams("arbitrary", "arbitrary"),
        name=f"lat_attn{layer}",
    )(*proj_outs, nab, prm["lamv"], prm["g_sub"], prm["dw"], prm["cb"], prm["ln_g"], prm["ln_b"])


def _out_ffn_kernel(final, tiles_per_mod, mod_base, x_ref, mx_ref, mod_ref, wout_ref, gff_ref,
                    w1_ref, w2_ref, gfin_ref, o_ref):
    row = mod_base + pl.program_id(0) // tiles_per_mod
    g1 = mod_ref[pl.ds(row, 1), 2 * D_MODEL:3 * D_MODEL]
    sh2 = mod_ref[pl.ds(row, 1), 3 * D_MODEL:4 * D_MODEL]
    sc2 = mod_ref[pl.ds(row, 1), 4 * D_MODEL:5 * D_MODEL]
    g2 = mod_ref[pl.ds(row, 1), 5 * D_MODEL:6 * D_MODEL]
    x1 = x_ref[...] + g1 * _dot(mx_ref[...], wout_ref[...])
    h2 = ((_rms(x1) * gff_ref[...]) * (1.0 + sc2) + sh2).astype(BF)
    acc = jnp.zeros((TM_FFN, D_MODEL), F32)
    for c in range(D_FF // D_MODEL):
        a = jnp.maximum(_dot(h2, w1_ref[:, c * D_MODEL:(c + 1) * D_MODEL]), 0.0)
        acc = acc + _dot((a * a).astype(BF), w2_ref[c * D_MODEL:(c + 1) * D_MODEL, :])
    x2 = x1 + g2 * acc
    o_ref[...] = _rms(x2) * gfin_ref[...] if final else x2


def _out_ffn(name, layer, tiles_per_mod, mod_base, x, mixed, prm, g_final):
    n = x.shape[0]
    tile = pl.BlockSpec((TM_FFN, D_MODEL), lambda i: (i, 0))
    return pl.pallas_call(
        functools.partial(_out_ffn_kernel, layer == DEPTH - 1, tiles_per_mod, mod_base),
        grid=(n // TM_FFN,),
        in_specs=[tile, tile,
                  _layer_spec(layer, (16, N_MOD * D_MODEL)),
                  _layer_spec(layer, (D_MODEL, D_MODEL)),
                  _layer_spec(layer, (1, D_MODEL)),
                  _layer_spec(layer, (D_MODEL, D_FF)),
                  _layer_spec(layer, (D_FF, D_MODEL)),
                  _const_spec((1, D_MODEL))],
        out_specs=tile,
        out_shape=jax.ShapeDtypeStruct((n, D_MODEL), F32),
        compiler_params=_cparams("arbitrary"),
        name=name,
    )(x, mixed, prm["mod"], prm["w_out"], prm["g_ff"], prm["w_ff1"], prm["w_ff2"], g_final)


def _rope_tables():
    t = np.arange(DEC_SEQ)
    rows = (t // GRID_W).astype(np.float32)
    cols = (t % GRID_W).astype(np.float32)
    c = np.arange(GROUP_W) % 32
    freqs = np.float32(ROPE_BASE) ** (-np.arange(8, dtype=np.float32) * np.float32(2.0) / np.float32(16))
    pos = np.where((c < 16)[None, :], rows[:, None], cols[:, None]).astype(np.float32)
    ang = (pos * freqs[(c % 16) % 8][None, :]).astype(np.float32)
    first = ((c % 16) < 8)[None, :]
    cos = np.cos(ang).astype(np.float32)
    sin = np.sin(ang).astype(np.float32)
    return jnp.asarray(cos), jnp.asarray(np.where(first, -sin, sin))


def _qup_gather_index():
    idx = np.full((N_HEADS * GROUP_W,), -1, np.int64)
    for h in range(N_HEADS):
        src = h * (MLA_NOPE + MLA_ROPE)
        dst = h * GROUP_W + (h % 2) * MLA_NOPE
        idx[dst:dst + MLA_NOPE] = np.arange(src, src + MLA_NOPE)
        idx[h * GROUP_W + 128:h * GROUP_W + 128 + MLA_ROPE] = np.arange(src + MLA_NOPE, src + MLA_NOPE + MLA_ROPE)
    return idx


def _kvup_perm():
    k = [h * 128 + d for h in range(N_HEADS) for d in range(MLA_NOPE)]
    v = [h * 128 + MLA_NOPE + d for h in range(N_HEADS) for d in range(HEAD_DIM)]
    return np.asarray(k + v)


def _heads_to_lanes(c):
    b, l, h, s, d = c.shape
    return c.transpose(0, 1, 3, 2, 4).reshape(b, l, s, h * d).astype(BF)


def kernel(x_prompt, x_sample, c, cache_da_k, cache_da_v, cache_mla_ckv, cache_mla_krope, cache_na_k, cache_na_v, c_ctx, w_mod, b_mod, g_norm_mix, g_norm_ff, w_in, da_lambda_q1, da_lambda_k1, da_lambda_q2, da_lambda_k2, g_da_subln, g_mla_q, w_mla_qup, g_mla_kv, w_mla_kvup, na_rpb, conv_dw, conv_b, conv_ln_g, conv_ln_b, w_out, w_ff1, w_ff2, g_final):
    w_in_p = lax.dynamic_update_slice(
        jnp.pad(w_in[..., :KR_ORIG_END].astype(BF), ((0, 0), (0, 0), (0, P_TOT - KR_ORIG_END))),
        w_in[..., KR_ORIG_END:].astype(BF), (0, 0, P_NAQ))
    qidx = _qup_gather_index()
    w_qup_e = jnp.where(jnp.asarray(qidx >= 0)[None, None, :],
                        jnp.take(w_mla_qup, jnp.asarray(np.maximum(qidx, 0)), axis=-1), 0.0).astype(BF)
    w_kvup_p = jnp.take(w_mla_kvup, jnp.asarray(_kvup_perm()), axis=-1).astype(BF)
    r3 = lambda a: a.reshape(DEPTH, 1, -1)
    prm = dict(
        g_mix=r3(g_norm_mix), w_in=w_in_p, g_q=r3(g_mla_q), w_qup=w_qup_e, g_kv=r3(g_mla_kv), w_kvup=w_kvup_p,
        lamv=jnp.stack([da_lambda_q1, da_lambda_k1, da_lambda_q2, da_lambda_k2], axis=1),
        g_sub=r3(jnp.tile(g_da_subln, (1, N_HEADS))),
        dw=jnp.concatenate([conv_dw, jnp.zeros((DEPTH, 1, GROUP_W), F32)], axis=1),
        cb=r3(conv_b), ln_g=r3(conv_ln_g), ln_b=r3(conv_ln_b),
        w_out=w_out.astype(BF), g_ff=r3(g_norm_ff), w_ff1=w_ff1.astype(BF), w_ff2=w_ff2.astype(BF))
    cv = jnp.concatenate([c_ctx[None, :], c, jnp.zeros((16 - 1 - DEC_BATCH, D_MODEL), F32)], axis=0)
    cos_t, sin_t = _rope_tables()
    g_final2 = g_final.reshape(1, D_MODEL)

    prm["mod"] = _modulation(cv, w_mod, b_mod)
    c_kr_pad = jnp.pad(cache_mla_krope, ((0, 0), (0, 0), (0, 0), (0, KR_PAD)))
    cka, ckb, cmv = _mla_cache(cache_mla_ckv, c_kr_pad, w_kvup_p)
    caches = (_heads_to_lanes(cache_da_k), _heads_to_lanes(cache_da_v), cka, ckb, cmv,
              _heads_to_lanes(cache_na_k), _heads_to_lanes(cache_na_v))
    nab = _na_bias(na_rpb)

    xp = x_prompt.reshape(BATCH * SEQ, D_MODEL)
    xs = x_sample.reshape(DEC_BATCH * DEC_SEQ, D_MODEL)
    new_ctx = ()
    for l in range(DEPTH):
        mixed_p, *new_ctx = _ctx_layer(l, xp, prm, tuple(new_ctx))
        xp = _out_ffn(f"ctx_ffn{l}", l, BATCH * SEQ // TM_FFN, 0, xp, mixed_p, prm, g_final2)
        proj_outs = _lat_proj(l, xs, prm, cos_t, sin_t, caches)
        mixed_s = _lat_attn(l, proj_outs, nab, prm)
        xs = _out_ffn(f"lat_ffn{l}", l, DEC_SEQ // TM_FFN, 1, xs, mixed_s, prm, g_final2)

    y_prompt = xp.reshape(BATCH, SEQ, D_MODEL)
    y_sample = xs.reshape(DEC_BATCH, DEC_SEQ, D_MODEL)
    return (y_prompt, y_sample) + tuple(new_ctx)
```

```python
import functools
import math

import numpy as np
import jax
import jax.numpy as jnp
from jax import lax
from jax.experimental import pallas as pl
from jax.experimental.pallas import tpu as pltpu

F32 = jnp.float32
BF = jnp.bfloat16

D_MODEL = 1024
BATCH = 16
SEQ = 256
DEPTH = 2
DEC_BATCH = 2
DEC_SEQ = 2048
PAST_LEN = 256
GRID_W = 64
GROUP_W = 256
HEAD_DIM = 64
N_HEADS = 4
DA_QK = 32
MLA_NOPE = 64
MLA_ROPE = 32
MLA_KV_LORA = 128
NA_KR = 8
NA_KC = 16
CONV_W = 31
D_FF = 4096
ROPE_BASE = 10000.0
EPS = 1e-6
N_MOD = 6
IN_COLS = 2464

P_DAQ, P_DAK, P_DAV, P_QD, P_KVD, P_KR = 0, 256, 512, 768, 1024, 1152
P_NAQ, P_NAK, P_NAV, P_CONV, P_TOT = 1280, 1536, 1792, 2048, 2560
P_SPLIT = P_NAQ
N_CTX_IN = 15
KR_ORIG_END = 1184
KR_PAD = 128 - MLA_ROPE

DA_SCALE = DA_QK ** -0.5
MLA_SCALE = (MLA_NOPE + MLA_ROPE) ** -0.5
NA_SCALE = HEAD_DIM ** -0.5
LOG2E = math.log2(math.e)

TQ = 256
N_QT = DEC_SEQ // TQ
KEYS = DEC_SEQ + PAST_LEN
ROWS_PER_TILE = TQ // GRID_W
N_ROWS = DEC_SEQ // GRID_W
NA_LOCAL = NA_KR * GRID_W
TM_FFN = 512
VMEM_LIMIT = 58 * 1024 * 1024

NT_DIMS = (((1,), (1,)), ((), ()))


def _cparams(*sem):
    return pltpu.CompilerParams(dimension_semantics=sem, vmem_limit_bytes=VMEM_LIMIT)


def _const_spec(shape):
    nd = len(shape)
    return pl.BlockSpec(shape, lambda *_: (0,) * nd, pipeline_mode=pl.Buffered(1))


def _layer_spec(layer, shape):
    nd = len(shape)
    return pl.BlockSpec((None,) + tuple(shape), lambda *_: (layer,) + (0,) * nd, pipeline_mode=pl.Buffered(1))


def _rms(x):
    return x * lax.rsqrt(jnp.mean(x * x, axis=-1, keepdims=True) + EPS)


def _dot(a, b):
    return jnp.dot(a, b, preferred_element_type=F32)


def _qk(q, k):
    return lax.dot_general(q, k, NT_DIMS, preferred_element_type=F32)


def _softmax_parts(s):
    m = jnp.max(s, axis=-1, keepdims=True)
    e = jnp.exp(s - m)
    return e, 1.0 / jnp.sum(e, axis=-1, keepdims=True)


def _softmax2_parts(s):
    m = jnp.max(s, axis=-1, keepdims=True)
    e = jnp.exp2(s - m)
    return e, 1.0 / jnp.sum(e, axis=-1, keepdims=True)


def _lane_ids(width):
    return lax.broadcasted_iota(jnp.int32, (1, width), 1)


def _diff_lambda(lamv_ref, lam_init):
    v = lamv_ref[...]
    a = jnp.exp(jnp.sum(v[0:1] * v[1:2], axis=-1, keepdims=True))
    b = jnp.exp(jnp.sum(v[2:3] * v[3:4], axis=-1, keepdims=True))
    return a - b + lam_init


def _group_of(lane, group):
    return lane >> (group.bit_length() - 1)


def _head_rms(x):
    r = _group_of(lax.broadcasted_iota(jnp.int32, (GROUP_W, GROUP_W), 0), HEAD_DIM)
    c = _group_of(lax.broadcasted_iota(jnp.int32, (GROUP_W, GROUP_W), 1), HEAD_DIM)
    ones_bd = jnp.where(r == c, 1.0, 0.0).astype(BF)
    sq = x * x
    hi = sq.astype(BF)
    lo = (sq - hi.astype(F32)).astype(BF)
    ss = _dot(hi, ones_bd) + _dot(lo, ones_bd)
    return x * lax.rsqrt(ss * (1.0 / HEAD_DIM) + EPS)


def _rope(z, cos, sin):
    lane = _lane_ids(z.shape[1])
    swapped = jnp.where((lane & 15) < 8, pltpu.roll(z, 120, 1), pltpu.roll(z, 8, 1))
    return z * cos + swapped * sin


def _mask_heads(qf, lane, group, idx):
    return jnp.where(_group_of(lane, group) == idx, qf, 0.0).astype(BF)


def _conv_ln_silu(gpad_ref, zsh_ref, n, dw_ref, cb_ref, lng_ref, lnb_ref):
    y = jnp.zeros((n, GROUP_W), F32) + cb_ref[...]
    for b in range(8):
        z = None
        for a in range(4):
            t = 8 * a + b - 1
            if 0 <= t < CONV_W:
                term = gpad_ref[pl.ds(8 * a, n + 8), :] * dw_ref[t:t + 1, :]
                z = term if z is None else z + term
        if b == 0:
            y = y + z[0:n]
        else:
            zsh_ref[b - 1] = z
            y = y + zsh_ref[b - 1, pl.ds(b, n), :]
    mu = jnp.mean(y, axis=-1, keepdims=True)
    yc = y - mu
    var = jnp.mean(yc * yc, axis=-1, keepdims=True)
    z = yc * lax.rsqrt(var + EPS) * lng_ref[...] + lnb_ref[...]
    return z * jax.nn.sigmoid(z)


def _mod_kernel(cv_ref, w_ref, b_ref, o_ref):
    c = cv_ref[...]
    a = (c * jax.nn.sigmoid(c)).astype(BF)
    o_ref[0] = _dot(a, w_ref[0].astype(BF)) + b_ref[0]


def _modulation(cv, w_mod, b_mod):
    nblk = 4
    wblk = N_MOD * D_MODEL // nblk
    return pl.pallas_call(
        _mod_kernel,
        grid=(DEPTH, nblk),
        in_specs=[pl.BlockSpec((16, D_MODEL), lambda l, j: (0, 0)),
                  pl.BlockSpec((1, D_MODEL, wblk), lambda l, j: (l, 0, j)),
                  pl.BlockSpec((1, 1, wblk), lambda l, j: (l, 0, j))],
        out_specs=pl.BlockSpec((1, 16, wblk), lambda l, j: (l, 0, j)),
        out_shape=jax.ShapeDtypeStruct((DEPTH, 16, N_MOD * D_MODEL), F32),
        compiler_params=_cparams("arbitrary", "arbitrary"),
        name="modulation",
    )(cv, w_mod, b_mod.reshape(DEPTH, 1, N_MOD * D_MODEL))


def _mla_cache_kernel(ckv_ref, kr_ref, w_ref, ka_ref, kb_ref, v_ref):
    kvm = _dot(ckv_ref[0, 0].astype(BF), w_ref[0])
    kr = kr_ref[0, 0].astype(BF)
    kn = kvm[:, :GROUP_W].astype(BF)
    ka_ref[0, 0] = jnp.concatenate([kn[:, :128], kr], axis=1)
    kb_ref[0, 0] = jnp.concatenate([kn[:, 128:], kr], axis=1)
    v_ref[0, 0] = kvm[:, GROUP_W:].astype(BF)


def _mla_cache(c_ckv, c_kr_pad, w_kvup_p):
    blk = pl.BlockSpec((1, 1, PAST_LEN, GROUP_W), lambda b, l: (b, l, 0, 0))
    shp = jax.ShapeDtypeStruct((DEC_BATCH, DEPTH, PAST_LEN, GROUP_W), BF)
    return pl.pallas_call(
        _mla_cache_kernel,
        grid=(DEC_BATCH, DEPTH),
        in_specs=[pl.BlockSpec((1, 1, PAST_LEN, MLA_KV_LORA), lambda b, l: (b, l, 0, 0)),
                  pl.BlockSpec((1, 1, PAST_LEN, 128), lambda b, l: (b, l, 0, 0)),
                  pl.BlockSpec((1, MLA_KV_LORA, 2 * GROUP_W), lambda b, l: (l, 0, 0))],
        out_specs=[blk, blk, blk],
        out_shape=[shp, shp, shp],
        compiler_params=_cparams("arbitrary", "arbitrary"),
        name="mla_cache",
    )(c_ckv, c_kr_pad, w_kvup_p)


N_DR = 2 * NA_KR - 1
N_DC = 2 * NA_KC - 1


def _na_bias_kernel(rpb_ref, o_ref, tp_ref):
    base = (pl.program_id(0) * N_HEADS + pl.program_id(1)) * (N_DR * N_DC)
    cq = lax.broadcasted_iota(jnp.int32, (GRID_W, 128), 0)
    lane = lax.broadcasted_iota(jnp.int32, (GRID_W, 128), 1)
    ck = lane & (GRID_W - 1)
    upper = lane >= GRID_W
    dc = jnp.clip(ck - cq, -(NA_KC - 1), NA_KC - 1) + NA_KC - 1
    cstart = jnp.clip(cq - NA_KC // 2, 0, GRID_W - NA_KC)
    ok = (ck >= cstart) & (ck < cstart + NA_KC)

    def pair(dr, carry):
        val = jnp.zeros((GRID_W, 128), F32)
        for d in range(N_DC):
            s0 = rpb_ref[base + dr * N_DC + d]
            s1 = rpb_ref[base + (dr + 1) * N_DC + d]
            val = jnp.where(dc == d, jnp.where(upper, s1, s0), val)
        tp_ref[dr] = jnp.where(ok, val, -jnp.inf)
        return carry

    lax.fori_loop(0, N_DR - 1, pair, 0)
    for dr0 in range(NA_KR):
        o_ref[0, dr0] = jnp.concatenate([tp_ref[dr0 + 2 * m] for m in range(NA_KR // 2)], axis=1)


def _na_bias(rpb):
    return pl.pallas_call(
        _na_bias_kernel,
        grid=(DEPTH, N_HEADS),
        in_specs=[pl.BlockSpec(memory_space=pltpu.SMEM)],
        out_specs=pl.BlockSpec((1, NA_KR, GRID_W, NA_LOCAL), lambda l, h: (l, 0, h, 0)),
        out_shape=jax.ShapeDtypeStruct((DEPTH, NA_KR, N_HEADS * GRID_W, NA_LOCAL), F32),
        scratch_shapes=[pltpu.VMEM((N_DR - 1, GRID_W, 128), F32)],
        compiler_params=_cparams("arbitrary", "arbitrary"),
        name="na_bias",
    )(rpb.reshape(-1))


def _ctx_kernel(lam_init, n_prev, *refs):
    (x_ref, mod_ref, gmix_ref, wina_ref, winb_ref, gq_ref, wqup_ref, gkv_ref, wkv_ref,
     lamv_ref, gsub_ref, dw_ref, cb_ref, lng_ref, lnb_ref) = refs[:N_CTX_IN]
    (mixed_ref, dak_ref, dav_ref, ckv_ref, kr_ref, nak_ref, nav_ref,
     proj_ref, gpad_ref, zsh_ref) = refs[N_CTX_IN + n_prev:]
    x = x_ref[...]
    sh1 = mod_ref[0:1, 0:D_MODEL]
    sc1 = mod_ref[0:1, D_MODEL:2 * D_MODEL]
    h = ((_rms(x) * gmix_ref[...]) * (1.0 + sc1) + sh1).astype(BF)
    proj_ref[:, 0:P_SPLIT] = _dot(h, wina_ref[...])
    proj_ref[:, P_SPLIT:P_TOT] = _dot(h, winb_ref[...])

    for hh in range(N_HEADS):
        lo, hi = hh * HEAD_DIM, (hh + 1) * HEAD_DIM
        dak_ref[0, hh] = proj_ref[:, P_DAK + lo:P_DAK + hi]
        dav_ref[0, hh] = proj_ref[:, P_DAV + lo:P_DAV + hi]
        nak_ref[0, hh] = proj_ref[:, P_NAK + lo:P_NAK + hi]
        nav_ref[0, hh] = proj_ref[:, P_NAV + lo:P_NAV + hi]

    lane = _lane_ids(GROUP_W)

    lam = _diff_lambda(lamv_ref, lam_init)

    qa = proj_ref[:, P_DAQ:P_DAQ + GROUP_W] * (DA_SCALE * LOG2E)
    ka = proj_ref[:, P_DAK:P_DAK + GROUP_W].astype(BF)
    va = proj_ref[:, P_DAV:P_DAV + GROUP_W].astype(BF)
    qn = proj_ref[:, P_NAQ:P_NAQ + GROUP_W] * (NA_SCALE * LOG2E)
    kn2 = proj_ref[:, P_NAK:P_NAK + GROUP_W].astype(BF)
    vn = proj_ref[:, P_NAV:P_NAV + GROUP_W].astype(BF)
    mla = {}

    def mla_prep():
        qd = _rms(proj_ref[:, P_QD:P_QD + GROUP_W]) * gq_ref[...]
        ckv = _rms(proj_ref[:, P_KVD:P_KVD + MLA_KV_LORA]) * gkv_ref[...]
        ckv_ref[0] = ckv
        kr_pad = proj_ref[:, P_KR:P_KR + 128]
        kr_ref[0] = kr_pad[:, 0:MLA_ROPE]
        mla["q"] = (_dot(qd.astype(BF), wqup_ref[...]) * (MLA_SCALE * LOG2E)).astype(BF)
        kvm = _dot(ckv.astype(BF), wkv_ref[...])
        kn = kvm[:, :GROUP_W].astype(BF)
        krb = kr_pad.astype(BF)
        mla["k"] = (jnp.concatenate([kn[:, :128], krb], axis=1), jnp.concatenate([kn[:, 128:], krb], axis=1))
        mla["v"] = kvm[:, GROUP_W:].astype(BF)

    def conv():
        g = proj_ref[:, P_CONV:P_CONV + GROUP_W] * jax.nn.sigmoid(proj_ref[:, P_CONV + GROUP_W:P_TOT])
        gpad_ref[0:16] = jnp.zeros((16, GROUP_W), F32)
        gpad_ref[16 + SEQ:32 + SEQ] = jnp.zeros((16, GROUP_W), F32)
        gpad_ref[16:16 + SEQ] = g
        return _conv_ln_silu(gpad_ref, zsh_ref, SEQ, dw_ref, cb_ref, lng_ref, lnb_ref)

    def scores(i):
        if i < N_HEADS:
            return (_qk(_mask_heads(qa, lane, DA_QK, 2 * i), ka),
                    _qk(_mask_heads(qa, lane, DA_QK, 2 * i + 1), ka))
        if i < 2 * N_HEADS:
            hh = i - N_HEADS
            return (_qk(mla["q"][:, hh * GROUP_W:(hh + 1) * GROUP_W], mla["k"][hh // 2]),)
        return (_qk(_mask_heads(qn, lane, HEAD_DIM, i - 2 * N_HEADS), kn2),)

    outs = [jnp.zeros((SEQ, GROUP_W), F32)] * 3
    s_next = scores(0)
    o_conv = None
    for i in range(3 * N_HEADS):
        s_cur = s_next
        if i + 1 < 3 * N_HEADS:
            s_next = scores(i + 1)
        if i == 0:
            mla_prep()
        if i == N_HEADS - 1:
            o_conv = conv()
        grp, hh = divmod(i, N_HEADS)
        if grp == 0:
            e1, r1 = _softmax2_parts(s_cur[0])
            e2, r2 = _softmax2_parts(s_cur[1])
            o = _dot((e1 * r1 - e2 * (lam * r2)).astype(BF), va)
        else:
            e, r = _softmax2_parts(s_cur[0])
            o = _dot(e.astype(BF), mla["v"] if grp == 1 else vn) * r
        outs[grp] = jnp.where(_group_of(lane, HEAD_DIM) == hh, o, outs[grp])
    o_da = _head_rms(outs[0]) * gsub_ref[...] * (1.0 - lam_init)
    o_mla, o_na = outs[1], outs[2]

    mixed_ref[...] = jnp.concatenate([o_da, o_mla, o_na, o_conv], axis=1).astype(BF)


def _ctx_layer(layer, x, prm, prev):
    lam_init = 0.8 - 0.6 * math.exp(-0.3 * layer)
    head_blk = pl.BlockSpec((1, None, N_HEADS, SEQ, HEAD_DIM), lambda b: (b, layer, 0, 0, 0))
    head_shp = jax.ShapeDtypeStruct((BATCH, DEPTH, N_HEADS, SEQ, HEAD_DIM), F32)
    in_specs = [pl.BlockSpec((SEQ, D_MODEL), lambda b: (b, 0)),
                _layer_spec(layer, (16, N_MOD * D_MODEL)),
                _layer_spec(layer, (1, D_MODEL)),
                _layer_spec(layer, (D_MODEL, P_SPLIT)),
                _layer_spec(layer, (D_MODEL, P_TOT - P_SPLIT)),
                _layer_spec(layer, (1, GROUP_W)),
                _layer_spec(layer, (GROUP_W, N_HEADS * GROUP_W)),
                _layer_spec(layer, (1, MLA_KV_LORA)),
                _layer_spec(layer, (MLA_KV_LORA, 2 * GROUP_W)),
                _layer_spec(layer, (4, DA_QK)),
                _layer_spec(layer, (1, GROUP_W)),
                _layer_spec(layer, (32, GROUP_W)),
                _layer_spec(layer, (1, GROUP_W)),
                _layer_spec(layer, (1, GROUP_W)),
                _layer_spec(layer, (1, GROUP_W))] + [pl.BlockSpec(memory_space=pl.ANY)] * len(prev)
    out_specs = [pl.BlockSpec((SEQ, D_MODEL), lambda b: (b, 0)),
                 head_blk, head_blk,
                 pl.BlockSpec((1, None, SEQ, MLA_KV_LORA), lambda b: (b, layer, 0, 0)),
                 pl.BlockSpec((1, None, SEQ, MLA_ROPE), lambda b: (b, layer, 0, 0)),
                 head_blk, head_blk]
    out_shape = [jax.ShapeDtypeStruct((BATCH * SEQ, D_MODEL), BF),
                 head_shp, head_shp,
                 jax.ShapeDtypeStruct((BATCH, DEPTH, SEQ, MLA_KV_LORA), F32),
                 jax.ShapeDtypeStruct((BATCH, DEPTH, SEQ, MLA_ROPE), F32),
                 head_shp, head_shp]
    n_in = len(in_specs) - len(prev)
    return pl.pallas_call(
        functools.partial(_ctx_kernel, lam_init, len(prev)),
        grid=(BATCH,),
        in_specs=in_specs,
        out_specs=out_specs,
        out_shape=out_shape,
        input_output_aliases={n_in + k: 1 + k for k in range(len(prev))},
        scratch_shapes=[pltpu.VMEM((SEQ, P_TOT), F32), pltpu.VMEM((SEQ + 32, GROUP_W), F32),
                        pltpu.VMEM((7, SEQ + 8, GROUP_W), F32)],
        compiler_params=_cparams("arbitrary"),
        name=f"ctx_layer{layer}",
    )(x, prm["mod"], prm["g_mix"], prm["w_in_a"], prm["w_in_b"], prm["g_q"], prm["w_qup"], prm["g_kv"], prm["w_kvup"],
      prm["lamv"], prm["g_sub"], prm["dw"], prm["cb"], prm["ln_g"], prm["ln_b"], *prev)


def _lat_proj_kernel(x_ref, mod_ref, gmix_ref, wina_ref, winb_ref, gq_ref, wqup_ref, gkv_ref, wkv_ref,
                     cos_ref, sin_ref, cdak_ref, cdav_ref, cka_ref, ckb_ref, cmv_ref, cnak_ref, cnav_ref,
                     daq_ref, mq_ref, naq_ref, g_ref,
                     dak_ref, dav_ref, ka_ref, kb_ref, mv_ref, nak_ref, nav_ref):
    b = pl.program_id(0)
    j = pl.program_id(1)

    @pl.when(j < N_QT)
    def _():
        x = x_ref[...]
        sh1 = mod_ref[pl.ds(1 + b, 1), 0:D_MODEL]
        sc1 = mod_ref[pl.ds(1 + b, 1), D_MODEL:2 * D_MODEL]
        h = (_rms(x) * gmix_ref[...]) * (1.0 + sc1) + sh1
        hb = h.astype(BF)
        proj = jnp.concatenate([_dot(hb, wina_ref[...]), _dot(hb, winb_ref[...])], axis=1)
        cos = cos_ref[...]
        sin = sin_ref[...]

        def rope2(z):
            return jnp.concatenate([_rope(z[:, :128], cos[:, :128], sin[:, :128]),
                                    _rope(z[:, 128:], cos[:, 128:], sin[:, 128:])], axis=1)

        daq_ref[0] = (rope2(proj[:, P_DAQ:P_DAQ + GROUP_W]) * (DA_SCALE * LOG2E)).astype(BF)
        dak_ref[0] = rope2(proj[:, P_DAK:P_DAK + GROUP_W]).astype(BF)
        dav_ref[0] = proj[:, P_DAV:P_DAV + GROUP_W].astype(BF)

        qd = _rms(proj[:, P_QD:P_QD + GROUP_W]) * gq_ref[...]
        qm = _dot(qd.astype(BF), wqup_ref[...])
        for hh in range(N_HEADS):
            nope = qm[:, hh * GROUP_W:hh * GROUP_W + 128]
            rope = _rope(qm[:, hh * GROUP_W + 128:(hh + 1) * GROUP_W], cos[:, :128], sin[:, :128])
            mq_ref[0, hh] = (jnp.concatenate([nope, rope], axis=1) * (MLA_SCALE * LOG2E)).astype(BF)
        ckv = _rms(proj[:, P_KVD:P_KVD + MLA_KV_LORA]) * gkv_ref[...]
        kvm = _dot(ckv.astype(BF), wkv_ref[...])
        kn = kvm[:, :GROUP_W].astype(BF)
        krb = _rope(proj[:, P_KR:P_KR + 128], cos[:, :128], sin[:, :128]).astype(BF)
        ka_ref[0] = jnp.concatenate([kn[:, :128], krb], axis=1)
        kb_ref[0] = jnp.concatenate([kn[:, 128:], krb], axis=1)
        mv_ref[0] = kvm[:, GROUP_W:].astype(BF)

        naq_ref[0] = (proj[:, P_NAQ:P_NAQ + GROUP_W] * NA_SCALE).astype(BF)
        nak_ref[0] = proj[:, P_NAK:P_NAK + GROUP_W].astype(BF)
        nav_ref[0] = proj[:, P_NAV:P_NAV + GROUP_W].astype(BF)
        g_ref[0] = proj[:, P_CONV:P_CONV + GROUP_W] * jax.nn.sigmoid(proj[:, P_CONV + GROUP_W:P_TOT])

    @pl.when(j == N_QT)
    def _():
        dak_ref[0] = cdak_ref[0]
        dav_ref[0] = cdav_ref[0]
        ka_ref[0] = cka_ref[0]
        kb_ref[0] = ckb_ref[0]
        mv_ref[0] = cmv_ref[0]
        nak_ref[0] = cnak_ref[0]
        nav_ref[0] = cnav_ref[0]


def _lat_proj(layer, x, prm, cos_t, sin_t, caches):
    jq = lambda j: jnp.minimum(j, N_QT - 1)
    cache_spec = pl.BlockSpec((1, None, PAST_LEN, GROUP_W), lambda b, j: (b, layer, 0, 0))
    q_spec = pl.BlockSpec((1, TQ, GROUP_W), lambda b, j: (b, jq(j), 0))
    k_spec = pl.BlockSpec((1, TQ, GROUP_W), lambda b, j: (b, j, 0))
    in_specs = [pl.BlockSpec((TQ, D_MODEL), lambda b, j: (b * N_QT + jq(j), 0)),
                _layer_spec(layer, (16, N_MOD * D_MODEL)),
                _layer_spec(layer, (1, D_MODEL)),
                _layer_spec(layer, (D_MODEL, P_SPLIT)),
                _layer_spec(layer, (D_MODEL, P_TOT - P_SPLIT)),
                _layer_spec(layer, (1, GROUP_W)),
                _layer_spec(layer, (GROUP_W, N_HEADS * GROUP_W)),
                _layer_spec(layer, (1, MLA_KV_LORA)),
                _layer_spec(layer, (MLA_KV_LORA, 2 * GROUP_W)),
                pl.BlockSpec((TQ, GROUP_W), lambda b, j: (jq(j), 0)),
                pl.BlockSpec((TQ, GROUP_W), lambda b, j: (jq(j), 0))] + [cache_spec] * 7
    q_shp = jax.ShapeDtypeStruct((DEC_BATCH, DEC_SEQ, GROUP_W), BF)
    k_shp = jax.ShapeDtypeStruct((DEC_BATCH, KEYS, GROUP_W), BF)
    out_specs = [q_spec,
                 pl.BlockSpec((1, N_HEADS, TQ, GROUP_W), lambda b, j: (b, 0, jq(j), 0)),
                 q_spec, q_spec] + [k_spec] * 7
    out_shape = [q_shp,
                 jax.ShapeDtypeStruct((DEC_BATCH, N_HEADS, DEC_SEQ, GROUP_W), BF),
                 q_shp,
                 jax.ShapeDtypeStruct((DEC_BATCH, DEC_SEQ, GROUP_W), F32)] + [k_shp] * 7
    return pl.pallas_call(
        _lat_proj_kernel,
        grid=(DEC_BATCH, N_QT + 1),
        in_specs=in_specs,
        out_specs=out_specs,
        out_shape=out_shape,
        compiler_params=_cparams("arbitrary", "arbitrary"),
        name=f"lat_proj{layer}",
    )(x, prm["mod"], prm["g_mix"], prm["w_in_a"], prm["w_in_b"], prm["g_q"], prm["w_qup"], prm["g_kv"], prm["w_kvup"],
      cos_t, sin_t, *caches)


def _lat_attn_kernel(lam_init, daq_ref, mq_ref, naq_ref, g_ref,
                     dak_ref, dav_ref, ka_ref, kb_ref, mv_ref, nak_ref, nav_ref,
                     nab_ref, lamv_ref, gsub_ref, dw_ref, cb_ref, lng_ref, lnb_ref,
                     mixed_ref, gpad_ref, zsh_ref, kt_ref):
    t = pl.program_id(1)
    lane = _lane_ids(GROUP_W)

    @pl.when(t == 0)
    def _():
        kt_ref[0] = dak_ref[0].T
        kt_ref[1] = ka_ref[0].T
        kt_ref[2] = kb_ref[0].T

    lam = _diff_lambda(lamv_ref, lam_init)
    qa = daq_ref[0].astype(F32)
    va = dav_ref[0]
    ka_t = kt_ref[0]
    vm = mv_ref[0]

    kc = nak_ref[0, DEC_SEQ:KEYS, :]
    vc = nav_ref[0, DEC_SEQ:KEYS, :]
    n_items = 2 * N_HEADS + ROWS_PER_TILE

    def na_window(j):
        r = t * ROWS_PER_TILE + j
        start = jnp.clip(r - NA_KR // 2, 0, N_ROWS - NA_KR)
        return start - r + NA_KR - 1, pl.multiple_of(start * GRID_W, GRID_W)

    def scores(i):
        if i < N_HEADS:
            return (_dot(_mask_heads(qa, lane, DA_QK, 2 * i), ka_t),
                    _dot(_mask_heads(qa, lane, DA_QK, 2 * i + 1), ka_t))
        if i < 2 * N_HEADS:
            hh = i - N_HEADS
            return (_dot(mq_ref[0, hh], kt_ref[1 + hh // 2]),)
        j = i - 2 * N_HEADS
        dr0, koff = na_window(j)
        qrow = naq_ref[0, j * GRID_W:(j + 1) * GRID_W, :].astype(F32)
        q4 = jnp.concatenate([_mask_heads(qrow, lane, HEAD_DIM, hh) for hh in range(N_HEADS)], axis=0)
        return (_qk(q4, nak_ref[0, pl.ds(koff, NA_LOCAL), :]) + nab_ref[dr0], _qk(q4, kc))

    o_da = jnp.zeros((TQ, GROUP_W), F32)
    o_mla = jnp.zeros((TQ, GROUP_W), F32)
    na_rows = []
    s_next = scores(0)

    base = pl.multiple_of(t * TQ, TQ)
    gpad_ref[16:16 + TQ] = g_ref[0, pl.ds(base, TQ), :]
    lo = g_ref[0, pl.ds(pl.multiple_of(jnp.maximum(base - 16, 0), 16), 16), :]
    hi = g_ref[0, pl.ds(pl.multiple_of(jnp.minimum(base + TQ, DEC_SEQ - 16), 16), 16), :]
    gpad_ref[0:16] = jnp.where(t > 0, lo, 0.0)
    gpad_ref[16 + TQ:32 + TQ] = jnp.where(t < N_QT - 1, hi, 0.0)
    o_conv = _conv_ln_silu(gpad_ref, zsh_ref, TQ, dw_ref, cb_ref, lng_ref, lnb_ref)

    for i in range(n_items):
        s_cur = s_next
        if i + 1 < n_items:
            s_next = scores(i + 1)
        if i < N_HEADS:
            e1, r1 = _softmax2_parts(s_cur[0])
            e2, r2 = _softmax2_parts(s_cur[1])
            p = (e1 * r1 - e2 * (lam * r2)).astype(BF)
            o_da = jnp.where(_group_of(lane, HEAD_DIM) == i, _dot(p, va), o_da)
        elif i < 2 * N_HEADS:
            e, r = _softmax2_parts(s_cur[0])
            o_mla = jnp.where(_group_of(lane, HEAD_DIM) == i - N_HEADS, _dot(e.astype(BF), vm) * r, o_mla)
        else:
            _, koff = na_window(i - 2 * N_HEADS)
            s_loc, s_ctx = s_cur
            m = jnp.maximum(jnp.max(s_loc, axis=-1, keepdims=True), jnp.max(s_ctx, axis=-1, keepdims=True))
            e_loc = jnp.exp(s_loc - m)
            e_ctx = jnp.exp(s_ctx - m)
            den = jnp.sum(e_loc, axis=-1, keepdims=True) + jnp.sum(e_ctx, axis=-1, keepdims=True)
            o4 = (_dot(e_loc.astype(BF), nav_ref[0, pl.ds(koff, NA_LOCAL), :])
                  + _dot(e_ctx.astype(BF), vc)) * (1.0 / den)
            o_row = o4[0:GRID_W]
            for hh in range(1, N_HEADS):
                o_row = jnp.where(_group_of(lane, HEAD_DIM) == hh, o4[hh * GRID_W:(hh + 1) * GRID_W], o_row)
            na_rows.append(o_row)
    o_da = _head_rms(o_da) * gsub_ref[...] * (1.0 - lam_init)
    o_na = jnp.concatenate(na_rows, axis=0)

    mixed_ref[...] = jnp.concatenate([o_da, o_mla, o_na, o_conv], axis=1).astype(BF)


def _lat_attn(layer, proj_outs, nab, prm):
    lam_init = 0.8 - 0.6 * math.exp(-0.3 * layer)
    q_spec = pl.BlockSpec((1, TQ, GROUP_W), lambda b, t: (b, t, 0))
    full_k = pl.BlockSpec((1, KEYS, GROUP_W), lambda b, t: (b, 0, 0))
    in_specs = [q_spec,
                pl.BlockSpec((1, N_HEADS, TQ, GROUP_W), lambda b, t: (b, 0, t, 0)),
                q_spec,
                pl.BlockSpec((1, DEC_SEQ, GROUP_W), lambda b, t: (b, 0, 0))] + [full_k] * 7 + [
                _layer_spec(layer, (NA_KR, N_HEADS * GRID_W, NA_LOCAL)),
                _layer_spec(layer, (4, DA_QK)),
                _layer_spec(layer, (1, GROUP_W)),
                _layer_spec(layer, (32, GROUP_W)),
                _layer_spec(layer, (1, GROUP_W)),
                _layer_spec(layer, (1, GROUP_W)),
                _layer_spec(layer, (1, GROUP_W))]
    return pl.pallas_call(
        functools.partial(_lat_attn_kernel, lam_init),
        grid=(DEC_BATCH, N_QT),
        in_specs=in_specs,
        out_specs=pl.BlockSpec((TQ, D_MODEL), lambda b, t: (b * N_QT + t, 0)),
        out_shape=jax.ShapeDtypeStruct((DEC_BATCH * DEC_SEQ, D_MODEL), BF),
        scratch_shapes=[pltpu.VMEM((TQ + 32, GROUP_W), F32), pltpu.VMEM((7, TQ + 8, GROUP_W), F32),
                        pltpu.VMEM((3, GROUP_W, KEYS), BF)],
        compiler_params=_cparams("arbitrary", "arbitrary"),
        name=f"lat_attn{layer}",
    )(*proj_outs, nab, prm["lamv"], prm["g_sub"], prm["dw"], prm["cb"], prm["ln_g"], prm["ln_b"])


def _out_ffn_kernel(final, tiles_per_mod, mod_base, x_ref, mx_ref, mod_ref, wout_ref, gff_ref,
                    w1_ref, w2_ref, gfin_ref, o_ref):
    row = mod_base + pl.program_id(0) // tiles_per_mod
    g1 = mod_ref[pl.ds(row, 1), 2 * D_MODEL:3 * D_MODEL]
    sh2 = mod_ref[pl.ds(row, 1), 3 * D_MODEL:4 * D_MODEL]
    sc2 = mod_ref[pl.ds(row, 1), 4 * D_MODEL:5 * D_MODEL]
    g2 = mod_ref[pl.ds(row, 1), 5 * D_MODEL:6 * D_MODEL]
    x1 = x_ref[...] + g1 * _dot(mx_ref[...], wout_ref[...])
    h2 = ((_rms(x1) * gff_ref[...]) * (1.0 + sc2) + sh2).astype(BF)
    acc = jnp.zeros((TM_FFN, D_MODEL), F32)
    for c in range(D_FF // D_MODEL):
        a = jnp.maximum(_dot(h2, w1_ref[:, c * D_MODEL:(c + 1) * D_MODEL]), 0.0)
        acc = acc + _dot((a * a).astype(BF), w2_ref[c * D_MODEL:(c + 1) * D_MODEL, :])
    x2 = x1 + g2 * acc
    o_ref[...] = _rms(x2) * gfin_ref[...] if final else x2


def _out_ffn(name, layer, tiles_per_mod, mod_base, x, mixed, prm, g_final):
    n = x.shape[0]
    tile = pl.BlockSpec((TM_FFN, D_MODEL), lambda i: (i, 0))
    return pl.pallas_call(
        functools.partial(_out_ffn_kernel, layer == DEPTH - 1, tiles_per_mod, mod_base),
        grid=(n // TM_FFN,),
        in_specs=[tile, tile,
                  _layer_spec(layer, (16, N_MOD * D_MODEL)),
                  _layer_spec(layer, (D_MODEL, D_MODEL)),
                  _layer_spec(layer, (1, D_MODEL)),
                  _layer_spec(layer, (D_MODEL, D_FF)),
                  _layer_spec(layer, (D_FF, D_MODEL)),
                  _const_spec((1, D_MODEL))],
        out_specs=tile,
        out_shape=jax.ShapeDtypeStruct((n, D_MODEL), F32),
        compiler_params=_cparams("arbitrary"),
        name=name,
    )(x, mixed, prm["mod"], prm["w_out"], prm["g_ff"], prm["w_ff1"], prm["w_ff2"], g_final)


def _rope_tables():
    t = np.arange(DEC_SEQ)
    rows = (t // GRID_W).astype(np.float32)
    cols = (t % GRID_W).astype(np.float32)
    c = np.arange(GROUP_W) % 32
    freqs = np.float32(ROPE_BASE) ** (-np.arange(8, dtype=np.float32) * np.float32(2.0) / np.float32(16))
    pos = np.where((c < 16)[None, :], rows[:, None], cols[:, None]).astype(np.float32)
    ang = (pos * freqs[(c % 16) % 8][None, :]).astype(np.float32)
    first = ((c % 16) < 8)[None, :]
    cos = np.cos(ang).astype(np.float32)
    sin = np.sin(ang).astype(np.float32)
    return jnp.asarray(cos), jnp.asarray(np.where(first, -sin, sin))


def _qup_gather_index():
    idx = np.full((N_HEADS * GROUP_W,), -1, np.int64)
    for h in range(N_HEADS):
        src = h * (MLA_NOPE + MLA_ROPE)
        dst = h * GROUP_W + (h % 2) * MLA_NOPE
        idx[dst:dst + MLA_NOPE] = np.arange(src, src + MLA_NOPE)
        idx[h * GROUP_W + 128:h * GROUP_W + 128 + MLA_ROPE] = np.arange(src + MLA_NOPE, src + MLA_NOPE + MLA_ROPE)
    return idx


def _kvup_perm():
    k = [h * 128 + d for h in range(N_HEADS) for d in range(MLA_NOPE)]
    v = [h * 128 + MLA_NOPE + d for h in range(N_HEADS) for d in range(HEAD_DIM)]
    return np.asarray(k + v)


def _heads_to_lanes(c):
    b, l, h, s, d = c.shape
    return c.transpose(0, 1, 3, 2, 4).reshape(b, l, s, h * d).astype(BF)


def kernel(x_prompt, x_sample, c, cache_da_k, cache_da_v, cache_mla_ckv, cache_mla_krope, cache_na_k, cache_na_v, c_ctx, w_mod, b_mod, g_norm_mix, g_norm_ff, w_in, da_lambda_q1, da_lambda_k1, da_lambda_q2, da_lambda_k2, g_da_subln, g_mla_q, w_mla_qup, g_mla_kv, w_mla_kvup, na_rpb, conv_dw, conv_b, conv_ln_g, conv_ln_b, w_out, w_ff1, w_ff2, g_final):
    w_in_a = jnp.pad(w_in[..., :KR_ORIG_END].astype(BF), ((0, 0), (0, 0), (0, KR_PAD)))
    w_in_b = w_in[..., KR_ORIG_END:].astype(BF)
    qidx = _qup_gather_index()
    w_qup_e = jnp.where(jnp.asarray(qidx >= 0)[None, None, :],
                        jnp.take(w_mla_qup, jnp.asarray(np.maximum(qidx, 0)), axis=-1), 0.0).astype(BF)
    w_kvup_p = jnp.take(w_mla_kvup, jnp.asarray(_kvup_perm()), axis=-1).astype(BF)
    r3 = lambda a: a.reshape(DEPTH, 1, -1)
    prm = dict(
        g_mix=r3(g_norm_mix), w_in_a=w_in_a, w_in_b=w_in_b, g_q=r3(g_mla_q), w_qup=w_qup_e, g_kv=r3(g_mla_kv), w_kvup=w_kvup_p,
        lamv=jnp.stack([da_lambda_q1, da_lambda_k1, da_lambda_q2, da_lambda_k2], axis=1),
        g_sub=r3(jnp.tile(g_da_subln, (1, N_HEADS))),
        dw=jnp.concatenate([conv_dw, jnp.zeros((DEPTH, 1, GROUP_W), F32)], axis=1),
        cb=r3(conv_b), ln_g=r3(conv_ln_g), ln_b=r3(conv_ln_b),
        w_out=w_out.astype(BF), g_ff=r3(g_norm_ff), w_ff1=w_ff1.astype(BF), w_ff2=w_ff2.astype(BF))
    cv = jnp.concatenate([c_ctx[None, :], c, jnp.zeros((16 - 1 - DEC_BATCH, D_MODEL), F32)], axis=0)
    cos_t, sin_t = _rope_tables()
    g_final2 = g_final.reshape(1, D_MODEL)

    prm["mod"] = _modulation(cv, w_mod, b_mod)
    c_kr_pad = jnp.pad(cache_mla_krope, ((0, 0), (0, 0), (0, 0), (0, KR_PAD)))
    cka, ckb, cmv = _mla_cache(cache_mla_ckv, c_kr_pad, w_kvup_p)
    caches = (_heads_to_lanes(cache_da_k), _heads_to_lanes(cache_da_v), cka, ckb, cmv,
              _heads_to_lanes(cache_na_k), _heads_to_lanes(cache_na_v))
    nab = _na_bias(na_rpb)

    xp = x_prompt.reshape(BATCH * SEQ, D_MODEL)
    xs = x_sample.reshape(DEC_BATCH * DEC_SEQ, D_MODEL)
    new_ctx = ()
    for l in range(DEPTH):
        mixed_p, *new_ctx = _ctx_layer(l, xp, prm, tuple(new_ctx))
        xp = _out_ffn(f"ctx_ffn{l}", l, BATCH * SEQ // TM_FFN, 0, xp, mixed_p, prm, g_final2)
        proj_outs = _lat_proj(l, xs, prm, cos_t, sin_t, caches)
        mixed_s = _lat_attn(l, proj_outs, nab, prm)
        xs = _out_ffn(f"lat_ffn{l}", l, DEC_SEQ // TM_FFN, 1, xs, mixed_s, prm, g_final2)

    y_prompt = xp.reshape(BATCH, SEQ, D_MODEL)
    y_sample = xs.reshape(DEC_BATCH, DEC_SEQ, D_MODEL)
    return (y_prompt, y_sample) + tuple(new_ctx)
```

```python
import functools
import math

import numpy as np
import jax
import jax.numpy as jnp
from jax import lax
from jax.experimental import pallas as pl
from jax.experimental.pallas import tpu as pltpu

F32 = jnp.float32
BF = jnp.bfloat16

D_MODEL = 1024
BATCH = 16
SEQ = 256
DEPTH = 2
DEC_BATCH = 2
DEC_SEQ = 2048
PAST_LEN = 256
GRID_W = 64
GROUP_W = 256
HEAD_DIM = 64
N_HEADS = 4
DA_QK = 32
MLA_NOPE = 64
MLA_ROPE = 32
MLA_KV_LORA = 128
NA_KR = 8
NA_KC = 16
CONV_W = 31
D_FF = 4096
ROPE_BASE = 10000.0
EPS = 1e-6
N_MOD = 6
IN_COLS = 2464

P_DAQ, P_DAK, P_DAV, P_QD, P_KVD, P_KR = 0, 256, 512, 768, 1024, 1152
P_NAQ, P_NAK, P_NAV, P_CONV, P_TOT = 1280, 1536, 1792, 2048, 2560
P_SPLIT = P_NAQ
N_CTX_IN = 15
KR_ORIG_END = 1184
KR_PAD = 128 - MLA_ROPE

DA_SCALE = DA_QK ** -0.5
MLA_SCALE = (MLA_NOPE + MLA_ROPE) ** -0.5
NA_SCALE = HEAD_DIM ** -0.5
LOG2E = math.log2(math.e)

TQ = 256
N_QT = DEC_SEQ // TQ
KEYS = DEC_SEQ + PAST_LEN
ROWS_PER_TILE = TQ // GRID_W
N_ROWS = DEC_SEQ // GRID_W
NA_LOCAL = NA_KR * GRID_W
TM_FFN = 512
VMEM_LIMIT = 58 * 1024 * 1024

NT_DIMS = (((1,), (1,)), ((), ()))


def _cparams(*sem):
    return pltpu.CompilerParams(dimension_semantics=sem, vmem_limit_bytes=VMEM_LIMIT)


def _const_spec(shape):
    nd = len(shape)
    return pl.BlockSpec(shape, lambda *_: (0,) * nd, pipeline_mode=pl.Buffered(1))


def _layer_spec(layer, shape):
    nd = len(shape)
    return pl.BlockSpec((None,) + tuple(shape), lambda *_: (layer,) + (0,) * nd, pipeline_mode=pl.Buffered(1))


def _rms(x):
    return x * lax.rsqrt(jnp.mean(x * x, axis=-1, keepdims=True) + EPS)


def _dot(a, b):
    return jnp.dot(a, b, preferred_element_type=F32)


def _qk(q, k):
    return lax.dot_general(q, k, NT_DIMS, preferred_element_type=F32)


def _softmax_parts(s):
    m = jnp.max(s, axis=-1, keepdims=True)
    e = jnp.exp(s - m)
    return e, 1.0 / jnp.sum(e, axis=-1, keepdims=True)


def _softmax2_parts(s):
    m = jnp.max(s, axis=-1, keepdims=True)
    e = jnp.exp2(s - m)
    return e, 1.0 / jnp.sum(e, axis=-1, keepdims=True)


def _lane_ids(width):
    return lax.broadcasted_iota(jnp.int32, (1, width), 1)


def _diff_lambda(lamv_ref, lam_init):
    v = lamv_ref[...]
    a = jnp.exp(jnp.sum(v[0:1] * v[1:2], axis=-1, keepdims=True))
    b = jnp.exp(jnp.sum(v[2:3] * v[3:4], axis=-1, keepdims=True))
    return a - b + lam_init


def _group_of(lane, group):
    return lane >> (group.bit_length() - 1)


def _head_rms(x):
    r = _group_of(lax.broadcasted_iota(jnp.int32, (GROUP_W, GROUP_W), 0), HEAD_DIM)
    c = _group_of(lax.broadcasted_iota(jnp.int32, (GROUP_W, GROUP_W), 1), HEAD_DIM)
    ones_bd = jnp.where(r == c, 1.0, 0.0).astype(BF)
    sq = x * x
    hi = sq.astype(BF)
    lo = (sq - hi.astype(F32)).astype(BF)
    ss = _dot(hi, ones_bd) + _dot(lo, ones_bd)
    return x * lax.rsqrt(ss * (1.0 / HEAD_DIM) + EPS)


def _rope(z, cos, sin):
    lane = _lane_ids(z.shape[1])
    swapped = jnp.where((lane & 15) < 8, pltpu.roll(z, 120, 1), pltpu.roll(z, 8, 1))
    return z * cos + swapped * sin


def _mask_heads(qf, lane, group, idx):
    return jnp.where(_group_of(lane, group) == idx, qf, 0.0).astype(BF)


def _conv_ln_silu(gpad_ref, zsh_ref, n, dw_ref, cb_ref, lng_ref, lnb_ref):
    y = jnp.zeros((n, GROUP_W), F32) + cb_ref[...]
    for b in range(8):
        z = None
        for a in range(4):
            t = 8 * a + b - 1
            if 0 <= t < CONV_W:
                term = gpad_ref[pl.ds(8 * a, n + 8), :] * dw_ref[t:t + 1, :]
                z = term if z is None else z + term
        if b == 0:
            y = y + z[0:n]
        else:
            zsh_ref[b - 1] = z
            y = y + zsh_ref[b - 1, pl.ds(b, n), :]
    mu = jnp.mean(y, axis=-1, keepdims=True)
    yc = y - mu
    var = jnp.mean(yc * yc, axis=-1, keepdims=True)
    z = yc * lax.rsqrt(var + EPS) * lng_ref[...] + lnb_ref[...]
    return z * jax.nn.sigmoid(z)


def _mod_kernel(cv_ref, w_ref, b_ref, o_ref):
    c = cv_ref[...]
    a = (c * jax.nn.sigmoid(c)).astype(BF)
    o_ref[0] = _dot(a, w_ref[0].astype(BF)) + b_ref[0]


def _modulation(cv, w_mod, b_mod):
    nblk = 4
    wblk = N_MOD * D_MODEL // nblk
    return pl.pallas_call(
        _mod_kernel,
        grid=(DEPTH, nblk),
        in_specs=[pl.BlockSpec((16, D_MODEL), lambda l, j: (0, 0)),
                  pl.BlockSpec((1, D_MODEL, wblk), lambda l, j: (l, 0, j)),
                  pl.BlockSpec((1, 1, wblk), lambda l, j: (l, 0, j))],
        out_specs=pl.BlockSpec((1, 16, wblk), lambda l, j: (l, 0, j)),
        out_shape=jax.ShapeDtypeStruct((DEPTH, 16, N_MOD * D_MODEL), F32),
        compiler_params=_cparams("arbitrary", "arbitrary"),
        name="modulation",
    )(cv, w_mod, b_mod.reshape(DEPTH, 1, N_MOD * D_MODEL))


def _mla_cache_kernel(ckv_ref, kr_ref, w_ref, ka_ref, kb_ref, v_ref):
    kvm = _dot(ckv_ref[0, 0].astype(BF), w_ref[0])
    kr = kr_ref[0, 0].astype(BF)
    kn = kvm[:, :GROUP_W].astype(BF)
    ka_ref[0, 0] = jnp.concatenate([kn[:, :128], kr], axis=1)
    kb_ref[0, 0] = jnp.concatenate([kn[:, 128:], kr], axis=1)
    v_ref[0, 0] = kvm[:, GROUP_W:].astype(BF)


def _mla_cache(c_ckv, c_kr_pad, w_kvup_p):
    blk = pl.BlockSpec((1, 1, PAST_LEN, GROUP_W), lambda b, l: (b, l, 0, 0))
    shp = jax.ShapeDtypeStruct((DEC_BATCH, DEPTH, PAST_LEN, GROUP_W), BF)
    return pl.pallas_call(
        _mla_cache_kernel,
        grid=(DEC_BATCH, DEPTH),
        in_specs=[pl.BlockSpec((1, 1, PAST_LEN, MLA_KV_LORA), lambda b, l: (b, l, 0, 0)),
                  pl.BlockSpec((1, 1, PAST_LEN, 128), lambda b, l: (b, l, 0, 0)),
                  pl.BlockSpec((1, MLA_KV_LORA, 2 * GROUP_W), lambda b, l: (l, 0, 0))],
        out_specs=[blk, blk, blk],
        out_shape=[shp, shp, shp],
        compiler_params=_cparams("arbitrary", "arbitrary"),
        name="mla_cache",
    )(c_ckv, c_kr_pad, w_kvup_p)


N_DR = 2 * NA_KR - 1
N_DC = 2 * NA_KC - 1


def _na_bias_kernel(rpb_ref, o_ref, tp_ref):
    base = (pl.program_id(0) * N_HEADS + pl.program_id(1)) * (N_DR * N_DC)
    cq = lax.broadcasted_iota(jnp.int32, (GRID_W, 128), 0)
    lane = lax.broadcasted_iota(jnp.int32, (GRID_W, 128), 1)
    ck = lane & (GRID_W - 1)
    upper = lane >= GRID_W
    cstart = jnp.clip(cq - NA_KC // 2, 0, GRID_W - NA_KC)
    ok = (ck >= cstart) & (ck < cstart + NA_KC)

    j = lax.broadcasted_iota(jnp.int32, (8, 128), 1)
    dclip = jnp.clip(jnp.where(j < GRID_W, j, j - 128), -(NA_KC - 1), NA_KC - 1) + NA_KC - 1

    def toeplitz(dr):
        u = jnp.zeros((8, 128), F32)
        for d in range(N_DC):
            u = jnp.where(dclip == d, rpb_ref[base + dr * N_DC + d], u)
        rows = jnp.broadcast_to(u[0:1], (GRID_W, 128))
        return pltpu.roll(rows, 0, 1, stride=1, stride_axis=0)

    tabs = [toeplitz(dr) for dr in range(N_DR)]
    for dr in range(N_DR - 1):
        val = jnp.where(upper, pltpu.roll(tabs[dr + 1], GRID_W, 1), tabs[dr])
        tp_ref[dr] = jnp.where(ok, val, -jnp.inf)
    for dr0 in range(NA_KR):
        o_ref[0, dr0] = jnp.concatenate([tp_ref[dr0 + 2 * m] for m in range(NA_KR // 2)], axis=1)


def _na_bias(rpb):
    return pl.pallas_call(
        _na_bias_kernel,
        grid=(DEPTH, N_HEADS),
        in_specs=[pl.BlockSpec(memory_space=pltpu.SMEM)],
        out_specs=pl.BlockSpec((1, NA_KR, GRID_W, NA_LOCAL), lambda l, h: (l, 0, h, 0)),
        out_shape=jax.ShapeDtypeStruct((DEPTH, NA_KR, N_HEADS * GRID_W, NA_LOCAL), F32),
        scratch_shapes=[pltpu.VMEM((N_DR - 1, GRID_W, 128), F32)],
        compiler_params=_cparams("arbitrary", "arbitrary"),
        name="na_bias",
    )(rpb.reshape(-1))


def _ctx_kernel(lam_init, n_prev, *refs):
    (x_ref, mod_ref, gmix_ref, wina_ref, winb_ref, gq_ref, wqup_ref, gkv_ref, wkv_ref,
     lamv_ref, gsub_ref, dw_ref, cb_ref, lng_ref, lnb_ref) = refs[:N_CTX_IN]
    (mixed_ref, dak_ref, dav_ref, ckv_ref, kr_ref, nak_ref, nav_ref,
     proj_ref, gpad_ref, zsh_ref) = refs[N_CTX_IN + n_prev:]
    x = x_ref[...]
    sh1 = mod_ref[0:1, 0:D_MODEL]
    sc1 = mod_ref[0:1, D_MODEL:2 * D_MODEL]
    h = ((_rms(x) * gmix_ref[...]) * (1.0 + sc1) + sh1).astype(BF)
    proj_ref[:, 0:P_SPLIT] = _dot(h, wina_ref[...])
    proj_ref[:, P_SPLIT:P_TOT] = _dot(h, winb_ref[...])

    for hh in range(N_HEADS):
        lo, hi = hh * HEAD_DIM, (hh + 1) * HEAD_DIM
        dak_ref[0, hh] = proj_ref[:, P_DAK + lo:P_DAK + hi]
        dav_ref[0, hh] = proj_ref[:, P_DAV + lo:P_DAV + hi]
        nak_ref[0, hh] = proj_ref[:, P_NAK + lo:P_NAK + hi]
        nav_ref[0, hh] = proj_ref[:, P_NAV + lo:P_NAV + hi]

    lane = _lane_ids(GROUP_W)

    lam = _diff_lambda(lamv_ref, lam_init)

    qa = proj_ref[:, P_DAQ:P_DAQ + GROUP_W] * (DA_SCALE * LOG2E)
    ka = proj_ref[:, P_DAK:P_DAK + GROUP_W].astype(BF)
    va = proj_ref[:, P_DAV:P_DAV + GROUP_W].astype(BF)
    qn = proj_ref[:, P_NAQ:P_NAQ + GROUP_W] * (NA_SCALE * LOG2E)
    kn2 = proj_ref[:, P_NAK:P_NAK + GROUP_W].astype(BF)
    vn = proj_ref[:, P_NAV:P_NAV + GROUP_W].astype(BF)
    mla = {}

    def mla_prep():
        qd = _rms(proj_ref[:, P_QD:P_QD + GROUP_W]) * gq_ref[...]
        ckv = _rms(proj_ref[:, P_KVD:P_KVD + MLA_KV_LORA]) * gkv_ref[...]
        ckv_ref[0] = ckv
        kr_pad = proj_ref[:, P_KR:P_KR + 128]
        kr_ref[0] = kr_pad[:, 0:MLA_ROPE]
        mla["q"] = (_dot(qd.astype(BF), wqup_ref[...]) * (MLA_SCALE * LOG2E)).astype(BF)
        kvm = _dot(ckv.astype(BF), wkv_ref[...])
        kn = kvm[:, :GROUP_W].astype(BF)
        krb = kr_pad.astype(BF)
        mla["k"] = (jnp.concatenate([kn[:, :128], krb], axis=1), jnp.concatenate([kn[:, 128:], krb], axis=1))
        mla["v"] = kvm[:, GROUP_W:].astype(BF)

    def conv():
        g = proj_ref[:, P_CONV:P_CONV + GROUP_W] * jax.nn.sigmoid(proj_ref[:, P_CONV + GROUP_W:P_TOT])
        gpad_ref[0:16] = jnp.zeros((16, GROUP_W), F32)
        gpad_ref[16 + SEQ:32 + SEQ] = jnp.zeros((16, GROUP_W), F32)
        gpad_ref[16:16 + SEQ] = g
        return _conv_ln_silu(gpad_ref, zsh_ref, SEQ, dw_ref, cb_ref, lng_ref, lnb_ref)

    def scores(i):
        if i < N_HEADS:
            return (_qk(_mask_heads(qa, lane, DA_QK, 2 * i), ka),
                    _qk(_mask_heads(qa, lane, DA_QK, 2 * i + 1), ka))
        if i < 2 * N_HEADS:
            hh = i - N_HEADS
            return (_qk(mla["q"][:, hh * GROUP_W:(hh + 1) * GROUP_W], mla["k"][hh // 2]),)
        return (_qk(_mask_heads(qn, lane, HEAD_DIM, i - 2 * N_HEADS), kn2),)

    outs = [jnp.zeros((SEQ, GROUP_W), F32)] * 3
    s_next = scores(0)
    o_conv = None
    for i in range(3 * N_HEADS):
        s_cur = s_next
        if i + 1 < 3 * N_HEADS:
            s_next = scores(i + 1)
        if i == 0:
            mla_prep()
        if i == N_HEADS - 1:
            o_conv = conv()
        grp, hh = divmod(i, N_HEADS)
        if grp == 0:
            e1, r1 = _softmax2_parts(s_cur[0])
            e2, r2 = _softmax2_parts(s_cur[1])
            o = _dot((e1 * r1 - e2 * (lam * r2)).astype(BF), va)
        else:
            e, r = _softmax2_parts(s_cur[0])
            o = _dot(e.astype(BF), mla["v"] if grp == 1 else vn) * r
        outs[grp] = jnp.where(_group_of(lane, HEAD_DIM) == hh, o, outs[grp])
    o_da = _head_rms(outs[0]) * gsub_ref[...] * (1.0 - lam_init)
    o_mla, o_na = outs[1], outs[2]

    mixed_ref[...] = jnp.concatenate([o_da, o_mla, o_na, o_conv], axis=1).astype(BF)


def _ctx_layer(layer, x, prm, prev):
    lam_init = 0.8 - 0.6 * math.exp(-0.3 * layer)
    head_blk = pl.BlockSpec((1, None, N_HEADS, SEQ, HEAD_DIM), lambda b: (b, layer, 0, 0, 0))
    head_shp = jax.ShapeDtypeStruct((BATCH, DEPTH, N_HEADS, SEQ, HEAD_DIM), F32)
    in_specs = [pl.BlockSpec((SEQ, D_MODEL), lambda b: (b, 0)),
                _layer_spec(layer, (16, N_MOD * D_MODEL)),
                _layer_spec(layer, (1, D_MODEL)),
                _layer_spec(layer, (D_MODEL, P_SPLIT)),
                _layer_spec(layer, (D_MODEL, P_TOT - P_SPLIT)),
                _layer_spec(layer, (1, GROUP_W)),
                _layer_spec(layer, (GROUP_W, N_HEADS * GROUP_W)),
                _layer_spec(layer, (1, MLA_KV_LORA)),
                _layer_spec(layer, (MLA_KV_LORA, 2 * GROUP_W)),
                _layer_spec(layer, (4, DA_QK)),
                _layer_spec(layer, (1, GROUP_W)),
                _layer_spec(layer, (32, GROUP_W)),
                _layer_spec(layer, (1, GROUP_W)),
                _layer_spec(layer, (1, GROUP_W)),
                _layer_spec(layer, (1, GROUP_W))] + [pl.BlockSpec(memory_space=pl.ANY)] * len(prev)
    out_specs = [pl.BlockSpec((SEQ, D_MODEL), lambda b: (b, 0)),
                 head_blk, head_blk,
                 pl.BlockSpec((1, None, SEQ, MLA_KV_LORA), lambda b: (b, layer, 0, 0)),
                 pl.BlockSpec((1, None, SEQ, MLA_ROPE), lambda b: (b, layer, 0, 0)),
                 head_blk, head_blk]
    out_shape = [jax.ShapeDtypeStruct((BATCH * SEQ, D_MODEL), BF),
                 head_shp, head_shp,
                 jax.ShapeDtypeStruct((BATCH, DEPTH, SEQ, MLA_KV_LORA), F32),
                 jax.ShapeDtypeStruct((BATCH, DEPTH, SEQ, MLA_ROPE), F32),
                 head_shp, head_shp]
    n_in = len(in_specs) - len(prev)
    return pl.pallas_call(
        functools.partial(_ctx_kernel, lam_init, len(prev)),
        grid=(BATCH,),
        in_specs=in_specs,
        out_specs=out_specs,
        out_shape=out_shape,
        input_output_aliases={n_in + k: 1 + k for k in range(len(prev))},
        scratch_shapes=[pltpu.VMEM((SEQ, P_TOT), F32), pltpu.VMEM((SEQ + 32, GROUP_W), F32),
                        pltpu.VMEM((7, SEQ + 8, GROUP_W), F32)],
        compiler_params=_cparams("arbitrary"),
        name=f"ctx_layer{layer}",
    )(x, prm["mod"], prm["g_mix"], prm["w_in_a"], prm["w_in_b"], prm["g_q"], prm["w_qup"], prm["g_kv"], prm["w_kvup"],
      prm["lamv"], prm["g_sub"], prm["dw"], prm["cb"], prm["ln_g"], prm["ln_b"], *prev)


def _lat_proj_kernel(x_ref, mod_ref, gmix_ref, wina_ref, winb_ref, gq_ref, wqup_ref, gkv_ref, wkv_ref,
                     cos_ref, sin_ref, cdak_ref, cdav_ref, cka_ref, ckb_ref, cmv_ref, cnak_ref, cnav_ref,
                     daq_ref, mq_ref, naq_ref, g_ref,
                     dak_ref, dav_ref, ka_ref, kb_ref, mv_ref, nak_ref, nav_ref):
    b = pl.program_id(0)
    j = pl.program_id(1)

    @pl.when(j < N_QT)
    def _():
        x = x_ref[...]
        sh1 = mod_ref[pl.ds(1 + b, 1), 0:D_MODEL]
        sc1 = mod_ref[pl.ds(1 + b, 1), D_MODEL:2 * D_MODEL]
        h = (_rms(x) * gmix_ref[...]) * (1.0 + sc1) + sh1
        hb = h.astype(BF)
        proj = jnp.concatenate([_dot(hb, wina_ref[...]), _dot(hb, winb_ref[...])], axis=1)
        cos = cos_ref[...]
        sin = sin_ref[...]

        def rope2(z):
            return jnp.concatenate([_rope(z[:, :128], cos[:, :128], sin[:, :128]),
                                    _rope(z[:, 128:], cos[:, 128:], sin[:, 128:])], axis=1)

        daq_ref[0] = (rope2(proj[:, P_DAQ:P_DAQ + GROUP_W]) * (DA_SCALE * LOG2E)).astype(BF)
        dak_ref[0] = rope2(proj[:, P_DAK:P_DAK + GROUP_W]).astype(BF)
        dav_ref[0] = proj[:, P_DAV:P_DAV + GROUP_W].astype(BF)

        qd = _rms(proj[:, P_QD:P_QD + GROUP_W]) * gq_ref[...]
        qm = _dot(qd.astype(BF), wqup_ref[...])
        for hh in range(N_HEADS):
            nope = qm[:, hh * GROUP_W:hh * GROUP_W + 128]
            rope = _rope(qm[:, hh * GROUP_W + 128:(hh + 1) * GROUP_W], cos[:, :128], sin[:, :128])
            mq_ref[0, hh] = (jnp.concatenate([nope, rope], axis=1) * (MLA_SCALE * LOG2E)).astype(BF)
        ckv = _rms(proj[:, P_KVD:P_KVD + MLA_KV_LORA]) * gkv_ref[...]
        kvm = _dot(ckv.astype(BF), wkv_ref[...])
        kn = kvm[:, :GROUP_W].astype(BF)
        krb = _rope(proj[:, P_KR:P_KR + 128], cos[:, :128], sin[:, :128]).astype(BF)
        ka_ref[0] = jnp.concatenate([kn[:, :128], krb], axis=1)
        kb_ref[0] = jnp.concatenate([kn[:, 128:], krb], axis=1)
        mv_ref[0] = kvm[:, GROUP_W:].astype(BF)

        naq_ref[0] = (proj[:, P_NAQ:P_NAQ + GROUP_W] * NA_SCALE).astype(BF)
        nak_ref[0] = proj[:, P_NAK:P_NAK + GROUP_W].astype(BF)
        nav_ref[0] = proj[:, P_NAV:P_NAV + GROUP_W].astype(BF)
        g_ref[0] = proj[:, P_CONV:P_CONV + GROUP_W] * jax.nn.sigmoid(proj[:, P_CONV + GROUP_W:P_TOT])

    @pl.when(j == N_QT)
    def _():
        dak_ref[0] = cdak_ref[0]
        dav_ref[0] = cdav_ref[0]
        ka_ref[0] = cka_ref[0]
        kb_ref[0] = ckb_ref[0]
        mv_ref[0] = cmv_ref[0]
        nak_ref[0] = cnak_ref[0]
        nav_ref[0] = cnav_ref[0]


def _lat_proj(layer, x, prm, cos_t, sin_t, caches):
    jq = lambda j: jnp.minimum(j, N_QT - 1)
    cache_spec = pl.BlockSpec((1, None, PAST_LEN, GROUP_W), lambda b, j: (b, layer, 0, 0))
    q_spec = pl.BlockSpec((1, TQ, GROUP_W), lambda b, j: (b, jq(j), 0))
    k_spec = pl.BlockSpec((1, TQ, GROUP_W), lambda b, j: (b, j, 0))
    in_specs = [pl.BlockSpec((TQ, D_MODEL), lambda b, j: (b * N_QT + jq(j), 0)),
                _layer_spec(layer, (16, N_MOD * D_MODEL)),
                _layer_spec(layer, (1, D_MODEL)),
                _layer_spec(layer, (D_MODEL, P_SPLIT)),
                _layer_spec(layer, (D_MODEL, P_TOT - P_SPLIT)),
                _layer_spec(layer, (1, GROUP_W)),
                _layer_spec(layer, (GROUP_W, N_HEADS * GROUP_W)),
                _layer_spec(layer, (1, MLA_KV_LORA)),
                _layer_spec(layer, (MLA_KV_LORA, 2 * GROUP_W)),
                pl.BlockSpec((TQ, GROUP_W), lambda b, j: (jq(j), 0)),
                pl.BlockSpec((TQ, GROUP_W), lambda b, j: (jq(j), 0))] + [cache_spec] * 7
    q_shp = jax.ShapeDtypeStruct((DEC_BATCH, DEC_SEQ, GROUP_W), BF)
    k_shp = jax.ShapeDtypeStruct((DEC_BATCH, KEYS, GROUP_W), BF)
    out_specs = [q_spec,
                 pl.BlockSpec((1, N_HEADS, TQ, GROUP_W), lambda b, j: (b, 0, jq(j), 0)),
                 q_spec, q_spec] + [k_spec] * 7
    out_shape = [q_shp,
                 jax.ShapeDtypeStruct((DEC_BATCH, N_HEADS, DEC_SEQ, GROUP_W), BF),
                 q_shp,
                 jax.ShapeDtypeStruct((DEC_BATCH, DEC_SEQ, GROUP_W), F32)] + [k_shp] * 7
    return pl.pallas_call(
        _lat_proj_kernel,
        grid=(DEC_BATCH, N_QT + 1),
        in_specs=in_specs,
        out_specs=out_specs,
        out_shape=out_shape,
        compiler_params=_cparams("arbitrary", "arbitrary"),
        name=f"lat_proj{layer}",
    )(x, prm["mod"], prm["g_mix"], prm["w_in_a"], prm["w_in_b"], prm["g_q"], prm["w_qup"], prm["g_kv"], prm["w_kvup"],
      cos_t, sin_t, *caches)


def _lat_attn_kernel(lam_init, daq_ref, mq_ref, naq_ref, g_ref,
                     dak_ref, dav_ref, ka_ref, kb_ref, mv_ref, nak_ref, nav_ref,
                     nab_ref, lamv_ref, gsub_ref, dw_ref, cb_ref, lng_ref, lnb_ref,
                     mixed_ref, gpad_ref, zsh_ref, kt_ref):
    t = pl.program_id(1)
    lane = _lane_ids(GROUP_W)

    @pl.when(t == 0)
    def _():
        kt_ref[0] = dak_ref[0].T
        kt_ref[1] = ka_ref[0].T
        kt_ref[2] = kb_ref[0].T

    lam = _diff_lambda(lamv_ref, lam_init)
    qa = daq_ref[0].astype(F32)
    va = dav_ref[0]
    ka_t = kt_ref[0]
    vm = mv_ref[0]

    kc = nak_ref[0, DEC_SEQ:KEYS, :]
    vc = nav_ref[0, DEC_SEQ:KEYS, :]
    n_items = 2 * N_HEADS + ROWS_PER_TILE

    def na_window(j):
        r = t * ROWS_PER_TILE + j
        start = jnp.clip(r - NA_KR // 2, 0, N_ROWS - NA_KR)
        return start - r + NA_KR - 1, pl.multiple_of(start * GRID_W, GRID_W)

    def scores(i):
        if i < N_HEADS:
            return (_dot(_mask_heads(qa, lane, DA_QK, 2 * i), ka_t),
                    _dot(_mask_heads(qa, lane, DA_QK, 2 * i + 1), ka_t))
        if i < 2 * N_HEADS:
            hh = i - N_HEADS
            return (_dot(mq_ref[0, hh], kt_ref[1 + hh // 2]),)
        j = i - 2 * N_HEADS
        dr0, koff = na_window(j)
        qrow = naq_ref[0, j * GRID_W:(j + 1) * GRID_W, :].astype(F32)
        q4 = jnp.concatenate([_mask_heads(qrow, lane, HEAD_DIM, hh) for hh in range(N_HEADS)], axis=0)
        return (_qk(q4, nak_ref[0, pl.ds(koff, NA_LOCAL), :]) + nab_ref[dr0], _qk(q4, kc))

    o_da = jnp.zeros((TQ, GROUP_W), F32)
    o_mla = jnp.zeros((TQ, GROUP_W), F32)
    na_rows = []
    s_next = scores(0)

    base = pl.multiple_of(t * TQ, TQ)
    gpad_ref[16:16 + TQ] = g_ref[0, pl.ds(base, TQ), :]
    lo = g_ref[0, pl.ds(pl.multiple_of(jnp.maximum(base - 16, 0), 16), 16), :]
    hi = g_ref[0, pl.ds(pl.multiple_of(jnp.minimum(base + TQ, DEC_SEQ - 16), 16), 16), :]
    gpad_ref[0:16] = jnp.where(t > 0, lo, 0.0)
    gpad_ref[16 + TQ:32 + TQ] = jnp.where(t < N_QT - 1, hi, 0.0)
    o_conv = _conv_ln_silu(gpad_ref, zsh_ref, TQ, dw_ref, cb_ref, lng_ref, lnb_ref)

    for i in range(n_items):
        s_cur = s_next
        if i + 1 < n_items:
            s_next = scores(i + 1)
        if i < N_HEADS:
            e1, r1 = _softmax2_parts(s_cur[0])
            e2, r2 = _softmax2_parts(s_cur[1])
            p = (e1 * r1 - e2 * (lam * r2)).astype(BF)
            o_da = jnp.where(_group_of(lane, HEAD_DIM) == i, _dot(p, va), o_da)
        elif i < 2 * N_HEADS:
            e, r = _softmax2_parts(s_cur[0])
            o_mla = jnp.where(_group_of(lane, HEAD_DIM) == i - N_HEADS, _dot(e.astype(BF), vm) * r, o_mla)
        else:
            _, koff = na_window(i - 2 * N_HEADS)
            s_loc, s_ctx = s_cur
            m = jnp.maximum(jnp.max(s_loc, axis=-1, keepdims=True), jnp.max(s_ctx, axis=-1, keepdims=True))
            e_loc = jnp.exp(s_loc - m)
            e_ctx = jnp.exp(s_ctx - m)
            den = jnp.sum(e_loc, axis=-1, keepdims=True) + jnp.sum(e_ctx, axis=-1, keepdims=True)
            o4 = (_dot(e_loc.astype(BF), nav_ref[0, pl.ds(koff, NA_LOCAL), :])
                  + _dot(e_ctx.astype(BF), vc)) * (1.0 / den)
            o_row = o4[0:GRID_W]
            for hh in range(1, N_HEADS):
                o_row = jnp.where(_group_of(lane, HEAD_DIM) == hh, o4[hh * GRID_W:(hh + 1) * GRID_W], o_row)
            na_rows.append(o_row)
    o_da = _head_rms(o_da) * gsub_ref[...] * (1.0 - lam_init)
    o_na = jnp.concatenate(na_rows, axis=0)

    mixed_ref[...] = jnp.concatenate([o_da, o_mla, o_na, o_conv], axis=1).astype(BF)


def _lat_attn(layer, proj_outs, nab, prm):
    lam_init = 0.8 - 0.6 * math.exp(-0.3 * layer)
    q_spec = pl.BlockSpec((1, TQ, GROUP_W), lambda b, t: (b, t, 0))
    full_k = pl.BlockSpec((1, KEYS, GROUP_W), lambda b, t: (b, 0, 0))
    in_specs = [q_spec,
                pl.BlockSpec((1, N_HEADS, TQ, GROUP_W), lambda b, t: (b, 0, t, 0)),
                q_spec,
                pl.BlockSpec((1, DEC_SEQ, GROUP_W), lambda b, t: (b, 0, 0))] + [full_k] * 7 + [
                _layer_spec(layer, (NA_KR, N_HEADS * GRID_W, NA_LOCAL)),
                _layer_spec(layer, (4, DA_QK)),
                _layer_spec(layer, (1, GROUP_W)),
                _layer_spec(layer, (32, GROUP_W)),
                _layer_spec(layer, (1, GROUP_W)),
                _layer_spec(layer, (1, GROUP_W)),
                _layer_spec(layer, (1, GROUP_W))]
    return pl.pallas_call(
        functools.partial(_lat_attn_kernel, lam_init),
        grid=(DEC_BATCH, N_QT),
        in_specs=in_specs,
        out_specs=pl.BlockSpec((TQ, D_MODEL), lambda b, t: (b * N_QT + t, 0)),
        out_shape=jax.ShapeDtypeStruct((DEC_BATCH * DEC_SEQ, D_MODEL), BF),
        scratch_shapes=[pltpu.VMEM((TQ + 32, GROUP_W), F32), pltpu.VMEM((7, TQ + 8, GROUP_W), F32),
                        pltpu.VMEM((3, GROUP_W, KEYS), BF)],
        compiler_params=_cparams("arbitrary", "arbitrary"),
        name=f"lat_attn{layer}",
    )(*proj_outs, nab, prm["lamv"], prm["g_sub"], prm["dw"], prm["cb"], prm["ln_g"], prm["ln_b"])


def _out_ffn_kernel(final, tiles_per_mod, mod_base, x_ref, mx_ref, mod_ref, wout_ref, gff_ref,
                    w1_ref, w2_ref, gfin_ref, o_ref):
    row = mod_base + pl.program_id(0) // tiles_per_mod
    g1 = mod_ref[pl.ds(row, 1), 2 * D_MODEL:3 * D_MODEL]
    sh2 = mod_ref[pl.ds(row, 1), 3 * D_MODEL:4 * D_MODEL]
    sc2 = mod_ref[pl.ds(row, 1), 4 * D_MODEL:5 * D_MODEL]
    g2 = mod_ref[pl.ds(row, 1), 5 * D_MODEL:6 * D_MODEL]
    x1 = x_ref[...] + g1 * _dot(mx_ref[...], wout_ref[...])
    h2 = ((_rms(x1) * gff_ref[...]) * (1.0 + sc2) + sh2).astype(BF)
    acc = jnp.zeros((TM_FFN, D_MODEL), F32)
    for c in range(D_FF // D_MODEL):
        a = jnp.maximum(_dot(h2, w1_ref[:, c * D_MODEL:(c + 1) * D_MODEL]), 0.0)
        acc = acc + _dot((a * a).astype(BF), w2_ref[c * D_MODEL:(c + 1) * D_MODEL, :])
    x2 = x1 + g2 * acc
    o_ref[...] = _rms(x2) * gfin_ref[...] if final else x2


def _out_ffn(name, layer, tiles_per_mod, mod_base, x, mixed, prm, g_final):
    n = x.shape[0]
    tile = pl.BlockSpec((TM_FFN, D_MODEL), lambda i: (i, 0))
    return pl.pallas_call(
        functools.partial(_out_ffn_kernel, layer == DEPTH - 1, tiles_per_mod, mod_base),
        grid=(n // TM_FFN,),
        in_specs=[tile, tile,
                  _layer_spec(layer, (16, N_MOD * D_MODEL)),
                  _layer_spec(layer, (D_MODEL, D_MODEL)),
                  _layer_spec(layer, (1, D_MODEL)),
                  _layer_spec(layer, (D_MODEL, D_FF)),
                  _layer_spec(layer, (D_FF, D_MODEL)),
                  _const_spec((1, D_MODEL))],
        out_specs=tile,
        out_shape=jax.ShapeDtypeStruct((n, D_MODEL), F32),
        compiler_params=_cparams("arbitrary"),
        name=name,
    )(x, mixed, prm["mod"], prm["w_out"], prm["g_ff"], prm["w_ff1"], prm["w_ff2"], g_final)


def _rope_tables():
    t = np.arange(DEC_SEQ)
    rows = (t // GRID_W).astype(np.float32)
    cols = (t % GRID_W).astype(np.float32)
    c = np.arange(GROUP_W) % 32
    freqs = np.float32(ROPE_BASE) ** (-np.arange(8, dtype=np.float32) * np.float32(2.0) / np.float32(16))
    pos = np.where((c < 16)[None, :], rows[:, None], cols[:, None]).astype(np.float32)
    ang = (pos * freqs[(c % 16) % 8][None, :]).astype(np.float32)
    first = ((c % 16) < 8)[None, :]
    cos = np.cos(ang).astype(np.float32)
    sin = np.sin(ang).astype(np.float32)
    return jnp.asarray(cos), jnp.asarray(np.where(first, -sin, sin))


def _qup_gather_index():
    idx = np.full((N_HEADS * GROUP_W,), -1, np.int64)
    for h in range(N_HEADS):
        src = h * (MLA_NOPE + MLA_ROPE)
        dst = h * GROUP_W + (h % 2) * MLA_NOPE
        idx[dst:dst + MLA_NOPE] = np.arange(src, src + MLA_NOPE)
        idx[h * GROUP_W + 128:h * GROUP_W + 128 + MLA_ROPE] = np.arange(src + MLA_NOPE, src + MLA_NOPE + MLA_ROPE)
    return idx


def _kvup_perm():
    k = [h * 128 + d for h in range(N_HEADS) for d in range(MLA_NOPE)]
    v = [h * 128 + MLA_NOPE + d for h in range(N_HEADS) for d in range(HEAD_DIM)]
    return np.asarray(k + v)


def _heads_to_lanes(c):
    b, l, h, s, d = c.shape
    return c.transpose(0, 1, 3, 2, 4).reshape(b, l, s, h * d).astype(BF)


def kernel(x_prompt, x_sample, c, cache_da_k, cache_da_v, cache_mla_ckv, cache_mla_krope, cache_na_k, cache_na_v, c_ctx, w_mod, b_mod, g_norm_mix, g_norm_ff, w_in, da_lambda_q1, da_lambda_k1, da_lambda_q2, da_lambda_k2, g_da_subln, g_mla_q, w_mla_qup, g_mla_kv, w_mla_kvup, na_rpb, conv_dw, conv_b, conv_ln_g, conv_ln_b, w_out, w_ff1, w_ff2, g_final):
    w_in_a = jnp.pad(w_in[..., :KR_ORIG_END].astype(BF), ((0, 0), (0, 0), (0, KR_PAD)))
    w_in_b = w_in[..., KR_ORIG_END:].astype(BF)
    qidx = _qup_gather_index()
    w_qup_e = jnp.where(jnp.asarray(qidx >= 0)[None, None, :],
                        jnp.take(w_mla_qup, jnp.asarray(np.maximum(qidx, 0)), axis=-1), 0.0).astype(BF)
    w_kvup_p = jnp.take(w_mla_kvup, jnp.asarray(_kvup_perm()), axis=-1).astype(BF)
    r3 = lambda a: a.reshape(DEPTH, 1, -1)
    prm = dict(
        g_mix=r3(g_norm_mix), w_in_a=w_in_a, w_in_b=w_in_b, g_q=r3(g_mla_q), w_qup=w_qup_e, g_kv=r3(g_mla_kv), w_kvup=w_kvup_p,
        lamv=jnp.stack([da_lambda_q1, da_lambda_k1, da_lambda_q2, da_lambda_k2], axis=1),
        g_sub=r3(jnp.tile(g_da_subln, (1, N_HEADS))),
        dw=jnp.concatenate([conv_dw, jnp.zeros((DEPTH, 1, GROUP_W), F32)], axis=1),
        cb=r3(conv_b), ln_g=r3(conv_ln_g), ln_b=r3(conv_ln_b),
        w_out=w_out.astype(BF), g_ff=r3(g_norm_ff), w_ff1=w_ff1.astype(BF), w_ff2=w_ff2.astype(BF))
    cv = jnp.concatenate([c_ctx[None, :], c, jnp.zeros((16 - 1 - DEC_BATCH, D_MODEL), F32)], axis=0)
    cos_t, sin_t = _rope_tables()
    g_final2 = g_final.reshape(1, D_MODEL)

    prm["mod"] = _modulation(cv, w_mod, b_mod)
    c_kr_pad = jnp.pad(cache_mla_krope, ((0, 0), (0, 0), (0, 0), (0, KR_PAD)))
    cka, ckb, cmv = _mla_cache(cache_mla_ckv, c_kr_pad, w_kvup_p)
    caches = (_heads_to_lanes(cache_da_k), _heads_to_lanes(cache_da_v), cka, ckb, cmv,
              _heads_to_lanes(cache_na_k), _heads_to_lanes(cache_na_v))
    nab = _na_bias(na_rpb)

    xp = x_prompt.reshape(BATCH * SEQ, D_MODEL)
    xs = x_sample.reshape(DEC_BATCH * DEC_SEQ, D_MODEL)
    new_ctx = ()
    for l in range(DEPTH):
        mixed_p, *new_ctx = _ctx_layer(l, xp, prm, tuple(new_ctx))
        xp = _out_ffn(f"ctx_ffn{l}", l, BATCH * SEQ // TM_FFN, 0, xp, mixed_p, prm, g_final2)
        proj_outs = _lat_proj(l, xs, prm, cos_t, sin_t, caches)
        mixed_s = _lat_attn(l, proj_outs, nab, prm)
        xs = _out_ffn(f"lat_ffn{l}", l, DEC_SEQ // TM_FFN, 1, xs, mixed_s, prm, g_final2)

    y_prompt = xp.reshape(BATCH, SEQ, D_MODEL)
    y_sample = xs.reshape(DEC_BATCH, DEC_SEQ, D_MODEL)
    return (y_prompt, y_sample) + tuple(new_ctx)
```

```python
import functools
import math

import numpy as np
import jax
import jax.numpy as jnp
from jax import lax
from jax.experimental import pallas as pl
from jax.experimental.pallas import tpu as pltpu

F32 = jnp.float32
BF = jnp.bfloat16

D_MODEL = 1024
BATCH = 16
SEQ = 256
DEPTH = 2
DEC_BATCH = 2
DEC_SEQ = 2048
PAST_LEN = 256
GRID_W = 64
GROUP_W = 256
HEAD_DIM = 64
N_HEADS = 4
DA_QK = 32
MLA_NOPE = 64
MLA_ROPE = 32
MLA_KV_LORA = 128
NA_KR = 8
NA_KC = 16
CONV_W = 31
D_FF = 4096
ROPE_BASE = 10000.0
EPS = 1e-6
N_MOD = 6
IN_COLS = 2464

P_DAQ, P_DAK, P_DAV, P_QD, P_KVD, P_KR = 0, 256, 512, 768, 1024, 1152
P_NAQ, P_NAK, P_NAV, P_CONV, P_TOT = 1280, 1536, 1792, 2048, 2560
P_SPLIT = P_NAQ
N_CTX_IN = 14
KR_ORIG_END = 1184
KR_PAD = 128 - MLA_ROPE

DA_SCALE = DA_QK ** -0.5
MLA_SCALE = (MLA_NOPE + MLA_ROPE) ** -0.5
NA_SCALE = HEAD_DIM ** -0.5
LOG2E = math.log2(math.e)

TQ = 256
N_QT = DEC_SEQ // TQ
KEYS = DEC_SEQ + PAST_LEN
ROWS_PER_TILE = TQ // GRID_W
N_ROWS = DEC_SEQ // GRID_W
NA_LOCAL = NA_KR * GRID_W
TM_FFN = 512
VMEM_LIMIT = 58 * 1024 * 1024

NT_DIMS = (((1,), (1,)), ((), ()))


def _cparams(*sem):
    return pltpu.CompilerParams(dimension_semantics=sem, vmem_limit_bytes=VMEM_LIMIT)


def _const_spec(shape):
    nd = len(shape)
    return pl.BlockSpec(shape, lambda *_: (0,) * nd, pipeline_mode=pl.Buffered(1))


def _layer_spec(layer, shape):
    nd = len(shape)
    return pl.BlockSpec((None,) + tuple(shape), lambda *_: (layer,) + (0,) * nd, pipeline_mode=pl.Buffered(1))


def _rms(x):
    return x * lax.rsqrt(jnp.mean(x * x, axis=-1, keepdims=True) + EPS)


def _dot(a, b):
    return jnp.dot(a, b, preferred_element_type=F32)


def _qk(q, k):
    return lax.dot_general(q, k, NT_DIMS, preferred_element_type=F32)


def _softmax_parts(s):
    m = jnp.max(s, axis=-1, keepdims=True)
    e = jnp.exp(s - m)
    return e, 1.0 / jnp.sum(e, axis=-1, keepdims=True)


def _softmax2_parts(s):
    m = jnp.max(s, axis=-1, keepdims=True)
    e = jnp.exp2(s - m)
    return e, 1.0 / jnp.sum(e, axis=-1, keepdims=True)


def _lane_ids(width):
    return lax.broadcasted_iota(jnp.int32, (1, width), 1)


def _diff_lambda(lamv_ref, lam_init):
    v = lamv_ref[...]
    a = jnp.exp(jnp.sum(v[0:1] * v[1:2], axis=-1, keepdims=True))
    b = jnp.exp(jnp.sum(v[2:3] * v[3:4], axis=-1, keepdims=True))
    return a - b + lam_init


def _group_of(lane, group):
    return lane >> (group.bit_length() - 1)


def _head_rms(x):
    r = _group_of(lax.broadcasted_iota(jnp.int32, (GROUP_W, GROUP_W), 0), HEAD_DIM)
    c = _group_of(lax.broadcasted_iota(jnp.int32, (GROUP_W, GROUP_W), 1), HEAD_DIM)
    ones_bd = jnp.where(r == c, 1.0, 0.0).astype(BF)
    sq = x * x
    hi = sq.astype(BF)
    lo = (sq - hi.astype(F32)).astype(BF)
    ss = _dot(hi, ones_bd) + _dot(lo, ones_bd)
    return x * lax.rsqrt(ss * (1.0 / HEAD_DIM) + EPS)


def _rope(z, cos, sin):
    lane = _lane_ids(z.shape[1])
    swapped = jnp.where((lane & 15) < 8, pltpu.roll(z, 120, 1), pltpu.roll(z, 8, 1))
    return z * cos + swapped * sin


def _stage_w_in(win_ref, wbf_ref):
    aligned = P_KR
    wbf_ref[:, 0:aligned] = win_ref[:, 0:aligned].astype(BF)
    wbf_ref[:, aligned:P_SPLIT] = jnp.zeros((D_MODEL, P_SPLIT - aligned), BF)
    wbf_ref[:, aligned:aligned + MLA_ROPE] = win_ref[:, aligned:KR_ORIG_END].astype(BF)
    wbf_ref[:, P_SPLIT:P_TOT] = win_ref[:, KR_ORIG_END:IN_COLS].astype(BF)


def _mask_heads(qf, lane, group, idx):
    return jnp.where(_group_of(lane, group) == idx, qf, 0.0).astype(BF)


def _conv_ln_silu(gpad_ref, zsh_ref, n, dw_ref, cb_ref, lng_ref, lnb_ref):
    y = jnp.zeros((n, GROUP_W), F32) + cb_ref[...]
    for b in range(8):
        z = None
        for a in range(4):
            t = 8 * a + b - 1
            if 0 <= t < CONV_W:
                term = gpad_ref[pl.ds(8 * a, n + 8), :] * dw_ref[t:t + 1, :]
                z = term if z is None else z + term
        if b == 0:
            y = y + z[0:n]
        else:
            zsh_ref[b - 1] = z
            y = y + zsh_ref[b - 1, pl.ds(b, n), :]
    mu = jnp.mean(y, axis=-1, keepdims=True)
    yc = y - mu
    var = jnp.mean(yc * yc, axis=-1, keepdims=True)
    z = yc * lax.rsqrt(var + EPS) * lng_ref[...] + lnb_ref[...]
    return z * jax.nn.sigmoid(z)


def _mod_kernel(cv_ref, w_ref, b_ref, o_ref):
    c = cv_ref[...]
    a = (c * jax.nn.sigmoid(c)).astype(BF)
    o_ref[0] = _dot(a, w_ref[0].astype(BF)) + b_ref[0]


def _modulation(cv, w_mod, b_mod):
    nblk = 4
    wblk = N_MOD * D_MODEL // nblk
    return pl.pallas_call(
        _mod_kernel,
        grid=(DEPTH, nblk),
        in_specs=[pl.BlockSpec((16, D_MODEL), lambda l, j: (0, 0)),
                  pl.BlockSpec((1, D_MODEL, wblk), lambda l, j: (l, 0, j)),
                  pl.BlockSpec((1, 1, wblk), lambda l, j: (l, 0, j))],
        out_specs=pl.BlockSpec((1, 16, wblk), lambda l, j: (l, 0, j)),
        out_shape=jax.ShapeDtypeStruct((DEPTH, 16, N_MOD * D_MODEL), F32),
        compiler_params=_cparams("arbitrary", "arbitrary"),
        name="modulation",
    )(cv, w_mod, b_mod.reshape(DEPTH, 1, N_MOD * D_MODEL))


def _mla_cache_kernel(ckv_ref, kr_ref, w_ref, ka_ref, kb_ref, v_ref):
    kvm = _dot(ckv_ref[0, 0].astype(BF), w_ref[0])
    kr = kr_ref[0, 0].astype(BF)
    kn = kvm[:, :GROUP_W].astype(BF)
    ka_ref[0, 0] = jnp.concatenate([kn[:, :128], kr], axis=1)
    kb_ref[0, 0] = jnp.concatenate([kn[:, 128:], kr], axis=1)
    v_ref[0, 0] = kvm[:, GROUP_W:].astype(BF)


def _mla_cache(c_ckv, c_kr_pad, w_kvup_p):
    blk = pl.BlockSpec((1, 1, PAST_LEN, GROUP_W), lambda b, l: (b, l, 0, 0))
    shp = jax.ShapeDtypeStruct((DEC_BATCH, DEPTH, PAST_LEN, GROUP_W), BF)
    return pl.pallas_call(
        _mla_cache_kernel,
        grid=(DEC_BATCH, DEPTH),
        in_specs=[pl.BlockSpec((1, 1, PAST_LEN, MLA_KV_LORA), lambda b, l: (b, l, 0, 0)),
                  pl.BlockSpec((1, 1, PAST_LEN, 128), lambda b, l: (b, l, 0, 0)),
                  pl.BlockSpec((1, MLA_KV_LORA, 2 * GROUP_W), lambda b, l: (l, 0, 0))],
        out_specs=[blk, blk, blk],
        out_shape=[shp, shp, shp],
        compiler_params=_cparams("arbitrary", "arbitrary"),
        name="mla_cache",
    )(c_ckv, c_kr_pad, w_kvup_p)


N_DR = 2 * NA_KR - 1
N_DC = 2 * NA_KC - 1


def _na_bias_kernel(rpb_ref, o_ref, tp_ref):
    base = (pl.program_id(0) * N_HEADS + pl.program_id(1)) * (N_DR * N_DC)
    cq = lax.broadcasted_iota(jnp.int32, (GRID_W, 128), 0)
    lane = lax.broadcasted_iota(jnp.int32, (GRID_W, 128), 1)
    ck = lane & (GRID_W - 1)
    upper = lane >= GRID_W
    cstart = jnp.clip(cq - NA_KC // 2, 0, GRID_W - NA_KC)
    ok = (ck >= cstart) & (ck < cstart + NA_KC)

    j = lax.broadcasted_iota(jnp.int32, (8, 128), 1)
    dclip = jnp.clip(jnp.where(j < GRID_W, j, j - 128), -(NA_KC - 1), NA_KC - 1) + NA_KC - 1

    def toeplitz(dr):
        u = jnp.zeros((8, 128), F32)
        for d in range(N_DC):
            u = jnp.where(dclip == d, rpb_ref[base + dr * N_DC + d], u)
        rows = jnp.broadcast_to(u[0:1], (GRID_W, 128))
        return pltpu.roll(rows, 0, 1, stride=1, stride_axis=0)

    tabs = [toeplitz(dr) for dr in range(N_DR)]
    for dr in range(N_DR - 1):
        val = jnp.where(upper, pltpu.roll(tabs[dr + 1], GRID_W, 1), tabs[dr])
        tp_ref[dr] = jnp.where(ok, val, -jnp.inf)
    for dr0 in range(NA_KR):
        o_ref[0, dr0] = jnp.concatenate([tp_ref[dr0 + 2 * m] for m in range(NA_KR // 2)], axis=1)


def _na_bias(rpb):
    return pl.pallas_call(
        _na_bias_kernel,
        grid=(DEPTH, N_HEADS),
        in_specs=[pl.BlockSpec(memory_space=pltpu.SMEM)],
        out_specs=pl.BlockSpec((1, NA_KR, GRID_W, NA_LOCAL), lambda l, h: (l, 0, h, 0)),
        out_shape=jax.ShapeDtypeStruct((DEPTH, NA_KR, N_HEADS * GRID_W, NA_LOCAL), F32),
        scratch_shapes=[pltpu.VMEM((N_DR - 1, GRID_W, 128), F32)],
        compiler_params=_cparams("arbitrary", "arbitrary"),
        name="na_bias",
    )(rpb.reshape(-1))


def _ctx_kernel(lam_init, n_prev, *refs):
    (x_ref, mod_ref, gmix_ref, win_ref, gq_ref, wqup_ref, gkv_ref, wkv_ref,
     lamv_ref, gsub_ref, dw_ref, cb_ref, lng_ref, lnb_ref) = refs[:N_CTX_IN]
    (mixed_ref, dak_ref, dav_ref, ckv_ref, kr_ref, nak_ref, nav_ref,
     proj_ref, gpad_ref, zsh_ref, wbf_ref) = refs[N_CTX_IN + n_prev:]

    @pl.when(pl.program_id(0) == 0)
    def _():
        _stage_w_in(win_ref, wbf_ref)

    x = x_ref[...]
    sh1 = mod_ref[0:1, 0:D_MODEL]
    sc1 = mod_ref[0:1, D_MODEL:2 * D_MODEL]
    h = ((_rms(x) * gmix_ref[...]) * (1.0 + sc1) + sh1).astype(BF)
    proj_ref[:, 0:P_SPLIT] = _dot(h, wbf_ref[:, 0:P_SPLIT])
    proj_ref[:, P_SPLIT:P_TOT] = _dot(h, wbf_ref[:, P_SPLIT:P_TOT])

    for hh in range(N_HEADS):
        lo, hi = hh * HEAD_DIM, (hh + 1) * HEAD_DIM
        dak_ref[0, hh] = proj_ref[:, P_DAK + lo:P_DAK + hi]
        dav_ref[0, hh] = proj_ref[:, P_DAV + lo:P_DAV + hi]
        nak_ref[0, hh] = proj_ref[:, P_NAK + lo:P_NAK + hi]
        nav_ref[0, hh] = proj_ref[:, P_NAV + lo:P_NAV + hi]

    lane = _lane_ids(GROUP_W)

    lam = _diff_lambda(lamv_ref, lam_init)

    qa = proj_ref[:, P_DAQ:P_DAQ + GROUP_W] * (DA_SCALE * LOG2E)
    ka = proj_ref[:, P_DAK:P_DAK + GROUP_W].astype(BF)
    va = proj_ref[:, P_DAV:P_DAV + GROUP_W].astype(BF)
    qn = proj_ref[:, P_NAQ:P_NAQ + GROUP_W] * (NA_SCALE * LOG2E)
    kn2 = proj_ref[:, P_NAK:P_NAK + GROUP_W].astype(BF)
    vn = proj_ref[:, P_NAV:P_NAV + GROUP_W].astype(BF)
    mla = {}

    def mla_prep():
        qd = _rms(proj_ref[:, P_QD:P_QD + GROUP_W]) * gq_ref[...]
        ckv = _rms(proj_ref[:, P_KVD:P_KVD + MLA_KV_LORA]) * gkv_ref[...]
        ckv_ref[0] = ckv
        kr_pad = proj_ref[:, P_KR:P_KR + 128]
        kr_ref[0] = kr_pad[:, 0:MLA_ROPE]
        mla["q"] = (_dot(qd.astype(BF), wqup_ref[...]) * (MLA_SCALE * LOG2E)).astype(BF)
        kvm = _dot(ckv.astype(BF), wkv_ref[...])
        kn = kvm[:, :GROUP_W].astype(BF)
        krb = kr_pad.astype(BF)
        mla["k"] = (jnp.concatenate([kn[:, :128], krb], axis=1), jnp.concatenate([kn[:, 128:], krb], axis=1))
        mla["v"] = kvm[:, GROUP_W:].astype(BF)

    def conv():
        g = proj_ref[:, P_CONV:P_CONV + GROUP_W] * jax.nn.sigmoid(proj_ref[:, P_CONV + GROUP_W:P_TOT])
        gpad_ref[0:16] = jnp.zeros((16, GROUP_W), F32)
        gpad_ref[16 + SEQ:32 + SEQ] = jnp.zeros((16, GROUP_W), F32)
        gpad_ref[16:16 + SEQ] = g
        return _conv_ln_silu(gpad_ref, zsh_ref, SEQ, dw_ref, cb_ref, lng_ref, lnb_ref)

    def scores(i):
        if i < N_HEADS:
            return (_qk(_mask_heads(qa, lane, DA_QK, 2 * i), ka),
                    _qk(_mask_heads(qa, lane, DA_QK, 2 * i + 1), ka))
        if i < 2 * N_HEADS:
            hh = i - N_HEADS
            return (_qk(mla["q"][:, hh * GROUP_W:(hh + 1) * GROUP_W], mla["k"][hh // 2]),)
        return (_qk(_mask_heads(qn, lane, HEAD_DIM, i - 2 * N_HEADS), kn2),)

    outs = [jnp.zeros((SEQ, GROUP_W), F32)] * 3
    s_next = scores(0)
    o_conv = None
    for i in range(3 * N_HEADS):
        s_cur = s_next
        if i + 1 < 3 * N_HEADS:
            s_next = scores(i + 1)
        if i == 0:
            mla_prep()
        if i == N_HEADS - 1:
            o_conv = conv()
        grp, hh = divmod(i, N_HEADS)
        if grp == 0:
            e1, r1 = _softmax2_parts(s_cur[0])
            e2, r2 = _softmax2_parts(s_cur[1])
            o = _dot((e1 * r1 - e2 * (lam * r2)).astype(BF), va)
        else:
            e, r = _softmax2_parts(s_cur[0])
            o = _dot(e.astype(BF), mla["v"] if grp == 1 else vn) * r
        outs[grp] = jnp.where(_group_of(lane, HEAD_DIM) == hh, o, outs[grp])
    o_da = _head_rms(outs[0]) * gsub_ref[...] * (1.0 - lam_init)
    o_mla, o_na = outs[1], outs[2]

    mixed_ref[...] = jnp.concatenate([o_da, o_mla, o_na, o_conv], axis=1).astype(BF)


def _ctx_layer(layer, x, prm, prev):
    lam_init = 0.8 - 0.6 * math.exp(-0.3 * layer)
    head_blk = pl.BlockSpec((1, None, N_HEADS, SEQ, HEAD_DIM), lambda b: (b, layer, 0, 0, 0))
    head_shp = jax.ShapeDtypeStruct((BATCH, DEPTH, N_HEADS, SEQ, HEAD_DIM), F32)
    in_specs = [pl.BlockSpec((SEQ, D_MODEL), lambda b: (b, 0)),
                _layer_spec(layer, (16, N_MOD * D_MODEL)),
                _layer_spec(layer, (1, D_MODEL)),
                _layer_spec(layer, (D_MODEL, IN_COLS)),
                _layer_spec(layer, (1, GROUP_W)),
                _layer_spec(layer, (GROUP_W, N_HEADS * GROUP_W)),
                _layer_spec(layer, (1, MLA_KV_LORA)),
                _layer_spec(layer, (MLA_KV_LORA, 2 * GROUP_W)),
                _layer_spec(layer, (4, DA_QK)),
                _layer_spec(layer, (1, GROUP_W)),
                _layer_spec(layer, (32, GROUP_W)),
                _layer_spec(layer, (1, GROUP_W)),
                _layer_spec(layer, (1, GROUP_W)),
                _layer_spec(layer, (1, GROUP_W))] + [pl.BlockSpec(memory_space=pl.ANY)] * len(prev)
    out_specs = [pl.BlockSpec((SEQ, D_MODEL), lambda b: (b, 0)),
                 head_blk, head_blk,
                 pl.BlockSpec((1, None, SEQ, MLA_KV_LORA), lambda b: (b, layer, 0, 0)),
                 pl.BlockSpec((1, None, SEQ, MLA_ROPE), lambda b: (b, layer, 0, 0)),
                 head_blk, head_blk]
    out_shape = [jax.ShapeDtypeStruct((BATCH * SEQ, D_MODEL), BF),
                 head_shp, head_shp,
                 jax.ShapeDtypeStruct((BATCH, DEPTH, SEQ, MLA_KV_LORA), F32),
                 jax.ShapeDtypeStruct((BATCH, DEPTH, SEQ, MLA_ROPE), F32),
                 head_shp, head_shp]
    n_in = len(in_specs) - len(prev)
    return pl.pallas_call(
        functools.partial(_ctx_kernel, lam_init, len(prev)),
        grid=(BATCH,),
        in_specs=in_specs,
        out_specs=out_specs,
        out_shape=out_shape,
        input_output_aliases={n_in + k: 1 + k for k in range(len(prev))},
        scratch_shapes=[pltpu.VMEM((SEQ, P_TOT), F32), pltpu.VMEM((SEQ + 32, GROUP_W), F32),
                        pltpu.VMEM((7, SEQ + 8, GROUP_W), F32), pltpu.VMEM((D_MODEL, P_TOT), BF)],
        compiler_params=_cparams("arbitrary"),
        name=f"ctx_layer{layer}",
    )(x, prm["mod"], prm["g_mix"], prm["w_in"], prm["g_q"], prm["w_qup"], prm["g_kv"], prm["w_kvup"],
      prm["lamv"], prm["g_sub"], prm["dw"], prm["cb"], prm["ln_g"], prm["ln_b"], *prev)


def _lat_proj_kernel(x_ref, mod_ref, gmix_ref, win_ref, gq_ref, wqup_ref, gkv_ref, wkv_ref,
                     cos_ref, sin_ref, cdak_ref, cdav_ref, cka_ref, ckb_ref, cmv_ref, cnak_ref, cnav_ref,
                     daq_ref, mq_ref, naq_ref, g_ref,
                     dak_ref, dav_ref, ka_ref, kb_ref, mv_ref, nak_ref, nav_ref, wbf_ref):
    b = pl.program_id(0)
    j = pl.program_id(1)

    @pl.when((b == 0) & (j == 0))
    def _():
        _stage_w_in(win_ref, wbf_ref)

    @pl.when(j < N_QT)
    def _():
        x = x_ref[...]
        sh1 = mod_ref[pl.ds(1 + b, 1), 0:D_MODEL]
        sc1 = mod_ref[pl.ds(1 + b, 1), D_MODEL:2 * D_MODEL]
        h = (_rms(x) * gmix_ref[...]) * (1.0 + sc1) + sh1
        hb = h.astype(BF)
        proj = jnp.concatenate([_dot(hb, wbf_ref[:, 0:P_SPLIT]), _dot(hb, wbf_ref[:, P_SPLIT:P_TOT])], axis=1)
        cos = cos_ref[...]
        sin = sin_ref[...]

        def rope2(z):
            return jnp.concatenate([_rope(z[:, :128], cos[:, :128], sin[:, :128]),
                                    _rope(z[:, 128:], cos[:, 128:], sin[:, 128:])], axis=1)

        daq_ref[0] = (rope2(proj[:, P_DAQ:P_DAQ + GROUP_W]) * (DA_SCALE * LOG2E)).astype(BF)
        dak_ref[0] = rope2(proj[:, P_DAK:P_DAK + GROUP_W]).astype(BF)
        dav_ref[0] = proj[:, P_DAV:P_DAV + GROUP_W].astype(BF)

        qd = _rms(proj[:, P_QD:P_QD + GROUP_W]) * gq_ref[...]
        qm = _dot(qd.astype(BF), wqup_ref[...])
        for hh in range(N_HEADS):
            nope = qm[:, hh * GROUP_W:hh * GROUP_W + 128]
            rope = _rope(qm[:, hh * GROUP_W + 128:(hh + 1) * GROUP_W], cos[:, :128], sin[:, :128])
            mq_ref[0, hh] = (jnp.concatenate([nope, rope], axis=1) * (MLA_SCALE * LOG2E)).astype(BF)
        ckv = _rms(proj[:, P_KVD:P_KVD + MLA_KV_LORA]) * gkv_ref[...]
        kvm = _dot(ckv.astype(BF), wkv_ref[...])
        kn = kvm[:, :GROUP_W].astype(BF)
        krb = _rope(proj[:, P_KR:P_KR + 128], cos[:, :128], sin[:, :128]).astype(BF)
        ka_ref[0] = jnp.concatenate([kn[:, :128], krb], axis=1)
        kb_ref[0] = jnp.concatenate([kn[:, 128:], krb], axis=1)
        mv_ref[0] = kvm[:, GROUP_W:].astype(BF)

        naq_ref[0] = (proj[:, P_NAQ:P_NAQ + GROUP_W] * NA_SCALE).astype(BF)
        nak_ref[0] = proj[:, P_NAK:P_NAK + GROUP_W].astype(BF)
        nav_ref[0] = proj[:, P_NAV:P_NAV + GROUP_W].astype(BF)
        g_ref[0] = proj[:, P_CONV:P_CONV + GROUP_W] * jax.nn.sigmoid(proj[:, P_CONV + GROUP_W:P_TOT])

    @pl.when(j == N_QT)
    def _():
        dak_ref[0] = cdak_ref[0]
        dav_ref[0] = cdav_ref[0]
        ka_ref[0] = cka_ref[0]
        kb_ref[0] = ckb_ref[0]
        mv_ref[0] = cmv_ref[0]
        nak_ref[0] = cnak_ref[0]
        nav_ref[0] = cnav_ref[0]


def _lat_proj(layer, x, prm, cos_t, sin_t, caches):
    jq = lambda j: jnp.minimum(j, N_QT - 1)
    cache_spec = pl.BlockSpec((1, None, PAST_LEN, GROUP_W), lambda b, j: (b, layer, 0, 0))
    q_spec = pl.BlockSpec((1, TQ, GROUP_W), lambda b, j: (b, jq(j), 0))
    k_spec = pl.BlockSpec((1, TQ, GROUP_W), lambda b, j: (b, j, 0))
    in_specs = [pl.BlockSpec((TQ, D_MODEL), lambda b, j: (b * N_QT + jq(j), 0)),
                _layer_spec(layer, (16, N_MOD * D_MODEL)),
                _layer_spec(layer, (1, D_MODEL)),
                _layer_spec(layer, (D_MODEL, IN_COLS)),
                _layer_spec(layer, (1, GROUP_W)),
                _layer_spec(layer, (GROUP_W, N_HEADS * GROUP_W)),
                _layer_spec(layer, (1, MLA_KV_LORA)),
                _layer_spec(layer, (MLA_KV_LORA, 2 * GROUP_W)),
                pl.BlockSpec((TQ, GROUP_W), lambda b, j: (jq(j), 0)),
                pl.BlockSpec((TQ, GROUP_W), lambda b, j: (jq(j), 0))] + [cache_spec] * 7
    q_shp = jax.ShapeDtypeStruct((DEC_BATCH, DEC_SEQ, GROUP_W), BF)
    k_shp = jax.ShapeDtypeStruct((DEC_BATCH, KEYS, GROUP_W), BF)
    out_specs = [q_spec,
                 pl.BlockSpec((1, N_HEADS, TQ, GROUP_W), lambda b, j: (b, 0, jq(j), 0)),
                 q_spec, q_spec] + [k_spec] * 7
    out_shape = [q_shp,
                 jax.ShapeDtypeStruct((DEC_BATCH, N_HEADS, DEC_SEQ, GROUP_W), BF),
                 q_shp,
                 jax.ShapeDtypeStruct((DEC_BATCH, DEC_SEQ, GROUP_W), F32)] + [k_shp] * 7
    return pl.pallas_call(
        _lat_proj_kernel,
        grid=(DEC_BATCH, N_QT + 1),
        in_specs=in_specs,
        out_specs=out_specs,
        out_shape=out_shape,
        scratch_shapes=[pltpu.VMEM((D_MODEL, P_TOT), BF)],
        compiler_params=_cparams("arbitrary", "arbitrary"),
        name=f"lat_proj{layer}",
    )(x, prm["mod"], prm["g_mix"], prm["w_in"], prm["g_q"], prm["w_qup"], prm["g_kv"], prm["w_kvup"],
      cos_t, sin_t, *caches)


def _lat_attn_kernel(lam_init, daq_ref, mq_ref, naq_ref, g_ref,
                     dak_ref, dav_ref, ka_ref, kb_ref, mv_ref, nak_ref, nav_ref,
                     nab_ref, lamv_ref, gsub_ref, dw_ref, cb_ref, lng_ref, lnb_ref,
                     mixed_ref, gpad_ref, zsh_ref, kt_ref):
    t = pl.program_id(1)
    lane = _lane_ids(GROUP_W)

    @pl.when(t == 0)
    def _():
        kt_ref[0] = dak_ref[0].T
        kt_ref[1] = ka_ref[0].T
        kt_ref[2] = kb_ref[0].T

    lam = _diff_lambda(lamv_ref, lam_init)
    qa = daq_ref[0].astype(F32)
    va = dav_ref[0]
    ka_t = kt_ref[0]
    vm = mv_ref[0]

    kc = nak_ref[0, DEC_SEQ:KEYS, :]
    vc = nav_ref[0, DEC_SEQ:KEYS, :]
    n_items = 2 * N_HEADS + ROWS_PER_TILE

    def na_window(j):
        r = t * ROWS_PER_TILE + j
        start = jnp.clip(r - NA_KR // 2, 0, N_ROWS - NA_KR)
        return start - r + NA_KR - 1, pl.multiple_of(start * GRID_W, GRID_W)

    def scores(i):
        if i < N_HEADS:
            return (_dot(_mask_heads(qa, lane, DA_QK, 2 * i), ka_t),
                    _dot(_mask_heads(qa, lane, DA_QK, 2 * i + 1), ka_t))
        if i < 2 * N_HEADS:
            hh = i - N_HEADS
            return (_dot(mq_ref[0, hh], kt_ref[1 + hh // 2]),)
        j = i - 2 * N_HEADS
        dr0, koff = na_window(j)
        qrow = naq_ref[0, j * GRID_W:(j + 1) * GRID_W, :].astype(F32)
        q4 = jnp.concatenate([_mask_heads(qrow, lane, HEAD_DIM, hh) for hh in range(N_HEADS)], axis=0)
        return (_qk(q4, nak_ref[0, pl.ds(koff, NA_LOCAL), :]) + nab_ref[dr0], _qk(q4, kc))

    o_da = jnp.zeros((TQ, GROUP_W), F32)
    o_mla = jnp.zeros((TQ, GROUP_W), F32)
    na_rows = []
    s_next = scores(0)

    base = pl.multiple_of(t * TQ, TQ)
    gpad_ref[16:16 + TQ] = g_ref[0, pl.ds(base, TQ), :]
    lo = g_ref[0, pl.ds(pl.multiple_of(jnp.maximum(base - 16, 0), 16), 16), :]
    hi = g_ref[0, pl.ds(pl.multiple_of(jnp.minimum(base + TQ, DEC_SEQ - 16), 16), 16), :]
    gpad_ref[0:16] = jnp.where(t > 0, lo, 0.0)
    gpad_ref[16 + TQ:32 + TQ] = jnp.where(t < N_QT - 1, hi, 0.0)
    o_conv = _conv_ln_silu(gpad_ref, zsh_ref, TQ, dw_ref, cb_ref, lng_ref, lnb_ref)

    for i in range(n_items):
        s_cur = s_next
        if i + 1 < n_items:
            s_next = scores(i + 1)
        if i < N_HEADS:
            e1, r1 = _softmax2_parts(s_cur[0])
            e2, r2 = _softmax2_parts(s_cur[1])
            p = (e1 * r1 - e2 * (lam * r2)).astype(BF)
            o_da = jnp.where(_group_of(lane, HEAD_DIM) == i, _dot(p, va), o_da)
        elif i < 2 * N_HEADS:
            e, r = _softmax2_parts(s_cur[0])
            o_mla = jnp.where(_group_of(lane, HEAD_DIM) == i - N_HEADS, _dot(e.astype(BF), vm) * r, o_mla)
        else:
            _, koff = na_window(i - 2 * N_HEADS)
            s_loc, s_ctx = s_cur
            m = jnp.maximum(jnp.max(s_loc, axis=-1, keepdims=True), jnp.max(s_ctx, axis=-1, keepdims=True))
            e_loc = jnp.exp(s_loc - m)
            e_ctx = jnp.exp(s_ctx - m)
            den = jnp.sum(e_loc, axis=-1, keepdims=True) + jnp.sum(e_ctx, axis=-1, keepdims=True)
            o4 = (_dot(e_loc.astype(BF), nav_ref[0, pl.ds(koff, NA_LOCAL), :])
                  + _dot(e_ctx.astype(BF), vc)) * (1.0 / den)
            o_row = o4[0:GRID_W]
            for hh in range(1, N_HEADS):
                o_row = jnp.where(_group_of(lane, HEAD_DIM) == hh, o4[hh * GRID_W:(hh + 1) * GRID_W], o_row)
            na_rows.append(o_row)
    o_da = _head_rms(o_da) * gsub_ref[...] * (1.0 - lam_init)
    o_na = jnp.concatenate(na_rows, axis=0)

    mixed_ref[...] = jnp.concatenate([o_da, o_mla, o_na, o_conv], axis=1).astype(BF)


def _lat_attn(layer, proj_outs, nab, prm):
    lam_init = 0.8 - 0.6 * math.exp(-0.3 * layer)
    q_spec = pl.BlockSpec((1, TQ, GROUP_W), lambda b, t: (b, t, 0))
    full_k = pl.BlockSpec((1, KEYS, GROUP_W), lambda b, t: (b, 0, 0))
    in_specs = [q_spec,
                pl.BlockSpec((1, N_HEADS, TQ, GROUP_W), lambda b, t: (b, 0, t, 0)),
                q_spec,
                pl.BlockSpec((1, DEC_SEQ, GROUP_W), lambda b, t: (b, 0, 0))] + [full_k] * 7 + [
                _layer_spec(layer, (NA_KR, N_HEADS * GRID_W, NA_LOCAL)),
                _layer_spec(layer, (4, DA_QK)),
                _layer_spec(layer, (1, GROUP_W)),
                _layer_spec(layer, (32, GROUP_W)),
                _layer_spec(layer, (1, GROUP_W)),
                _layer_spec(layer, (1, GROUP_W)),
                _layer_spec(layer, (1, GROUP_W))]
    return pl.pallas_call(
        functools.partial(_lat_attn_kernel, lam_init),
        grid=(DEC_BATCH, N_QT),
        in_specs=in_specs,
        out_specs=pl.BlockSpec((TQ, D_MODEL), lambda b, t: (b * N_QT + t, 0)),
        out_shape=jax.ShapeDtypeStruct((DEC_BATCH * DEC_SEQ, D_MODEL), BF),
        scratch_shapes=[pltpu.VMEM((TQ + 32, GROUP_W), F32), pltpu.VMEM((7, TQ + 8, GROUP_W), F32),
                        pltpu.VMEM((3, GROUP_W, KEYS), BF)],
        compiler_params=_cparams("arbitrary", "arbitrary"),
        name=f"lat_attn{layer}",
    )(*proj_outs, nab, prm["lamv"], prm["g_sub"], prm["dw"], prm["cb"], prm["ln_g"], prm["ln_b"])


def _out_ffn_kernel(final, tiles_per_mod, mod_base, x_ref, mx_ref, mod_ref, wout_ref, gff_ref,
                    w1_ref, w2_ref, gfin_ref, o_ref):
    row = mod_base + pl.program_id(0) // tiles_per_mod
    g1 = mod_ref[pl.ds(row, 1), 2 * D_MODEL:3 * D_MODEL]
    sh2 = mod_ref[pl.ds(row, 1), 3 * D_MODEL:4 * D_MODEL]
    sc2 = mod_ref[pl.ds(row, 1), 4 * D_MODEL:5 * D_MODEL]
    g2 = mod_ref[pl.ds(row, 1), 5 * D_MODEL:6 * D_MODEL]
    x1 = x_ref[...] + g1 * _dot(mx_ref[...], wout_ref[...])
    h2 = ((_rms(x1) * gff_ref[...]) * (1.0 + sc2) + sh2).astype(BF)
    acc = jnp.zeros((TM_FFN, D_MODEL), F32)
    for c in range(D_FF // D_MODEL):
        a = jnp.maximum(_dot(h2, w1_ref[:, c * D_MODEL:(c + 1) * D_MODEL]), 0.0)
        acc = acc + _dot((a * a).astype(BF), w2_ref[c * D_MODEL:(c + 1) * D_MODEL, :])
    x2 = x1 + g2 * acc
    o_ref[...] = _rms(x2) * gfin_ref[...] if final else x2


def _out_ffn(name, layer, tiles_per_mod, mod_base, x, mixed, prm, g_final):
    n = x.shape[0]
    tile = pl.BlockSpec((TM_FFN, D_MODEL), lambda i: (i, 0))
    return pl.pallas_call(
        functools.partial(_out_ffn_kernel, layer == DEPTH - 1, tiles_per_mod, mod_base),
        grid=(n // TM_FFN,),
        in_specs=[tile, tile,
                  _layer_spec(layer, (16, N_MOD * D_MODEL)),
                  _layer_spec(layer, (D_MODEL, D_MODEL)),
                  _layer_spec(layer, (1, D_MODEL)),
                  _layer_spec(layer, (D_MODEL, D_FF)),
                  _layer_spec(layer, (D_FF, D_MODEL)),
                  _const_spec((1, D_MODEL))],
        out_specs=tile,
        out_shape=jax.ShapeDtypeStruct((n, D_MODEL), F32),
        compiler_params=_cparams("arbitrary"),
        name=name,
    )(x, mixed, prm["mod"], prm["w_out"], prm["g_ff"], prm["w_ff1"], prm["w_ff2"], g_final)


def _rope_tables():
    t = np.arange(DEC_SEQ)
    rows = (t // GRID_W).astype(np.float32)
    cols = (t % GRID_W).astype(np.float32)
    c = np.arange(GROUP_W) % 32
    freqs = np.float32(ROPE_BASE) ** (-np.arange(8, dtype=np.float32) * np.float32(2.0) / np.float32(16))
    pos = np.where((c < 16)[None, :], rows[:, None], cols[:, None]).astype(np.float32)
    ang = (pos * freqs[(c % 16) % 8][None, :]).astype(np.float32)
    first = ((c % 16) < 8)[None, :]
    cos = np.cos(ang).astype(np.float32)
    sin = np.sin(ang).astype(np.float32)
    return jnp.asarray(cos), jnp.asarray(np.where(first, -sin, sin))


def _qup_gather_index():
    idx = np.full((N_HEADS * GROUP_W,), -1, np.int64)
    for h in range(N_HEADS):
        src = h * (MLA_NOPE + MLA_ROPE)
        dst = h * GROUP_W + (h % 2) * MLA_NOPE
        idx[dst:dst + MLA_NOPE] = np.arange(src, src + MLA_NOPE)
        idx[h * GROUP_W + 128:h * GROUP_W + 128 + MLA_ROPE] = np.arange(src + MLA_NOPE, src + MLA_NOPE + MLA_ROPE)
    return idx


def _kvup_perm():
    k = [h * 128 + d for h in range(N_HEADS) for d in range(MLA_NOPE)]
    v = [h * 128 + MLA_NOPE + d for h in range(N_HEADS) for d in range(HEAD_DIM)]
    return np.asarray(k + v)


def _heads_to_lanes(c):
    b, l, h, s, d = c.shape
    return c.transpose(0, 1, 3, 2, 4).reshape(b, l, s, h * d).astype(BF)


def kernel(x_prompt, x_sample, c, cache_da_k, cache_da_v, cache_mla_ckv, cache_mla_krope, cache_na_k, cache_na_v, c_ctx, w_mod, b_mod, g_norm_mix, g_norm_ff, w_in, da_lambda_q1, da_lambda_k1, da_lambda_q2, da_lambda_k2, g_da_subln, g_mla_q, w_mla_qup, g_mla_kv, w_mla_kvup, na_rpb, conv_dw, conv_b, conv_ln_g, conv_ln_b, w_out, w_ff1, w_ff2, g_final):
    qidx = _qup_gather_index()
    w_qup_e = jnp.where(jnp.asarray(qidx >= 0)[None, None, :],
                        jnp.take(w_mla_qup, jnp.asarray(np.maximum(qidx, 0)), axis=-1), 0.0).astype(BF)
    w_kvup_p = jnp.take(w_mla_kvup, jnp.asarray(_kvup_perm()), axis=-1).astype(BF)
    r3 = lambda a: a.reshape(DEPTH, 1, -1)
    prm = dict(
        g_mix=r3(g_norm_mix), w_in=w_in, g_q=r3(g_mla_q), w_qup=w_qup_e, g_kv=r3(g_mla_kv), w_kvup=w_kvup_p,
        lamv=jnp.stack([da_lambda_q1, da_lambda_k1, da_lambda_q2, da_lambda_k2], axis=1),
        g_sub=r3(jnp.tile(g_da_subln, (1, N_HEADS))),
        dw=jnp.concatenate([conv_dw, jnp.zeros((DEPTH, 1, GROUP_W), F32)], axis=1),
        cb=r3(conv_b), ln_g=r3(conv_ln_g), ln_b=r3(conv_ln_b),
        w_out=w_out.astype(BF), g_ff=r3(g_norm_ff), w_ff1=w_ff1.astype(BF), w_ff2=w_ff2.astype(BF))
    cv = jnp.concatenate([c_ctx[None, :], c, jnp.zeros((16 - 1 - DEC_BATCH, D_MODEL), F32)], axis=0)
    cos_t, sin_t = _rope_tables()
    g_final2 = g_final.reshape(1, D_MODEL)

    prm["mod"] = _modulation(cv, w_mod, b_mod)
    c_kr_pad = jnp.pad(cache_mla_krope, ((0, 0), (0, 0), (0, 0), (0, KR_PAD)))
    cka, ckb, cmv = _mla_cache(cache_mla_ckv, c_kr_pad, w_kvup_p)
    caches = (_heads_to_lanes(cache_da_k), _heads_to_lanes(cache_da_v), cka, ckb, cmv,
              _heads_to_lanes(cache_na_k), _heads_to_lanes(cache_na_v))
    nab = _na_bias(na_rpb)

    xp = x_prompt.reshape(BATCH * SEQ, D_MODEL)
    xs = x_sample.reshape(DEC_BATCH * DEC_SEQ, D_MODEL)
    new_ctx = ()
    for l in range(DEPTH):
        mixed_p, *new_ctx = _ctx_layer(l, xp, prm, tuple(new_ctx))
        xp = _out_ffn(f"ctx_ffn{l}", l, BATCH * SEQ // TM_FFN, 0, xp, mixed_p, prm, g_final2)
        proj_outs = _lat_proj(l, xs, prm, cos_t, sin_t, caches)
        mixed_s = _lat_attn(l, proj_outs, nab, prm)
        xs = _out_ffn(f"lat_ffn{l}", l, DEC_SEQ // TM_FFN, 1, xs, mixed_s, prm, g_final2)

    y_prompt = xp.reshape(BATCH, SEQ, D_MODEL)
    y_sample = xs.reshape(DEC_BATCH, DEC_SEQ, D_MODEL)
    return (y_prompt, y_sample) + tuple(new_ctx)
```

```python
import functools
import math

import numpy as np
import jax
import jax.numpy as jnp
from jax import lax
from jax.experimental import pallas as pl
from jax.experimental.pallas import tpu as pltpu

F32 = jnp.float32
BF = jnp.bfloat16

D_MODEL = 1024
BATCH = 16
SEQ = 256
DEPTH = 2
DEC_BATCH = 2
DEC_SEQ = 2048
PAST_LEN = 256
GRID_W = 64
GROUP_W = 256
HEAD_DIM = 64
N_HEADS = 4
DA_QK = 32
MLA_NOPE = 64
MLA_ROPE = 32
MLA_KV_LORA = 128
NA_KR = 8
NA_KC = 16
CONV_W = 31
D_FF = 4096
ROPE_BASE = 10000.0
EPS = 1e-6
N_MOD = 6
IN_COLS = 2464

P_DAQ, P_DAK, P_DAV, P_QD, P_KVD, P_KR = 0, 256, 512, 768, 1024, 1152
P_NAQ, P_NAK, P_NAV, P_CONV, P_TOT = 1280, 1536, 1792, 2048, 2560
P_SPLIT = P_NAQ
N_CTX_IN = 14
KR_ORIG_END = 1184
KR_PAD = 128 - MLA_ROPE

DA_SCALE = DA_QK ** -0.5
MLA_SCALE = (MLA_NOPE + MLA_ROPE) ** -0.5
NA_SCALE = HEAD_DIM ** -0.5
LOG2E = math.log2(math.e)

TQ = 256
N_QT = DEC_SEQ // TQ
KEYS = DEC_SEQ + PAST_LEN
ROWS_PER_TILE = TQ // GRID_W
N_ROWS = DEC_SEQ // GRID_W
NA_LOCAL = NA_KR * GRID_W
TM_FFN = 512
VMEM_LIMIT = 58 * 1024 * 1024

NT_DIMS = (((1,), (1,)), ((), ()))


def _cparams(*sem):
    return pltpu.CompilerParams(dimension_semantics=sem, vmem_limit_bytes=VMEM_LIMIT)


def _const_spec(shape):
    nd = len(shape)
    return pl.BlockSpec(shape, lambda *_: (0,) * nd, pipeline_mode=pl.Buffered(1))


def _layer_spec(layer, shape):
    nd = len(shape)
    return pl.BlockSpec((None,) + tuple(shape), lambda *_: (layer,) + (0,) * nd, pipeline_mode=pl.Buffered(1))


def _rms(x):
    return x * lax.rsqrt(jnp.mean(x * x, axis=-1, keepdims=True) + EPS)


def _dot(a, b):
    return jnp.dot(a, b, preferred_element_type=F32)


def _qk(q, k):
    return lax.dot_general(q, k, NT_DIMS, preferred_element_type=F32)


def _softmax_parts(s):
    m = jnp.max(s, axis=-1, keepdims=True)
    e = jnp.exp(s - m)
    return e, 1.0 / jnp.sum(e, axis=-1, keepdims=True)


def _softmax2_parts(s):
    m = jnp.max(s, axis=-1, keepdims=True)
    e = jnp.exp2(s - m)
    return e, 1.0 / jnp.sum(e, axis=-1, keepdims=True)


def _lane_ids(width):
    return lax.broadcasted_iota(jnp.int32, (1, width), 1)


def _diff_lambda(lamv_ref, lam_init):
    v = lamv_ref[...]
    a = jnp.exp(jnp.sum(v[0:1] * v[1:2], axis=-1, keepdims=True))
    b = jnp.exp(jnp.sum(v[2:3] * v[3:4], axis=-1, keepdims=True))
    return a - b + lam_init


def _group_of(lane, group):
    return lane >> (group.bit_length() - 1)


def _head_rms(x):
    r = _group_of(lax.broadcasted_iota(jnp.int32, (GROUP_W, GROUP_W), 0), HEAD_DIM)
    c = _group_of(lax.broadcasted_iota(jnp.int32, (GROUP_W, GROUP_W), 1), HEAD_DIM)
    ones_bd = jnp.where(r == c, 1.0, 0.0).astype(BF)
    sq = x * x
    hi = sq.astype(BF)
    lo = (sq - hi.astype(F32)).astype(BF)
    ss = _dot(hi, ones_bd) + _dot(lo, ones_bd)
    return x * lax.rsqrt(ss * (1.0 / HEAD_DIM) + EPS)


def _rope(z, cos, sin):
    lane = _lane_ids(z.shape[1])
    swapped = jnp.where((lane & 15) < 8, pltpu.roll(z, 120, 1), pltpu.roll(z, 8, 1))
    return z * cos + swapped * sin


def _stage_w_in(wint_ref, wbt_ref):
    wbt_ref[0:KR_ORIG_END, :] = wint_ref[0:KR_ORIG_END, :].astype(BF)
    wbt_ref[KR_ORIG_END:P_SPLIT, :] = jnp.zeros((P_SPLIT - KR_ORIG_END, D_MODEL), BF)
    wbt_ref[P_SPLIT:P_TOT, :] = wint_ref[KR_ORIG_END:IN_COLS, :].astype(BF)


def _mask_heads(qf, lane, group, idx):
    return jnp.where(_group_of(lane, group) == idx, qf, 0.0).astype(BF)


def _conv_ln_silu(gpad_ref, zsh_ref, n, dw_ref, cb_ref, lng_ref, lnb_ref):
    y = jnp.zeros((n, GROUP_W), F32) + cb_ref[...]
    for b in range(8):
        z = None
        for a in range(4):
            t = 8 * a + b - 1
            if 0 <= t < CONV_W:
                term = gpad_ref[pl.ds(8 * a, n + 8), :] * dw_ref[t:t + 1, :]
                z = term if z is None else z + term
        if b == 0:
            y = y + z[0:n]
        else:
            zsh_ref[b - 1] = z
            y = y + zsh_ref[b - 1, pl.ds(b, n), :]
    mu = jnp.mean(y, axis=-1, keepdims=True)
    yc = y - mu
    var = jnp.mean(yc * yc, axis=-1, keepdims=True)
    z = yc * lax.rsqrt(var + EPS) * lng_ref[...] + lnb_ref[...]
    return z * jax.nn.sigmoid(z)


def _mod_kernel(cv_ref, w_ref, b_ref, o_ref):
    c = cv_ref[...]
    a = (c * jax.nn.sigmoid(c)).astype(BF)
    o_ref[0] = _dot(a, w_ref[0].astype(BF)) + b_ref[0]


def _modulation(cv, w_mod, b_mod):
    nblk = 4
    wblk = N_MOD * D_MODEL // nblk
    return pl.pallas_call(
        _mod_kernel,
        grid=(DEPTH, nblk),
        in_specs=[pl.BlockSpec((16, D_MODEL), lambda l, j: (0, 0)),
                  pl.BlockSpec((1, D_MODEL, wblk), lambda l, j: (l, 0, j)),
                  pl.BlockSpec((1, 1, wblk), lambda l, j: (l, 0, j))],
        out_specs=pl.BlockSpec((1, 16, wblk), lambda l, j: (l, 0, j)),
        out_shape=jax.ShapeDtypeStruct((DEPTH, 16, N_MOD * D_MODEL), F32),
        compiler_params=_cparams("arbitrary", "arbitrary"),
        name="modulation",
    )(cv, w_mod, b_mod.reshape(DEPTH, 1, N_MOD * D_MODEL))


def _mla_cache_kernel(ckv_ref, kr_ref, w_ref, ka_ref, kb_ref, v_ref):
    kvm = _dot(ckv_ref[0, 0].astype(BF), w_ref[0])
    kr = kr_ref[0, 0].astype(BF)
    kn = kvm[:, :GROUP_W].astype(BF)
    ka_ref[0, 0] = jnp.concatenate([kn[:, :128], kr], axis=1)
    kb_ref[0, 0] = jnp.concatenate([kn[:, 128:], kr], axis=1)
    v_ref[0, 0] = kvm[:, GROUP_W:].astype(BF)


def _mla_cache(c_ckv, c_kr_pad, w_kvup_p):
    blk = pl.BlockSpec((1, 1, PAST_LEN, GROUP_W), lambda b, l: (b, l, 0, 0))
    shp = jax.ShapeDtypeStruct((DEC_BATCH, DEPTH, PAST_LEN, GROUP_W), BF)
    return pl.pallas_call(
        _mla_cache_kernel,
        grid=(DEC_BATCH, DEPTH),
        in_specs=[pl.BlockSpec((1, 1, PAST_LEN, MLA_KV_LORA), lambda b, l: (b, l, 0, 0)),
                  pl.BlockSpec((1, 1, PAST_LEN, 128), lambda b, l: (b, l, 0, 0)),
                  pl.BlockSpec((1, MLA_KV_LORA, 2 * GROUP_W), lambda b, l: (l, 0, 0))],
        out_specs=[blk, blk, blk],
        out_shape=[shp, shp, shp],
        compiler_params=_cparams("arbitrary", "arbitrary"),
        name="mla_cache",
    )(c_ckv, c_kr_pad, w_kvup_p)


N_DR = 2 * NA_KR - 1
N_DC = 2 * NA_KC - 1


def _na_bias_kernel(rpb_ref, o_ref, tp_ref):
    base = (pl.program_id(0) * N_HEADS + pl.program_id(1)) * (N_DR * N_DC)
    cq = lax.broadcasted_iota(jnp.int32, (GRID_W, 128), 0)
    lane = lax.broadcasted_iota(jnp.int32, (GRID_W, 128), 1)
    ck = lane & (GRID_W - 1)
    upper = lane >= GRID_W
    cstart = jnp.clip(cq - NA_KC // 2, 0, GRID_W - NA_KC)
    ok = (ck >= cstart) & (ck < cstart + NA_KC)

    j = lax.broadcasted_iota(jnp.int32, (8, 128), 1)
    dclip = jnp.clip(jnp.where(j < GRID_W, j, j - 128), -(NA_KC - 1), NA_KC - 1) + NA_KC - 1

    def toeplitz(dr):
        u = jnp.zeros((8, 128), F32)
        for d in range(N_DC):
            u = jnp.where(dclip == d, rpb_ref[base + dr * N_DC + d], u)
        rows = jnp.broadcast_to(u[0:1], (GRID_W, 128))
        return pltpu.roll(rows, 0, 1, stride=1, stride_axis=0)

    tabs = [toeplitz(dr) for dr in range(N_DR)]
    for dr in range(N_DR - 1):
        val = jnp.where(upper, pltpu.roll(tabs[dr + 1], GRID_W, 1), tabs[dr])
        tp_ref[dr] = jnp.where(ok, val, -jnp.inf)
    for dr0 in range(NA_KR):
        o_ref[0, dr0] = jnp.concatenate([tp_ref[dr0 + 2 * m] for m in range(NA_KR // 2)], axis=1)


def _na_bias(rpb):
    return pl.pallas_call(
        _na_bias_kernel,
        grid=(DEPTH, N_HEADS),
        in_specs=[pl.BlockSpec(memory_space=pltpu.SMEM)],
        out_specs=pl.BlockSpec((1, NA_KR, GRID_W, NA_LOCAL), lambda l, h: (l, 0, h, 0)),
        out_shape=jax.ShapeDtypeStruct((DEPTH, NA_KR, N_HEADS * GRID_W, NA_LOCAL), F32),
        scratch_shapes=[pltpu.VMEM((N_DR - 1, GRID_W, 128), F32)],
        compiler_params=_cparams("arbitrary", "arbitrary"),
        name="na_bias",
    )(rpb.reshape(-1))


def _ctx_kernel(lam_init, n_prev, *refs):
    (x_ref, mod_ref, gmix_ref, win_ref, gq_ref, wqup_ref, gkv_ref, wkv_ref,
     lamv_ref, gsub_ref, dw_ref, cb_ref, lng_ref, lnb_ref) = refs[:N_CTX_IN]
    (mixed_ref, dak_ref, dav_ref, ckv_ref, kr_ref, nak_ref, nav_ref,
     proj_ref, gpad_ref, zsh_ref, wbf_ref) = refs[N_CTX_IN + n_prev:]

    @pl.when(pl.program_id(0) == 0)
    def _():
        _stage_w_in(win_ref, wbf_ref)

    x = x_ref[...]
    sh1 = mod_ref[0:1, 0:D_MODEL]
    sc1 = mod_ref[0:1, D_MODEL:2 * D_MODEL]
    h = ((_rms(x) * gmix_ref[...]) * (1.0 + sc1) + sh1).astype(BF)
    proj_ref[:, 0:P_SPLIT] = _qk(h, wbf_ref[0:P_SPLIT, :])
    proj_ref[:, P_SPLIT:P_TOT] = _qk(h, wbf_ref[P_SPLIT:P_TOT, :])

    for hh in range(N_HEADS):
        lo, hi = hh * HEAD_DIM, (hh + 1) * HEAD_DIM
        dak_ref[0, hh] = proj_ref[:, P_DAK + lo:P_DAK + hi]
        dav_ref[0, hh] = proj_ref[:, P_DAV + lo:P_DAV + hi]
        nak_ref[0, hh] = proj_ref[:, P_NAK + lo:P_NAK + hi]
        nav_ref[0, hh] = proj_ref[:, P_NAV + lo:P_NAV + hi]

    lane = _lane_ids(GROUP_W)

    lam = _diff_lambda(lamv_ref, lam_init)

    qa = proj_ref[:, P_DAQ:P_DAQ + GROUP_W] * (DA_SCALE * LOG2E)
    ka = proj_ref[:, P_DAK:P_DAK + GROUP_W].astype(BF)
    va = proj_ref[:, P_DAV:P_DAV + GROUP_W].astype(BF)
    qn = proj_ref[:, P_NAQ:P_NAQ + GROUP_W] * (NA_SCALE * LOG2E)
    kn2 = proj_ref[:, P_NAK:P_NAK + GROUP_W].astype(BF)
    vn = proj_ref[:, P_NAV:P_NAV + GROUP_W].astype(BF)
    mla = {}

    def mla_prep():
        qd = _rms(proj_ref[:, P_QD:P_QD + GROUP_W]) * gq_ref[...]
        ckv = _rms(proj_ref[:, P_KVD:P_KVD + MLA_KV_LORA]) * gkv_ref[...]
        ckv_ref[0] = ckv
        kr_pad = proj_ref[:, P_KR:P_KR + 128]
        kr_ref[0] = kr_pad[:, 0:MLA_ROPE]
        mla["q"] = (_dot(qd.astype(BF), wqup_ref[...]) * (MLA_SCALE * LOG2E)).astype(BF)
        kvm = _dot(ckv.astype(BF), wkv_ref[...])
        kn = kvm[:, :GROUP_W].astype(BF)
        krb = kr_pad.astype(BF)
        mla["k"] = (jnp.concatenate([kn[:, :128], krb], axis=1), jnp.concatenate([kn[:, 128:], krb], axis=1))
        mla["v"] = kvm[:, GROUP_W:].astype(BF)

    def conv():
        g = proj_ref[:, P_CONV:P_CONV + GROUP_W] * jax.nn.sigmoid(proj_ref[:, P_CONV + GROUP_W:P_TOT])
        gpad_ref[0:16] = jnp.zeros((16, GROUP_W), F32)
        gpad_ref[16 + SEQ:32 + SEQ] = jnp.zeros((16, GROUP_W), F32)
        gpad_ref[16:16 + SEQ] = g
        return _conv_ln_silu(gpad_ref, zsh_ref, SEQ, dw_ref, cb_ref, lng_ref, lnb_ref)

    def scores(i):
        if i < N_HEADS:
            return (_qk(_mask_heads(qa, lane, DA_QK, 2 * i), ka),
                    _qk(_mask_heads(qa, lane, DA_QK, 2 * i + 1), ka))
        if i < 2 * N_HEADS:
            hh = i - N_HEADS
            return (_qk(mla["q"][:, hh * GROUP_W:(hh + 1) * GROUP_W], mla["k"][hh // 2]),)
        return (_qk(_mask_heads(qn, lane, HEAD_DIM, i - 2 * N_HEADS), kn2),)

    outs = [jnp.zeros((SEQ, GROUP_W), F32)] * 3
    s_next = scores(0)
    o_conv = None
    for i in range(3 * N_HEADS):
        s_cur = s_next
        if i + 1 < 3 * N_HEADS:
            s_next = scores(i + 1)
        if i == 0:
            mla_prep()
        if i == N_HEADS - 1:
            o_conv = conv()
        grp, hh = divmod(i, N_HEADS)
        if grp == 0:
            e1, r1 = _softmax2_parts(s_cur[0])
            e2, r2 = _softmax2_parts(s_cur[1])
            o = _dot((e1 * r1 - e2 * (lam * r2)).astype(BF), va)
        else:
            e, r = _softmax2_parts(s_cur[0])
            o = _dot(e.astype(BF), mla["v"] if grp == 1 else vn) * r
        outs[grp] = jnp.where(_group_of(lane, HEAD_DIM) == hh, o, outs[grp])
    o_da = _head_rms(outs[0]) * gsub_ref[...] * (1.0 - lam_init)
    o_mla, o_na = outs[1], outs[2]

    mixed_ref[...] = jnp.concatenate([o_da, o_mla, o_na, o_conv], axis=1).astype(BF)


def _ctx_layer(layer, x, prm, prev):
    lam_init = 0.8 - 0.6 * math.exp(-0.3 * layer)
    head_blk = pl.BlockSpec((1, None, N_HEADS, SEQ, HEAD_DIM), lambda b: (b, layer, 0, 0, 0))
    head_shp = jax.ShapeDtypeStruct((BATCH, DEPTH, N_HEADS, SEQ, HEAD_DIM), F32)
    in_specs = [pl.BlockSpec((SEQ, D_MODEL), lambda b: (b, 0)),
                _layer_spec(layer, (16, N_MOD * D_MODEL)),
                _layer_spec(layer, (1, D_MODEL)),
                _layer_spec(layer, (IN_COLS, D_MODEL)),
                _layer_spec(layer, (1, GROUP_W)),
                _layer_spec(layer, (GROUP_W, N_HEADS * GROUP_W)),
                _layer_spec(layer, (1, MLA_KV_LORA)),
                _layer_spec(layer, (MLA_KV_LORA, 2 * GROUP_W)),
                _layer_spec(layer, (4, DA_QK)),
                _layer_spec(layer, (1, GROUP_W)),
                _layer_spec(layer, (32, GROUP_W)),
                _layer_spec(layer, (1, GROUP_W)),
                _layer_spec(layer, (1, GROUP_W)),
                _layer_spec(layer, (1, GROUP_W))] + [pl.BlockSpec(memory_space=pl.ANY)] * len(prev)
    out_specs = [pl.BlockSpec((SEQ, D_MODEL), lambda b: (b, 0)),
                 head_blk, head_blk,
                 pl.BlockSpec((1, None, SEQ, MLA_KV_LORA), lambda b: (b, layer, 0, 0)),
                 pl.BlockSpec((1, None, SEQ, MLA_ROPE), lambda b: (b, layer, 0, 0)),
                 head_blk, head_blk]
    out_shape = [jax.ShapeDtypeStruct((BATCH * SEQ, D_MODEL), BF),
                 head_shp, head_shp,
                 jax.ShapeDtypeStruct((BATCH, DEPTH, SEQ, MLA_KV_LORA), F32),
                 jax.ShapeDtypeStruct((BATCH, DEPTH, SEQ, MLA_ROPE), F32),
                 head_shp, head_shp]
    n_in = len(in_specs) - len(prev)
    return pl.pallas_call(
        functools.partial(_ctx_kernel, lam_init, len(prev)),
        grid=(BATCH,),
        in_specs=in_specs,
        out_specs=out_specs,
        out_shape=out_shape,
        input_output_aliases={n_in + k: 1 + k for k in range(len(prev))},
        scratch_shapes=[pltpu.VMEM((SEQ, P_TOT), F32), pltpu.VMEM((SEQ + 32, GROUP_W), F32),
                        pltpu.VMEM((7, SEQ + 8, GROUP_W), F32), pltpu.VMEM((P_TOT, D_MODEL), BF)],
        compiler_params=_cparams("arbitrary"),
        name=f"ctx_layer{layer}",
    )(x, prm["mod"], prm["g_mix"], prm["w_in"], prm["g_q"], prm["w_qup"], prm["g_kv"], prm["w_kvup"],
      prm["lamv"], prm["g_sub"], prm["dw"], prm["cb"], prm["ln_g"], prm["ln_b"], *prev)


def _lat_proj_kernel(x_ref, mod_ref, gmix_ref, win_ref, gq_ref, wqup_ref, gkv_ref, wkv_ref,
                     cos_ref, sin_ref, cdak_ref, cdav_ref, cka_ref, ckb_ref, cmv_ref, cnak_ref, cnav_ref,
                     daq_ref, mq_ref, naq_ref, g_ref,
                     dak_ref, dav_ref, ka_ref, kb_ref, mv_ref, nak_ref, nav_ref, wbf_ref):
    b = pl.program_id(0)
    j = pl.program_id(1)

    @pl.when((b == 0) & (j == 0))
    def _():
        _stage_w_in(win_ref, wbf_ref)

    @pl.when(j < N_QT)
    def _():
        x = x_ref[...]
        sh1 = mod_ref[pl.ds(1 + b, 1), 0:D_MODEL]
        sc1 = mod_ref[pl.ds(1 + b, 1), D_MODEL:2 * D_MODEL]
        h = (_rms(x) * gmix_ref[...]) * (1.0 + sc1) + sh1
        hb = h.astype(BF)
        proj = jnp.concatenate([_qk(hb, wbf_ref[0:P_SPLIT, :]), _qk(hb, wbf_ref[P_SPLIT:P_TOT, :])], axis=1)
        cos = cos_ref[...]
        sin = sin_ref[...]

        def rope2(z):
            return jnp.concatenate([_rope(z[:, :128], cos[:, :128], sin[:, :128]),
                                    _rope(z[:, 128:], cos[:, 128:], sin[:, 128:])], axis=1)

        daq_ref[0] = (rope2(proj[:, P_DAQ:P_DAQ + GROUP_W]) * (DA_SCALE * LOG2E)).astype(BF)
        dak_ref[0] = rope2(proj[:, P_DAK:P_DAK + GROUP_W]).astype(BF)
        dav_ref[0] = proj[:, P_DAV:P_DAV + GROUP_W].astype(BF)

        qd = _rms(proj[:, P_QD:P_QD + GROUP_W]) * gq_ref[...]
        qm = _dot(qd.astype(BF), wqup_ref[...])
        for hh in range(N_HEADS):
            nope = qm[:, hh * GROUP_W:hh * GROUP_W + 128]
            rope = _rope(qm[:, hh * GROUP_W + 128:(hh + 1) * GROUP_W], cos[:, :128], sin[:, :128])
            mq_ref[0, hh] = (jnp.concatenate([nope, rope], axis=1) * (MLA_SCALE * LOG2E)).astype(BF)
        ckv = _rms(proj[:, P_KVD:P_KVD + MLA_KV_LORA]) * gkv_ref[...]
        kvm = _dot(ckv.astype(BF), wkv_ref[...])
        kn = kvm[:, :GROUP_W].astype(BF)
        krb = _rope(proj[:, P_KR:P_KR + 128], cos[:, :128], sin[:, :128]).astype(BF)
        ka_ref[0] = jnp.concatenate([kn[:, :128], krb], axis=1)
        kb_ref[0] = jnp.concatenate([kn[:, 128:], krb], axis=1)
        mv_ref[0] = kvm[:, GROUP_W:].astype(BF)

        naq_ref[0] = (proj[:, P_NAQ:P_NAQ + GROUP_W] * NA_SCALE).astype(BF)
        nak_ref[0] = proj[:, P_NAK:P_NAK + GROUP_W].astype(BF)
        nav_ref[0] = proj[:, P_NAV:P_NAV + GROUP_W].astype(BF)
        g_ref[0] = proj[:, P_CONV:P_CONV + GROUP_W] * jax.nn.sigmoid(proj[:, P_CONV + GROUP_W:P_TOT])

    @pl.when(j == N_QT)
    def _():
        dak_ref[0] = cdak_ref[0]
        dav_ref[0] = cdav_ref[0]
        ka_ref[0] = cka_ref[0]
        kb_ref[0] = ckb_ref[0]
        mv_ref[0] = cmv_ref[0]
        nak_ref[0] = cnak_ref[0]
        nav_ref[0] = cnav_ref[0]


def _lat_proj(layer, x, prm, cos_t, sin_t, caches):
    jq = lambda j: jnp.minimum(j, N_QT - 1)
    cache_spec = pl.BlockSpec((1, None, PAST_LEN, GROUP_W), lambda b, j: (b, layer, 0, 0))
    q_spec = pl.BlockSpec((1, TQ, GROUP_W), lambda b, j: (b, jq(j), 0))
    k_spec = pl.BlockSpec((1, TQ, GROUP_W), lambda b, j: (b, j, 0))
    in_specs = [pl.BlockSpec((TQ, D_MODEL), lambda b, j: (b * N_QT + jq(j), 0)),
                _layer_spec(layer, (16, N_MOD * D_MODEL)),
                _layer_spec(layer, (1, D_MODEL)),
                _layer_spec(layer, (IN_COLS, D_MODEL)),
                _layer_spec(layer, (1, GROUP_W)),
                _layer_spec(layer, (GROUP_W, N_HEADS * GROUP_W)),
                _layer_spec(layer, (1, MLA_KV_LORA)),
                _layer_spec(layer, (MLA_KV_LORA, 2 * GROUP_W)),
                pl.BlockSpec((TQ, GROUP_W), lambda b, j: (jq(j), 0)),
                pl.BlockSpec((TQ, GROUP_W), lambda b, j: (jq(j), 0))] + [cache_spec] * 7
    q_shp = jax.ShapeDtypeStruct((DEC_BATCH, DEC_SEQ, GROUP_W), BF)
    k_shp = jax.ShapeDtypeStruct((DEC_BATCH, KEYS, GROUP_W), BF)
    out_specs = [q_spec,
                 pl.BlockSpec((1, N_HEADS, TQ, GROUP_W), lambda b, j: (b, 0, jq(j), 0)),
                 q_spec, q_spec] + [k_spec] * 7
    out_shape = [q_shp,
                 jax.ShapeDtypeStruct((DEC_BATCH, N_HEADS, DEC_SEQ, GROUP_W), BF),
                 q_shp,
                 jax.ShapeDtypeStruct((DEC_BATCH, DEC_SEQ, GROUP_W), F32)] + [k_shp] * 7
    return pl.pallas_call(
        _lat_proj_kernel,
        grid=(DEC_BATCH, N_QT + 1),
        in_specs=in_specs,
        out_specs=out_specs,
        out_shape=out_shape,
        scratch_shapes=[pltpu.VMEM((P_TOT, D_MODEL), BF)],
        compiler_params=_cparams("arbitrary", "arbitrary"),
        name=f"lat_proj{layer}",
    )(x, prm["mod"], prm["g_mix"], prm["w_in"], prm["g_q"], prm["w_qup"], prm["g_kv"], prm["w_kvup"],
      cos_t, sin_t, *caches)


def _lat_attn_kernel(lam_init, daq_ref, mq_ref, naq_ref, g_ref,
                     dak_ref, dav_ref, ka_ref, kb_ref, mv_ref, nak_ref, nav_ref,
                     nab_ref, lamv_ref, gsub_ref, dw_ref, cb_ref, lng_ref, lnb_ref,
                     mixed_ref, gpad_ref, zsh_ref, kt_ref):
    t = pl.program_id(1)
    lane = _lane_ids(GROUP_W)

    @pl.when(t == 0)
    def _():
        kt_ref[0] = dak_ref[0].T
        kt_ref[1] = ka_ref[0].T
        kt_ref[2] = kb_ref[0].T

    lam = _diff_lambda(lamv_ref, lam_init)
    qa = daq_ref[0].astype(F32)
    va = dav_ref[0]
    ka_t = kt_ref[0]
    vm = mv_ref[0]

    kc = nak_ref[0, DEC_SEQ:KEYS, :]
    vc = nav_ref[0, DEC_SEQ:KEYS, :]
    n_items = 2 * N_HEADS + ROWS_PER_TILE

    def na_window(j):
        r = t * ROWS_PER_TILE + j
        start = jnp.clip(r - NA_KR // 2, 0, N_ROWS - NA_KR)
        return start - r + NA_KR - 1, pl.multiple_of(start * GRID_W, GRID_W)

    def scores(i):
        if i < N_HEADS:
            return (_dot(_mask_heads(qa, lane, DA_QK, 2 * i), ka_t),
                    _dot(_mask_heads(qa, lane, DA_QK, 2 * i + 1), ka_t))
        if i < 2 * N_HEADS:
            hh = i - N_HEADS
            return (_dot(mq_ref[0, hh], kt_ref[1 + hh // 2]),)
        j = i - 2 * N_HEADS
        dr0, koff = na_window(j)
        qrow = naq_ref[0, j * GRID_W:(j + 1) * GRID_W, :].astype(F32)
        q4 = jnp.concatenate([_mask_heads(qrow, lane, HEAD_DIM, hh) for hh in range(N_HEADS)], axis=0)
        return (_qk(q4, nak_ref[0, pl.ds(koff, NA_LOCAL), :]) + nab_ref[dr0], _qk(q4, kc))

    o_da = jnp.zeros((TQ, GROUP_W), F32)
    o_mla = jnp.zeros((TQ, GROUP_W), F32)
    na_rows = []
    s_next = scores(0)

    base = pl.multiple_of(t * TQ, TQ)
    gpad_ref[16:16 + TQ] = g_ref[0, pl.ds(base, TQ), :]
    lo = g_ref[0, pl.ds(pl.multiple_of(jnp.maximum(base - 16, 0), 16), 16), :]
    hi = g_ref[0, pl.ds(pl.multiple_of(jnp.minimum(base + TQ, DEC_SEQ - 16), 16), 16), :]
    gpad_ref[0:16] = jnp.where(t > 0, lo, 0.0)
    gpad_ref[16 + TQ:32 + TQ] = jnp.where(t < N_QT - 1, hi, 0.0)
    o_conv = _conv_ln_silu(gpad_ref, zsh_ref, TQ, dw_ref, cb_ref, lng_ref, lnb_ref)

    for i in range(n_items):
        s_cur = s_next
        if i + 1 < n_items:
            s_next = scores(i + 1)
        if i < N_HEADS:
            e1, r1 = _softmax2_parts(s_cur[0])
            e2, r2 = _softmax2_parts(s_cur[1])
            p = (e1 * r1 - e2 * (lam * r2)).astype(BF)
            o_da = jnp.where(_group_of(lane, HEAD_DIM) == i, _dot(p, va), o_da)
        elif i < 2 * N_HEADS:
            e, r = _softmax2_parts(s_cur[0])
            o_mla = jnp.where(_group_of(lane, HEAD_DIM) == i - N_HEADS, _dot(e.astype(BF), vm) * r, o_mla)
        else:
            _, koff = na_window(i - 2 * N_HEADS)
            s_loc, s_ctx = s_cur
            m = jnp.maximum(jnp.max(s_loc, axis=-1, keepdims=True), jnp.max(s_ctx, axis=-1, keepdims=True))
            e_loc = jnp.exp(s_loc - m)
            e_ctx = jnp.exp(s_ctx - m)
            den = jnp.sum(e_loc, axis=-1, keepdims=True) + jnp.sum(e_ctx, axis=-1, keepdims=True)
            o4 = (_dot(e_loc.astype(BF), nav_ref[0, pl.ds(koff, NA_LOCAL), :])
                  + _dot(e_ctx.astype(BF), vc)) * (1.0 / den)
            o_row = o4[0:GRID_W]
            for hh in range(1, N_HEADS):
                o_row = jnp.where(_group_of(lane, HEAD_DIM) == hh, o4[hh * GRID_W:(hh + 1) * GRID_W], o_row)
            na_rows.append(o_row)
    o_da = _head_rms(o_da) * gsub_ref[...] * (1.0 - lam_init)
    o_na = jnp.concatenate(na_rows, axis=0)

    mixed_ref[...] = jnp.concatenate([o_da, o_mla, o_na, o_conv], axis=1).astype(BF)


def _lat_attn(layer, proj_outs, nab, prm):
    lam_init = 0.8 - 0.6 * math.exp(-0.3 * layer)
    q_spec = pl.BlockSpec((1, TQ, GROUP_W), lambda b, t: (b, t, 0))
    full_k = pl.BlockSpec((1, KEYS, GROUP_W), lambda b, t: (b, 0, 0))
    in_specs = [q_spec,
                pl.BlockSpec((1, N_HEADS, TQ, GROUP_W), lambda b, t: (b, 0, t, 0)),
                q_spec,
                pl.BlockSpec((1, DEC_SEQ, GROUP_W), lambda b, t: (b, 0, 0))] + [full_k] * 7 + [
                _layer_spec(layer, (NA_KR, N_HEADS * GRID_W, NA_LOCAL)),
                _layer_spec(layer, (4, DA_QK)),
                _layer_spec(layer, (1, GROUP_W)),
                _layer_spec(layer, (32, GROUP_W)),
                _layer_spec(layer, (1, GROUP_W)),
                _layer_spec(layer, (1, GROUP_W)),
                _layer_spec(layer, (1, GROUP_W))]
    return pl.pallas_call(
        functools.partial(_lat_attn_kernel, lam_init),
        grid=(DEC_BATCH, N_QT),
        in_specs=in_specs,
        out_specs=pl.BlockSpec((TQ, D_MODEL), lambda b, t: (b * N_QT + t, 0)),
        out_shape=jax.ShapeDtypeStruct((DEC_BATCH * DEC_SEQ, D_MODEL), BF),
        scratch_shapes=[pltpu.VMEM((TQ + 32, GROUP_W), F32), pltpu.VMEM((7, TQ + 8, GROUP_W), F32),
                        pltpu.VMEM((3, GROUP_W, KEYS), BF)],
        compiler_params=_cparams("arbitrary", "arbitrary"),
        name=f"lat_attn{layer}",
    )(*proj_outs, nab, prm["lamv"], prm["g_sub"], prm["dw"], prm["cb"], prm["ln_g"], prm["ln_b"])


def _out_ffn_kernel(final, tiles_per_mod, mod_base, x_ref, mx_ref, mod_ref, wout_ref, gff_ref,
                    w1_ref, w2_ref, gfin_ref, o_ref):
    row = mod_base + pl.program_id(0) // tiles_per_mod
    g1 = mod_ref[pl.ds(row, 1), 2 * D_MODEL:3 * D_MODEL]
    sh2 = mod_ref[pl.ds(row, 1), 3 * D_MODEL:4 * D_MODEL]
    sc2 = mod_ref[pl.ds(row, 1), 4 * D_MODEL:5 * D_MODEL]
    g2 = mod_ref[pl.ds(row, 1), 5 * D_MODEL:6 * D_MODEL]
    x1 = x_ref[...] + g1 * _dot(mx_ref[...], wout_ref[...])
    h2 = ((_rms(x1) * gff_ref[...]) * (1.0 + sc2) + sh2).astype(BF)
    acc = jnp.zeros((TM_FFN, D_MODEL), F32)
    for c in range(D_FF // D_MODEL):
        a = jnp.maximum(_dot(h2, w1_ref[:, c * D_MODEL:(c + 1) * D_MODEL]), 0.0)
        acc = acc + _dot((a * a).astype(BF), w2_ref[c * D_MODEL:(c + 1) * D_MODEL, :])
    x2 = x1 + g2 * acc
    o_ref[...] = _rms(x2) * gfin_ref[...] if final else x2


def _out_ffn(name, layer, tiles_per_mod, mod_base, x, mixed, prm, g_final):
    n = x.shape[0]
    tile = pl.BlockSpec((TM_FFN, D_MODEL), lambda i: (i, 0))
    return pl.pallas_call(
        functools.partial(_out_ffn_kernel, layer == DEPTH - 1, tiles_per_mod, mod_base),
        grid=(n // TM_FFN,),
        in_specs=[tile, tile,
                  _layer_spec(layer, (16, N_MOD * D_MODEL)),
                  _layer_spec(layer, (D_MODEL, D_MODEL)),
                  _layer_spec(layer, (1, D_MODEL)),
                  _layer_spec(layer, (D_MODEL, D_FF)),
                  _layer_spec(layer, (D_FF, D_MODEL)),
                  _const_spec((1, D_MODEL))],
        out_specs=tile,
        out_shape=jax.ShapeDtypeStruct((n, D_MODEL), F32),
        compiler_params=_cparams("arbitrary"),
        name=name,
    )(x, mixed, prm["mod"], prm["w_out"], prm["g_ff"], prm["w_ff1"], prm["w_ff2"], g_final)


def _rope_tables():
    t = np.arange(DEC_SEQ)
    rows = (t // GRID_W).astype(np.float32)
    cols = (t % GRID_W).astype(np.float32)
    c = np.arange(GROUP_W) % 32
    freqs = np.float32(ROPE_BASE) ** (-np.arange(8, dtype=np.float32) * np.float32(2.0) / np.float32(16))
    pos = np.where((c < 16)[None, :], rows[:, None], cols[:, None]).astype(np.float32)
    ang = (pos * freqs[(c % 16) % 8][None, :]).astype(np.float32)
    first = ((c % 16) < 8)[None, :]
    cos = np.cos(ang).astype(np.float32)
    sin = np.sin(ang).astype(np.float32)
    return jnp.asarray(cos), jnp.asarray(np.where(first, -sin, sin))


def _qup_gather_index():
    idx = np.full((N_HEADS * GROUP_W,), -1, np.int64)
    for h in range(N_HEADS):
        src = h * (MLA_NOPE + MLA_ROPE)
        dst = h * GROUP_W + (h % 2) * MLA_NOPE
        idx[dst:dst + MLA_NOPE] = np.arange(src, src + MLA_NOPE)
        idx[h * GROUP_W + 128:h * GROUP_W + 128 + MLA_ROPE] = np.arange(src + MLA_NOPE, src + MLA_NOPE + MLA_ROPE)
    return idx


def _kvup_perm():
    k = [h * 128 + d for h in range(N_HEADS) for d in range(MLA_NOPE)]
    v = [h * 128 + MLA_NOPE + d for h in range(N_HEADS) for d in range(HEAD_DIM)]
    return np.asarray(k + v)


def _heads_to_lanes(c):
    b, l, h, s, d = c.shape
    return c.transpose(0, 1, 3, 2, 4).reshape(b, l, s, h * d).astype(BF)


def kernel(x_prompt, x_sample, c, cache_da_k, cache_da_v, cache_mla_ckv, cache_mla_krope, cache_na_k, cache_na_v, c_ctx, w_mod, b_mod, g_norm_mix, g_norm_ff, w_in, da_lambda_q1, da_lambda_k1, da_lambda_q2, da_lambda_k2, g_da_subln, g_mla_q, w_mla_qup, g_mla_kv, w_mla_kvup, na_rpb, conv_dw, conv_b, conv_ln_g, conv_ln_b, w_out, w_ff1, w_ff2, g_final):
    qidx = _qup_gather_index()
    w_qup_e = jnp.where(jnp.asarray(qidx >= 0)[None, None, :],
                        jnp.take(w_mla_qup, jnp.asarray(np.maximum(qidx, 0)), axis=-1), 0.0).astype(BF)
    w_kvup_p = jnp.take(w_mla_kvup, jnp.asarray(_kvup_perm()), axis=-1).astype(BF)
    r3 = lambda a: a.reshape(DEPTH, 1, -1)
    prm = dict(
        g_mix=r3(g_norm_mix), w_in=jnp.swapaxes(w_in, 1, 2), g_q=r3(g_mla_q), w_qup=w_qup_e, g_kv=r3(g_mla_kv), w_kvup=w_kvup_p,
        lamv=jnp.stack([da_lambda_q1, da_lambda_k1, da_lambda_q2, da_lambda_k2], axis=1),
        g_sub=r3(jnp.tile(g_da_subln, (1, N_HEADS))),
        dw=jnp.concatenate([conv_dw, jnp.zeros((DEPTH, 1, GROUP_W), F32)], axis=1),
        cb=r3(conv_b), ln_g=r3(conv_ln_g), ln_b=r3(conv_ln_b),
        w_out=w_out.astype(BF), g_ff=r3(g_norm_ff), w_ff1=w_ff1.astype(BF), w_ff2=w_ff2.astype(BF))
    cv = jnp.concatenate([c_ctx[None, :], c, jnp.zeros((16 - 1 - DEC_BATCH, D_MODEL), F32)], axis=0)
    cos_t, sin_t = _rope_tables()
    g_final2 = g_final.reshape(1, D_MODEL)

    prm["mod"] = _modulation(cv, w_mod, b_mod)
    c_kr_pad = jnp.pad(cache_mla_krope, ((0, 0), (0, 0), (0, 0), (0, KR_PAD)))
    cka, ckb, cmv = _mla_cache(cache_mla_ckv, c_kr_pad, w_kvup_p)
    caches = (_heads_to_lanes(cache_da_k), _heads_to_lanes(cache_da_v), cka, ckb, cmv,
              _heads_to_lanes(cache_na_k), _heads_to_lanes(cache_na_v))
    nab = _na_bias(na_rpb)

    xp = x_prompt.reshape(BATCH * SEQ, D_MODEL)
    xs = x_sample.reshape(DEC_BATCH * DEC_SEQ, D_MODEL)
    new_ctx = ()
    for l in range(DEPTH):
        mixed_p, *new_ctx = _ctx_layer(l, xp, prm, tuple(new_ctx))
        xp = _out_ffn(f"ctx_ffn{l}", l, BATCH * SEQ // TM_FFN, 0, xp, mixed_p, prm, g_final2)
        proj_outs = _lat_proj(l, xs, prm, cos_t, sin_t, caches)
        mixed_s = _lat_attn(l, proj_outs, nab, prm)
        xs = _out_ffn(f"lat_ffn{l}", l, DEC_SEQ // TM_FFN, 1, xs, mixed_s, prm, g_final2)

    y_prompt = xp.reshape(BATCH, SEQ, D_MODEL)
    y_sample = xs.reshape(DEC_BATCH, DEC_SEQ, D_MODEL)
    return (y_prompt, y_sample) + tuple(new_ctx)
```

```python
import functools
import math

import numpy as np
import jax
import jax.numpy as jnp
from jax import lax
from jax.experimental import pallas as pl
from jax.experimental.pallas import tpu as pltpu

F32 = jnp.float32
BF = jnp.bfloat16

D_MODEL = 1024
BATCH = 16
SEQ = 256
DEPTH = 2
DEC_BATCH = 2
DEC_SEQ = 2048
PAST_LEN = 256
GRID_W = 64
GROUP_W = 256
HEAD_DIM = 64
N_HEADS = 4
DA_QK = 32
MLA_NOPE = 64
MLA_ROPE = 32
MLA_KV_LORA = 128
NA_KR = 8
NA_KC = 16
CONV_W = 31
D_FF = 4096
ROPE_BASE = 10000.0
EPS = 1e-6
N_MOD = 6
IN_COLS = 2464

P_DAQ, P_DAK, P_DAV, P_QD, P_KVD, P_KR = 0, 256, 512, 768, 1024, 1152
P_NAQ, P_NAK, P_NAV, P_CONV, P_TOT = 1280, 1536, 1792, 2048, 2560
P_SPLIT = P_NAQ
N_CTX_IN = 14
KR_ORIG_END = 1184
KR_PAD = 128 - MLA_ROPE

DA_SCALE = DA_QK ** -0.5
MLA_SCALE = (MLA_NOPE + MLA_ROPE) ** -0.5
NA_SCALE = HEAD_DIM ** -0.5
LOG2E = math.log2(math.e)

TQ = 256
N_QT = DEC_SEQ // TQ
KEYS = DEC_SEQ + PAST_LEN
ROWS_PER_TILE = TQ // GRID_W
N_ROWS = DEC_SEQ // GRID_W
NA_LOCAL = NA_KR * GRID_W
TM_FFN = 512
VMEM_LIMIT = 58 * 1024 * 1024

NT_DIMS = (((1,), (1,)), ((), ()))


def _cparams(*sem):
    return pltpu.CompilerParams(dimension_semantics=sem, vmem_limit_bytes=VMEM_LIMIT)


def _const_spec(shape):
    nd = len(shape)
    return pl.BlockSpec(shape, lambda *_: (0,) * nd, pipeline_mode=pl.Buffered(1))


def _layer_spec(layer, shape):
    nd = len(shape)
    return pl.BlockSpec((None,) + tuple(shape), lambda *_: (layer,) + (0,) * nd, pipeline_mode=pl.Buffered(1))


def _rms(x):
    return x * lax.rsqrt(jnp.mean(x * x, axis=-1, keepdims=True) + EPS)


def _row(ref, layer):
    return ref[layer:layer + 1, :]


def _dot(a, b):
    return jnp.dot(a, b, preferred_element_type=F32)


def _qk(q, k):
    return lax.dot_general(q, k, NT_DIMS, preferred_element_type=F32)


def _softmax_parts(s):
    m = jnp.max(s, axis=-1, keepdims=True)
    e = jnp.exp(s - m)
    return e, 1.0 / jnp.sum(e, axis=-1, keepdims=True)


def _softmax2_parts(s):
    m = jnp.max(s, axis=-1, keepdims=True)
    e = jnp.exp2(s - m)
    return e, 1.0 / jnp.sum(e, axis=-1, keepdims=True)


def _lane_ids(width):
    return lax.broadcasted_iota(jnp.int32, (1, width), 1)


def _diff_lambda(lamv_ref, lam_init):
    v = lamv_ref[...]
    a = jnp.exp(jnp.sum(v[0:1] * v[1:2], axis=-1, keepdims=True))
    b = jnp.exp(jnp.sum(v[2:3] * v[3:4], axis=-1, keepdims=True))
    return a - b + lam_init


def _group_of(lane, group):
    return lane >> (group.bit_length() - 1)


def _head_rms(x):
    r = _group_of(lax.broadcasted_iota(jnp.int32, (GROUP_W, GROUP_W), 0), HEAD_DIM)
    c = _group_of(lax.broadcasted_iota(jnp.int32, (GROUP_W, GROUP_W), 1), HEAD_DIM)
    ones_bd = jnp.where(r == c, 1.0, 0.0).astype(BF)
    sq = x * x
    hi = sq.astype(BF)
    lo = (sq - hi.astype(F32)).astype(BF)
    ss = _dot(hi, ones_bd) + _dot(lo, ones_bd)
    return x * lax.rsqrt(ss * (1.0 / HEAD_DIM) + EPS)


def _rope(z, cos, sin):
    lane = _lane_ids(z.shape[1])
    swapped = jnp.where((lane & 15) < 8, pltpu.roll(z, 120, 1), pltpu.roll(z, 8, 1))
    return z * cos + swapped * sin


def _stage_w_in(wint_ref, wbt_ref):
    wbt_ref[0:KR_ORIG_END, :] = wint_ref[0:KR_ORIG_END, :].astype(BF)
    wbt_ref[KR_ORIG_END:P_SPLIT, :] = jnp.zeros((P_SPLIT - KR_ORIG_END, D_MODEL), BF)
    wbt_ref[P_SPLIT:P_TOT, :] = wint_ref[KR_ORIG_END:IN_COLS, :].astype(BF)


def _mask_heads(qf, lane, group, idx):
    return jnp.where(_group_of(lane, group) == idx, qf, 0.0).astype(BF)


def _conv_ln_silu(layer, gpad_ref, zsh_ref, n, dw_ref, cb_ref, lng_ref, lnb_ref):
    y = jnp.zeros((n, GROUP_W), F32) + _row(cb_ref, layer)
    for b in range(8):
        z = None
        for a in range(4):
            t = 8 * a + b - 1
            if 0 <= t < CONV_W:
                term = gpad_ref[pl.ds(8 * a, n + 8), :] * dw_ref[t:t + 1, :]
                z = term if z is None else z + term
        if b == 0:
            y = y + z[0:n]
        else:
            zsh_ref[b - 1] = z
            y = y + zsh_ref[b - 1, pl.ds(b, n), :]
    mu = jnp.mean(y, axis=-1, keepdims=True)
    yc = y - mu
    var = jnp.mean(yc * yc, axis=-1, keepdims=True)
    z = yc * lax.rsqrt(var + EPS) * _row(lng_ref, layer) + _row(lnb_ref, layer)
    return z * jax.nn.sigmoid(z)


def _mod_kernel(cv_ref, w_ref, b_ref, o_ref):
    c = cv_ref[...]
    a = (c * jax.nn.sigmoid(c)).astype(BF)
    o_ref[0] = _dot(a, w_ref[0].astype(BF)) + b_ref[0]


def _modulation(cv, w_mod, b_mod):
    nblk = 4
    wblk = N_MOD * D_MODEL // nblk
    return pl.pallas_call(
        _mod_kernel,
        grid=(DEPTH, nblk),
        in_specs=[pl.BlockSpec((16, D_MODEL), lambda l, j: (0, 0)),
                  pl.BlockSpec((1, D_MODEL, wblk), lambda l, j: (l, 0, j)),
                  pl.BlockSpec((1, 1, wblk), lambda l, j: (l, 0, j))],
        out_specs=pl.BlockSpec((1, 16, wblk), lambda l, j: (l, 0, j)),
        out_shape=jax.ShapeDtypeStruct((DEPTH, 16, N_MOD * D_MODEL), F32),
        compiler_params=_cparams("arbitrary", "arbitrary"),
        name="modulation",
    )(cv, w_mod, b_mod.reshape(DEPTH, 1, N_MOD * D_MODEL))


def _mla_cache_kernel(ckv_ref, kr_ref, w_ref, ka_ref, kb_ref, v_ref):
    kvm = _dot(ckv_ref[0, 0].astype(BF), w_ref[0])
    kr = kr_ref[0, 0].astype(BF)
    kn = kvm[:, :GROUP_W].astype(BF)
    ka_ref[0, 0] = jnp.concatenate([kn[:, :128], kr], axis=1)
    kb_ref[0, 0] = jnp.concatenate([kn[:, 128:], kr], axis=1)
    v_ref[0, 0] = kvm[:, GROUP_W:].astype(BF)


def _mla_cache(c_ckv, c_kr_pad, w_kvup_p):
    blk = pl.BlockSpec((1, 1, PAST_LEN, GROUP_W), lambda b, l: (b, l, 0, 0))
    shp = jax.ShapeDtypeStruct((DEC_BATCH, DEPTH, PAST_LEN, GROUP_W), BF)
    return pl.pallas_call(
        _mla_cache_kernel,
        grid=(DEC_BATCH, DEPTH),
        in_specs=[pl.BlockSpec((1, 1, PAST_LEN, MLA_KV_LORA), lambda b, l: (b, l, 0, 0)),
                  pl.BlockSpec((1, 1, PAST_LEN, 128), lambda b, l: (b, l, 0, 0)),
                  pl.BlockSpec((1, MLA_KV_LORA, 2 * GROUP_W), lambda b, l: (l, 0, 0))],
        out_specs=[blk, blk, blk],
        out_shape=[shp, shp, shp],
        compiler_params=_cparams("arbitrary", "arbitrary"),
        name="mla_cache",
    )(c_ckv, c_kr_pad, w_kvup_p)


N_DR = 2 * NA_KR - 1
N_DC = 2 * NA_KC - 1


def _na_bias_kernel(rpb_ref, o_ref, tp_ref):
    base = (pl.program_id(0) * N_HEADS + pl.program_id(1)) * (N_DR * N_DC)
    cq = lax.broadcasted_iota(jnp.int32, (GRID_W, 128), 0)
    lane = lax.broadcasted_iota(jnp.int32, (GRID_W, 128), 1)
    ck = lane & (GRID_W - 1)
    upper = lane >= GRID_W
    cstart = jnp.clip(cq - NA_KC // 2, 0, GRID_W - NA_KC)
    ok = (ck >= cstart) & (ck < cstart + NA_KC)

    j = lax.broadcasted_iota(jnp.int32, (8, 128), 1)
    dclip = jnp.clip(jnp.where(j < GRID_W, j, j - 128), -(NA_KC - 1), NA_KC - 1) + NA_KC - 1

    def toeplitz(dr):
        u = jnp.zeros((8, 128), F32)
        for d in range(N_DC):
            u = jnp.where(dclip == d, rpb_ref[base + dr * N_DC + d], u)
        rows = jnp.broadcast_to(u[0:1], (GRID_W, 128))
        return pltpu.roll(rows, 0, 1, stride=1, stride_axis=0)

    tabs = [toeplitz(dr) for dr in range(N_DR)]
    for dr in range(N_DR - 1):
        val = jnp.where(upper, pltpu.roll(tabs[dr + 1], GRID_W, 1), tabs[dr])
        tp_ref[dr] = jnp.where(ok, val, -jnp.inf)
    for dr0 in range(NA_KR):
        o_ref[0, dr0] = jnp.concatenate([tp_ref[dr0 + 2 * m] for m in range(NA_KR // 2)], axis=1)


def _na_bias(rpb):
    return pl.pallas_call(
        _na_bias_kernel,
        grid=(DEPTH, N_HEADS),
        in_specs=[pl.BlockSpec(memory_space=pltpu.SMEM)],
        out_specs=pl.BlockSpec((1, NA_KR, GRID_W, NA_LOCAL), lambda l, h: (l, 0, h, 0)),
        out_shape=jax.ShapeDtypeStruct((DEPTH, NA_KR, N_HEADS * GRID_W, NA_LOCAL), F32),
        scratch_shapes=[pltpu.VMEM((N_DR - 1, GRID_W, 128), F32)],
        compiler_params=_cparams("arbitrary", "arbitrary"),
        name="na_bias",
    )(rpb.reshape(-1))


def _ctx_kernel(layer, lam_init, n_prev, *refs):
    (x_ref, mod_ref, gmix_ref, win_ref, gq_ref, wqup_ref, gkv_ref, wkv_ref,
     lamv_ref, gsub_ref, dw_ref, cb_ref, lng_ref, lnb_ref) = refs[:N_CTX_IN]
    (mixed_ref, dak_ref, dav_ref, ckv_ref, kr_ref, nak_ref, nav_ref,
     proj_ref, gpad_ref, zsh_ref, wbf_ref) = refs[N_CTX_IN + n_prev:]

    @pl.when(pl.program_id(0) == 0)
    def _():
        _stage_w_in(win_ref, wbf_ref)

    x = x_ref[...]
    sh1 = mod_ref[0:1, 0:D_MODEL]
    sc1 = mod_ref[0:1, D_MODEL:2 * D_MODEL]
    h = ((_rms(x) * _row(gmix_ref, layer)) * (1.0 + sc1) + sh1).astype(BF)
    proj_ref[:, 0:P_SPLIT] = _qk(h, wbf_ref[0:P_SPLIT, :])
    proj_ref[:, P_SPLIT:P_TOT] = _qk(h, wbf_ref[P_SPLIT:P_TOT, :])

    for hh in range(N_HEADS):
        lo, hi = hh * HEAD_DIM, (hh + 1) * HEAD_DIM
        dak_ref[0, hh] = proj_ref[:, P_DAK + lo:P_DAK + hi]
        dav_ref[0, hh] = proj_ref[:, P_DAV + lo:P_DAV + hi]
        nak_ref[0, hh] = proj_ref[:, P_NAK + lo:P_NAK + hi]
        nav_ref[0, hh] = proj_ref[:, P_NAV + lo:P_NAV + hi]

    lane = _lane_ids(GROUP_W)

    lam = _diff_lambda(lamv_ref, lam_init)

    qa = proj_ref[:, P_DAQ:P_DAQ + GROUP_W] * (DA_SCALE * LOG2E)
    ka = proj_ref[:, P_DAK:P_DAK + GROUP_W].astype(BF)
    va = proj_ref[:, P_DAV:P_DAV + GROUP_W].astype(BF)
    qn = proj_ref[:, P_NAQ:P_NAQ + GROUP_W] * (NA_SCALE * LOG2E)
    kn2 = proj_ref[:, P_NAK:P_NAK + GROUP_W].astype(BF)
    vn = proj_ref[:, P_NAV:P_NAV + GROUP_W].astype(BF)
    mla = {}

    def mla_prep():
        qd = _rms(proj_ref[:, P_QD:P_QD + GROUP_W]) * _row(gq_ref, layer)
        ckv = _rms(proj_ref[:, P_KVD:P_KVD + MLA_KV_LORA]) * _row(gkv_ref, layer)
        ckv_ref[0] = ckv
        kr_pad = proj_ref[:, P_KR:P_KR + 128]
        kr_ref[0] = kr_pad[:, 0:MLA_ROPE]
        mla["q"] = (_dot(qd.astype(BF), wqup_ref[...]) * (MLA_SCALE * LOG2E)).astype(BF)
        kvm = _dot(ckv.astype(BF), wkv_ref[...])
        kn = kvm[:, :GROUP_W].astype(BF)
        krb = kr_pad.astype(BF)
        mla["k"] = (jnp.concatenate([kn[:, :128], krb], axis=1), jnp.concatenate([kn[:, 128:], krb], axis=1))
        mla["v"] = kvm[:, GROUP_W:].astype(BF)

    def conv():
        g = proj_ref[:, P_CONV:P_CONV + GROUP_W] * jax.nn.sigmoid(proj_ref[:, P_CONV + GROUP_W:P_TOT])
        gpad_ref[0:16] = jnp.zeros((16, GROUP_W), F32)
        gpad_ref[16 + SEQ:32 + SEQ] = jnp.zeros((16, GROUP_W), F32)
        gpad_ref[16:16 + SEQ] = g
        return _conv_ln_silu(layer, gpad_ref, zsh_ref, SEQ, dw_ref, cb_ref, lng_ref, lnb_ref)

    def scores(i):
        if i < N_HEADS:
            return (_qk(_mask_heads(qa, lane, DA_QK, 2 * i), ka),
                    _qk(_mask_heads(qa, lane, DA_QK, 2 * i + 1), ka))
        if i < 2 * N_HEADS:
            hh = i - N_HEADS
            return (_qk(mla["q"][:, hh * GROUP_W:(hh + 1) * GROUP_W], mla["k"][hh // 2]),)
        return (_qk(_mask_heads(qn, lane, HEAD_DIM, i - 2 * N_HEADS), kn2),)

    outs = [jnp.zeros((SEQ, GROUP_W), F32)] * 3
    s_next = scores(0)
    o_conv = None
    for i in range(3 * N_HEADS):
        s_cur = s_next
        if i + 1 < 3 * N_HEADS:
            s_next = scores(i + 1)
        if i == 0:
            mla_prep()
        if i == N_HEADS - 1:
            o_conv = conv()
        grp, hh = divmod(i, N_HEADS)
        if grp == 0:
            e1, r1 = _softmax2_parts(s_cur[0])
            e2, r2 = _softmax2_parts(s_cur[1])
            o = _dot((e1 * r1 - e2 * (lam * r2)).astype(BF), va)
        else:
            e, r = _softmax2_parts(s_cur[0])
            o = _dot(e.astype(BF), mla["v"] if grp == 1 else vn) * r
        outs[grp] = jnp.where(_group_of(lane, HEAD_DIM) == hh, o, outs[grp])
    o_da = _head_rms(outs[0]) * _row(gsub_ref, layer) * (1.0 - lam_init)
    o_mla, o_na = outs[1], outs[2]

    mixed_ref[...] = jnp.concatenate([o_da, o_mla, o_na, o_conv], axis=1).astype(BF)


def _ctx_layer(layer, x, prm, prev):
    lam_init = 0.8 - 0.6 * math.exp(-0.3 * layer)
    head_blk = pl.BlockSpec((1, None, N_HEADS, SEQ, HEAD_DIM), lambda b: (b, layer, 0, 0, 0))
    head_shp = jax.ShapeDtypeStruct((BATCH, DEPTH, N_HEADS, SEQ, HEAD_DIM), F32)
    in_specs = [pl.BlockSpec((SEQ, D_MODEL), lambda b: (b, 0)),
                _layer_spec(layer, (16, N_MOD * D_MODEL)),
                _const_spec((DEPTH, D_MODEL)),
                _layer_spec(layer, (IN_COLS, D_MODEL)),
                _const_spec((DEPTH, GROUP_W)),
                _layer_spec(layer, (GROUP_W, N_HEADS * GROUP_W)),
                _const_spec((DEPTH, MLA_KV_LORA)),
                _layer_spec(layer, (MLA_KV_LORA, 2 * GROUP_W)),
                _layer_spec(layer, (4, DA_QK)),
                _const_spec((DEPTH, GROUP_W)),
                _layer_spec(layer, (32, GROUP_W)),
                _const_spec((DEPTH, GROUP_W)),
                _const_spec((DEPTH, GROUP_W)),
                _const_spec((DEPTH, GROUP_W))] + [pl.BlockSpec(memory_space=pl.ANY)] * len(prev)
    out_specs = [pl.BlockSpec((SEQ, D_MODEL), lambda b: (b, 0)),
                 head_blk, head_blk,
                 pl.BlockSpec((1, None, SEQ, MLA_KV_LORA), lambda b: (b, layer, 0, 0)),
                 pl.BlockSpec((1, None, SEQ, MLA_ROPE), lambda b: (b, layer, 0, 0)),
                 head_blk, head_blk]
    out_shape = [jax.ShapeDtypeStruct((BATCH * SEQ, D_MODEL), BF),
                 head_shp, head_shp,
                 jax.ShapeDtypeStruct((BATCH, DEPTH, SEQ, MLA_KV_LORA), F32),
                 jax.ShapeDtypeStruct((BATCH, DEPTH, SEQ, MLA_ROPE), F32),
                 head_shp, head_shp]
    n_in = len(in_specs) - len(prev)
    return pl.pallas_call(
        functools.partial(_ctx_kernel, layer, lam_init, len(prev)),
        grid=(BATCH,),
        in_specs=in_specs,
        out_specs=out_specs,
        out_shape=out_shape,
        input_output_aliases={n_in + k: 1 + k for k in range(len(prev))},
        scratch_shapes=[pltpu.VMEM((SEQ, P_TOT), F32), pltpu.VMEM((SEQ + 32, GROUP_W), F32),
                        pltpu.VMEM((7, SEQ + 8, GROUP_W), F32), pltpu.VMEM((P_TOT, D_MODEL), BF)],
        compiler_params=_cparams("arbitrary"),
        name=f"ctx_layer{layer}",
    )(x, prm["mod"], prm["g_mix"], prm["w_in"], prm["g_q"], prm["w_qup"], prm["g_kv"], prm["w_kvup"],
      prm["lamv"], prm["g_sub"], prm["dw"], prm["cb"], prm["ln_g"], prm["ln_b"], *prev)


def _lat_proj_kernel(layer, x_ref, mod_ref, gmix_ref, win_ref, gq_ref, wqup_ref, gkv_ref, wkv_ref,
                     cos_ref, sin_ref, cdak_ref, cdav_ref, cka_ref, ckb_ref, cmv_ref, cnak_ref, cnav_ref,
                     daq_ref, mq_ref, naq_ref, g_ref,
                     dak_ref, dav_ref, ka_ref, kb_ref, mv_ref, nak_ref, nav_ref, wbf_ref):
    b = pl.program_id(0)
    j = pl.program_id(1)

    @pl.when((b == 0) & (j == 0))
    def _():
        _stage_w_in(win_ref, wbf_ref)

    @pl.when(j < N_QT)
    def _():
        x = x_ref[...]
        sh1 = mod_ref[pl.ds(1 + b, 1), 0:D_MODEL]
        sc1 = mod_ref[pl.ds(1 + b, 1), D_MODEL:2 * D_MODEL]
        h = (_rms(x) * _row(gmix_ref, layer)) * (1.0 + sc1) + sh1
        hb = h.astype(BF)
        proj = jnp.concatenate([_qk(hb, wbf_ref[0:P_SPLIT, :]), _qk(hb, wbf_ref[P_SPLIT:P_TOT, :])], axis=1)
        cos = cos_ref[...]
        sin = sin_ref[...]

        def rope2(z):
            return jnp.concatenate([_rope(z[:, :128], cos[:, :128], sin[:, :128]),
                                    _rope(z[:, 128:], cos[:, 128:], sin[:, 128:])], axis=1)

        daq_ref[0] = (rope2(proj[:, P_DAQ:P_DAQ + GROUP_W]) * (DA_SCALE * LOG2E)).astype(BF)
        dak_ref[0] = rope2(proj[:, P_DAK:P_DAK + GROUP_W]).astype(BF)
        dav_ref[0] = proj[:, P_DAV:P_DAV + GROUP_W].astype(BF)

        qd = _rms(proj[:, P_QD:P_QD + GROUP_W]) * _row(gq_ref, layer)
        qm = _dot(qd.astype(BF), wqup_ref[...])
        for hh in range(N_HEADS):
            nope = qm[:, hh * GROUP_W:hh * GROUP_W + 128]
            rope = _rope(qm[:, hh * GROUP_W + 128:(hh + 1) * GROUP_W], cos[:, :128], sin[:, :128])
            mq_ref[0, hh] = (jnp.concatenate([nope, rope], axis=1) * (MLA_SCALE * LOG2E)).astype(BF)
        ckv = _rms(proj[:, P_KVD:P_KVD + MLA_KV_LORA]) * _row(gkv_ref, layer)
        kvm = _dot(ckv.astype(BF), wkv_ref[...])
        kn = kvm[:, :GROUP_W].astype(BF)
        krb = _rope(proj[:, P_KR:P_KR + 128], cos[:, :128], sin[:, :128]).astype(BF)
        ka_ref[0] = jnp.concatenate([kn[:, :128], krb], axis=1)
        kb_ref[0] = jnp.concatenate([kn[:, 128:], krb], axis=1)
        mv_ref[0] = kvm[:, GROUP_W:].astype(BF)

        naq_ref[0] = (proj[:, P_NAQ:P_NAQ + GROUP_W] * NA_SCALE).astype(BF)
        nak_ref[0] = proj[:, P_NAK:P_NAK + GROUP_W].astype(BF)
        nav_ref[0] = proj[:, P_NAV:P_NAV + GROUP_W].astype(BF)
        g_ref[0] = proj[:, P_CONV:P_CONV + GROUP_W] * jax.nn.sigmoid(proj[:, P_CONV + GROUP_W:P_TOT])

    @pl.when(j == N_QT)
    def _():
        dak_ref[0] = cdak_ref[0]
        dav_ref[0] = cdav_ref[0]
        ka_ref[0] = cka_ref[0]
        kb_ref[0] = ckb_ref[0]
        mv_ref[0] = cmv_ref[0]
        nak_ref[0] = cnak_ref[0]
        nav_ref[0] = cnav_ref[0]


def _lat_proj(layer, x, prm, cos_t, sin_t, caches):
    jq = lambda j: jnp.minimum(j, N_QT - 1)
    cache_spec = pl.BlockSpec((1, None, PAST_LEN, GROUP_W), lambda b, j: (b, layer, 0, 0))
    q_spec = pl.BlockSpec((1, TQ, GROUP_W), lambda b, j: (b, jq(j), 0))
    k_spec = pl.BlockSpec((1, TQ, GROUP_W), lambda b, j: (b, j, 0))
    in_specs = [pl.BlockSpec((TQ, D_MODEL), lambda b, j: (b * N_QT + jq(j), 0)),
                _layer_spec(layer, (16, N_MOD * D_MODEL)),
                _const_spec((DEPTH, D_MODEL)),
                _layer_spec(layer, (IN_COLS, D_MODEL)),
                _const_spec((DEPTH, GROUP_W)),
                _layer_spec(layer, (GROUP_W, N_HEADS * GROUP_W)),
                _const_spec((DEPTH, MLA_KV_LORA)),
                _layer_spec(layer, (MLA_KV_LORA, 2 * GROUP_W)),
                pl.BlockSpec((TQ, GROUP_W), lambda b, j: (jq(j), 0)),
                pl.BlockSpec((TQ, GROUP_W), lambda b, j: (jq(j), 0))] + [cache_spec] * 7
    q_shp = jax.ShapeDtypeStruct((DEC_BATCH, DEC_SEQ, GROUP_W), BF)
    k_shp = jax.ShapeDtypeStruct((DEC_BATCH, KEYS, GROUP_W), BF)
    out_specs = [q_spec,
                 pl.BlockSpec((1, N_HEADS, TQ, GROUP_W), lambda b, j: (b, 0, jq(j), 0)),
                 q_spec, q_spec] + [k_spec] * 7
    out_shape = [q_shp,
                 jax.ShapeDtypeStruct((DEC_BATCH, N_HEADS, DEC_SEQ, GROUP_W), BF),
                 q_shp,
                 jax.ShapeDtypeStruct((DEC_BATCH, DEC_SEQ, GROUP_W), F32)] + [k_shp] * 7
    return pl.pallas_call(
        functools.partial(_lat_proj_kernel, layer),
        grid=(DEC_BATCH, N_QT + 1),
        in_specs=in_specs,
        out_specs=out_specs,
        out_shape=out_shape,
        scratch_shapes=[pltpu.VMEM((P_TOT, D_MODEL), BF)],
        compiler_params=_cparams("arbitrary", "arbitrary"),
        name=f"lat_proj{layer}",
    )(x, prm["mod"], prm["g_mix"], prm["w_in"], prm["g_q"], prm["w_qup"], prm["g_kv"], prm["w_kvup"],
      cos_t, sin_t, *caches)


def _lat_attn_kernel(layer, lam_init, daq_ref, mq_ref, naq_ref, g_ref,
                     dak_ref, dav_ref, ka_ref, kb_ref, mv_ref, nak_ref, nav_ref,
                     nab_ref, lamv_ref, gsub_ref, dw_ref, cb_ref, lng_ref, lnb_ref,
                     mixed_ref, gpad_ref, zsh_ref, kt_ref):
    t = pl.program_id(1)
    lane = _lane_ids(GROUP_W)

    @pl.when(t == 0)
    def _():
        kt_ref[0] = dak_ref[0].T
        kt_ref[1] = ka_ref[0].T
        kt_ref[2] = kb_ref[0].T

    lam = _diff_lambda(lamv_ref, lam_init)
    qa = daq_ref[0].astype(F32)
    va = dav_ref[0]
    ka_t = kt_ref[0]
    vm = mv_ref[0]

    kc = nak_ref[0, DEC_SEQ:KEYS, :]
    vc = nav_ref[0, DEC_SEQ:KEYS, :]
    n_items = 2 * N_HEADS + ROWS_PER_TILE

    def na_window(j):
        r = t * ROWS_PER_TILE + j
        start = jnp.clip(r - NA_KR // 2, 0, N_ROWS - NA_KR)
        return start - r + NA_KR - 1, pl.multiple_of(start * GRID_W, GRID_W)

    def scores(i):
        if i < N_HEADS:
            return (_dot(_mask_heads(qa, lane, DA_QK, 2 * i), ka_t),
                    _dot(_mask_heads(qa, lane, DA_QK, 2 * i + 1), ka_t))
        if i < 2 * N_HEADS:
            hh = i - N_HEADS
            return (_dot(mq_ref[0, hh], kt_ref[1 + hh // 2]),)
        j = i - 2 * N_HEADS
        dr0, koff = na_window(j)
        qrow = naq_ref[0, j * GRID_W:(j + 1) * GRID_W, :].astype(F32)
        q4 = jnp.concatenate([_mask_heads(qrow, lane, HEAD_DIM, hh) for hh in range(N_HEADS)], axis=0)
        return (_qk(q4, nak_ref[0, pl.ds(koff, NA_LOCAL), :]) + nab_ref[dr0], _qk(q4, kc))

    o_da = jnp.zeros((TQ, GROUP_W), F32)
    o_mla = jnp.zeros((TQ, GROUP_W), F32)
    na_rows = []
    s_next = scores(0)

    base = pl.multiple_of(t * TQ, TQ)
    gpad_ref[16:16 + TQ] = g_ref[0, pl.ds(base, TQ), :]
    lo = g_ref[0, pl.ds(pl.multiple_of(jnp.maximum(base - 16, 0), 16), 16), :]
    hi = g_ref[0, pl.ds(pl.multiple_of(jnp.minimum(base + TQ, DEC_SEQ - 16), 16), 16), :]
    gpad_ref[0:16] = jnp.where(t > 0, lo, 0.0)
    gpad_ref[16 + TQ:32 + TQ] = jnp.where(t < N_QT - 1, hi, 0.0)
    o_conv = _conv_ln_silu(layer, gpad_ref, zsh_ref, TQ, dw_ref, cb_ref, lng_ref, lnb_ref)

    for i in range(n_items):
        s_cur = s_next
        if i + 1 < n_items:
            s_next = scores(i + 1)
        if i < N_HEADS:
            e1, r1 = _softmax2_parts(s_cur[0])
            e2, r2 = _softmax2_parts(s_cur[1])
            p = (e1 * r1 - e2 * (lam * r2)).astype(BF)
            o_da = jnp.where(_group_of(lane, HEAD_DIM) == i, _dot(p, va), o_da)
        elif i < 2 * N_HEADS:
            e, r = _softmax2_parts(s_cur[0])
            o_mla = jnp.where(_group_of(lane, HEAD_DIM) == i - N_HEADS, _dot(e.astype(BF), vm) * r, o_mla)
        else:
            _, koff = na_window(i - 2 * N_HEADS)
            s_loc, s_ctx = s_cur
            m = jnp.maximum(jnp.max(s_loc, axis=-1, keepdims=True), jnp.max(s_ctx, axis=-1, keepdims=True))
            e_loc = jnp.exp(s_loc - m)
            e_ctx = jnp.exp(s_ctx - m)
            den = jnp.sum(e_loc, axis=-1, keepdims=True) + jnp.sum(e_ctx, axis=-1, keepdims=True)
            o4 = (_dot(e_loc.astype(BF), nav_ref[0, pl.ds(koff, NA_LOCAL), :])
                  + _dot(e_ctx.astype(BF), vc)) * (1.0 / den)
            o_row = o4[0:GRID_W]
            for hh in range(1, N_HEADS):
                o_row = jnp.where(_group_of(lane, HEAD_DIM) == hh, o4[hh * GRID_W:(hh + 1) * GRID_W], o_row)
            na_rows.append(o_row)
    o_da = _head_rms(o_da) * _row(gsub_ref, layer) * (1.0 - lam_init)
    o_na = jnp.concatenate(na_rows, axis=0)

    mixed_ref[...] = jnp.concatenate([o_da, o_mla, o_na, o_conv], axis=1).astype(BF)


def _lat_attn(layer, proj_outs, nab, prm):
    lam_init = 0.8 - 0.6 * math.exp(-0.3 * layer)
    q_spec = pl.BlockSpec((1, TQ, GROUP_W), lambda b, t: (b, t, 0))
    full_k = pl.BlockSpec((1, KEYS, GROUP_W), lambda b, t: (b, 0, 0))
    in_specs = [q_spec,
                pl.BlockSpec((1, N_HEADS, TQ, GROUP_W), lambda b, t: (b, 0, t, 0)),
                q_spec,
                pl.BlockSpec((1, DEC_SEQ, GROUP_W), lambda b, t: (b, 0, 0))] + [full_k] * 7 + [
                _layer_spec(layer, (NA_KR, N_HEADS * GRID_W, NA_LOCAL)),
                _layer_spec(layer, (4, DA_QK)),
                _const_spec((DEPTH, GROUP_W)),
                _layer_spec(layer, (32, GROUP_W)),
                _const_spec((DEPTH, GROUP_W)),
                _const_spec((DEPTH, GROUP_W)),
                _const_spec((DEPTH, GROUP_W))]
    return pl.pallas_call(
        functools.partial(_lat_attn_kernel, layer, lam_init),
        grid=(DEC_BATCH, N_QT),
        in_specs=in_specs,
        out_specs=pl.BlockSpec((TQ, D_MODEL), lambda b, t: (b * N_QT + t, 0)),
        out_shape=jax.ShapeDtypeStruct((DEC_BATCH * DEC_SEQ, D_MODEL), BF),
        scratch_shapes=[pltpu.VMEM((TQ + 32, GROUP_W), F32), pltpu.VMEM((7, TQ + 8, GROUP_W), F32),
                        pltpu.VMEM((3, GROUP_W, KEYS), BF)],
        compiler_params=_cparams("arbitrary", "arbitrary"),
        name=f"lat_attn{layer}",
    )(*proj_outs, nab, prm["lamv"], prm["g_sub"], prm["dw"], prm["cb"], prm["ln_g"], prm["ln_b"])


def _out_ffn_kernel(layer, tiles_per_mod, mod_base, x_ref, mx_ref, mod_ref, wout_ref, gff_ref,
                    w1_ref, w2_ref, gfin_ref, o_ref):
    row = mod_base + pl.program_id(0) // tiles_per_mod
    g1 = mod_ref[pl.ds(row, 1), 2 * D_MODEL:3 * D_MODEL]
    sh2 = mod_ref[pl.ds(row, 1), 3 * D_MODEL:4 * D_MODEL]
    sc2 = mod_ref[pl.ds(row, 1), 4 * D_MODEL:5 * D_MODEL]
    g2 = mod_ref[pl.ds(row, 1), 5 * D_MODEL:6 * D_MODEL]
    x1 = x_ref[...] + g1 * _dot(mx_ref[...], wout_ref[...])
    h2 = ((_rms(x1) * _row(gff_ref, layer)) * (1.0 + sc2) + sh2).astype(BF)
    acc = jnp.zeros((TM_FFN, D_MODEL), F32)
    for c in range(D_FF // D_MODEL):
        a = jnp.maximum(_dot(h2, w1_ref[:, c * D_MODEL:(c + 1) * D_MODEL]), 0.0)
        acc = acc + _dot((a * a).astype(BF), w2_ref[c * D_MODEL:(c + 1) * D_MODEL, :])
    x2 = x1 + g2 * acc
    o_ref[...] = _rms(x2) * gfin_ref[...] if layer == DEPTH - 1 else x2


def _out_ffn(name, layer, tiles_per_mod, mod_base, x, mixed, prm, g_final):
    n = x.shape[0]
    tile = pl.BlockSpec((TM_FFN, D_MODEL), lambda i: (i, 0))
    return pl.pallas_call(
        functools.partial(_out_ffn_kernel, layer, tiles_per_mod, mod_base),
        grid=(n // TM_FFN,),
        in_specs=[tile, tile,
                  _layer_spec(layer, (16, N_MOD * D_MODEL)),
                  _layer_spec(layer, (D_MODEL, D_MODEL)),
                  _const_spec((DEPTH, D_MODEL)),
                  _layer_spec(layer, (D_MODEL, D_FF)),
                  _layer_spec(layer, (D_FF, D_MODEL)),
                  _const_spec((1, D_MODEL))],
        out_specs=tile,
        out_shape=jax.ShapeDtypeStruct((n, D_MODEL), F32),
        compiler_params=_cparams("arbitrary"),
        name=name,
    )(x, mixed, prm["mod"], prm["w_out"], prm["g_ff"], prm["w_ff1"], prm["w_ff2"], g_final)


def _rope_tables():
    t = np.arange(DEC_SEQ)
    rows = (t // GRID_W).astype(np.float32)
    cols = (t % GRID_W).astype(np.float32)
    c = np.arange(GROUP_W) % 32
    freqs = np.float32(ROPE_BASE) ** (-np.arange(8, dtype=np.float32) * np.float32(2.0) / np.float32(16))
    pos = np.where((c < 16)[None, :], rows[:, None], cols[:, None]).astype(np.float32)
    ang = (pos * freqs[(c % 16) % 8][None, :]).astype(np.float32)
    first = ((c % 16) < 8)[None, :]
    cos = np.cos(ang).astype(np.float32)
    sin = np.sin(ang).astype(np.float32)
    return jnp.asarray(cos), jnp.asarray(np.where(first, -sin, sin))


def _qup_gather_index():
    idx = np.full((N_HEADS * GROUP_W,), -1, np.int64)
    for h in range(N_HEADS):
        src = h * (MLA_NOPE + MLA_ROPE)
        dst = h * GROUP_W + (h % 2) * MLA_NOPE
        idx[dst:dst + MLA_NOPE] = np.arange(src, src + MLA_NOPE)
        idx[h * GROUP_W + 128:h * GROUP_W + 128 + MLA_ROPE] = np.arange(src + MLA_NOPE, src + MLA_NOPE + MLA_ROPE)
    return idx


def _kvup_perm():
    k = [h * 128 + d for h in range(N_HEADS) for d in range(MLA_NOPE)]
    v = [h * 128 + MLA_NOPE + d for h in range(N_HEADS) for d in range(HEAD_DIM)]
    return np.asarray(k + v)


def _heads_to_lanes(c):
    b, l, h, s, d = c.shape
    return c.transpose(0, 1, 3, 2, 4).reshape(b, l, s, h * d).astype(BF)


def kernel(x_prompt, x_sample, c, cache_da_k, cache_da_v, cache_mla_ckv, cache_mla_krope, cache_na_k, cache_na_v, c_ctx, w_mod, b_mod, g_norm_mix, g_norm_ff, w_in, da_lambda_q1, da_lambda_k1, da_lambda_q2, da_lambda_k2, g_da_subln, g_mla_q, w_mla_qup, g_mla_kv, w_mla_kvup, na_rpb, conv_dw, conv_b, conv_ln_g, conv_ln_b, w_out, w_ff1, w_ff2, g_final):
    qidx = _qup_gather_index()
    w_qup_e = jnp.where(jnp.asarray(qidx >= 0)[None, None, :],
                        jnp.take(w_mla_qup, jnp.asarray(np.maximum(qidx, 0)), axis=-1), 0.0).astype(BF)
    w_kvup_p = jnp.take(w_mla_kvup, jnp.asarray(_kvup_perm()), axis=-1).astype(BF)
    prm = dict(
        g_mix=g_norm_mix, w_in=jnp.swapaxes(w_in, 1, 2), g_q=g_mla_q, w_qup=w_qup_e, g_kv=g_mla_kv, w_kvup=w_kvup_p,
        lamv=jnp.stack([da_lambda_q1, da_lambda_k1, da_lambda_q2, da_lambda_k2], axis=1),
        g_sub=jnp.tile(g_da_subln, (1, N_HEADS)),
        dw=jnp.concatenate([conv_dw, jnp.zeros((DEPTH, 1, GROUP_W), F32)], axis=1),
        cb=conv_b, ln_g=conv_ln_g, ln_b=conv_ln_b,
        w_out=w_out.astype(BF), g_ff=g_norm_ff, w_ff1=w_ff1.astype(BF), w_ff2=w_ff2.astype(BF))
    cv = jnp.concatenate([c_ctx[None, :], c, jnp.zeros((16 - 1 - DEC_BATCH, D_MODEL), F32)], axis=0)
    cos_t, sin_t = _rope_tables()
    g_final2 = g_final.reshape(1, D_MODEL)

    prm["mod"] = _modulation(cv, w_mod, b_mod)
    c_kr_pad = jnp.pad(cache_mla_krope, ((0, 0), (0, 0), (0, 0), (0, KR_PAD)))
    cka, ckb, cmv = _mla_cache(cache_mla_ckv, c_kr_pad, w_kvup_p)
    caches = (_heads_to_lanes(cache_da_k), _heads_to_lanes(cache_da_v), cka, ckb, cmv,
              _heads_to_lanes(cache_na_k), _heads_to_lanes(cache_na_v))
    nab = _na_bias(na_rpb)

    xp = x_prompt.reshape(BATCH * SEQ, D_MODEL)
    xs = x_sample.reshape(DEC_BATCH * DEC_SEQ, D_MODEL)
    new_ctx = ()
    for l in range(DEPTH):
        mixed_p, *new_ctx = _ctx_layer(l, xp, prm, tuple(new_ctx))
        xp = _out_ffn(f"ctx_ffn{l}", l, BATCH * SEQ // TM_FFN, 0, xp, mixed_p, prm, g_final2)
        proj_outs = _lat_proj(l, xs, prm, cos_t, sin_t, caches)
        mixed_s = _lat_attn(l, proj_outs, nab, prm)
        xs = _out_ffn(f"lat_ffn{l}", l, DEC_SEQ // TM_FFN, 1, xs, mixed_s, prm, g_final2)

    y_prompt = xp.reshape(BATCH, SEQ, D_MODEL)
    y_sample = xs.reshape(DEC_BATCH, DEC_SEQ, D_MODEL)
    return (y_prompt, y_sample) + tuple(new_ctx)
```

```python
import functools
import math

import numpy as np
import jax
import jax.numpy as jnp
from jax import lax
from jax.experimental import pallas as pl
from jax.experimental.pallas import tpu as pltpu

F32 = jnp.float32
BF = jnp.bfloat16

D_MODEL = 1024
BATCH = 16
SEQ = 256
DEPTH = 2
DEC_BATCH = 2
DEC_SEQ = 2048
PAST_LEN = 256
GRID_W = 64
GROUP_W = 256
HEAD_DIM = 64
N_HEADS = 4
DA_QK = 32
MLA_NOPE = 64
MLA_ROPE = 32
MLA_KV_LORA = 128
NA_KR = 8
NA_KC = 16
CONV_W = 31
D_FF = 4096
ROPE_BASE = 10000.0
EPS = 1e-6
N_MOD = 6
IN_COLS = 2464

P_DAQ, P_DAK, P_DAV, P_QD, P_KVD, P_KR = 0, 256, 512, 768, 1024, 1152
P_NAQ, P_NAK, P_NAV, P_CONV, P_TOT = 1280, 1536, 1792, 2048, 2560
P_SPLIT = P_NAQ
N_CTX_IN = 14
KR_ORIG_END = 1184
KR_PAD = 128 - MLA_ROPE

DA_SCALE = DA_QK ** -0.5
MLA_SCALE = (MLA_NOPE + MLA_ROPE) ** -0.5
NA_SCALE = HEAD_DIM ** -0.5
LOG2E = math.log2(math.e)

TQ = 256
N_QT = DEC_SEQ // TQ
KEYS = DEC_SEQ + PAST_LEN
ROWS_PER_TILE = TQ // GRID_W
N_ROWS = DEC_SEQ // GRID_W
NA_LOCAL = NA_KR * GRID_W
TM_FFN = 512
TM_PROJ = 512
VMEM_LIMIT = 58 * 1024 * 1024

NT_DIMS = (((1,), (1,)), ((), ()))


def _cparams(*sem):
    return pltpu.CompilerParams(dimension_semantics=sem, vmem_limit_bytes=VMEM_LIMIT)


def _const_spec(shape):
    nd = len(shape)
    return pl.BlockSpec(shape, lambda *_: (0,) * nd, pipeline_mode=pl.Buffered(1))


def _layer_spec(layer, shape):
    nd = len(shape)
    return pl.BlockSpec((None,) + tuple(shape), lambda *_: (layer,) + (0,) * nd, pipeline_mode=pl.Buffered(1))


def _rms(x):
    return x * lax.rsqrt(jnp.mean(x * x, axis=-1, keepdims=True) + EPS)


def _row(ref, layer):
    return ref[layer:layer + 1, :]


def _dot(a, b):
    return jnp.dot(a, b, preferred_element_type=F32)


def _qk(q, k):
    return lax.dot_general(q, k, NT_DIMS, preferred_element_type=F32)


def _softmax_parts(s):
    m = jnp.max(s, axis=-1, keepdims=True)
    e = jnp.exp(s - m)
    return e, 1.0 / jnp.sum(e, axis=-1, keepdims=True)


def _softmax2_parts(s):
    m = jnp.max(s, axis=-1, keepdims=True)
    e = jnp.exp2(s - m)
    return e, 1.0 / jnp.sum(e, axis=-1, keepdims=True)


def _lane_ids(width):
    return lax.broadcasted_iota(jnp.int32, (1, width), 1)


def _diff_lambda(lamv_ref, lam_init):
    v = lamv_ref[...]
    a = jnp.exp(jnp.sum(v[0:1] * v[1:2], axis=-1, keepdims=True))
    b = jnp.exp(jnp.sum(v[2:3] * v[3:4], axis=-1, keepdims=True))
    return a - b + lam_init


def _group_of(lane, group):
    return lane >> (group.bit_length() - 1)


def _head_rms(x):
    r = _group_of(lax.broadcasted_iota(jnp.int32, (GROUP_W, GROUP_W), 0), HEAD_DIM)
    c = _group_of(lax.broadcasted_iota(jnp.int32, (GROUP_W, GROUP_W), 1), HEAD_DIM)
    ones_bd = jnp.where(r == c, 1.0, 0.0).astype(BF)
    sq = x * x
    hi = sq.astype(BF)
    lo = (sq - hi.astype(F32)).astype(BF)
    ss = _dot(hi, ones_bd) + _dot(lo, ones_bd)
    return x * lax.rsqrt(ss * (1.0 / HEAD_DIM) + EPS)


def _rope(z, cos, sin):
    lane = _lane_ids(z.shape[1])
    swapped = jnp.where((lane & 15) < 8, pltpu.roll(z, 120, 1), pltpu.roll(z, 8, 1))
    return z * cos + swapped * sin


def _stage_w_in(wint_ref, wbt_ref):
    wbt_ref[0:KR_ORIG_END, :] = wint_ref[0:KR_ORIG_END, :].astype(BF)
    wbt_ref[KR_ORIG_END:P_SPLIT, :] = jnp.zeros((P_SPLIT - KR_ORIG_END, D_MODEL), BF)
    wbt_ref[P_SPLIT:P_TOT, :] = wint_ref[KR_ORIG_END:IN_COLS, :].astype(BF)


def _mask_heads(qf, lane, group, idx):
    return jnp.where(_group_of(lane, group) == idx, qf, 0.0).astype(BF)


def _conv_ln_silu(layer, gpad_ref, zsh_ref, n, dw_ref, cb_ref, lng_ref, lnb_ref):
    y = jnp.zeros((n, GROUP_W), F32) + _row(cb_ref, layer)
    for b in range(8):
        z = None
        for a in range(4):
            t = 8 * a + b - 1
            if 0 <= t < CONV_W:
                term = gpad_ref[pl.ds(8 * a, n + 8), :] * dw_ref[t:t + 1, :]
                z = term if z is None else z + term
        if b == 0:
            y = y + z[0:n]
        else:
            zsh_ref[b - 1] = z
            y = y + zsh_ref[b - 1, pl.ds(b, n), :]
    mu = jnp.mean(y, axis=-1, keepdims=True)
    yc = y - mu
    var = jnp.mean(yc * yc, axis=-1, keepdims=True)
    z = yc * lax.rsqrt(var + EPS) * _row(lng_ref, layer) + _row(lnb_ref, layer)
    return z * jax.nn.sigmoid(z)


def _mod_kernel(cv_ref, w_ref, b_ref, o_ref):
    c = cv_ref[...]
    a = (c * jax.nn.sigmoid(c)).astype(BF)
    o_ref[0] = _dot(a, w_ref[0].astype(BF)) + b_ref[0]


def _modulation(cv, w_mod, b_mod):
    nblk = 2
    wblk = N_MOD * D_MODEL // nblk
    return pl.pallas_call(
        _mod_kernel,
        grid=(DEPTH, nblk),
        in_specs=[pl.BlockSpec((16, D_MODEL), lambda l, j: (0, 0)),
                  pl.BlockSpec((1, D_MODEL, wblk), lambda l, j: (l, 0, j)),
                  pl.BlockSpec((1, 1, wblk), lambda l, j: (l, 0, j))],
        out_specs=pl.BlockSpec((1, 16, wblk), lambda l, j: (l, 0, j)),
        out_shape=jax.ShapeDtypeStruct((DEPTH, 16, N_MOD * D_MODEL), F32),
        compiler_params=_cparams("arbitrary", "arbitrary"),
        name="modulation",
    )(cv, w_mod, b_mod.reshape(DEPTH, 1, N_MOD * D_MODEL))


def _mla_cache_kernel(ckv_ref, kr_ref, w_ref, ka_ref, kb_ref, v_ref):
    kvm = _dot(ckv_ref[0, 0].astype(BF), w_ref[0])
    kr = kr_ref[0, 0].astype(BF)
    kn = kvm[:, :GROUP_W].astype(BF)
    ka_ref[0, 0] = jnp.concatenate([kn[:, :128], kr], axis=1)
    kb_ref[0, 0] = jnp.concatenate([kn[:, 128:], kr], axis=1)
    v_ref[0, 0] = kvm[:, GROUP_W:].astype(BF)


def _mla_cache(c_ckv, c_kr_pad, w_kvup_p):
    blk = pl.BlockSpec((1, 1, PAST_LEN, GROUP_W), lambda b, l: (b, l, 0, 0))
    shp = jax.ShapeDtypeStruct((DEC_BATCH, DEPTH, PAST_LEN, GROUP_W), BF)
    return pl.pallas_call(
        _mla_cache_kernel,
        grid=(DEC_BATCH, DEPTH),
        in_specs=[pl.BlockSpec((1, 1, PAST_LEN, MLA_KV_LORA), lambda b, l: (b, l, 0, 0)),
                  pl.BlockSpec((1, 1, PAST_LEN, 128), lambda b, l: (b, l, 0, 0)),
                  pl.BlockSpec((1, MLA_KV_LORA, 2 * GROUP_W), lambda b, l: (l, 0, 0))],
        out_specs=[blk, blk, blk],
        out_shape=[shp, shp, shp],
        compiler_params=_cparams("arbitrary", "arbitrary"),
        name="mla_cache",
    )(c_ckv, c_kr_pad, w_kvup_p)


N_DR = 2 * NA_KR - 1
N_DC = 2 * NA_KC - 1


def _na_bias_kernel(rpb_ref, o_ref, tp_ref):
    base = (pl.program_id(0) * N_HEADS + pl.program_id(1)) * (N_DR * N_DC)
    cq = lax.broadcasted_iota(jnp.int32, (GRID_W, 128), 0)
    lane = lax.broadcasted_iota(jnp.int32, (GRID_W, 128), 1)
    ck = lane & (GRID_W - 1)
    upper = lane >= GRID_W
    cstart = jnp.clip(cq - NA_KC // 2, 0, GRID_W - NA_KC)
    ok = (ck >= cstart) & (ck < cstart + NA_KC)

    j = lax.broadcasted_iota(jnp.int32, (8, 128), 1)
    dclip = jnp.clip(jnp.where(j < GRID_W, j, j - 128), -(NA_KC - 1), NA_KC - 1) + NA_KC - 1

    def toeplitz(dr):
        u = jnp.zeros((8, 128), F32)
        for d in range(N_DC):
            u = jnp.where(dclip == d, rpb_ref[base + dr * N_DC + d], u)
        rows = jnp.broadcast_to(u[0:1], (GRID_W, 128))
        return pltpu.roll(rows, 0, 1, stride=1, stride_axis=0)

    tabs = [toeplitz(dr) for dr in range(N_DR)]
    for dr in range(N_DR - 1):
        val = jnp.where(upper, pltpu.roll(tabs[dr + 1], GRID_W, 1), tabs[dr])
        tp_ref[dr] = jnp.where(ok, val, -jnp.inf)
    for dr0 in range(NA_KR):
        o_ref[0, dr0] = jnp.concatenate([tp_ref[dr0 + 2 * m] for m in range(NA_KR // 2)], axis=1)


def _na_bias(rpb):
    return pl.pallas_call(
        _na_bias_kernel,
        grid=(DEPTH, N_HEADS),
        in_specs=[pl.BlockSpec(memory_space=pltpu.SMEM)],
        out_specs=pl.BlockSpec((1, NA_KR, GRID_W, NA_LOCAL), lambda l, h: (l, 0, h, 0)),
        out_shape=jax.ShapeDtypeStruct((DEPTH, NA_KR, N_HEADS * GRID_W, NA_LOCAL), F32),
        scratch_shapes=[pltpu.VMEM((N_DR - 1, GRID_W, 128), F32)],
        compiler_params=_cparams("arbitrary", "arbitrary"),
        name="na_bias",
    )(rpb.reshape(-1))


def _ctx_kernel(layer, lam_init, n_prev, *refs):
    (x_ref, mod_ref, gmix_ref, win_ref, gq_ref, wqup_ref, gkv_ref, wkv_ref,
     lamv_ref, gsub_ref, dw_ref, cb_ref, lng_ref, lnb_ref) = refs[:N_CTX_IN]
    (mixed_ref, dak_ref, dav_ref, ckv_ref, kr_ref, nak_ref, nav_ref,
     proj_ref, gpad_ref, zsh_ref, wbf_ref) = refs[N_CTX_IN + n_prev:]

    @pl.when(pl.program_id(0) == 0)
    def _():
        _stage_w_in(win_ref, wbf_ref)

    x = x_ref[...]
    sh1 = mod_ref[0:1, 0:D_MODEL]
    sc1 = mod_ref[0:1, D_MODEL:2 * D_MODEL]
    h = ((_rms(x) * _row(gmix_ref, layer)) * (1.0 + sc1) + sh1).astype(BF)
    proj_ref[:, 0:P_SPLIT] = _qk(h, wbf_ref[0:P_SPLIT, :])
    proj_ref[:, P_SPLIT:P_TOT] = _qk(h, wbf_ref[P_SPLIT:P_TOT, :])

    for hh in range(N_HEADS):
        lo, hi = hh * HEAD_DIM, (hh + 1) * HEAD_DIM
        dak_ref[0, hh] = proj_ref[:, P_DAK + lo:P_DAK + hi]
        dav_ref[0, hh] = proj_ref[:, P_DAV + lo:P_DAV + hi]
        nak_ref[0, hh] = proj_ref[:, P_NAK + lo:P_NAK + hi]
        nav_ref[0, hh] = proj_ref[:, P_NAV + lo:P_NAV + hi]

    lane = _lane_ids(GROUP_W)

    lam = _diff_lambda(lamv_ref, lam_init)

    qa = proj_ref[:, P_DAQ:P_DAQ + GROUP_W] * (DA_SCALE * LOG2E)
    ka = proj_ref[:, P_DAK:P_DAK + GROUP_W].astype(BF)
    va = proj_ref[:, P_DAV:P_DAV + GROUP_W].astype(BF)
    qn = proj_ref[:, P_NAQ:P_NAQ + GROUP_W] * (NA_SCALE * LOG2E)
    kn2 = proj_ref[:, P_NAK:P_NAK + GROUP_W].astype(BF)
    vn = proj_ref[:, P_NAV:P_NAV + GROUP_W].astype(BF)
    mla = {}

    def mla_prep():
        qd = _rms(proj_ref[:, P_QD:P_QD + GROUP_W]) * _row(gq_ref, layer)
        ckv = _rms(proj_ref[:, P_KVD:P_KVD + MLA_KV_LORA]) * _row(gkv_ref, layer)
        ckv_ref[0] = ckv
        kr_pad = proj_ref[:, P_KR:P_KR + 128]
        kr_ref[0] = kr_pad[:, 0:MLA_ROPE]
        mla["q"] = (_dot(qd.astype(BF), wqup_ref[...]) * (MLA_SCALE * LOG2E)).astype(BF)
        kvm = _dot(ckv.astype(BF), wkv_ref[...])
        kn = kvm[:, :GROUP_W].astype(BF)
        krb = kr_pad.astype(BF)
        mla["k"] = (jnp.concatenate([kn[:, :128], krb], axis=1), jnp.concatenate([kn[:, 128:], krb], axis=1))
        mla["v"] = kvm[:, GROUP_W:].astype(BF)

    def conv():
        g = proj_ref[:, P_CONV:P_CONV + GROUP_W] * jax.nn.sigmoid(proj_ref[:, P_CONV + GROUP_W:P_TOT])
        gpad_ref[0:16] = jnp.zeros((16, GROUP_W), F32)
        gpad_ref[16 + SEQ:32 + SEQ] = jnp.zeros((16, GROUP_W), F32)
        gpad_ref[16:16 + SEQ] = g
        return _conv_ln_silu(layer, gpad_ref, zsh_ref, SEQ, dw_ref, cb_ref, lng_ref, lnb_ref)

    def scores(i):
        if i < N_HEADS:
            return (_qk(_mask_heads(qa, lane, DA_QK, 2 * i), ka),
                    _qk(_mask_heads(qa, lane, DA_QK, 2 * i + 1), ka))
        if i < 2 * N_HEADS:
            hh = i - N_HEADS
            return (_qk(mla["q"][:, hh * GROUP_W:(hh + 1) * GROUP_W], mla["k"][hh // 2]),)
        return (_qk(_mask_heads(qn, lane, HEAD_DIM, i - 2 * N_HEADS), kn2),)

    outs = [jnp.zeros((SEQ, GROUP_W), F32)] * 3
    s_next = scores(0)
    o_conv = None
    for i in range(3 * N_HEADS):
        s_cur = s_next
        if i + 1 < 3 * N_HEADS:
            s_next = scores(i + 1)
        if i == 0:
            mla_prep()
        if i == N_HEADS - 1:
            o_conv = conv()
        grp, hh = divmod(i, N_HEADS)
        if grp == 0:
            e1, r1 = _softmax2_parts(s_cur[0])
            e2, r2 = _softmax2_parts(s_cur[1])
            o = _dot((e1 * r1 - e2 * (lam * r2)).astype(BF), va)
        else:
            e, r = _softmax2_parts(s_cur[0])
            o = _dot(e.astype(BF), mla["v"] if grp == 1 else vn) * r
        outs[grp] = jnp.where(_group_of(lane, HEAD_DIM) == hh, o, outs[grp])
    o_da = _head_rms(outs[0]) * _row(gsub_ref, layer) * (1.0 - lam_init)
    o_mla, o_na = outs[1], outs[2]

    mixed_ref[...] = jnp.concatenate([o_da, o_mla, o_na, o_conv], axis=1).astype(BF)


def _ctx_layer(layer, x, prm, prev):
    lam_init = 0.8 - 0.6 * math.exp(-0.3 * layer)
    head_blk = pl.BlockSpec((1, None, N_HEADS, SEQ, HEAD_DIM), lambda b: (b, layer, 0, 0, 0))
    head_shp = jax.ShapeDtypeStruct((BATCH, DEPTH, N_HEADS, SEQ, HEAD_DIM), F32)
    in_specs = [pl.BlockSpec((SEQ, D_MODEL), lambda b: (b, 0)),
                _layer_spec(layer, (16, N_MOD * D_MODEL)),
                _const_spec((DEPTH, D_MODEL)),
                _layer_spec(layer, (IN_COLS, D_MODEL)),
                _const_spec((DEPTH, GROUP_W)),
                _layer_spec(layer, (GROUP_W, N_HEADS * GROUP_W)),
                _const_spec((DEPTH, MLA_KV_LORA)),
                _layer_spec(layer, (MLA_KV_LORA, 2 * GROUP_W)),
                _layer_spec(layer, (4, DA_QK)),
                _const_spec((DEPTH, GROUP_W)),
                _layer_spec(layer, (32, GROUP_W)),
                _const_spec((DEPTH, GROUP_W)),
                _const_spec((DEPTH, GROUP_W)),
                _const_spec((DEPTH, GROUP_W))] + [pl.BlockSpec(memory_space=pl.ANY)] * len(prev)
    out_specs = [pl.BlockSpec((SEQ, D_MODEL), lambda b: (b, 0)),
                 head_blk, head_blk,
                 pl.BlockSpec((1, None, SEQ, MLA_KV_LORA), lambda b: (b, layer, 0, 0)),
                 pl.BlockSpec((1, None, SEQ, MLA_ROPE), lambda b: (b, layer, 0, 0)),
                 head_blk, head_blk]
    out_shape = [jax.ShapeDtypeStruct((BATCH * SEQ, D_MODEL), BF),
                 head_shp, head_shp,
                 jax.ShapeDtypeStruct((BATCH, DEPTH, SEQ, MLA_KV_LORA), F32),
                 jax.ShapeDtypeStruct((BATCH, DEPTH, SEQ, MLA_ROPE), F32),
                 head_shp, head_shp]
    n_in = len(in_specs) - len(prev)
    return pl.pallas_call(
        functools.partial(_ctx_kernel, layer, lam_init, len(prev)),
        grid=(BATCH,),
        in_specs=in_specs,
        out_specs=out_specs,
        out_shape=out_shape,
        input_output_aliases={n_in + k: 1 + k for k in range(len(prev))},
        scratch_shapes=[pltpu.VMEM((SEQ, P_TOT), F32), pltpu.VMEM((SEQ + 32, GROUP_W), F32),
                        pltpu.VMEM((7, SEQ + 8, GROUP_W), F32), pltpu.VMEM((P_TOT, D_MODEL), BF)],
        compiler_params=_cparams("arbitrary"),
        name=f"ctx_layer{layer}",
    )(x, prm["mod"], prm["g_mix"], prm["w_in"], prm["g_q"], prm["w_qup"], prm["g_kv"], prm["w_kvup"],
      prm["lamv"], prm["g_sub"], prm["dw"], prm["cb"], prm["ln_g"], prm["ln_b"], *prev)


def _lat_proj_kernel(layer, x_ref, mod_ref, gmix_ref, win_ref, gq_ref, wqup_ref, gkv_ref, wkv_ref,
                     cos_ref, sin_ref,
                     daq_ref, mq_ref, naq_ref, g_ref,
                     dak_ref, dav_ref, ka_ref, kb_ref, mv_ref, nak_ref, nav_ref, wbf_ref):
    b = pl.program_id(0)

    @pl.when((b == 0) & (pl.program_id(1) == 0))
    def _():
        _stage_w_in(win_ref, wbf_ref)

    x = x_ref[...]
    sh1 = mod_ref[pl.ds(1 + b, 1), 0:D_MODEL]
    sc1 = mod_ref[pl.ds(1 + b, 1), D_MODEL:2 * D_MODEL]
    h = (_rms(x) * _row(gmix_ref, layer)) * (1.0 + sc1) + sh1
    hb = h.astype(BF)
    proj = jnp.concatenate([_qk(hb, wbf_ref[0:P_SPLIT, :]), _qk(hb, wbf_ref[P_SPLIT:P_TOT, :])], axis=1)
    cos = cos_ref[...]
    sin = sin_ref[...]

    def rope2(z):
        return jnp.concatenate([_rope(z[:, :128], cos[:, :128], sin[:, :128]),
                                _rope(z[:, 128:], cos[:, 128:], sin[:, 128:])], axis=1)

    daq_ref[0] = (rope2(proj[:, P_DAQ:P_DAQ + GROUP_W]) * (DA_SCALE * LOG2E)).astype(BF)
    dak_ref[0] = rope2(proj[:, P_DAK:P_DAK + GROUP_W]).astype(BF)
    dav_ref[0] = proj[:, P_DAV:P_DAV + GROUP_W].astype(BF)

    qd = _rms(proj[:, P_QD:P_QD + GROUP_W]) * _row(gq_ref, layer)
    qm = _dot(qd.astype(BF), wqup_ref[...])
    for hh in range(N_HEADS):
        nope = qm[:, hh * GROUP_W:hh * GROUP_W + 128]
        rope = _rope(qm[:, hh * GROUP_W + 128:(hh + 1) * GROUP_W], cos[:, :128], sin[:, :128])
        mq_ref[0, hh] = (jnp.concatenate([nope, rope], axis=1) * (MLA_SCALE * LOG2E)).astype(BF)
    ckv = _rms(proj[:, P_KVD:P_KVD + MLA_KV_LORA]) * _row(gkv_ref, layer)
    kvm = _dot(ckv.astype(BF), wkv_ref[...])
    kn = kvm[:, :GROUP_W].astype(BF)
    krb = _rope(proj[:, P_KR:P_KR + 128], cos[:, :128], sin[:, :128]).astype(BF)
    ka_ref[0] = jnp.concatenate([kn[:, :128], krb], axis=1)
    kb_ref[0] = jnp.concatenate([kn[:, 128:], krb], axis=1)
    mv_ref[0] = kvm[:, GROUP_W:].astype(BF)

    naq_ref[0] = (proj[:, P_NAQ:P_NAQ + GROUP_W] * NA_SCALE).astype(BF)
    nak_ref[0] = proj[:, P_NAK:P_NAK + GROUP_W].astype(BF)
    nav_ref[0] = proj[:, P_NAV:P_NAV + GROUP_W].astype(BF)
    g_ref[0] = proj[:, P_CONV:P_CONV + GROUP_W] * jax.nn.sigmoid(proj[:, P_CONV + GROUP_W:P_TOT])


def _lat_proj(layer, x, prm, cos_t, sin_t):
    n_t = DEC_SEQ // TM_PROJ
    row_spec = pl.BlockSpec((1, TM_PROJ, GROUP_W), lambda b, j: (b, j, 0))
    in_specs = [pl.BlockSpec((TM_PROJ, D_MODEL), lambda b, j: (b * n_t + j, 0)),
                _layer_spec(layer, (16, N_MOD * D_MODEL)),
                _const_spec((DEPTH, D_MODEL)),
                _layer_spec(layer, (IN_COLS, D_MODEL)),
                _const_spec((DEPTH, GROUP_W)),
                _layer_spec(layer, (GROUP_W, N_HEADS * GROUP_W)),
                _const_spec((DEPTH, MLA_KV_LORA)),
                _layer_spec(layer, (MLA_KV_LORA, 2 * GROUP_W)),
                pl.BlockSpec((TM_PROJ, GROUP_W), lambda b, j: (j, 0)),
                pl.BlockSpec((TM_PROJ, GROUP_W), lambda b, j: (j, 0))]
    row_shp = jax.ShapeDtypeStruct((DEC_BATCH, DEC_SEQ, GROUP_W), BF)
    out_specs = [row_spec,
                 pl.BlockSpec((1, N_HEADS, TM_PROJ, GROUP_W), lambda b, j: (b, 0, j, 0)),
                 row_spec, row_spec] + [row_spec] * 7
    out_shape = [row_shp,
                 jax.ShapeDtypeStruct((DEC_BATCH, N_HEADS, DEC_SEQ, GROUP_W), BF),
                 row_shp,
                 jax.ShapeDtypeStruct((DEC_BATCH, DEC_SEQ, GROUP_W), F32)] + [row_shp] * 7
    return pl.pallas_call(
        functools.partial(_lat_proj_kernel, layer),
        grid=(DEC_BATCH, n_t),
        in_specs=in_specs,
        out_specs=out_specs,
        out_shape=out_shape,
        scratch_shapes=[pltpu.VMEM((P_TOT, D_MODEL), BF)],
        compiler_params=_cparams("arbitrary", "arbitrary"),
        name=f"lat_proj{layer}",
    )(x, prm["mod"], prm["g_mix"], prm["w_in"], prm["g_q"], prm["w_qup"], prm["g_kv"], prm["w_kvup"],
      cos_t, sin_t)


def _lat_attn_kernel(layer, lam_init, daq_ref, mq_ref, naq_ref, g_ref,
                     dak_ref, dav_ref, ka_ref, kb_ref, mv_ref, nak_ref, nav_ref,
                     cdak_ref, cdav_ref, cka_ref, ckb_ref, cmv_ref, cnak_ref, cnav_ref,
                     nab_ref, lamv_ref, gsub_ref, dw_ref, cb_ref, lng_ref, lnb_ref,
                     mixed_ref, gpad_ref, zsh_ref, kt_ref, v_ref):
    t = pl.program_id(1)
    lane = _lane_ids(GROUP_W)

    @pl.when(t == 0)
    def _():
        for slot, (lat, cache) in enumerate(((dak_ref, cdak_ref), (ka_ref, cka_ref), (kb_ref, ckb_ref))):
            kt_ref[slot, :, 0:DEC_SEQ] = lat[0].T
            kt_ref[slot, :, DEC_SEQ:KEYS] = cache[0].T
        for slot, (lat, cache) in enumerate(((dav_ref, cdav_ref), (mv_ref, cmv_ref))):
            v_ref[slot, 0:DEC_SEQ, :] = lat[0]
            v_ref[slot, DEC_SEQ:KEYS, :] = cache[0]

    lam = _diff_lambda(lamv_ref, lam_init)
    qa = daq_ref[0].astype(F32)
    va = v_ref[0]
    ka_t = kt_ref[0]
    vm = v_ref[1]

    kc = cnak_ref[0]
    vc = cnav_ref[0]
    n_items = 2 * N_HEADS + ROWS_PER_TILE

    def na_window(j):
        r = t * ROWS_PER_TILE + j
        start = jnp.clip(r - NA_KR // 2, 0, N_ROWS - NA_KR)
        return start - r + NA_KR - 1, pl.multiple_of(start * GRID_W, GRID_W)

    def scores(i):
        if i < N_HEADS:
            return (_dot(_mask_heads(qa, lane, DA_QK, 2 * i), ka_t),
                    _dot(_mask_heads(qa, lane, DA_QK, 2 * i + 1), ka_t))
        if i < 2 * N_HEADS:
            hh = i - N_HEADS
            return (_dot(mq_ref[0, hh], kt_ref[1 + hh // 2]),)
        j = i - 2 * N_HEADS
        dr0, koff = na_window(j)
        qrow = naq_ref[0, j * GRID_W:(j + 1) * GRID_W, :].astype(F32)
        q4 = jnp.concatenate([_mask_heads(qrow, lane, HEAD_DIM, hh) for hh in range(N_HEADS)], axis=0)
        return (_qk(q4, nak_ref[0, pl.ds(koff, NA_LOCAL), :]) + nab_ref[dr0], _qk(q4, kc))

    o_da = jnp.zeros((TQ, GROUP_W), F32)
    o_mla = jnp.zeros((TQ, GROUP_W), F32)
    na_rows = []
    s_next = scores(0)

    base = pl.multiple_of(t * TQ, TQ)
    gpad_ref[16:16 + TQ] = g_ref[0, pl.ds(base, TQ), :]
    lo = g_ref[0, pl.ds(pl.multiple_of(jnp.maximum(base - 16, 0), 16), 16), :]
    hi = g_ref[0, pl.ds(pl.multiple_of(jnp.minimum(base + TQ, DEC_SEQ - 16), 16), 16), :]
    gpad_ref[0:16] = jnp.where(t > 0, lo, 0.0)
    gpad_ref[16 + TQ:32 + TQ] = jnp.where(t < N_QT - 1, hi, 0.0)
    o_conv = _conv_ln_silu(layer, gpad_ref, zsh_ref, TQ, dw_ref, cb_ref, lng_ref, lnb_ref)

    for i in range(n_items):
        s_cur = s_next
        if i + 1 < n_items:
            s_next = scores(i + 1)
        if i < N_HEADS:
            e1, r1 = _softmax2_parts(s_cur[0])
            e2, r2 = _softmax2_parts(s_cur[1])
            p = (e1 - e2 * (lam * r2 / r1)).astype(BF)
            o_da = jnp.where(_group_of(lane, HEAD_DIM) == i, _dot(p, va) * r1, o_da)
        elif i < 2 * N_HEADS:
            e, r = _softmax2_parts(s_cur[0])
            o_mla = jnp.where(_group_of(lane, HEAD_DIM) == i - N_HEADS, _dot(e.astype(BF), vm) * r, o_mla)
        else:
            _, koff = na_window(i - 2 * N_HEADS)
            s_loc, s_ctx = s_cur
            m = jnp.maximum(jnp.max(s_loc, axis=-1, keepdims=True), jnp.max(s_ctx, axis=-1, keepdims=True))
            e_loc = jnp.exp(s_loc - m)
            e_ctx = jnp.exp(s_ctx - m)
            den = jnp.sum(e_loc, axis=-1, keepdims=True) + jnp.sum(e_ctx, axis=-1, keepdims=True)
            o4 = (_dot(e_loc.astype(BF), nav_ref[0, pl.ds(koff, NA_LOCAL), :])
                  + _dot(e_ctx.astype(BF), vc)) * (1.0 / den)
            o_row = o4[0:GRID_W]
            for hh in range(1, N_HEADS):
                o_row = jnp.where(_group_of(lane, HEAD_DIM) == hh, o4[hh * GRID_W:(hh + 1) * GRID_W], o_row)
            na_rows.append(o_row)
    o_da = _head_rms(o_da) * _row(gsub_ref, layer) * (1.0 - lam_init)
    o_na = jnp.concatenate(na_rows, axis=0)

    mixed_ref[...] = jnp.concatenate([o_da, o_mla, o_na, o_conv], axis=1).astype(BF)


def _lat_attn(layer, proj_outs, caches, nab, prm):
    lam_init = 0.8 - 0.6 * math.exp(-0.3 * layer)
    q_spec = pl.BlockSpec((1, TQ, GROUP_W), lambda b, t: (b, t, 0))
    full_k = pl.BlockSpec((1, DEC_SEQ, GROUP_W), lambda b, t: (b, 0, 0))
    cache_spec = pl.BlockSpec((1, None, PAST_LEN, GROUP_W), lambda b, t: (b, layer, 0, 0))
    in_specs = [q_spec,
                pl.BlockSpec((1, N_HEADS, TQ, GROUP_W), lambda b, t: (b, 0, t, 0)),
                q_spec, full_k] + [full_k] * 7 + [cache_spec] * 7 + [
                _layer_spec(layer, (NA_KR, N_HEADS * GRID_W, NA_LOCAL)),
                _layer_spec(layer, (4, DA_QK)),
                _const_spec((DEPTH, GROUP_W)),
                _layer_spec(layer, (32, GROUP_W)),
                _const_spec((DEPTH, GROUP_W)),
                _const_spec((DEPTH, GROUP_W)),
                _const_spec((DEPTH, GROUP_W))]
    return pl.pallas_call(
        functools.partial(_lat_attn_kernel, layer, lam_init),
        grid=(DEC_BATCH, N_QT),
        in_specs=in_specs,
        out_specs=pl.BlockSpec((TQ, D_MODEL), lambda b, t: (b * N_QT + t, 0)),
        out_shape=jax.ShapeDtypeStruct((DEC_BATCH * DEC_SEQ, D_MODEL), BF),
        scratch_shapes=[pltpu.VMEM((TQ + 32, GROUP_W), F32), pltpu.VMEM((7, TQ + 8, GROUP_W), F32),
                        pltpu.VMEM((3, GROUP_W, KEYS), BF), pltpu.VMEM((2, KEYS, GROUP_W), BF)],
        compiler_params=_cparams("arbitrary", "arbitrary"),
        name=f"lat_attn{layer}",
    )(*proj_outs, *caches, nab, prm["lamv"], prm["g_sub"], prm["dw"], prm["cb"], prm["ln_g"], prm["ln_b"])


def _out_ffn_kernel(layer, tiles_per_mod, mod_base, x_ref, mx_ref, mod_ref, wout_ref, gff_ref,
                    w1_ref, w2_ref, gfin_ref, o_ref):
    row = mod_base + pl.program_id(0) // tiles_per_mod
    g1 = mod_ref[pl.ds(row, 1), 2 * D_MODEL:3 * D_MODEL]
    sh2 = mod_ref[pl.ds(row, 1), 3 * D_MODEL:4 * D_MODEL]
    sc2 = mod_ref[pl.ds(row, 1), 4 * D_MODEL:5 * D_MODEL]
    g2 = mod_ref[pl.ds(row, 1), 5 * D_MODEL:6 * D_MODEL]
    x1 = x_ref[...] + g1 * _dot(mx_ref[...], wout_ref[...])
    h2 = ((_rms(x1) * _row(gff_ref, layer)) * (1.0 + sc2) + sh2).astype(BF)
    acc = jnp.zeros((TM_FFN, D_MODEL), F32)
    for c in range(D_FF // D_MODEL):
        a = jnp.maximum(_dot(h2, w1_ref[:, c * D_MODEL:(c + 1) * D_MODEL]), 0.0)
        acc = acc + _dot((a * a).astype(BF), w2_ref[c * D_MODEL:(c + 1) * D_MODEL, :])
    x2 = x1 + g2 * acc
    o_ref[...] = _rms(x2) * gfin_ref[...] if layer == DEPTH - 1 else x2


def _out_ffn(name, layer, tiles_per_mod, mod_base, x, mixed, prm, g_final):
    n = x.shape[0]
    tile = pl.BlockSpec((TM_FFN, D_MODEL), lambda i: (i, 0))
    return pl.pallas_call(
        functools.partial(_out_ffn_kernel, layer, tiles_per_mod, mod_base),
        grid=(n // TM_FFN,),
        in_specs=[tile, tile,
                  _layer_spec(layer, (16, N_MOD * D_MODEL)),
                  _layer_spec(layer, (D_MODEL, D_MODEL)),
                  _const_spec((DEPTH, D_MODEL)),
                  _layer_spec(layer, (D_MODEL, D_FF)),
                  _layer_spec(layer, (D_FF, D_MODEL)),
                  _const_spec((1, D_MODEL))],
        out_specs=tile,
        out_shape=jax.ShapeDtypeStruct((n, D_MODEL), F32),
        compiler_params=_cparams("arbitrary"),
        name=name,
    )(x, mixed, prm["mod"], prm["w_out"], prm["g_ff"], prm["w_ff1"], prm["w_ff2"], g_final)


def _rope_tables():
    t = np.arange(DEC_SEQ)
    rows = (t // GRID_W).astype(np.float32)
    cols = (t % GRID_W).astype(np.float32)
    c = np.arange(GROUP_W) % 32
    freqs = np.float32(ROPE_BASE) ** (-np.arange(8, dtype=np.float32) * np.float32(2.0) / np.float32(16))
    pos = np.where((c < 16)[None, :], rows[:, None], cols[:, None]).astype(np.float32)
    ang = (pos * freqs[(c % 16) % 8][None, :]).astype(np.float32)
    first = ((c % 16) < 8)[None, :]
    cos = np.cos(ang).astype(np.float32)
    sin = np.sin(ang).astype(np.float32)
    return jnp.asarray(cos), jnp.asarray(np.where(first, -sin, sin))


def _qup_gather_index():
    idx = np.full((N_HEADS * GROUP_W,), -1, np.int64)
    for h in range(N_HEADS):
        src = h * (MLA_NOPE + MLA_ROPE)
        dst = h * GROUP_W + (h % 2) * MLA_NOPE
        idx[dst:dst + MLA_NOPE] = np.arange(src, src + MLA_NOPE)
        idx[h * GROUP_W + 128:h * GROUP_W + 128 + MLA_ROPE] = np.arange(src + MLA_NOPE, src + MLA_NOPE + MLA_ROPE)
    return idx


def _kvup_perm():
    k = [h * 128 + d for h in range(N_HEADS) for d in range(MLA_NOPE)]
    v = [h * 128 + MLA_NOPE + d for h in range(N_HEADS) for d in range(HEAD_DIM)]
    return np.asarray(k + v)


def _heads_to_lanes(c):
    b, l, h, s, d = c.shape
    return c.transpose(0, 1, 3, 2, 4).reshape(b, l, s, h * d).astype(BF)


def kernel(x_prompt, x_sample, c, cache_da_k, cache_da_v, cache_mla_ckv, cache_mla_krope, cache_na_k, cache_na_v, c_ctx, w_mod, b_mod, g_norm_mix, g_norm_ff, w_in, da_lambda_q1, da_lambda_k1, da_lambda_q2, da_lambda_k2, g_da_subln, g_mla_q, w_mla_qup, g_mla_kv, w_mla_kvup, na_rpb, conv_dw, conv_b, conv_ln_g, conv_ln_b, w_out, w_ff1, w_ff2, g_final):
    qidx = _qup_gather_index()
    w_qup_e = jnp.where(jnp.asarray(qidx >= 0)[None, None, :],
                        jnp.take(w_mla_qup, jnp.asarray(np.maximum(qidx, 0)), axis=-1), 0.0).astype(BF)
    w_kvup_p = jnp.take(w_mla_kvup, jnp.asarray(_kvup_perm()), axis=-1).astype(BF)
    prm = dict(
        g_mix=g_norm_mix, w_in=jnp.swapaxes(w_in, 1, 2), g_q=g_mla_q, w_qup=w_qup_e, g_kv=g_mla_kv, w_kvup=w_kvup_p,
        lamv=jnp.stack([da_lambda_q1, da_lambda_k1, da_lambda_q2, da_lambda_k2], axis=1),
        g_sub=jnp.tile(g_da_subln, (1, N_HEADS)),
        dw=jnp.concatenate([conv_dw, jnp.zeros((DEPTH, 1, GROUP_W), F32)], axis=1),
        cb=conv_b, ln_g=conv_ln_g, ln_b=conv_ln_b,
        w_out=w_out.astype(BF), g_ff=g_norm_ff, w_ff1=w_ff1.astype(BF), w_ff2=w_ff2.astype(BF))
    cv = jnp.concatenate([c_ctx[None, :], c, jnp.zeros((16 - 1 - DEC_BATCH, D_MODEL), F32)], axis=0)
    cos_t, sin_t = _rope_tables()
    g_final2 = g_final.reshape(1, D_MODEL)

    prm["mod"] = _modulation(cv, w_mod, b_mod)
    c_kr_pad = jnp.pad(cache_mla_krope, ((0, 0), (0, 0), (0, 0), (0, KR_PAD)))
    cka, ckb, cmv = _mla_cache(cache_mla_ckv, c_kr_pad, w_kvup_p)
    caches = (_heads_to_lanes(cache_da_k), _heads_to_lanes(cache_da_v), cka, ckb, cmv,
              _heads_to_lanes(cache_na_k), _heads_to_lanes(cache_na_v))
    nab = _na_bias(na_rpb)

    xp = x_prompt.reshape(BATCH * SEQ, D_MODEL)
    xs = x_sample.reshape(DEC_BATCH * DEC_SEQ, D_MODEL)
    new_ctx = ()
    for l in range(DEPTH):
        mixed_p, *new_ctx = _ctx_layer(l, xp, prm, tuple(new_ctx))
        xp = _out_ffn(f"ctx_ffn{l}", l, BATCH * SEQ // TM_FFN, 0, xp, mixed_p, prm, g_final2)
        proj_outs = _lat_proj(l, xs, prm, cos_t, sin_t)
        mixed_s = _lat_attn(l, proj_outs, caches, nab, prm)
        xs = _out_ffn(f"lat_ffn{l}", l, DEC_SEQ // TM_FFN, 1, xs, mixed_s, prm, g_final2)

    y_prompt = xp.reshape(BATCH, SEQ, D_MODEL)
    y_sample = xs.reshape(DEC_BATCH, DEC_SEQ, D_MODEL)
    return (y_prompt, y_sample) + tuple(new_ctx)
```

```python
import functools
import math

import numpy as np
import jax
import jax.numpy as jnp
from jax import lax
from jax.experimental import pallas as pl
from jax.experimental.pallas import tpu as pltpu

F32 = jnp.float32
BF = jnp.bfloat16

D_MODEL = 1024
BATCH = 16
SEQ = 256
DEPTH = 2
DEC_BATCH = 2
DEC_SEQ = 2048
PAST_LEN = 256
GRID_W = 64
GROUP_W = 256
HEAD_DIM = 64
N_HEADS = 4
DA_QK = 32
MLA_NOPE = 64
MLA_ROPE = 32
MLA_KV_LORA = 128
NA_KR = 8
NA_KC = 16
CONV_W = 31
D_FF = 4096
ROPE_BASE = 10000.0
EPS = 1e-6
N_MOD = 6
IN_COLS = 2464

P_DAQ, P_DAK, P_DAV, P_QD, P_KVD, P_KR = 0, 256, 512, 768, 1024, 1152
P_NAQ, P_NAK, P_NAV, P_CONV, P_TOT = 1280, 1536, 1792, 2048, 2560
P_SPLIT = P_NAQ
N_CTX_IN = 14
KR_ORIG_END = 1184
KR_PAD = 128 - MLA_ROPE

DA_SCALE = DA_QK ** -0.5
MLA_SCALE = (MLA_NOPE + MLA_ROPE) ** -0.5
NA_SCALE = HEAD_DIM ** -0.5
LOG2E = math.log2(math.e)

TQ = 256
N_QT = DEC_SEQ // TQ
KEYS = DEC_SEQ + PAST_LEN
ROWS_PER_TILE = TQ // GRID_W
N_ROWS = DEC_SEQ // GRID_W
NA_LOCAL = NA_KR * GRID_W
TM_FFN = 512
TM_PROJ = 1024
VMEM_LIMIT = 58 * 1024 * 1024

NT_DIMS = (((1,), (1,)), ((), ()))


def _cparams(*sem):
    return pltpu.CompilerParams(dimension_semantics=sem, vmem_limit_bytes=VMEM_LIMIT)


def _const_spec(shape):
    nd = len(shape)
    return pl.BlockSpec(shape, lambda *_: (0,) * nd, pipeline_mode=pl.Buffered(1))


def _layer_spec(layer, shape):
    nd = len(shape)
    return pl.BlockSpec((None,) + tuple(shape), lambda *_: (layer,) + (0,) * nd, pipeline_mode=pl.Buffered(1))


def _rms(x):
    return x * lax.rsqrt(jnp.mean(x * x, axis=-1, keepdims=True) + EPS)


def _row(ref, layer):
    return ref[layer:layer + 1, :]


def _dot(a, b):
    return jnp.dot(a, b, preferred_element_type=F32)


def _qk(q, k):
    return lax.dot_general(q, k, NT_DIMS, preferred_element_type=F32)


def _softmax2_parts(s):
    m = jnp.max(s, axis=-1, keepdims=True)
    e = jnp.exp2(s - m)
    return e, 1.0 / jnp.sum(e, axis=-1, keepdims=True)


def _lane_ids(width):
    return lax.broadcasted_iota(jnp.int32, (1, width), 1)


def _diff_lambda(lamv_ref, lam_init):
    v = lamv_ref[...]
    a = jnp.exp(jnp.sum(v[0:1] * v[1:2], axis=-1, keepdims=True))
    b = jnp.exp(jnp.sum(v[2:3] * v[3:4], axis=-1, keepdims=True))
    return a - b + lam_init


def _group_of(lane, group):
    return lane >> (group.bit_length() - 1)


def _head_rms(x):
    r = _group_of(lax.broadcasted_iota(jnp.int32, (GROUP_W, GROUP_W), 0), HEAD_DIM)
    c = _group_of(lax.broadcasted_iota(jnp.int32, (GROUP_W, GROUP_W), 1), HEAD_DIM)
    ones_bd = jnp.where(r == c, 1.0, 0.0).astype(BF)
    sq = x * x
    hi = sq.astype(BF)
    lo = (sq - hi.astype(F32)).astype(BF)
    ss = _dot(hi, ones_bd) + _dot(lo, ones_bd)
    return x * lax.rsqrt(ss * (1.0 / HEAD_DIM) + EPS)


def _rope(z, cos, sin):
    lane = _lane_ids(z.shape[1])
    swapped = jnp.where((lane & 15) < 8, pltpu.roll(z, 120, 1), pltpu.roll(z, 8, 1))
    return z * cos + swapped * sin


def _stage_w_in(wint_ref, wbt_ref):
    wbt_ref[0:KR_ORIG_END, :] = wint_ref[0:KR_ORIG_END, :].astype(BF)
    wbt_ref[KR_ORIG_END:P_SPLIT, :] = jnp.zeros((P_SPLIT - KR_ORIG_END, D_MODEL), BF)
    wbt_ref[P_SPLIT:P_TOT, :] = wint_ref[KR_ORIG_END:IN_COLS, :].astype(BF)


def _mask_heads(qf, lane, group, idx):
    return jnp.where(_group_of(lane, group) == idx, qf, 0.0).astype(BF)


def _conv_ln_silu(layer, gpad_ref, zsh_ref, n, dw_ref, cb_ref, lng_ref, lnb_ref):
    y = jnp.zeros((n, GROUP_W), F32) + _row(cb_ref, layer)
    for b in range(8):
        z = None
        for a in range(4):
            t = 8 * a + b - 1
            if 0 <= t < CONV_W:
                term = gpad_ref[pl.ds(8 * a, n + 8), :] * dw_ref[t:t + 1, :]
                z = term if z is None else z + term
        if b == 0:
            y = y + z[0:n]
        else:
            zsh_ref[b - 1] = z
            y = y + zsh_ref[b - 1, pl.ds(b, n), :]
    mu = jnp.mean(y, axis=-1, keepdims=True)
    yc = y - mu
    var = jnp.mean(yc * yc, axis=-1, keepdims=True)
    z = yc * lax.rsqrt(var + EPS) * _row(lng_ref, layer) + _row(lnb_ref, layer)
    return z * jax.nn.sigmoid(z)


def _mod_kernel(cv_ref, w_ref, b_ref, o_ref):
    c = cv_ref[...]
    a = (c * jax.nn.sigmoid(c)).astype(BF)
    o_ref[0] = _dot(a, w_ref[0].astype(BF)) + b_ref[0]


def _modulation(cv, w_mod, b_mod):
    nblk = 4
    wblk = N_MOD * D_MODEL // nblk
    return pl.pallas_call(
        _mod_kernel,
        grid=(DEPTH, nblk),
        in_specs=[pl.BlockSpec((16, D_MODEL), lambda l, j: (0, 0)),
                  pl.BlockSpec((1, D_MODEL, wblk), lambda l, j: (l, 0, j)),
                  pl.BlockSpec((1, 1, wblk), lambda l, j: (l, 0, j))],
        out_specs=pl.BlockSpec((1, 16, wblk), lambda l, j: (l, 0, j)),
        out_shape=jax.ShapeDtypeStruct((DEPTH, 16, N_MOD * D_MODEL), F32),
        compiler_params=_cparams("arbitrary", "arbitrary"),
        name="modulation",
    )(cv, w_mod, b_mod.reshape(DEPTH, 1, N_MOD * D_MODEL))


def _mla_cache_kernel(ckv_ref, kr_ref, w_ref, ka_ref, kb_ref, v_ref):
    kvm = _dot(ckv_ref[0, 0].astype(BF), w_ref[0])
    kr = kr_ref[0, 0].astype(BF)
    kn = kvm[:, :GROUP_W].astype(BF)
    ka_ref[0, 0] = jnp.concatenate([kn[:, :128], kr], axis=1)
    kb_ref[0, 0] = jnp.concatenate([kn[:, 128:], kr], axis=1)
    v_ref[0, 0] = kvm[:, GROUP_W:].astype(BF)


def _mla_cache(c_ckv, c_kr_pad, w_kvup_p):
    blk = pl.BlockSpec((1, 1, PAST_LEN, GROUP_W), lambda b, l: (b, l, 0, 0))
    shp = jax.ShapeDtypeStruct((DEC_BATCH, DEPTH, PAST_LEN, GROUP_W), BF)
    return pl.pallas_call(
        _mla_cache_kernel,
        grid=(DEC_BATCH, DEPTH),
        in_specs=[pl.BlockSpec((1, 1, PAST_LEN, MLA_KV_LORA), lambda b, l: (b, l, 0, 0)),
                  pl.BlockSpec((1, 1, PAST_LEN, 128), lambda b, l: (b, l, 0, 0)),
                  pl.BlockSpec((1, MLA_KV_LORA, 2 * GROUP_W), lambda b, l: (l, 0, 0))],
        out_specs=[blk, blk, blk],
        out_shape=[shp, shp, shp],
        compiler_params=_cparams("arbitrary", "arbitrary"),
        name="mla_cache",
    )(c_ckv, c_kr_pad, w_kvup_p)


N_DR = 2 * NA_KR - 1
N_DC = 2 * NA_KC - 1


def _na_bias_kernel(rpb_ref, o_ref, tp_ref):
    base = (pl.program_id(0) * N_HEADS + pl.program_id(1)) * (N_DR * N_DC)
    cq = lax.broadcasted_iota(jnp.int32, (GRID_W, 128), 0)
    lane = lax.broadcasted_iota(jnp.int32, (GRID_W, 128), 1)
    ck = lane & (GRID_W - 1)
    upper = lane >= GRID_W
    cstart = jnp.clip(cq - NA_KC // 2, 0, GRID_W - NA_KC)
    ok = (ck >= cstart) & (ck < cstart + NA_KC)

    j = lax.broadcasted_iota(jnp.int32, (8, 128), 1)
    dclip = jnp.clip(jnp.where(j < GRID_W, j, j - 128), -(NA_KC - 1), NA_KC - 1) + NA_KC - 1

    def toeplitz(dr):
        u = jnp.zeros((8, 128), F32)
        for d in range(N_DC):
            u = jnp.where(dclip == d, rpb_ref[base + dr * N_DC + d], u)
        rows = jnp.broadcast_to(u[0:1], (GRID_W, 128))
        return pltpu.roll(rows, 0, 1, stride=1, stride_axis=0)

    tabs = [toeplitz(dr) for dr in range(N_DR)]
    for dr in range(N_DR - 1):
        val = jnp.where(upper, pltpu.roll(tabs[dr + 1], GRID_W, 1), tabs[dr])
        tp_ref[dr] = jnp.where(ok, val, -jnp.inf)
    for dr0 in range(NA_KR):
        o_ref[0, dr0] = jnp.concatenate([tp_ref[dr0 + 2 * m] for m in range(NA_KR // 2)], axis=1)


def _na_bias(rpb):
    return pl.pallas_call(
        _na_bias_kernel,
        grid=(DEPTH, N_HEADS),
        in_specs=[pl.BlockSpec(memory_space=pltpu.SMEM)],
        out_specs=pl.BlockSpec((1, NA_KR, GRID_W, NA_LOCAL), lambda l, h: (l, 0, h, 0)),
        out_shape=jax.ShapeDtypeStruct((DEPTH, NA_KR, N_HEADS * GRID_W, NA_LOCAL), F32),
        scratch_shapes=[pltpu.VMEM((N_DR - 1, GRID_W, 128), F32)],
        compiler_params=_cparams("arbitrary", "arbitrary"),
        name="na_bias",
    )(rpb.reshape(-1))


def _ctx_kernel(layer, lam_init, n_prev, *refs):
    (x_ref, mod_ref, gmix_ref, win_ref, gq_ref, wqup_ref, gkv_ref, wkv_ref,
     lamv_ref, gsub_ref, dw_ref, cb_ref, lng_ref, lnb_ref) = refs[:N_CTX_IN]
    (mixed_ref, dak_ref, dav_ref, ckv_ref, kr_ref, nak_ref, nav_ref,
     proj_ref, gpad_ref, zsh_ref, wbf_ref) = refs[N_CTX_IN + n_prev:]

    @pl.when(pl.program_id(0) == 0)
    def _():
        _stage_w_in(win_ref, wbf_ref)

    x = x_ref[...]
    sh1 = mod_ref[0:1, 0:D_MODEL]
    sc1 = mod_ref[0:1, D_MODEL:2 * D_MODEL]
    h = ((_rms(x) * _row(gmix_ref, layer)) * (1.0 + sc1) + sh1).astype(BF)
    proj_ref[:, 0:P_SPLIT] = _qk(h, wbf_ref[0:P_SPLIT, :])
    proj_ref[:, P_SPLIT:P_TOT] = _qk(h, wbf_ref[P_SPLIT:P_TOT, :])

    for ref, col in ((dak_ref, P_DAK), (dav_ref, P_DAV), (nak_ref, P_NAK), (nav_ref, P_NAV)):
        slab_t = proj_ref[:, col:col + GROUP_W].T
        for hh in range(N_HEADS):
            ref[0, hh] = slab_t[hh * HEAD_DIM:(hh + 1) * HEAD_DIM, :]

    lane = _lane_ids(GROUP_W)

    lam = _diff_lambda(lamv_ref, lam_init)

    qa = proj_ref[:, P_DAQ:P_DAQ + GROUP_W] * (DA_SCALE * LOG2E)
    ka = proj_ref[:, P_DAK:P_DAK + GROUP_W].astype(BF)
    va = proj_ref[:, P_DAV:P_DAV + GROUP_W].astype(BF)
    qn = proj_ref[:, P_NAQ:P_NAQ + GROUP_W] * (NA_SCALE * LOG2E)
    kn2 = proj_ref[:, P_NAK:P_NAK + GROUP_W].astype(BF)
    vn = proj_ref[:, P_NAV:P_NAV + GROUP_W].astype(BF)
    mla = {}

    def mla_prep():
        qd = _rms(proj_ref[:, P_QD:P_QD + GROUP_W]) * _row(gq_ref, layer)
        ckv = _rms(proj_ref[:, P_KVD:P_KVD + MLA_KV_LORA]) * _row(gkv_ref, layer)
        ckv_ref[0] = ckv
        kr_pad = proj_ref[:, P_KR:P_KR + 128]
        kr_ref[0] = kr_pad.T[0:MLA_ROPE, :]
        mla["q"] = (_dot(qd.astype(BF), wqup_ref[...]) * (MLA_SCALE * LOG2E)).astype(BF)
        kvm = _dot(ckv.astype(BF), wkv_ref[...])
        kn = kvm[:, :GROUP_W].astype(BF)
        krb = kr_pad.astype(BF)
        mla["k"] = (jnp.concatenate([kn[:, :128], krb], axis=1), jnp.concatenate([kn[:, 128:], krb], axis=1))
        mla["v"] = kvm[:, GROUP_W:].astype(BF)

    def conv():
        g = proj_ref[:, P_CONV:P_CONV + GROUP_W] * jax.nn.sigmoid(proj_ref[:, P_CONV + GROUP_W:P_TOT])
        gpad_ref[0:16] = jnp.zeros((16, GROUP_W), F32)
        gpad_ref[16 + SEQ:32 + SEQ] = jnp.zeros((16, GROUP_W), F32)
        gpad_ref[16:16 + SEQ] = g
        return _conv_ln_silu(layer, gpad_ref, zsh_ref, SEQ, dw_ref, cb_ref, lng_ref, lnb_ref)

    def scores(i):
        if i < N_HEADS:
            return (_qk(_mask_heads(qa, lane, DA_QK, 2 * i), ka),
                    _qk(_mask_heads(qa, lane, DA_QK, 2 * i + 1), ka))
        if i < 2 * N_HEADS:
            hh = i - N_HEADS
            return (_qk(mla["q"][:, hh * GROUP_W:(hh + 1) * GROUP_W], mla["k"][hh // 2]),)
        return (_qk(_mask_heads(qn, lane, HEAD_DIM, i - 2 * N_HEADS), kn2),)

    outs = [jnp.zeros((SEQ, GROUP_W), F32)] * 3
    s_next = scores(0)
    o_conv = None
    for i in range(3 * N_HEADS):
        s_cur = s_next
        if i + 1 < 3 * N_HEADS:
            s_next = scores(i + 1)
        if i == 0:
            mla_prep()
        if i == N_HEADS - 1:
            o_conv = conv()
        grp, hh = divmod(i, N_HEADS)
        if grp == 0:
            e1, r1 = _softmax2_parts(s_cur[0])
            e2, r2 = _softmax2_parts(s_cur[1])
            o = _dot((e1 * r1 - e2 * (lam * r2)).astype(BF), va)
        else:
            e, r = _softmax2_parts(s_cur[0])
            o = _dot(e.astype(BF), mla["v"] if grp == 1 else vn) * r
        outs[grp] = jnp.where(_group_of(lane, HEAD_DIM) == hh, o, outs[grp])
    o_da = _head_rms(outs[0]) * _row(gsub_ref, layer) * (1.0 - lam_init)
    o_mla, o_na = outs[1], outs[2]

    mixed_ref[...] = jnp.concatenate([o_da, o_mla, o_na, o_conv], axis=1).astype(BF)


def _ctx_layer(layer, x, prm, prev):
    lam_init = 0.8 - 0.6 * math.exp(-0.3 * layer)
    head_blk = pl.BlockSpec((1, None, N_HEADS, HEAD_DIM, SEQ), lambda b: (b, layer, 0, 0, 0))
    head_shp = jax.ShapeDtypeStruct((BATCH, DEPTH, N_HEADS, HEAD_DIM, SEQ), F32)
    in_specs = [pl.BlockSpec((SEQ, D_MODEL), lambda b: (b, 0)),
                _layer_spec(layer, (16, N_MOD * D_MODEL)),
                _const_spec((DEPTH, D_MODEL)),
                _layer_spec(layer, (IN_COLS, D_MODEL)),
                _const_spec((DEPTH, GROUP_W)),
                _layer_spec(layer, (GROUP_W, N_HEADS * GROUP_W)),
                _const_spec((DEPTH, MLA_KV_LORA)),
                _layer_spec(layer, (MLA_KV_LORA, 2 * GROUP_W)),
                _layer_spec(layer, (4, DA_QK)),
                _const_spec((DEPTH, GROUP_W)),
                _layer_spec(layer, (32, GROUP_W)),
                _const_spec((DEPTH, GROUP_W)),
                _const_spec((DEPTH, GROUP_W)),
                _const_spec((DEPTH, GROUP_W))] + [pl.BlockSpec(memory_space=pl.ANY)] * len(prev)
    out_specs = [pl.BlockSpec((SEQ, D_MODEL), lambda b: (b, 0)),
                 head_blk, head_blk,
                 pl.BlockSpec((1, None, SEQ, MLA_KV_LORA), lambda b: (b, layer, 0, 0)),
                 pl.BlockSpec((1, None, MLA_ROPE, SEQ), lambda b: (b, layer, 0, 0)),
                 head_blk, head_blk]
    out_shape = [jax.ShapeDtypeStruct((BATCH * SEQ, D_MODEL), BF),
                 head_shp, head_shp,
                 jax.ShapeDtypeStruct((BATCH, DEPTH, SEQ, MLA_KV_LORA), F32),
                 jax.ShapeDtypeStruct((BATCH, DEPTH, MLA_ROPE, SEQ), F32),
                 head_shp, head_shp]
    n_in = len(in_specs) - len(prev)
    return pl.pallas_call(
        functools.partial(_ctx_kernel, layer, lam_init, len(prev)),
        grid=(BATCH,),
        in_specs=in_specs,
        out_specs=out_specs,
        out_shape=out_shape,
        input_output_aliases={n_in + k: 1 + k for k in range(len(prev))},
        scratch_shapes=[pltpu.VMEM((SEQ, P_TOT), F32), pltpu.VMEM((SEQ + 32, GROUP_W), F32),
                        pltpu.VMEM((7, SEQ + 8, GROUP_W), F32), pltpu.VMEM((P_TOT, D_MODEL), BF)],
        compiler_params=_cparams("arbitrary"),
        name=f"ctx_layer{layer}",
    )(x, prm["mod"], prm["g_mix"], prm["w_in"], prm["g_q"], prm["w_qup"], prm["g_kv"], prm["w_kvup"],
      prm["lamv"], prm["g_sub"], prm["dw"], prm["cb"], prm["ln_g"], prm["ln_b"], *prev)


def _lat_proj_kernel(layer, x_ref, mod_ref, gmix_ref, win_ref, gq_ref, wqup_ref, gkv_ref, wkv_ref,
                     cos_ref, sin_ref,
                     daq_ref, mq_ref, naq_ref, g_ref,
                     dak_ref, dav_ref, ka_ref, kb_ref, mv_ref, nak_ref, nav_ref, wbf_ref):
    b = pl.program_id(0)

    @pl.when((b == 0) & (pl.program_id(1) == 0))
    def _():
        _stage_w_in(win_ref, wbf_ref)

    x = x_ref[...]
    sh1 = mod_ref[pl.ds(1 + b, 1), 0:D_MODEL]
    sc1 = mod_ref[pl.ds(1 + b, 1), D_MODEL:2 * D_MODEL]
    h = (_rms(x) * _row(gmix_ref, layer)) * (1.0 + sc1) + sh1
    hb = h.astype(BF)
    proj = jnp.concatenate([_qk(hb, wbf_ref[0:P_SPLIT, :]), _qk(hb, wbf_ref[P_SPLIT:P_TOT, :])], axis=1)
    cos = cos_ref[...]
    sin = sin_ref[...]

    def rope2(z):
        return jnp.concatenate([_rope(z[:, :128], cos[:, :128], sin[:, :128]),
                                _rope(z[:, 128:], cos[:, 128:], sin[:, 128:])], axis=1)

    daq_ref[0] = (rope2(proj[:, P_DAQ:P_DAQ + GROUP_W]) * (DA_SCALE * LOG2E)).astype(BF)
    dak_ref[0] = rope2(proj[:, P_DAK:P_DAK + GROUP_W]).astype(BF)
    dav_ref[0] = proj[:, P_DAV:P_DAV + GROUP_W].astype(BF)

    qd = _rms(proj[:, P_QD:P_QD + GROUP_W]) * _row(gq_ref, layer)
    qm = _dot(qd.astype(BF), wqup_ref[...])
    for hh in range(N_HEADS):
        nope = qm[:, hh * GROUP_W:hh * GROUP_W + 128]
        rope = _rope(qm[:, hh * GROUP_W + 128:(hh + 1) * GROUP_W], cos[:, :128], sin[:, :128])
        mq_ref[0, hh] = (jnp.concatenate([nope, rope], axis=1) * (MLA_SCALE * LOG2E)).astype(BF)
    ckv = _rms(proj[:, P_KVD:P_KVD + MLA_KV_LORA]) * _row(gkv_ref, layer)
    kvm = _dot(ckv.astype(BF), wkv_ref[...])
    kn = kvm[:, :GROUP_W].astype(BF)
    krb = _rope(proj[:, P_KR:P_KR + 128], cos[:, :128], sin[:, :128]).astype(BF)
    ka_ref[0] = jnp.concatenate([kn[:, :128], krb], axis=1)
    kb_ref[0] = jnp.concatenate([kn[:, 128:], krb], axis=1)
    mv_ref[0] = kvm[:, GROUP_W:].astype(BF)

    naq_ref[0] = (proj[:, P_NAQ:P_NAQ + GROUP_W] * NA_SCALE).astype(BF)
    nak_ref[0] = proj[:, P_NAK:P_NAK + GROUP_W].astype(BF)
    nav_ref[0] = proj[:, P_NAV:P_NAV + GROUP_W].astype(BF)
    g_ref[0] = proj[:, P_CONV:P_CONV + GROUP_W] * jax.nn.sigmoid(proj[:, P_CONV + GROUP_W:P_TOT])


def _lat_proj(layer, x, prm, cos_t, sin_t):
    n_t = DEC_SEQ // TM_PROJ
    row_spec = pl.BlockSpec((1, TM_PROJ, GROUP_W), lambda b, j: (b, j, 0))
    in_specs = [pl.BlockSpec((TM_PROJ, D_MODEL), lambda b, j: (b * n_t + j, 0)),
                _layer_spec(layer, (16, N_MOD * D_MODEL)),
                _const_spec((DEPTH, D_MODEL)),
                _layer_spec(layer, (IN_COLS, D_MODEL)),
                _const_spec((DEPTH, GROUP_W)),
                _layer_spec(layer, (GROUP_W, N_HEADS * GROUP_W)),
                _const_spec((DEPTH, MLA_KV_LORA)),
                _layer_spec(layer, (MLA_KV_LORA, 2 * GROUP_W)),
                pl.BlockSpec((TM_PROJ, GROUP_W), lambda b, j: (j, 0)),
                pl.BlockSpec((TM_PROJ, GROUP_W), lambda b, j: (j, 0))]
    row_shp = jax.ShapeDtypeStruct((DEC_BATCH, DEC_SEQ, GROUP_W), BF)
    out_specs = [row_spec,
                 pl.BlockSpec((1, N_HEADS, TM_PROJ, GROUP_W), lambda b, j: (b, 0, j, 0)),
                 row_spec, row_spec] + [row_spec] * 7
    out_shape = [row_shp,
                 jax.ShapeDtypeStruct((DEC_BATCH, N_HEADS, DEC_SEQ, GROUP_W), BF),
                 row_shp,
                 jax.ShapeDtypeStruct((DEC_BATCH, DEC_SEQ, GROUP_W), F32)] + [row_shp] * 7
    return pl.pallas_call(
        functools.partial(_lat_proj_kernel, layer),
        grid=(DEC_BATCH, n_t),
        in_specs=in_specs,
        out_specs=out_specs,
        out_shape=out_shape,
        scratch_shapes=[pltpu.VMEM((P_TOT, D_MODEL), BF)],
        compiler_params=_cparams("arbitrary", "arbitrary"),
        name=f"lat_proj{layer}",
    )(x, prm["mod"], prm["g_mix"], prm["w_in"], prm["g_q"], prm["w_qup"], prm["g_kv"], prm["w_kvup"],
      cos_t, sin_t)


def _lat_attn_kernel(layer, lam_init, daq_ref, mq_ref, naq_ref, g_ref,
                     dak_ref, dav_ref, ka_ref, kb_ref, mv_ref, nak_ref, nav_ref,
                     cdak_ref, cdav_ref, cka_ref, ckb_ref, cmv_ref, cnak_ref, cnav_ref,
                     nab_ref, lamv_ref, gsub_ref, dw_ref, cb_ref, lng_ref, lnb_ref,
                     mixed_ref, gpad_ref, zsh_ref, kt_ref, v_ref):
    t = pl.program_id(1)
    lane = _lane_ids(GROUP_W)

    @pl.when(t == 0)
    def _():
        for slot, (lat, cache) in enumerate(((dak_ref, cdak_ref), (ka_ref, cka_ref), (kb_ref, ckb_ref))):
            kt_ref[slot, :, 0:DEC_SEQ] = lat[0].T
            kt_ref[slot, :, DEC_SEQ:KEYS] = cache[0].T
        for slot, (lat, cache) in enumerate(((dav_ref, cdav_ref), (mv_ref, cmv_ref))):
            v_ref[slot, 0:DEC_SEQ, :] = lat[0]
            v_ref[slot, DEC_SEQ:KEYS, :] = cache[0]

    lam = _diff_lambda(lamv_ref, lam_init)
    qa = daq_ref[0].astype(F32)
    va = v_ref[0]
    ka_t = kt_ref[0]
    vm = v_ref[1]

    kc = cnak_ref[0]
    vc = cnav_ref[0]
    n_items = 2 * N_HEADS + ROWS_PER_TILE

    def na_window(j):
        r = t * ROWS_PER_TILE + j
        start = jnp.clip(r - NA_KR // 2, 0, N_ROWS - NA_KR)
        return start - r + NA_KR - 1, pl.multiple_of(start * GRID_W, GRID_W)

    def scores(i):
        if i < N_HEADS:
            return (_dot(_mask_heads(qa, lane, DA_QK, 2 * i), ka_t),
                    _dot(_mask_heads(qa, lane, DA_QK, 2 * i + 1), ka_t))
        if i < 2 * N_HEADS:
            hh = i - N_HEADS
            return (_dot(mq_ref[0, hh], kt_ref[1 + hh // 2]),)
        j = i - 2 * N_HEADS
        dr0, koff = na_window(j)
        qrow = naq_ref[0, j * GRID_W:(j + 1) * GRID_W, :].astype(F32)
        q4 = jnp.concatenate([_mask_heads(qrow, lane, HEAD_DIM, hh) for hh in range(N_HEADS)], axis=0)
        return (_qk(q4, nak_ref[0, pl.ds(koff, NA_LOCAL), :]) + nab_ref[dr0], _qk(q4, kc))

    o_da = jnp.zeros((TQ, GROUP_W), F32)
    o_mla = jnp.zeros((TQ, GROUP_W), F32)
    na_rows = []
    s_next = scores(0)

    base = pl.multiple_of(t * TQ, TQ)
    gpad_ref[16:16 + TQ] = g_ref[0, pl.ds(base, TQ), :]
    lo = g_ref[0, pl.ds(pl.multiple_of(jnp.maximum(base - 16, 0), 16), 16), :]
    hi = g_ref[0, pl.ds(pl.multiple_of(jnp.minimum(base + TQ, DEC_SEQ - 16), 16), 16), :]
    gpad_ref[0:16] = jnp.where(t > 0, lo, 0.0)
    gpad_ref[16 + TQ:32 + TQ] = jnp.where(t < N_QT - 1, hi, 0.0)
    o_conv = _conv_ln_silu(layer, gpad_ref, zsh_ref, TQ, dw_ref, cb_ref, lng_ref, lnb_ref)

    for i in range(n_items):
        s_cur = s_next
        if i + 1 < n_items:
            s_next = scores(i + 1)
        if i < N_HEADS:
            e1, r1 = _softmax2_parts(s_cur[0])
            e2, r2 = _softmax2_parts(s_cur[1])
            p = (e1 - e2 * (lam * r2 / r1)).astype(BF)
            o_da = jnp.where(_group_of(lane, HEAD_DIM) == i, _dot(p, va) * r1, o_da)
        elif i < 2 * N_HEADS:
            e, r = _softmax2_parts(s_cur[0])
            o_mla = jnp.where(_group_of(lane, HEAD_DIM) == i - N_HEADS, _dot(e.astype(BF), vm) * r, o_mla)
        else:
            _, koff = na_window(i - 2 * N_HEADS)
            s_loc, s_ctx = s_cur
            m = jnp.maximum(jnp.max(s_loc, axis=-1, keepdims=True), jnp.max(s_ctx, axis=-1, keepdims=True))
            e_loc = jnp.exp(s_loc - m)
            e_ctx = jnp.exp(s_ctx - m)
            den = jnp.sum(e_loc, axis=-1, keepdims=True) + jnp.sum(e_ctx, axis=-1, keepdims=True)
            o4 = (_dot(e_loc.astype(BF), nav_ref[0, pl.ds(koff, NA_LOCAL), :])
                  + _dot(e_ctx.astype(BF), vc)) * (1.0 / den)
            o_row = o4[0:GRID_W]
            for hh in range(1, N_HEADS):
                o_row = jnp.where(_group_of(lane, HEAD_DIM) == hh, o4[hh * GRID_W:(hh + 1) * GRID_W], o_row)
            na_rows.append(o_row)
    o_da = _head_rms(o_da) * _row(gsub_ref, layer) * (1.0 - lam_init)
    o_na = jnp.concatenate(na_rows, axis=0)

    mixed_ref[...] = jnp.concatenate([o_da, o_mla, o_na, o_conv], axis=1).astype(BF)


def _lat_attn(layer, proj_outs, caches, nab, prm):
    lam_init = 0.8 - 0.6 * math.exp(-0.3 * layer)
    q_spec = pl.BlockSpec((1, TQ, GROUP_W), lambda b, t: (b, t, 0))
    full_k = pl.BlockSpec((1, DEC_SEQ, GROUP_W), lambda b, t: (b, 0, 0))
    cache_spec = pl.BlockSpec((1, None, PAST_LEN, GROUP_W), lambda b, t: (b, layer, 0, 0))
    in_specs = [q_spec,
                pl.BlockSpec((1, N_HEADS, TQ, GROUP_W), lambda b, t: (b, 0, t, 0)),
                q_spec, full_k] + [full_k] * 7 + [cache_spec] * 7 + [
                _layer_spec(layer, (NA_KR, N_HEADS * GRID_W, NA_LOCAL)),
                _layer_spec(layer, (4, DA_QK)),
                _const_spec((DEPTH, GROUP_W)),
                _layer_spec(layer, (32, GROUP_W)),
                _const_spec((DEPTH, GROUP_W)),
                _const_spec((DEPTH, GROUP_W)),
                _const_spec((DEPTH, GROUP_W))]
    return pl.pallas_call(
        functools.partial(_lat_attn_kernel, layer, lam_init),
        grid=(DEC_BATCH, N_QT),
        in_specs=in_specs,
        out_specs=pl.BlockSpec((TQ, D_MODEL), lambda b, t: (b * N_QT + t, 0)),
        out_shape=jax.ShapeDtypeStruct((DEC_BATCH * DEC_SEQ, D_MODEL), BF),
        scratch_shapes=[pltpu.VMEM((TQ + 32, GROUP_W), F32), pltpu.VMEM((7, TQ + 8, GROUP_W), F32),
                        pltpu.VMEM((3, GROUP_W, KEYS), BF), pltpu.VMEM((2, KEYS, GROUP_W), BF)],
        compiler_params=_cparams("arbitrary", "arbitrary"),
        name=f"lat_attn{layer}",
    )(*proj_outs, *caches, nab, prm["lamv"], prm["g_sub"], prm["dw"], prm["cb"], prm["ln_g"], prm["ln_b"])


def _out_ffn_kernel(layer, tiles_per_mod, mod_base, x_ref, mx_ref, mod_ref, wout_ref, gff_ref,
                    w1_ref, w2_ref, gfin_ref, o_ref):
    row = mod_base + pl.program_id(0) // tiles_per_mod
    g1 = mod_ref[pl.ds(row, 1), 2 * D_MODEL:3 * D_MODEL]
    sh2 = mod_ref[pl.ds(row, 1), 3 * D_MODEL:4 * D_MODEL]
    sc2 = mod_ref[pl.ds(row, 1), 4 * D_MODEL:5 * D_MODEL]
    g2 = mod_ref[pl.ds(row, 1), 5 * D_MODEL:6 * D_MODEL]
    x1 = x_ref[...] + g1 * _dot(mx_ref[...], wout_ref[...])
    h2 = ((_rms(x1) * _row(gff_ref, layer)) * (1.0 + sc2) + sh2).astype(BF)
    acc = jnp.zeros((TM_FFN, D_MODEL), F32)
    for c in range(D_FF // D_MODEL):
        a = jnp.maximum(_dot(h2, w1_ref[:, c * D_MODEL:(c + 1) * D_MODEL]), 0.0)
        acc = acc + _dot((a * a).astype(BF), w2_ref[c * D_MODEL:(c + 1) * D_MODEL, :])
    x2 = x1 + g2 * acc
    o_ref[...] = _rms(x2) * gfin_ref[...] if layer == DEPTH - 1 else x2


def _out_ffn(name, layer, tiles_per_mod, mod_base, x, mixed, prm, g_final):
    n = x.shape[0]
    tile = pl.BlockSpec((TM_FFN, D_MODEL), lambda i: (i, 0))
    return pl.pallas_call(
        functools.partial(_out_ffn_kernel, layer, tiles_per_mod, mod_base),
        grid=(n // TM_FFN,),
        in_specs=[tile, tile,
                  _layer_spec(layer, (16, N_MOD * D_MODEL)),
                  _layer_spec(layer, (D_MODEL, D_MODEL)),
                  _const_spec((DEPTH, D_MODEL)),
                  _layer_spec(layer, (D_MODEL, D_FF)),
                  _layer_spec(layer, (D_FF, D_MODEL)),
                  _const_spec((1, D_MODEL))],
        out_specs=tile,
        out_shape=jax.ShapeDtypeStruct((n, D_MODEL), F32),
        compiler_params=_cparams("arbitrary"),
        name=name,
    )(x, mixed, prm["mod"], prm["w_out"], prm["g_ff"], prm["w_ff1"], prm["w_ff2"], g_final)


def _rope_tables():
    t = np.arange(DEC_SEQ)
    rows = (t // GRID_W).astype(np.float32)
    cols = (t % GRID_W).astype(np.float32)
    c = np.arange(GROUP_W) % 32
    freqs = np.float32(ROPE_BASE) ** (-np.arange(8, dtype=np.float32) * np.float32(2.0) / np.float32(16))
    pos = np.where((c < 16)[None, :], rows[:, None], cols[:, None]).astype(np.float32)
    ang = (pos * freqs[(c % 16) % 8][None, :]).astype(np.float32)
    first = ((c % 16) < 8)[None, :]
    cos = np.cos(ang).astype(np.float32)
    sin = np.sin(ang).astype(np.float32)
    return jnp.asarray(cos), jnp.asarray(np.where(first, -sin, sin))


def _qup_gather_index():
    idx = np.full((N_HEADS * GROUP_W,), -1, np.int64)
    for h in range(N_HEADS):
        src = h * (MLA_NOPE + MLA_ROPE)
        dst = h * GROUP_W + (h % 2) * MLA_NOPE
        idx[dst:dst + MLA_NOPE] = np.arange(src, src + MLA_NOPE)
        idx[h * GROUP_W + 128:h * GROUP_W + 128 + MLA_ROPE] = np.arange(src + MLA_NOPE, src + MLA_NOPE + MLA_ROPE)
    return idx


def _kvup_perm():
    k = [h * 128 + d for h in range(N_HEADS) for d in range(MLA_NOPE)]
    v = [h * 128 + MLA_NOPE + d for h in range(N_HEADS) for d in range(HEAD_DIM)]
    return np.asarray(k + v)


def _heads_to_lanes(c):
    b, l, h, s, d = c.shape
    return c.transpose(0, 1, 3, 2, 4).reshape(b, l, s, h * d).astype(BF)


def kernel(x_prompt, x_sample, c, cache_da_k, cache_da_v, cache_mla_ckv, cache_mla_krope, cache_na_k, cache_na_v, c_ctx, w_mod, b_mod, g_norm_mix, g_norm_ff, w_in, da_lambda_q1, da_lambda_k1, da_lambda_q2, da_lambda_k2, g_da_subln, g_mla_q, w_mla_qup, g_mla_kv, w_mla_kvup, na_rpb, conv_dw, conv_b, conv_ln_g, conv_ln_b, w_out, w_ff1, w_ff2, g_final):
    qidx = _qup_gather_index()
    w_qup_e = jnp.where(jnp.asarray(qidx >= 0)[None, None, :],
                        jnp.take(w_mla_qup, jnp.asarray(np.maximum(qidx, 0)), axis=-1), 0.0).astype(BF)
    w_kvup_p = jnp.take(w_mla_kvup, jnp.asarray(_kvup_perm()), axis=-1).astype(BF)
    prm = dict(
        g_mix=g_norm_mix, w_in=jnp.swapaxes(w_in, 1, 2), g_q=g_mla_q, w_qup=w_qup_e, g_kv=g_mla_kv, w_kvup=w_kvup_p,
        lamv=jnp.stack([da_lambda_q1, da_lambda_k1, da_lambda_q2, da_lambda_k2], axis=1),
        g_sub=jnp.tile(g_da_subln, (1, N_HEADS)),
        dw=jnp.concatenate([conv_dw, jnp.zeros((DEPTH, 1, GROUP_W), F32)], axis=1),
        cb=conv_b, ln_g=conv_ln_g, ln_b=conv_ln_b,
        w_out=w_out.astype(BF), g_ff=g_norm_ff, w_ff1=w_ff1.astype(BF), w_ff2=w_ff2.astype(BF))
    cv = jnp.concatenate([c_ctx[None, :], c, jnp.zeros((16 - 1 - DEC_BATCH, D_MODEL), F32)], axis=0)
    cos_t, sin_t = _rope_tables()
    g_final2 = g_final.reshape(1, D_MODEL)

    prm["mod"] = _modulation(cv, w_mod, b_mod)
    c_kr_pad = jnp.pad(cache_mla_krope, ((0, 0), (0, 0), (0, 0), (0, KR_PAD)))
    cka, ckb, cmv = _mla_cache(cache_mla_ckv, c_kr_pad, w_kvup_p)
    caches = (_heads_to_lanes(cache_da_k), _heads_to_lanes(cache_da_v), cka, ckb, cmv,
              _heads_to_lanes(cache_na_k), _heads_to_lanes(cache_na_v))
    nab = _na_bias(na_rpb)

    xp = x_prompt.reshape(BATCH * SEQ, D_MODEL)
    xs = x_sample.reshape(DEC_BATCH * DEC_SEQ, D_MODEL)
    new_ctx = ()
    for l in range(DEPTH):
        mixed_p, *new_ctx = _ctx_layer(l, xp, prm, tuple(new_ctx))
        xp = _out_ffn(f"ctx_ffn{l}", l, BATCH * SEQ // TM_FFN, 0, xp, mixed_p, prm, g_final2)
        proj_outs = _lat_proj(l, xs, prm, cos_t, sin_t)
        mixed_s = _lat_attn(l, proj_outs, caches, nab, prm)
        xs = _out_ffn(f"lat_ffn{l}", l, DEC_SEQ // TM_FFN, 1, xs, mixed_s, prm, g_final2)

    y_prompt = xp.reshape(BATCH, SEQ, D_MODEL)
    y_sample = xs.reshape(DEC_BATCH, DEC_SEQ, D_MODEL)
    new_ctx = [a if k == 2 else jnp.swapaxes(a, -1, -2) for k, a in enumerate(new_ctx)]
    return (y_prompt, y_sample) + tuple(new_ctx)
```

```python
import functools
import math

import numpy as np
import jax
import jax.numpy as jnp
from jax import lax
from jax.experimental import pallas as pl
from jax.experimental.pallas import tpu as pltpu

F32 = jnp.float32
BF = jnp.bfloat16

D_MODEL = 1024
BATCH = 16
SEQ = 256
DEPTH = 2
DEC_BATCH = 2
DEC_SEQ = 2048
PAST_LEN = 256
GRID_W = 64
GROUP_W = 256
HEAD_DIM = 64
N_HEADS = 4
DA_QK = 32
MLA_NOPE = 64
MLA_ROPE = 32
MLA_KV_LORA = 128
NA_KR = 8
NA_KC = 16
CONV_W = 31
D_FF = 4096
ROPE_BASE = 10000.0
EPS = 1e-6
N_MOD = 6
IN_COLS = 2464

P_DAQ, P_DAK, P_DAV, P_QD, P_KVD, P_KR = 0, 256, 512, 768, 1024, 1152
P_NAQ, P_NAK, P_NAV, P_CONV, P_TOT = 1280, 1536, 1792, 2048, 2560
P_SPLIT = P_NAQ
N_CTX_IN = 14
KR_ORIG_END = 1184
KR_PAD = 128 - MLA_ROPE

DA_SCALE = DA_QK ** -0.5
MLA_SCALE = (MLA_NOPE + MLA_ROPE) ** -0.5
NA_SCALE = HEAD_DIM ** -0.5
LOG2E = math.log2(math.e)

TQ = 256
N_QT = DEC_SEQ // TQ
KEYS = DEC_SEQ + PAST_LEN
ROWS_PER_TILE = TQ // GRID_W
N_ROWS = DEC_SEQ // GRID_W
NA_LOCAL = NA_KR * GRID_W
TM_FFN = 512
TM_PROJ = 512
VMEM_LIMIT = 58 * 1024 * 1024

NT_DIMS = (((1,), (1,)), ((), ()))


def _cparams(*sem):
    return pltpu.CompilerParams(dimension_semantics=sem, vmem_limit_bytes=VMEM_LIMIT)


def _const_spec(shape):
    nd = len(shape)
    return pl.BlockSpec(shape, lambda *_: (0,) * nd, pipeline_mode=pl.Buffered(1))


def _layer_spec(layer, shape):
    nd = len(shape)
    return pl.BlockSpec((None,) + tuple(shape), lambda *_: (layer,) + (0,) * nd, pipeline_mode=pl.Buffered(1))


def _rms(x):
    return x * lax.rsqrt(jnp.mean(x * x, axis=-1, keepdims=True) + EPS)


def _row(ref, layer):
    return ref[layer:layer + 1, :]


def _dot(a, b):
    return jnp.dot(a, b, preferred_element_type=F32)


def _qk(q, k):
    return lax.dot_general(q, k, NT_DIMS, preferred_element_type=F32)


def _softmax2_parts(s):
    m = jnp.max(s, axis=-1, keepdims=True)
    e = jnp.exp2(s - m)
    return e, 1.0 / jnp.sum(e, axis=-1, keepdims=True)


def _lane_ids(width):
    return lax.broadcasted_iota(jnp.int32, (1, width), 1)


def _diff_lambda(lamv_ref, lam_init):
    v = lamv_ref[...]
    a = jnp.exp(jnp.sum(v[0:1] * v[1:2], axis=-1, keepdims=True))
    b = jnp.exp(jnp.sum(v[2:3] * v[3:4], axis=-1, keepdims=True))
    return a - b + lam_init


def _group_of(lane, group):
    return lane >> (group.bit_length() - 1)


def _head_rms(x):
    r = _group_of(lax.broadcasted_iota(jnp.int32, (GROUP_W, GROUP_W), 0), HEAD_DIM)
    c = _group_of(lax.broadcasted_iota(jnp.int32, (GROUP_W, GROUP_W), 1), HEAD_DIM)
    ones_bd = jnp.where(r == c, 1.0, 0.0).astype(BF)
    sq = x * x
    hi = sq.astype(BF)
    lo = (sq - hi.astype(F32)).astype(BF)
    ss = _dot(hi, ones_bd) + _dot(lo, ones_bd)
    return x * lax.rsqrt(ss * (1.0 / HEAD_DIM) + EPS)


def _rope(z, cos, sin):
    lane = _lane_ids(z.shape[1])
    swapped = jnp.where((lane & 15) < 8, pltpu.roll(z, 120, 1), pltpu.roll(z, 8, 1))
    return z * cos + swapped * sin


def _stage_w_in(wint_ref, wbt_ref):
    wbt_ref[0:KR_ORIG_END, :] = wint_ref[0:KR_ORIG_END, :].astype(BF)
    wbt_ref[KR_ORIG_END:P_SPLIT, :] = jnp.zeros((P_SPLIT - KR_ORIG_END, D_MODEL), BF)
    wbt_ref[P_SPLIT:P_TOT, :] = wint_ref[KR_ORIG_END:IN_COLS, :].astype(BF)


def _mask_heads(qf, lane, group, idx):
    return jnp.where(_group_of(lane, group) == idx, qf, 0.0).astype(BF)


def _conv_ln_silu(layer, gpad_ref, zsh_ref, n, dw_ref, cb_ref, lng_ref, lnb_ref):
    y = jnp.zeros((n, GROUP_W), F32) + _row(cb_ref, layer)
    for b in range(8):
        z = None
        for a in range(4):
            t = 8 * a + b - 1
            if 0 <= t < CONV_W:
                term = gpad_ref[pl.ds(8 * a, n + 8), :] * dw_ref[t:t + 1, :]
                z = term if z is None else z + term
        if b == 0:
            y = y + z[0:n]
        else:
            zsh_ref[b - 1] = z
            y = y + zsh_ref[b - 1, pl.ds(b, n), :]
    mu = jnp.mean(y, axis=-1, keepdims=True)
    yc = y - mu
    var = jnp.mean(yc * yc, axis=-1, keepdims=True)
    z = yc * lax.rsqrt(var + EPS) * _row(lng_ref, layer) + _row(lnb_ref, layer)
    return z * jax.nn.sigmoid(z)


def _mod_kernel(cv_ref, w_ref, b_ref, o_ref):
    c = cv_ref[...]
    a = (c * jax.nn.sigmoid(c)).astype(BF)
    o_ref[0] = _dot(a, w_ref[0].astype(BF)) + b_ref[0]


def _modulation(cv, w_mod, b_mod):
    nblk = 4
    wblk = N_MOD * D_MODEL // nblk
    return pl.pallas_call(
        _mod_kernel,
        grid=(DEPTH, nblk),
        in_specs=[pl.BlockSpec((16, D_MODEL), lambda l, j: (0, 0)),
                  pl.BlockSpec((1, D_MODEL, wblk), lambda l, j: (l, 0, j)),
                  pl.BlockSpec((1, 1, wblk), lambda l, j: (l, 0, j))],
        out_specs=pl.BlockSpec((1, 16, wblk), lambda l, j: (l, 0, j)),
        out_shape=jax.ShapeDtypeStruct((DEPTH, 16, N_MOD * D_MODEL), F32),
        compiler_params=_cparams("arbitrary", "arbitrary"),
        name="modulation",
    )(cv, w_mod, b_mod.reshape(DEPTH, 1, N_MOD * D_MODEL))


def _mla_cache_kernel(ckv_ref, kr_ref, w_ref, ka_ref, kb_ref, v_ref):
    kvm = _dot(ckv_ref[0, 0].astype(BF), w_ref[0])
    kr = kr_ref[0, 0].astype(BF)
    kn = kvm[:, :GROUP_W].astype(BF)
    ka_ref[0, 0] = jnp.concatenate([kn[:, :128], kr], axis=1)
    kb_ref[0, 0] = jnp.concatenate([kn[:, 128:], kr], axis=1)
    v_ref[0, 0] = kvm[:, GROUP_W:].astype(BF)


def _mla_cache(c_ckv, c_kr_pad, w_kvup_p):
    blk = pl.BlockSpec((1, 1, PAST_LEN, GROUP_W), lambda b, l: (b, l, 0, 0))
    shp = jax.ShapeDtypeStruct((DEC_BATCH, DEPTH, PAST_LEN, GROUP_W), BF)
    return pl.pallas_call(
        _mla_cache_kernel,
        grid=(DEC_BATCH, DEPTH),
        in_specs=[pl.BlockSpec((1, 1, PAST_LEN, MLA_KV_LORA), lambda b, l: (b, l, 0, 0)),
                  pl.BlockSpec((1, 1, PAST_LEN, 128), lambda b, l: (b, l, 0, 0)),
                  pl.BlockSpec((1, MLA_KV_LORA, 2 * GROUP_W), lambda b, l: (l, 0, 0))],
        out_specs=[blk, blk, blk],
        out_shape=[shp, shp, shp],
        compiler_params=_cparams("arbitrary", "arbitrary"),
        name="mla_cache",
    )(c_ckv, c_kr_pad, w_kvup_p)


N_DR = 2 * NA_KR - 1
N_DC = 2 * NA_KC - 1


def _na_bias_kernel(rpb_ref, o_ref, tp_ref):
    base = (pl.program_id(0) * N_HEADS + pl.program_id(1)) * (N_DR * N_DC)
    cq = lax.broadcasted_iota(jnp.int32, (GRID_W, 128), 0)
    lane = lax.broadcasted_iota(jnp.int32, (GRID_W, 128), 1)
    ck = lane & (GRID_W - 1)
    upper = lane >= GRID_W
    cstart = jnp.clip(cq - NA_KC // 2, 0, GRID_W - NA_KC)
    ok = (ck >= cstart) & (ck < cstart + NA_KC)

    j = lax.broadcasted_iota(jnp.int32, (8, 128), 1)
    dclip = jnp.clip(jnp.where(j < GRID_W, j, j - 128), -(NA_KC - 1), NA_KC - 1) + NA_KC - 1

    def toeplitz(dr):
        u = jnp.zeros((8, 128), F32)
        for d in range(N_DC):
            u = jnp.where(dclip == d, rpb_ref[base + dr * N_DC + d], u)
        rows = jnp.broadcast_to(u[0:1], (GRID_W, 128))
        return pltpu.roll(rows, 0, 1, stride=1, stride_axis=0)

    tabs = [toeplitz(dr) for dr in range(N_DR)]
    for dr in range(N_DR - 1):
        val = jnp.where(upper, pltpu.roll(tabs[dr + 1], GRID_W, 1), tabs[dr])
        tp_ref[dr] = jnp.where(ok, val, -jnp.inf)
    for dr0 in range(NA_KR):
        o_ref[0, dr0] = jnp.concatenate([tp_ref[dr0 + 2 * m] for m in range(NA_KR // 2)], axis=1)


def _na_bias(rpb):
    return pl.pallas_call(
        _na_bias_kernel,
        grid=(DEPTH, N_HEADS),
        in_specs=[pl.BlockSpec(memory_space=pltpu.SMEM)],
        out_specs=pl.BlockSpec((1, NA_KR, GRID_W, NA_LOCAL), lambda l, h: (l, 0, h, 0)),
        out_shape=jax.ShapeDtypeStruct((DEPTH, NA_KR, N_HEADS * GRID_W, NA_LOCAL), F32),
        scratch_shapes=[pltpu.VMEM((N_DR - 1, GRID_W, 128), F32)],
        compiler_params=_cparams("arbitrary", "arbitrary"),
        name="na_bias",
    )(rpb.reshape(-1))


def _ctx_kernel(layer, lam_init, n_prev, *refs):
    (x_ref, mod_ref, gmix_ref, win_ref, gq_ref, wqup_ref, gkv_ref, wkv_ref,
     lamv_ref, gsub_ref, dw_ref, cb_ref, lng_ref, lnb_ref) = refs[:N_CTX_IN]
    (mixed_ref, dak_ref, dav_ref, ckv_ref, kr_ref, nak_ref, nav_ref,
     proj_ref, gpad_ref, zsh_ref, wbf_ref) = refs[N_CTX_IN + n_prev:]

    @pl.when(pl.program_id(0) == 0)
    def _():
        _stage_w_in(win_ref, wbf_ref)

    x = x_ref[...]
    sh1 = mod_ref[0:1, 0:D_MODEL]
    sc1 = mod_ref[0:1, D_MODEL:2 * D_MODEL]
    h = ((_rms(x) * _row(gmix_ref, layer)) * (1.0 + sc1) + sh1).astype(BF)
    proj_ref[:, 0:P_SPLIT] = _qk(h, wbf_ref[0:P_SPLIT, :])
    proj_ref[:, P_SPLIT:P_TOT] = _qk(h, wbf_ref[P_SPLIT:P_TOT, :])

    for ref, col in ((dak_ref, P_DAK), (dav_ref, P_DAV), (nak_ref, P_NAK), (nav_ref, P_NAV)):
        slab_t = proj_ref[:, col:col + GROUP_W].T
        for hh in range(N_HEADS):
            ref[0, hh] = slab_t[hh * HEAD_DIM:(hh + 1) * HEAD_DIM, :]

    lane = _lane_ids(GROUP_W)

    lam = _diff_lambda(lamv_ref, lam_init)

    qa = proj_ref[:, P_DAQ:P_DAQ + GROUP_W] * (DA_SCALE * LOG2E)
    ka = proj_ref[:, P_DAK:P_DAK + GROUP_W].astype(BF)
    va = proj_ref[:, P_DAV:P_DAV + GROUP_W].astype(BF)
    qn = proj_ref[:, P_NAQ:P_NAQ + GROUP_W] * (NA_SCALE * LOG2E)
    kn2 = proj_ref[:, P_NAK:P_NAK + GROUP_W].astype(BF)
    vn = proj_ref[:, P_NAV:P_NAV + GROUP_W].astype(BF)
    mla = {}

    def mla_prep():
        qd = _rms(proj_ref[:, P_QD:P_QD + GROUP_W]) * _row(gq_ref, layer)
        ckv = _rms(proj_ref[:, P_KVD:P_KVD + MLA_KV_LORA]) * _row(gkv_ref, layer)
        ckv_ref[0] = ckv
        kr_pad = proj_ref[:, P_KR:P_KR + 128]
        kr_ref[0] = kr_pad.T[0:MLA_ROPE, :]
        mla["q"] = (_dot(qd.astype(BF), wqup_ref[...]) * (MLA_SCALE * LOG2E)).astype(BF)
        kvm = _dot(ckv.astype(BF), wkv_ref[...])
        kn = kvm[:, :GROUP_W].astype(BF)
        krb = kr_pad.astype(BF)
        mla["k"] = (jnp.concatenate([kn[:, :128], krb], axis=1), jnp.concatenate([kn[:, 128:], krb], axis=1))
        mla["v"] = kvm[:, GROUP_W:].astype(BF)

    def conv():
        g = proj_ref[:, P_CONV:P_CONV + GROUP_W] * jax.nn.sigmoid(proj_ref[:, P_CONV + GROUP_W:P_TOT])
        gpad_ref[0:16] = jnp.zeros((16, GROUP_W), F32)
        gpad_ref[16 + SEQ:32 + SEQ] = jnp.zeros((16, GROUP_W), F32)
        gpad_ref[16:16 + SEQ] = g
        return _conv_ln_silu(layer, gpad_ref, zsh_ref, SEQ, dw_ref, cb_ref, lng_ref, lnb_ref)

    def scores(i):
        if i < N_HEADS:
            return (_qk(_mask_heads(qa, lane, DA_QK, 2 * i), ka),
                    _qk(_mask_heads(qa, lane, DA_QK, 2 * i + 1), ka))
        if i < 2 * N_HEADS:
            hh = i - N_HEADS
            return (_qk(mla["q"][:, hh * GROUP_W:(hh + 1) * GROUP_W], mla["k"][hh // 2]),)
        return (_qk(_mask_heads(qn, lane, HEAD_DIM, i - 2 * N_HEADS), kn2),)

    outs = [jnp.zeros((SEQ, GROUP_W), F32)] * 3
    s_next = scores(0)
    o_conv = None
    for i in range(3 * N_HEADS):
        s_cur = s_next
        if i + 1 < 3 * N_HEADS:
            s_next = scores(i + 1)
        if i == 0:
            mla_prep()
        if i == N_HEADS - 1:
            o_conv = conv()
        grp, hh = divmod(i, N_HEADS)
        if grp == 0:
            e1, r1 = _softmax2_parts(s_cur[0])
            e2, r2 = _softmax2_parts(s_cur[1])
            o = _dot((e1 * r1 - e2 * (lam * r2)).astype(BF), va)
        else:
            e, r = _softmax2_parts(s_cur[0])
            o = _dot(e.astype(BF), mla["v"] if grp == 1 else vn) * r
        outs[grp] = jnp.where(_group_of(lane, HEAD_DIM) == hh, o, outs[grp])
    o_da = _head_rms(outs[0]) * _row(gsub_ref, layer) * (1.0 - lam_init)
    o_mla, o_na = outs[1], outs[2]

    mixed_ref[...] = jnp.concatenate([o_da, o_mla, o_na, o_conv], axis=1).astype(BF)


def _ctx_layer(layer, x, prm, prev):
    lam_init = 0.8 - 0.6 * math.exp(-0.3 * layer)
    head_blk = pl.BlockSpec((1, None, N_HEADS, HEAD_DIM, SEQ), lambda b: (b, layer, 0, 0, 0))
    head_shp = jax.ShapeDtypeStruct((BATCH, DEPTH, N_HEADS, HEAD_DIM, SEQ), F32)
    in_specs = [pl.BlockSpec((SEQ, D_MODEL), lambda b: (b, 0)),
                _layer_spec(layer, (16, N_MOD * D_MODEL)),
                _const_spec((DEPTH, D_MODEL)),
                _layer_spec(layer, (IN_COLS, D_MODEL)),
                _const_spec((DEPTH, GROUP_W)),
                _layer_spec(layer, (GROUP_W, N_HEADS * GROUP_W)),
                _const_spec((DEPTH, MLA_KV_LORA)),
                _layer_spec(layer, (MLA_KV_LORA, 2 * GROUP_W)),
                _layer_spec(layer, (4, DA_QK)),
                _const_spec((DEPTH, GROUP_W)),
                _layer_spec(layer, (32, GROUP_W)),
                _const_spec((DEPTH, GROUP_W)),
                _const_spec((DEPTH, GROUP_W)),
                _const_spec((DEPTH, GROUP_W))] + [pl.BlockSpec(memory_space=pl.ANY)] * len(prev)
    out_specs = [pl.BlockSpec((SEQ, D_MODEL), lambda b: (b, 0)),
                 head_blk, head_blk,
                 pl.BlockSpec((1, None, SEQ, MLA_KV_LORA), lambda b: (b, layer, 0, 0)),
                 pl.BlockSpec((1, None, MLA_ROPE, SEQ), lambda b: (b, layer, 0, 0)),
                 head_blk, head_blk]
    out_shape = [jax.ShapeDtypeStruct((BATCH * SEQ, D_MODEL), BF),
                 head_shp, head_shp,
                 jax.ShapeDtypeStruct((BATCH, DEPTH, SEQ, MLA_KV_LORA), F32),
                 jax.ShapeDtypeStruct((BATCH, DEPTH, MLA_ROPE, SEQ), F32),
                 head_shp, head_shp]
    n_in = len(in_specs) - len(prev)
    return pl.pallas_call(
        functools.partial(_ctx_kernel, layer, lam_init, len(prev)),
        grid=(BATCH,),
        in_specs=in_specs,
        out_specs=out_specs,
        out_shape=out_shape,
        input_output_aliases={n_in + k: 1 + k for k in range(len(prev))},
        scratch_shapes=[pltpu.VMEM((SEQ, P_TOT), F32), pltpu.VMEM((SEQ + 32, GROUP_W), F32),
                        pltpu.VMEM((7, SEQ + 8, GROUP_W), F32), pltpu.VMEM((P_TOT, D_MODEL), BF)],
        compiler_params=_cparams("arbitrary"),
        name=f"ctx_layer{layer}",
    )(x, prm["mod"], prm["g_mix"], prm["w_in"], prm["g_q"], prm["w_qup"], prm["g_kv"], prm["w_kvup"],
      prm["lamv"], prm["g_sub"], prm["dw"], prm["cb"], prm["ln_g"], prm["ln_b"], *prev)


def _lat_proj_kernel(layer, x_ref, mod_ref, gmix_ref, win_ref, gq_ref, wqup_ref, gkv_ref, wkv_ref,
                     cos_ref, sin_ref,
                     daq_ref, mq_ref, naq_ref, g_ref,
                     dak_ref, dav_ref, ka_ref, kb_ref, mv_ref, nak_ref, nav_ref, wbf_ref):
    b = pl.program_id(0)

    @pl.when((b == 0) & (pl.program_id(1) == 0))
    def _():
        _stage_w_in(win_ref, wbf_ref)

    x = x_ref[...]
    sh1 = mod_ref[pl.ds(1 + b, 1), 0:D_MODEL]
    sc1 = mod_ref[pl.ds(1 + b, 1), D_MODEL:2 * D_MODEL]
    h = (_rms(x) * _row(gmix_ref, layer)) * (1.0 + sc1) + sh1
    hb = h.astype(BF)
    proj = jnp.concatenate([_qk(hb, wbf_ref[0:P_SPLIT, :]), _qk(hb, wbf_ref[P_SPLIT:P_TOT, :])], axis=1)
    cos = cos_ref[...]
    sin = sin_ref[...]

    def rope2(z):
        return jnp.concatenate([_rope(z[:, :128], cos[:, :128], sin[:, :128]),
                                _rope(z[:, 128:], cos[:, 128:], sin[:, 128:])], axis=1)

    daq_ref[0] = (rope2(proj[:, P_DAQ:P_DAQ + GROUP_W]) * (DA_SCALE * LOG2E)).astype(BF)
    dak_ref[0] = rope2(proj[:, P_DAK:P_DAK + GROUP_W]).astype(BF)
    dav_ref[0] = proj[:, P_DAV:P_DAV + GROUP_W].astype(BF)

    qd = _rms(proj[:, P_QD:P_QD + GROUP_W]) * _row(gq_ref, layer)
    qm = _dot(qd.astype(BF), wqup_ref[...])
    for hh in range(N_HEADS):
        nope = qm[:, hh * GROUP_W:hh * GROUP_W + 128]
        rope = _rope(qm[:, hh * GROUP_W + 128:(hh + 1) * GROUP_W], cos[:, :128], sin[:, :128])
        mq_ref[0, hh] = (jnp.concatenate([nope, rope], axis=1) * (MLA_SCALE * LOG2E)).astype(BF)
    ckv = _rms(proj[:, P_KVD:P_KVD + MLA_KV_LORA]) * _row(gkv_ref, layer)
    kvm = _dot(ckv.astype(BF), wkv_ref[...])
    kn = kvm[:, :GROUP_W].astype(BF)
    krb = _rope(proj[:, P_KR:P_KR + 128], cos[:, :128], sin[:, :128]).astype(BF)
    ka_ref[0] = jnp.concatenate([kn[:, :128], krb], axis=1)
    kb_ref[0] = jnp.concatenate([kn[:, 128:], krb], axis=1)
    mv_ref[0] = kvm[:, GROUP_W:].astype(BF)

    naq_ref[0] = (proj[:, P_NAQ:P_NAQ + GROUP_W] * NA_SCALE).astype(BF)
    nak_ref[0] = proj[:, P_NAK:P_NAK + GROUP_W].astype(BF)
    nav_ref[0] = proj[:, P_NAV:P_NAV + GROUP_W].astype(BF)
    g_ref[0] = proj[:, P_CONV:P_CONV + GROUP_W] * jax.nn.sigmoid(proj[:, P_CONV + GROUP_W:P_TOT])


def _lat_proj(layer, x, prm, cos_t, sin_t):
    n_t = DEC_SEQ // TM_PROJ
    row_spec = pl.BlockSpec((1, TM_PROJ, GROUP_W), lambda b, j: (b, j, 0))
    in_specs = [pl.BlockSpec((TM_PROJ, D_MODEL), lambda b, j: (b * n_t + j, 0)),
                _layer_spec(layer, (16, N_MOD * D_MODEL)),
                _const_spec((DEPTH, D_MODEL)),
                _layer_spec(layer, (IN_COLS, D_MODEL)),
                _const_spec((DEPTH, GROUP_W)),
                _layer_spec(layer, (GROUP_W, N_HEADS * GROUP_W)),
                _const_spec((DEPTH, MLA_KV_LORA)),
                _layer_spec(layer, (MLA_KV_LORA, 2 * GROUP_W)),
                pl.BlockSpec((TM_PROJ, GROUP_W), lambda b, j: (j, 0)),
                pl.BlockSpec((TM_PROJ, GROUP_W), lambda b, j: (j, 0))]
    row_shp = jax.ShapeDtypeStruct((DEC_BATCH, DEC_SEQ, GROUP_W), BF)
    out_specs = [row_spec,
                 pl.BlockSpec((1, N_HEADS, TM_PROJ, GROUP_W), lambda b, j: (b, 0, j, 0)),
                 row_spec, row_spec] + [row_spec] * 7
    out_shape = [row_shp,
                 jax.ShapeDtypeStruct((DEC_BATCH, N_HEADS, DEC_SEQ, GROUP_W), BF),
                 row_shp,
                 jax.ShapeDtypeStruct((DEC_BATCH, DEC_SEQ, GROUP_W), F32)] + [row_shp] * 7
    return pl.pallas_call(
        functools.partial(_lat_proj_kernel, layer),
        grid=(DEC_BATCH, n_t),
        in_specs=in_specs,
        out_specs=out_specs,
        out_shape=out_shape,
        scratch_shapes=[pltpu.VMEM((P_TOT, D_MODEL), BF)],
        compiler_params=_cparams("arbitrary", "arbitrary"),
        name=f"lat_proj{layer}",
    )(x, prm["mod"], prm["g_mix"], prm["w_in"], prm["g_q"], prm["w_qup"], prm["g_kv"], prm["w_kvup"],
      cos_t, sin_t)


def _lat_attn_kernel(layer, lam_init, daq_ref, mq_ref, naq_ref, g_ref,
                     dak_ref, dav_ref, ka_ref, kb_ref, mv_ref, nak_ref, nav_ref,
                     cdak_ref, cdav_ref, cka_ref, ckb_ref, cmv_ref, cnak_ref, cnav_ref,
                     nab_ref, lamv_ref, gsub_ref, dw_ref, cb_ref, lng_ref, lnb_ref,
                     mixed_ref, gpad_ref, zsh_ref, kt_ref, v_ref):
    t = pl.program_id(1)
    lane = _lane_ids(GROUP_W)

    @pl.when(t == 0)
    def _():
        for slot, (lat, cache) in enumerate(((dak_ref, cdak_ref), (ka_ref, cka_ref), (kb_ref, ckb_ref))):
            kt_ref[slot, :, 0:DEC_SEQ] = lat[0].T
            kt_ref[slot, :, DEC_SEQ:KEYS] = cache[0].T
        for slot, (lat, cache) in enumerate(((dav_ref, cdav_ref), (mv_ref, cmv_ref))):
            v_ref[slot, 0:DEC_SEQ, :] = lat[0]
            v_ref[slot, DEC_SEQ:KEYS, :] = cache[0]

    lam = _diff_lambda(lamv_ref, lam_init)
    qa = daq_ref[0].astype(F32)
    va = v_ref[0]
    ka_t = kt_ref[0]
    vm = v_ref[1]

    kc = cnak_ref[0]
    vc = cnav_ref[0]
    n_items = 2 * N_HEADS + ROWS_PER_TILE

    def na_window(j):
        r = t * ROWS_PER_TILE + j
        start = jnp.clip(r - NA_KR // 2, 0, N_ROWS - NA_KR)
        return start - r + NA_KR - 1, pl.multiple_of(start * GRID_W, GRID_W)

    def scores(i):
        if i < N_HEADS:
            return (_dot(_mask_heads(qa, lane, DA_QK, 2 * i), ka_t),
                    _dot(_mask_heads(qa, lane, DA_QK, 2 * i + 1), ka_t))
        if i < 2 * N_HEADS:
            hh = i - N_HEADS
            return (_dot(mq_ref[0, hh], kt_ref[1 + hh // 2]),)
        j = i - 2 * N_HEADS
        dr0, koff = na_window(j)
        qrow = naq_ref[0, j * GRID_W:(j + 1) * GRID_W, :].astype(F32)
        q4 = jnp.concatenate([_mask_heads(qrow, lane, HEAD_DIM, hh) for hh in range(N_HEADS)], axis=0)
        return (_qk(q4, nak_ref[0, pl.ds(koff, NA_LOCAL), :]) + nab_ref[dr0], _qk(q4, kc))

    o_da = jnp.zeros((TQ, GROUP_W), F32)
    o_mla = jnp.zeros((TQ, GROUP_W), F32)
    na_rows = []
    s_next = scores(0)

    base = pl.multiple_of(t * TQ, TQ)
    gpad_ref[16:16 + TQ] = g_ref[0, pl.ds(base, TQ), :]
    lo = g_ref[0, pl.ds(pl.multiple_of(jnp.maximum(base - 16, 0), 16), 16), :]
    hi = g_ref[0, pl.ds(pl.multiple_of(jnp.minimum(base + TQ, DEC_SEQ - 16), 16), 16), :]
    gpad_ref[0:16] = jnp.where(t > 0, lo, 0.0)
    gpad_ref[16 + TQ:32 + TQ] = jnp.where(t < N_QT - 1, hi, 0.0)
    o_conv = _conv_ln_silu(layer, gpad_ref, zsh_ref, TQ, dw_ref, cb_ref, lng_ref, lnb_ref)

    for i in range(n_items):
        s_cur = s_next
        if i + 1 < n_items:
            s_next = scores(i + 1)
        if i < N_HEADS:
            e1, r1 = _softmax2_parts(s_cur[0])
            e2, r2 = _softmax2_parts(s_cur[1])
            p = (e1 - e2 * (lam * r2 / r1)).astype(BF)
            o_da = jnp.where(_group_of(lane, HEAD_DIM) == i, _dot(p, va) * r1, o_da)
        elif i < 2 * N_HEADS:
            e, r = _softmax2_parts(s_cur[0])
            o_mla = jnp.where(_group_of(lane, HEAD_DIM) == i - N_HEADS, _dot(e.astype(BF), vm) * r, o_mla)
        else:
            _, koff = na_window(i - 2 * N_HEADS)
            s_loc, s_ctx = s_cur
            m = jnp.maximum(jnp.max(s_loc, axis=-1, keepdims=True), jnp.max(s_ctx, axis=-1, keepdims=True))
            e_loc = jnp.exp(s_loc - m)
            e_ctx = jnp.exp(s_ctx - m)
            den = jnp.sum(e_loc, axis=-1, keepdims=True) + jnp.sum(e_ctx, axis=-1, keepdims=True)
            o4 = (_dot(e_loc.astype(BF), nav_ref[0, pl.ds(koff, NA_LOCAL), :])
                  + _dot(e_ctx.astype(BF), vc)) * (1.0 / den)
            o_row = o4[0:GRID_W]
            for hh in range(1, N_HEADS):
                o_row = jnp.where(_group_of(lane, HEAD_DIM) == hh, o4[hh * GRID_W:(hh + 1) * GRID_W], o_row)
            na_rows.append(o_row)
    o_da = _head_rms(o_da) * _row(gsub_ref, layer) * (1.0 - lam_init)
    o_na = jnp.concatenate(na_rows, axis=0)

    mixed_ref[...] = jnp.concatenate([o_da, o_mla, o_na, o_conv], axis=1).astype(BF)


def _lat_attn(layer, proj_outs, caches, nab, prm):
    lam_init = 0.8 - 0.6 * math.exp(-0.3 * layer)
    q_spec = pl.BlockSpec((1, TQ, GROUP_W), lambda b, t: (b, t, 0))
    full_k = pl.BlockSpec((1, DEC_SEQ, GROUP_W), lambda b, t: (b, 0, 0))
    cache_spec = pl.BlockSpec((1, None, PAST_LEN, GROUP_W), lambda b, t: (b, layer, 0, 0))
    in_specs = [q_spec,
                pl.BlockSpec((1, N_HEADS, TQ, GROUP_W), lambda b, t: (b, 0, t, 0)),
                q_spec, full_k] + [full_k] * 7 + [cache_spec] * 7 + [
                _layer_spec(layer, (NA_KR, N_HEADS * GRID_W, NA_LOCAL)),
                _layer_spec(layer, (4, DA_QK)),
                _const_spec((DEPTH, GROUP_W)),
                _layer_spec(layer, (32, GROUP_W)),
                _const_spec((DEPTH, GROUP_W)),
                _const_spec((DEPTH, GROUP_W)),
                _const_spec((DEPTH, GROUP_W))]
    return pl.pallas_call(
        functools.partial(_lat_attn_kernel, layer, lam_init),
        grid=(DEC_BATCH, N_QT),
        in_specs=in_specs,
        out_specs=pl.BlockSpec((TQ, D_MODEL), lambda b, t: (b * N_QT + t, 0)),
        out_shape=jax.ShapeDtypeStruct((DEC_BATCH * DEC_SEQ, D_MODEL), BF),
        scratch_shapes=[pltpu.VMEM((TQ + 32, GROUP_W), F32), pltpu.VMEM((7, TQ + 8, GROUP_W), F32),
                        pltpu.VMEM((3, GROUP_W, KEYS), BF), pltpu.VMEM((2, KEYS, GROUP_W), BF)],
        compiler_params=_cparams("arbitrary", "arbitrary"),
        name=f"lat_attn{layer}",
    )(*proj_outs, *caches, nab, prm["lamv"], prm["g_sub"], prm["dw"], prm["cb"], prm["ln_g"], prm["ln_b"])


def _out_ffn_kernel(layer, tiles_per_mod, mod_base, x_ref, mx_ref, mod_ref, wout_ref, gff_ref,
                    w1_ref, w2_ref, gfin_ref, o_ref):
    row = mod_base + pl.program_id(0) // tiles_per_mod
    g1 = mod_ref[pl.ds(row, 1), 2 * D_MODEL:3 * D_MODEL]
    sh2 = mod_ref[pl.ds(row, 1), 3 * D_MODEL:4 * D_MODEL]
    sc2 = mod_ref[pl.ds(row, 1), 4 * D_MODEL:5 * D_MODEL]
    g2 = mod_ref[pl.ds(row, 1), 5 * D_MODEL:6 * D_MODEL]
    x1 = x_ref[...] + g1 * _dot(mx_ref[...], wout_ref[...])
    h2 = ((_rms(x1) * _row(gff_ref, layer)) * (1.0 + sc2) + sh2).astype(BF)
    acc = jnp.zeros((TM_FFN, D_MODEL), F32)
    for c in range(D_FF // D_MODEL):
        a = jnp.maximum(_dot(h2, w1_ref[:, c * D_MODEL:(c + 1) * D_MODEL]), 0.0)
        acc = acc + _dot((a * a).astype(BF), w2_ref[c * D_MODEL:(c + 1) * D_MODEL, :])
    x2 = x1 + g2 * acc
    o_ref[...] = _rms(x2) * gfin_ref[...] if layer == DEPTH - 1 else x2


def _out_ffn(name, layer, tiles_per_mod, mod_base, x, mixed, prm, g_final):
    n = x.shape[0]
    tile = pl.BlockSpec((TM_FFN, D_MODEL), lambda i: (i, 0))
    return pl.pallas_call(
        functools.partial(_out_ffn_kernel, layer, tiles_per_mod, mod_base),
        grid=(n // TM_FFN,),
        in_specs=[tile, tile,
                  _layer_spec(layer, (16, N_MOD * D_MODEL)),
                  _layer_spec(layer, (D_MODEL, D_MODEL)),
                  _const_spec((DEPTH, D_MODEL)),
                  _layer_spec(layer, (D_MODEL, D_FF)),
                  _layer_spec(layer, (D_FF, D_MODEL)),
                  _const_spec((1, D_MODEL))],
        out_specs=tile,
        out_shape=jax.ShapeDtypeStruct((n, D_MODEL), F32),
        compiler_params=_cparams("arbitrary"),
        name=name,
    )(x, mixed, prm["mod"], prm["w_out"], prm["g_ff"], prm["w_ff1"], prm["w_ff2"], g_final)


def _rope_tables():
    t = np.arange(DEC_SEQ)
    rows = (t // GRID_W).astype(np.float32)
    cols = (t % GRID_W).astype(np.float32)
    c = np.arange(GROUP_W) % 32
    freqs = np.float32(ROPE_BASE) ** (-np.arange(8, dtype=np.float32) * np.float32(2.0) / np.float32(16))
    pos = np.where((c < 16)[None, :], rows[:, None], cols[:, None]).astype(np.float32)
    ang = (pos * freqs[(c % 16) % 8][None, :]).astype(np.float32)
    first = ((c % 16) < 8)[None, :]
    cos = np.cos(ang).astype(np.float32)
    sin = np.sin(ang).astype(np.float32)
    return jnp.asarray(cos), jnp.asarray(np.where(first, -sin, sin))


def _qup_gather_index():
    idx = np.full((N_HEADS * GROUP_W,), -1, np.int64)
    for h in range(N_HEADS):
        src = h * (MLA_NOPE + MLA_ROPE)
        dst = h * GROUP_W + (h % 2) * MLA_NOPE
        idx[dst:dst + MLA_NOPE] = np.arange(src, src + MLA_NOPE)
        idx[h * GROUP_W + 128:h * GROUP_W + 128 + MLA_ROPE] = np.arange(src + MLA_NOPE, src + MLA_NOPE + MLA_ROPE)
    return idx


def _kvup_perm():
    k = [h * 128 + d for h in range(N_HEADS) for d in range(MLA_NOPE)]
    v = [h * 128 + MLA_NOPE + d for h in range(N_HEADS) for d in range(HEAD_DIM)]
    return np.asarray(k + v)


def _heads_to_lanes(c):
    b, l, h, s, d = c.shape
    return c.transpose(0, 1, 3, 2, 4).reshape(b, l, s, h * d).astype(BF)


def kernel(x_prompt, x_sample, c, cache_da_k, cache_da_v, cache_mla_ckv, cache_mla_krope, cache_na_k, cache_na_v, c_ctx, w_mod, b_mod, g_norm_mix, g_norm_ff, w_in, da_lambda_q1, da_lambda_k1, da_lambda_q2, da_lambda_k2, g_da_subln, g_mla_q, w_mla_qup, g_mla_kv, w_mla_kvup, na_rpb, conv_dw, conv_b, conv_ln_g, conv_ln_b, w_out, w_ff1, w_ff2, g_final):
    qidx = _qup_gather_index()
    w_qup_e = jnp.where(jnp.asarray(qidx >= 0)[None, None, :],
                        jnp.take(w_mla_qup, jnp.asarray(np.maximum(qidx, 0)), axis=-1), 0.0).astype(BF)
    w_kvup_p = jnp.take(w_mla_kvup, jnp.asarray(_kvup_perm()), axis=-1).astype(BF)
    prm = dict(
        g_mix=g_norm_mix, w_in=jnp.swapaxes(w_in, 1, 2), g_q=g_mla_q, w_qup=w_qup_e, g_kv=g_mla_kv, w_kvup=w_kvup_p,
        lamv=jnp.stack([da_lambda_q1, da_lambda_k1, da_lambda_q2, da_lambda_k2], axis=1),
        g_sub=jnp.tile(g_da_subln, (1, N_HEADS)),
        dw=jnp.concatenate([conv_dw, jnp.zeros((DEPTH, 1, GROUP_W), F32)], axis=1),
        cb=conv_b, ln_g=conv_ln_g, ln_b=conv_ln_b,
        w_out=w_out.astype(BF), g_ff=g_norm_ff, w_ff1=w_ff1.astype(BF), w_ff2=w_ff2.astype(BF))
    cv = jnp.concatenate([c_ctx[None, :], c, jnp.zeros((16 - 1 - DEC_BATCH, D_MODEL), F32)], axis=0)
    cos_t, sin_t = _rope_tables()
    g_final2 = g_final.reshape(1, D_MODEL)

    prm["mod"] = _modulation(cv, w_mod, b_mod)
    c_kr_pad = jnp.pad(cache_mla_krope, ((0, 0), (0, 0), (0, 0), (0, KR_PAD)))
    cka, ckb, cmv = _mla_cache(cache_mla_ckv, c_kr_pad, w_kvup_p)
    caches = (_heads_to_lanes(cache_da_k), _heads_to_lanes(cache_da_v), cka, ckb, cmv,
              _heads_to_lanes(cache_na_k), _heads_to_lanes(cache_na_v))
    nab = _na_bias(na_rpb)

    xp = x_prompt.reshape(BATCH * SEQ, D_MODEL)
    xs = x_sample.reshape(DEC_BATCH * DEC_SEQ, D_MODEL)
    new_ctx = ()
    for l in range(DEPTH):
        mixed_p, *new_ctx = _ctx_layer(l, xp, prm, tuple(new_ctx))
        xp = _out_ffn(f"ctx_ffn{l}", l, BATCH * SEQ // TM_FFN, 0, xp, mixed_p, prm, g_final2)
        proj_outs = _lat_proj(l, xs, prm, cos_t, sin_t)
        mixed_s = _lat_attn(l, proj_outs, caches, nab, prm)
        xs = _out_ffn(f"lat_ffn{l}", l, DEC_SEQ // TM_FFN, 1, xs, mixed_s, prm, g_final2)

    y_prompt = xp.reshape(BATCH, SEQ, D_MODEL)
    y_sample = xs.reshape(DEC_BATCH, DEC_SEQ, D_MODEL)
    new_ctx = [a if k == 2 else jnp.swapaxes(a, -1, -2) for k, a in enumerate(new_ctx)]
    return (y_prompt, y_sample) + tuple(new_ctx)
```

```python
import functools
import math

import numpy as np
import jax
import jax.numpy as jnp
from jax import lax
from jax.experimental import pallas as pl
from jax.experimental.pallas import tpu as pltpu

F32 = jnp.float32
BF = jnp.bfloat16

D_MODEL = 1024
BATCH = 16
SEQ = 256
DEPTH = 2
DEC_BATCH = 2
DEC_SEQ = 2048
PAST_LEN = 256
GRID_W = 64
GROUP_W = 256
HEAD_DIM = 64
N_HEADS = 4
DA_QK = 32
MLA_NOPE = 64
MLA_ROPE = 32
MLA_KV_LORA = 128
NA_KR = 8
NA_KC = 16
CONV_W = 31
D_FF = 4096
ROPE_BASE = 10000.0
EPS = 1e-6
N_MOD = 6
IN_COLS = 2464

P_DAQ, P_DAK, P_DAV, P_QD, P_KVD, P_KR = 0, 256, 512, 768, 1024, 1152
P_NAQ, P_NAK, P_NAV, P_CONV, P_TOT = 1280, 1536, 1792, 2048, 2560
P_SPLIT = P_NAQ
N_CTX_IN = 14
KR_ORIG_END = 1184
KR_PAD = 128 - MLA_ROPE

DA_SCALE = DA_QK ** -0.5
MLA_SCALE = (MLA_NOPE + MLA_ROPE) ** -0.5
NA_SCALE = HEAD_DIM ** -0.5
LOG2E = math.log2(math.e)

TQ = 256
N_QT = DEC_SEQ // TQ
KEYS = DEC_SEQ + PAST_LEN
ROWS_PER_TILE = TQ // GRID_W
N_ROWS = DEC_SEQ // GRID_W
NA_LOCAL = NA_KR * GRID_W
TM_FFN = 1024
TM_PROJ = 512
VMEM_LIMIT = 58 * 1024 * 1024

NT_DIMS = (((1,), (1,)), ((), ()))


def _cparams(*sem):
    return pltpu.CompilerParams(dimension_semantics=sem, vmem_limit_bytes=VMEM_LIMIT)


def _const_spec(shape):
    nd = len(shape)
    return pl.BlockSpec(shape, lambda *_: (0,) * nd, pipeline_mode=pl.Buffered(1))


def _layer_spec(layer, shape):
    nd = len(shape)
    return pl.BlockSpec((None,) + tuple(shape), lambda *_: (layer,) + (0,) * nd, pipeline_mode=pl.Buffered(1))


def _rms(x):
    return x * lax.rsqrt(jnp.mean(x * x, axis=-1, keepdims=True) + EPS)


def _row(ref, layer):
    return ref[layer:layer + 1, :]


def _dot(a, b):
    return jnp.dot(a, b, preferred_element_type=F32)


def _qk(q, k):
    return lax.dot_general(q, k, NT_DIMS, preferred_element_type=F32)


def _softmax2_parts(s):
    m = jnp.max(s, axis=-1, keepdims=True)
    e = jnp.exp2(s - m)
    return e, 1.0 / jnp.sum(e, axis=-1, keepdims=True)


def _lane_ids(width):
    return lax.broadcasted_iota(jnp.int32, (1, width), 1)


def _diff_lambda(lamv_ref, lam_init):
    v = lamv_ref[...]
    a = jnp.exp(jnp.sum(v[0:1] * v[1:2], axis=-1, keepdims=True))
    b = jnp.exp(jnp.sum(v[2:3] * v[3:4], axis=-1, keepdims=True))
    return a - b + lam_init


def _group_of(lane, group):
    return lane >> (group.bit_length() - 1)


def _head_rms(x):
    r = _group_of(lax.broadcasted_iota(jnp.int32, (GROUP_W, GROUP_W), 0), HEAD_DIM)
    c = _group_of(lax.broadcasted_iota(jnp.int32, (GROUP_W, GROUP_W), 1), HEAD_DIM)
    ones_bd = jnp.where(r == c, 1.0, 0.0).astype(BF)
    sq = x * x
    hi = sq.astype(BF)
    lo = (sq - hi.astype(F32)).astype(BF)
    ss = _dot(hi, ones_bd) + _dot(lo, ones_bd)
    return x * lax.rsqrt(ss * (1.0 / HEAD_DIM) + EPS)


def _rope(z, cos, sin):
    lane = _lane_ids(z.shape[1])
    swapped = jnp.where((lane & 15) < 8, pltpu.roll(z, 120, 1), pltpu.roll(z, 8, 1))
    return z * cos + swapped * sin


def _stage_w_in(wint_ref, wbt_ref):
    wbt_ref[0:KR_ORIG_END, :] = wint_ref[0:KR_ORIG_END, :].astype(BF)
    wbt_ref[KR_ORIG_END:P_SPLIT, :] = jnp.zeros((P_SPLIT - KR_ORIG_END, D_MODEL), BF)
    wbt_ref[P_SPLIT:P_TOT, :] = wint_ref[KR_ORIG_END:IN_COLS, :].astype(BF)


def _mask_heads(qf, lane, group, idx):
    return jnp.where(_group_of(lane, group) == idx, qf, 0.0).astype(BF)


def _conv_ln_silu(layer, gpad_ref, zsh_ref, n, dw_ref, cb_ref, lng_ref, lnb_ref):
    y = jnp.zeros((n, GROUP_W), F32) + _row(cb_ref, layer)
    for b in range(8):
        z = None
        for a in range(4):
            t = 8 * a + b - 1
            if 0 <= t < CONV_W:
                term = gpad_ref[pl.ds(8 * a, n + 8), :] * dw_ref[t:t + 1, :]
                z = term if z is None else z + term
        if b == 0:
            y = y + z[0:n]
        else:
            zsh_ref[b - 1] = z
            y = y + zsh_ref[b - 1, pl.ds(b, n), :]
    mu = jnp.mean(y, axis=-1, keepdims=True)
    yc = y - mu
    var = jnp.mean(yc * yc, axis=-1, keepdims=True)
    z = yc * lax.rsqrt(var + EPS) * _row(lng_ref, layer) + _row(lnb_ref, layer)
    return z * jax.nn.sigmoid(z)


def _mod_kernel(cv_ref, w_ref, b_ref, o_ref):
    c = cv_ref[...]
    a = (c * jax.nn.sigmoid(c)).astype(BF)
    o_ref[0] = _dot(a, w_ref[0].astype(BF)) + b_ref[0]


def _modulation(cv, w_mod, b_mod):
    nblk = 4
    wblk = N_MOD * D_MODEL // nblk
    return pl.pallas_call(
        _mod_kernel,
        grid=(DEPTH, nblk),
        in_specs=[pl.BlockSpec((16, D_MODEL), lambda l, j: (0, 0)),
                  pl.BlockSpec((1, D_MODEL, wblk), lambda l, j: (l, 0, j)),
                  pl.BlockSpec((1, 1, wblk), lambda l, j: (l, 0, j))],
        out_specs=pl.BlockSpec((1, 16, wblk), lambda l, j: (l, 0, j)),
        out_shape=jax.ShapeDtypeStruct((DEPTH, 16, N_MOD * D_MODEL), F32),
        compiler_params=_cparams("arbitrary", "arbitrary"),
        name="modulation",
    )(cv, w_mod, b_mod.reshape(DEPTH, 1, N_MOD * D_MODEL))


def _mla_cache_kernel(ckv_ref, kr_ref, w_ref, ka_ref, kb_ref, v_ref):
    kvm = _dot(ckv_ref[0, 0].astype(BF), w_ref[0])
    kr = kr_ref[0, 0].astype(BF)
    kn = kvm[:, :GROUP_W].astype(BF)
    ka_ref[0, 0] = jnp.concatenate([kn[:, :128], kr], axis=1)
    kb_ref[0, 0] = jnp.concatenate([kn[:, 128:], kr], axis=1)
    v_ref[0, 0] = kvm[:, GROUP_W:].astype(BF)


def _mla_cache(c_ckv, c_kr_pad, w_kvup_p):
    blk = pl.BlockSpec((1, 1, PAST_LEN, GROUP_W), lambda b, l: (b, l, 0, 0))
    shp = jax.ShapeDtypeStruct((DEC_BATCH, DEPTH, PAST_LEN, GROUP_W), BF)
    return pl.pallas_call(
        _mla_cache_kernel,
        grid=(DEC_BATCH, DEPTH),
        in_specs=[pl.BlockSpec((1, 1, PAST_LEN, MLA_KV_LORA), lambda b, l: (b, l, 0, 0)),
                  pl.BlockSpec((1, 1, PAST_LEN, 128), lambda b, l: (b, l, 0, 0)),
                  pl.BlockSpec((1, MLA_KV_LORA, 2 * GROUP_W), lambda b, l: (l, 0, 0))],
        out_specs=[blk, blk, blk],
        out_shape=[shp, shp, shp],
        compiler_params=_cparams("arbitrary", "arbitrary"),
        name="mla_cache",
    )(c_ckv, c_kr_pad, w_kvup_p)


N_DR = 2 * NA_KR - 1
N_DC = 2 * NA_KC - 1


def _na_bias_kernel(rpb_ref, o_ref, tp_ref):
    base = (pl.program_id(0) * N_HEADS + pl.program_id(1)) * (N_DR * N_DC)
    cq = lax.broadcasted_iota(jnp.int32, (GRID_W, 128), 0)
    lane = lax.broadcasted_iota(jnp.int32, (GRID_W, 128), 1)
    ck = lane & (GRID_W - 1)
    upper = lane >= GRID_W
    cstart = jnp.clip(cq - NA_KC // 2, 0, GRID_W - NA_KC)
    ok = (ck >= cstart) & (ck < cstart + NA_KC)

    j = lax.broadcasted_iota(jnp.int32, (8, 128), 1)
    dclip = jnp.clip(jnp.where(j < GRID_W, j, j - 128), -(NA_KC - 1), NA_KC - 1) + NA_KC - 1

    def toeplitz(dr):
        u = jnp.zeros((8, 128), F32)
        for d in range(N_DC):
            u = jnp.where(dclip == d, rpb_ref[base + dr * N_DC + d], u)
        rows = jnp.broadcast_to(u[0:1], (GRID_W, 128))
        return pltpu.roll(rows, 0, 1, stride=1, stride_axis=0)

    tabs = [toeplitz(dr) for dr in range(N_DR)]
    for dr in range(N_DR - 1):
        val = jnp.where(upper, pltpu.roll(tabs[dr + 1], GRID_W, 1), tabs[dr])
        tp_ref[dr] = jnp.where(ok, val, -jnp.inf)
    for dr0 in range(NA_KR):
        o_ref[0, dr0] = jnp.concatenate([tp_ref[dr0 + 2 * m] for m in range(NA_KR // 2)], axis=1)


def _na_bias(rpb):
    return pl.pallas_call(
        _na_bias_kernel,
        grid=(DEPTH, N_HEADS),
        in_specs=[pl.BlockSpec(memory_space=pltpu.SMEM)],
        out_specs=pl.BlockSpec((1, NA_KR, GRID_W, NA_LOCAL), lambda l, h: (l, 0, h, 0)),
        out_shape=jax.ShapeDtypeStruct((DEPTH, NA_KR, N_HEADS * GRID_W, NA_LOCAL), F32),
        scratch_shapes=[pltpu.VMEM((N_DR - 1, GRID_W, 128), F32)],
        compiler_params=_cparams("arbitrary", "arbitrary"),
        name="na_bias",
    )(rpb.reshape(-1))


def _ctx_kernel(layer, lam_init, n_prev, *refs):
    (x_ref, mod_ref, gmix_ref, win_ref, gq_ref, wqup_ref, gkv_ref, wkv_ref,
     lamv_ref, gsub_ref, dw_ref, cb_ref, lng_ref, lnb_ref) = refs[:N_CTX_IN]
    (mixed_ref, dak_ref, dav_ref, ckv_ref, kr_ref, nak_ref, nav_ref,
     proj_ref, gpad_ref, zsh_ref, wbf_ref) = refs[N_CTX_IN + n_prev:]

    @pl.when(pl.program_id(0) == 0)
    def _():
        _stage_w_in(win_ref, wbf_ref)

    x = x_ref[...]
    sh1 = mod_ref[0:1, 0:D_MODEL]
    sc1 = mod_ref[0:1, D_MODEL:2 * D_MODEL]
    h = ((_rms(x) * _row(gmix_ref, layer)) * (1.0 + sc1) + sh1).astype(BF)
    proj_ref[:, 0:P_SPLIT] = _qk(h, wbf_ref[0:P_SPLIT, :])
    proj_ref[:, P_SPLIT:P_TOT] = _qk(h, wbf_ref[P_SPLIT:P_TOT, :])

    for ref, col in ((dak_ref, P_DAK), (dav_ref, P_DAV), (nak_ref, P_NAK), (nav_ref, P_NAV)):
        slab_t = proj_ref[:, col:col + GROUP_W].T
        for hh in range(N_HEADS):
            ref[0, hh] = slab_t[hh * HEAD_DIM:(hh + 1) * HEAD_DIM, :]

    lane = _lane_ids(GROUP_W)

    lam = _diff_lambda(lamv_ref, lam_init)

    qa = proj_ref[:, P_DAQ:P_DAQ + GROUP_W] * (DA_SCALE * LOG2E)
    ka = proj_ref[:, P_DAK:P_DAK + GROUP_W].astype(BF)
    va = proj_ref[:, P_DAV:P_DAV + GROUP_W].astype(BF)
    qn = proj_ref[:, P_NAQ:P_NAQ + GROUP_W] * (NA_SCALE * LOG2E)
    kn2 = proj_ref[:, P_NAK:P_NAK + GROUP_W].astype(BF)
    vn = proj_ref[:, P_NAV:P_NAV + GROUP_W].astype(BF)
    mla = {}

    def mla_prep():
        qd = _rms(proj_ref[:, P_QD:P_QD + GROUP_W]) * _row(gq_ref, layer)
        ckv = _rms(proj_ref[:, P_KVD:P_KVD + MLA_KV_LORA]) * _row(gkv_ref, layer)
        ckv_ref[0] = ckv
        kr_pad = proj_ref[:, P_KR:P_KR + 128]
        kr_ref[0] = kr_pad.T[0:MLA_ROPE, :]
        mla["q"] = (_dot(qd.astype(BF), wqup_ref[...]) * (MLA_SCALE * LOG2E)).astype(BF)
        kvm = _dot(ckv.astype(BF), wkv_ref[...])
        kn = kvm[:, :GROUP_W].astype(BF)
        krb = kr_pad.astype(BF)
        mla["k"] = (jnp.concatenate([kn[:, :128], krb], axis=1), jnp.concatenate([kn[:, 128:], krb], axis=1))
        mla["v"] = kvm[:, GROUP_W:].astype(BF)

    def conv():
        g = proj_ref[:, P_CONV:P_CONV + GROUP_W] * jax.nn.sigmoid(proj_ref[:, P_CONV + GROUP_W:P_TOT])
        gpad_ref[0:16] = jnp.zeros((16, GROUP_W), F32)
        gpad_ref[16 + SEQ:32 + SEQ] = jnp.zeros((16, GROUP_W), F32)
        gpad_ref[16:16 + SEQ] = g
        return _conv_ln_silu(layer, gpad_ref, zsh_ref, SEQ, dw_ref, cb_ref, lng_ref, lnb_ref)

    def scores(i):
        if i < N_HEADS:
            return (_qk(_mask_heads(qa, lane, DA_QK, 2 * i), ka),
                    _qk(_mask_heads(qa, lane, DA_QK, 2 * i + 1), ka))
        if i < 2 * N_HEADS:
            hh = i - N_HEADS
            return (_qk(mla["q"][:, hh * GROUP_W:(hh + 1) * GROUP_W], mla["k"][hh // 2]),)
        return (_qk(_mask_heads(qn, lane, HEAD_DIM, i - 2 * N_HEADS), kn2),)

    outs = [jnp.zeros((SEQ, GROUP_W), F32)] * 3
    s_next = scores(0)
    o_conv = None
    for i in range(3 * N_HEADS):
        s_cur = s_next
        if i + 1 < 3 * N_HEADS:
            s_next = scores(i + 1)
        if i == 0:
            mla_prep()
        if i == N_HEADS - 1:
            o_conv = conv()
        grp, hh = divmod(i, N_HEADS)
        if grp == 0:
            e1, r1 = _softmax2_parts(s_cur[0])
            e2, r2 = _softmax2_parts(s_cur[1])
            o = _dot((e1 * r1 - e2 * (lam * r2)).astype(BF), va)
        else:
            e, r = _softmax2_parts(s_cur[0])
            o = _dot(e.astype(BF), mla["v"] if grp == 1 else vn) * r
        outs[grp] = jnp.where(_group_of(lane, HEAD_DIM) == hh, o, outs[grp])
    o_da = _head_rms(outs[0]) * _row(gsub_ref, layer) * (1.0 - lam_init)
    o_mla, o_na = outs[1], outs[2]

    mixed_ref[...] = jnp.concatenate([o_da, o_mla, o_na, o_conv], axis=1).astype(BF)


def _ctx_layer(layer, x, prm, prev):
    lam_init = 0.8 - 0.6 * math.exp(-0.3 * layer)
    head_blk = pl.BlockSpec((1, None, N_HEADS, HEAD_DIM, SEQ), lambda b: (b, layer, 0, 0, 0))
    head_shp = jax.ShapeDtypeStruct((BATCH, DEPTH, N_HEADS, HEAD_DIM, SEQ), F32)
    in_specs = [pl.BlockSpec((SEQ, D_MODEL), lambda b: (b, 0)),
                _layer_spec(layer, (16, N_MOD * D_MODEL)),
                _const_spec((DEPTH, D_MODEL)),
                _layer_spec(layer, (IN_COLS, D_MODEL)),
                _const_spec((DEPTH, GROUP_W)),
                _layer_spec(layer, (GROUP_W, N_HEADS * GROUP_W)),
                _const_spec((DEPTH, MLA_KV_LORA)),
                _layer_spec(layer, (MLA_KV_LORA, 2 * GROUP_W)),
                _layer_spec(layer, (4, DA_QK)),
                _const_spec((DEPTH, GROUP_W)),
                _layer_spec(layer, (32, GROUP_W)),
                _const_spec((DEPTH, GROUP_W)),
                _const_spec((DEPTH, GROUP_W)),
                _const_spec((DEPTH, GROUP_W))] + [pl.BlockSpec(memory_space=pl.ANY)] * len(prev)
    out_specs = [pl.BlockSpec((SEQ, D_MODEL), lambda b: (b, 0)),
                 head_blk, head_blk,
                 pl.BlockSpec((1, None, SEQ, MLA_KV_LORA), lambda b: (b, layer, 0, 0)),
                 pl.BlockSpec((1, None, MLA_ROPE, SEQ), lambda b: (b, layer, 0, 0)),
                 head_blk, head_blk]
    out_shape = [jax.ShapeDtypeStruct((BATCH * SEQ, D_MODEL), BF),
                 head_shp, head_shp,
                 jax.ShapeDtypeStruct((BATCH, DEPTH, SEQ, MLA_KV_LORA), F32),
                 jax.ShapeDtypeStruct((BATCH, DEPTH, MLA_ROPE, SEQ), F32),
                 head_shp, head_shp]
    n_in = len(in_specs) - len(prev)
    return pl.pallas_call(
        functools.partial(_ctx_kernel, layer, lam_init, len(prev)),
        grid=(BATCH,),
        in_specs=in_specs,
        out_specs=out_specs,
        out_shape=out_shape,
        input_output_aliases={n_in + k: 1 + k for k in range(len(prev))},
        scratch_shapes=[pltpu.VMEM((SEQ, P_TOT), F32), pltpu.VMEM((SEQ + 32, GROUP_W), F32),
                        pltpu.VMEM((7, SEQ + 8, GROUP_W), F32), pltpu.VMEM((P_TOT, D_MODEL), BF)],
        compiler_params=_cparams("arbitrary"),
        name=f"ctx_layer{layer}",
    )(x, prm["mod"], prm["g_mix"], prm["w_in"], prm["g_q"], prm["w_qup"], prm["g_kv"], prm["w_kvup"],
      prm["lamv"], prm["g_sub"], prm["dw"], prm["cb"], prm["ln_g"], prm["ln_b"], *prev)


def _lat_proj_kernel(layer, x_ref, mod_ref, gmix_ref, win_ref, gq_ref, wqup_ref, gkv_ref, wkv_ref,
                     cos_ref, sin_ref,
                     daq_ref, mq_ref, naq_ref, g_ref,
                     dak_ref, dav_ref, ka_ref, kb_ref, mv_ref, nak_ref, nav_ref, wbf_ref):
    b = pl.program_id(0)

    @pl.when((b == 0) & (pl.program_id(1) == 0))
    def _():
        _stage_w_in(win_ref, wbf_ref)

    x = x_ref[...]
    sh1 = mod_ref[pl.ds(1 + b, 1), 0:D_MODEL]
    sc1 = mod_ref[pl.ds(1 + b, 1), D_MODEL:2 * D_MODEL]
    h = (_rms(x) * _row(gmix_ref, layer)) * (1.0 + sc1) + sh1
    hb = h.astype(BF)
    proj = jnp.concatenate([_qk(hb, wbf_ref[0:P_SPLIT, :]), _qk(hb, wbf_ref[P_SPLIT:P_TOT, :])], axis=1)
    cos = cos_ref[...]
    sin = sin_ref[...]

    def rope2(z):
        return jnp.concatenate([_rope(z[:, :128], cos[:, :128], sin[:, :128]),
                                _rope(z[:, 128:], cos[:, 128:], sin[:, 128:])], axis=1)

    daq_ref[0] = (rope2(proj[:, P_DAQ:P_DAQ + GROUP_W]) * (DA_SCALE * LOG2E)).astype(BF)
    dak_ref[0] = rope2(proj[:, P_DAK:P_DAK + GROUP_W]).astype(BF)
    dav_ref[0] = proj[:, P_DAV:P_DAV + GROUP_W].astype(BF)

    qd = _rms(proj[:, P_QD:P_QD + GROUP_W]) * _row(gq_ref, layer)
    qm = _dot(qd.astype(BF), wqup_ref[...])
    for hh in range(N_HEADS):
        nope = qm[:, hh * GROUP_W:hh * GROUP_W + 128]
        rope = _rope(qm[:, hh * GROUP_W + 128:(hh + 1) * GROUP_W], cos[:, :128], sin[:, :128])
        mq_ref[0, hh] = (jnp.concatenate([nope, rope], axis=1) * (MLA_SCALE * LOG2E)).astype(BF)
    ckv = _rms(proj[:, P_KVD:P_KVD + MLA_KV_LORA]) * _row(gkv_ref, layer)
    kvm = _dot(ckv.astype(BF), wkv_ref[...])
    kn = kvm[:, :GROUP_W].astype(BF)
    krb = _rope(proj[:, P_KR:P_KR + 128], cos[:, :128], sin[:, :128]).astype(BF)
    ka_ref[0] = jnp.concatenate([kn[:, :128], krb], axis=1)
    kb_ref[0] = jnp.concatenate([kn[:, 128:], krb], axis=1)
    mv_ref[0] = kvm[:, GROUP_W:].astype(BF)

    naq_ref[0] = (proj[:, P_NAQ:P_NAQ + GROUP_W] * NA_SCALE).astype(BF)
    nak_ref[0] = proj[:, P_NAK:P_NAK + GROUP_W].astype(BF)
    nav_ref[0] = proj[:, P_NAV:P_NAV + GROUP_W].astype(BF)
    g_ref[0] = proj[:, P_CONV:P_CONV + GROUP_W] * jax.nn.sigmoid(proj[:, P_CONV + GROUP_W:P_TOT])


def _lat_proj(layer, x, prm, cos_t, sin_t):
    n_t = DEC_SEQ // TM_PROJ
    row_spec = pl.BlockSpec((1, TM_PROJ, GROUP_W), lambda b, j: (b, j, 0))
    in_specs = [pl.BlockSpec((TM_PROJ, D_MODEL), lambda b, j: (b * n_t + j, 0)),
                _layer_spec(layer, (16, N_MOD * D_MODEL)),
                _const_spec((DEPTH, D_MODEL)),
                _layer_spec(layer, (IN_COLS, D_MODEL)),
                _const_spec((DEPTH, GROUP_W)),
                _layer_spec(layer, (GROUP_W, N_HEADS * GROUP_W)),
                _const_spec((DEPTH, MLA_KV_LORA)),
                _layer_spec(layer, (MLA_KV_LORA, 2 * GROUP_W)),
                pl.BlockSpec((TM_PROJ, GROUP_W), lambda b, j: (j, 0)),
                pl.BlockSpec((TM_PROJ, GROUP_W), lambda b, j: (j, 0))]
    row_shp = jax.ShapeDtypeStruct((DEC_BATCH, DEC_SEQ, GROUP_W), BF)
    out_specs = [row_spec,
                 pl.BlockSpec((1, N_HEADS, TM_PROJ, GROUP_W), lambda b, j: (b, 0, j, 0)),
                 row_spec, row_spec] + [row_spec] * 7
    out_shape = [row_shp,
                 jax.ShapeDtypeStruct((DEC_BATCH, N_HEADS, DEC_SEQ, GROUP_W), BF),
                 row_shp,
                 jax.ShapeDtypeStruct((DEC_BATCH, DEC_SEQ, GROUP_W), F32)] + [row_shp] * 7
    return pl.pallas_call(
        functools.partial(_lat_proj_kernel, layer),
        grid=(DEC_BATCH, n_t),
        in_specs=in_specs,
        out_specs=out_specs,
        out_shape=out_shape,
        scratch_shapes=[pltpu.VMEM((P_TOT, D_MODEL), BF)],
        compiler_params=_cparams("arbitrary", "arbitrary"),
        name=f"lat_proj{layer}",
    )(x, prm["mod"], prm["g_mix"], prm["w_in"], prm["g_q"], prm["w_qup"], prm["g_kv"], prm["w_kvup"],
      cos_t, sin_t)


def _lat_attn_kernel(layer, lam_init, daq_ref, mq_ref, naq_ref, g_ref,
                     dak_ref, dav_ref, ka_ref, kb_ref, mv_ref, nak_ref, nav_ref,
                     cdak_ref, cdav_ref, cka_ref, ckb_ref, cmv_ref, cnak_ref, cnav_ref,
                     nab_ref, lamv_ref, gsub_ref, dw_ref, cb_ref, lng_ref, lnb_ref,
                     mixed_ref, gpad_ref, zsh_ref, kt_ref, v_ref):
    t = pl.program_id(1)
    lane = _lane_ids(GROUP_W)

    @pl.when(t == 0)
    def _():
        for slot, (lat, cache) in enumerate(((dak_ref, cdak_ref), (ka_ref, cka_ref), (kb_ref, ckb_ref))):
            kt_ref[slot, :, 0:DEC_SEQ] = lat[0].T
            kt_ref[slot, :, DEC_SEQ:KEYS] = cache[0].T
        for slot, (lat, cache) in enumerate(((dav_ref, cdav_ref), (mv_ref, cmv_ref))):
            v_ref[slot, 0:DEC_SEQ, :] = lat[0]
            v_ref[slot, DEC_SEQ:KEYS, :] = cache[0]

    lam = _diff_lambda(lamv_ref, lam_init)
    qa = daq_ref[0].astype(F32)
    va = v_ref[0]
    ka_t = kt_ref[0]
    vm = v_ref[1]

    kc = cnak_ref[0]
    vc = cnav_ref[0]
    n_items = 2 * N_HEADS + ROWS_PER_TILE

    def na_window(j):
        r = t * ROWS_PER_TILE + j
        start = jnp.clip(r - NA_KR // 2, 0, N_ROWS - NA_KR)
        return start - r + NA_KR - 1, pl.multiple_of(start * GRID_W, GRID_W)

    def scores(i):
        if i < N_HEADS:
            return (_dot(_mask_heads(qa, lane, DA_QK, 2 * i), ka_t),
                    _dot(_mask_heads(qa, lane, DA_QK, 2 * i + 1), ka_t))
        if i < 2 * N_HEADS:
            hh = i - N_HEADS
            return (_dot(mq_ref[0, hh], kt_ref[1 + hh // 2]),)
        j = i - 2 * N_HEADS
        dr0, koff = na_window(j)
        qrow = naq_ref[0, j * GRID_W:(j + 1) * GRID_W, :].astype(F32)
        q4 = jnp.concatenate([_mask_heads(qrow, lane, HEAD_DIM, hh) for hh in range(N_HEADS)], axis=0)
        return (_qk(q4, nak_ref[0, pl.ds(koff, NA_LOCAL), :]) + nab_ref[dr0], _qk(q4, kc))

    o_da = jnp.zeros((TQ, GROUP_W), F32)
    o_mla = jnp.zeros((TQ, GROUP_W), F32)
    na_rows = []
    s_next = scores(0)

    base = pl.multiple_of(t * TQ, TQ)
    gpad_ref[16:16 + TQ] = g_ref[0, pl.ds(base, TQ), :]
    lo = g_ref[0, pl.ds(pl.multiple_of(jnp.maximum(base - 16, 0), 16), 16), :]
    hi = g_ref[0, pl.ds(pl.multiple_of(jnp.minimum(base + TQ, DEC_SEQ - 16), 16), 16), :]
    gpad_ref[0:16] = jnp.where(t > 0, lo, 0.0)
    gpad_ref[16 + TQ:32 + TQ] = jnp.where(t < N_QT - 1, hi, 0.0)
    o_conv = _conv_ln_silu(layer, gpad_ref, zsh_ref, TQ, dw_ref, cb_ref, lng_ref, lnb_ref)

    for i in range(n_items):
        s_cur = s_next
        if i + 1 < n_items:
            s_next = scores(i + 1)
        if i < N_HEADS:
            e1, r1 = _softmax2_parts(s_cur[0])
            e2, r2 = _softmax2_parts(s_cur[1])
            p = (e1 - e2 * (lam * r2 / r1)).astype(BF)
            o_da = jnp.where(_group_of(lane, HEAD_DIM) == i, _dot(p, va) * r1, o_da)
        elif i < 2 * N_HEADS:
            e, r = _softmax2_parts(s_cur[0])
            o_mla = jnp.where(_group_of(lane, HEAD_DIM) == i - N_HEADS, _dot(e.astype(BF), vm) * r, o_mla)
        else:
            _, koff = na_window(i - 2 * N_HEADS)
            s_loc, s_ctx = s_cur
            m = jnp.maximum(jnp.max(s_loc, axis=-1, keepdims=True), jnp.max(s_ctx, axis=-1, keepdims=True))
            e_loc = jnp.exp(s_loc - m)
            e_ctx = jnp.exp(s_ctx - m)
            den = jnp.sum(e_loc, axis=-1, keepdims=True) + jnp.sum(e_ctx, axis=-1, keepdims=True)
            o4 = (_dot(e_loc.astype(BF), nav_ref[0, pl.ds(koff, NA_LOCAL), :])
                  + _dot(e_ctx.astype(BF), vc)) * (1.0 / den)
            o_row = o4[0:GRID_W]
            for hh in range(1, N_HEADS):
                o_row = jnp.where(_group_of(lane, HEAD_DIM) == hh, o4[hh * GRID_W:(hh + 1) * GRID_W], o_row)
            na_rows.append(o_row)
    o_da = _head_rms(o_da) * _row(gsub_ref, layer) * (1.0 - lam_init)
    o_na = jnp.concatenate(na_rows, axis=0)

    mixed_ref[...] = jnp.concatenate([o_da, o_mla, o_na, o_conv], axis=1).astype(BF)


def _lat_attn(layer, proj_outs, caches, nab, prm):
    lam_init = 0.8 - 0.6 * math.exp(-0.3 * layer)
    q_spec = pl.BlockSpec((1, TQ, GROUP_W), lambda b, t: (b, t, 0))
    full_k = pl.BlockSpec((1, DEC_SEQ, GROUP_W), lambda b, t: (b, 0, 0))
    cache_spec = pl.BlockSpec((1, None, PAST_LEN, GROUP_W), lambda b, t: (b, layer, 0, 0))
    in_specs = [q_spec,
                pl.BlockSpec((1, N_HEADS, TQ, GROUP_W), lambda b, t: (b, 0, t, 0)),
                q_spec, full_k] + [full_k] * 7 + [cache_spec] * 7 + [
                _layer_spec(layer, (NA_KR, N_HEADS * GRID_W, NA_LOCAL)),
                _layer_spec(layer, (4, DA_QK)),
                _const_spec((DEPTH, GROUP_W)),
                _layer_spec(layer, (32, GROUP_W)),
                _const_spec((DEPTH, GROUP_W)),
                _const_spec((DEPTH, GROUP_W)),
                _const_spec((DEPTH, GROUP_W))]
    return pl.pallas_call(
        functools.partial(_lat_attn_kernel, layer, lam_init),
        grid=(DEC_BATCH, N_QT),
        in_specs=in_specs,
        out_specs=pl.BlockSpec((TQ, D_MODEL), lambda b, t: (b * N_QT + t, 0)),
        out_shape=jax.ShapeDtypeStruct((DEC_BATCH * DEC_SEQ, D_MODEL), BF),
        scratch_shapes=[pltpu.VMEM((TQ + 32, GROUP_W), F32), pltpu.VMEM((7, TQ + 8, GROUP_W), F32),
                        pltpu.VMEM((3, GROUP_W, KEYS), BF), pltpu.VMEM((2, KEYS, GROUP_W), BF)],
        compiler_params=_cparams("arbitrary", "arbitrary"),
        name=f"lat_attn{layer}",
    )(*proj_outs, *caches, nab, prm["lamv"], prm["g_sub"], prm["dw"], prm["cb"], prm["ln_g"], prm["ln_b"])


def _out_ffn_kernel(layer, tiles_per_mod, mod_base, x_ref, mx_ref, mod_ref, wout_ref, gff_ref,
                    w1_ref, w2_ref, gfin_ref, o_ref):
    row = mod_base + pl.program_id(0) // tiles_per_mod
    g1 = mod_ref[pl.ds(row, 1), 2 * D_MODEL:3 * D_MODEL]
    sh2 = mod_ref[pl.ds(row, 1), 3 * D_MODEL:4 * D_MODEL]
    sc2 = mod_ref[pl.ds(row, 1), 4 * D_MODEL:5 * D_MODEL]
    g2 = mod_ref[pl.ds(row, 1), 5 * D_MODEL:6 * D_MODEL]
    x1 = x_ref[...] + g1 * _dot(mx_ref[...], wout_ref[...])
    h2 = ((_rms(x1) * _row(gff_ref, layer)) * (1.0 + sc2) + sh2).astype(BF)
    acc = jnp.zeros((TM_FFN, D_MODEL), F32)
    for c in range(D_FF // D_MODEL):
        a = jnp.maximum(_dot(h2, w1_ref[:, c * D_MODEL:(c + 1) * D_MODEL]), 0.0)
        acc = acc + _dot((a * a).astype(BF), w2_ref[c * D_MODEL:(c + 1) * D_MODEL, :])
    x2 = x1 + g2 * acc
    o_ref[...] = _rms(x2) * gfin_ref[...] if layer == DEPTH - 1 else x2


def _out_ffn(name, layer, tiles_per_mod, mod_base, x, mixed, prm, g_final):
    n = x.shape[0]
    tile = pl.BlockSpec((TM_FFN, D_MODEL), lambda i: (i, 0))
    return pl.pallas_call(
        functools.partial(_out_ffn_kernel, layer, tiles_per_mod, mod_base),
        grid=(n // TM_FFN,),
        in_specs=[tile, tile,
                  _layer_spec(layer, (16, N_MOD * D_MODEL)),
                  _layer_spec(layer, (D_MODEL, D_MODEL)),
                  _const_spec((DEPTH, D_MODEL)),
                  _layer_spec(layer, (D_MODEL, D_FF)),
                  _layer_spec(layer, (D_FF, D_MODEL)),
                  _const_spec((1, D_MODEL))],
        out_specs=tile,
        out_shape=jax.ShapeDtypeStruct((n, D_MODEL), F32),
        compiler_params=_cparams("arbitrary"),
        name=name,
    )(x, mixed, prm["mod"], prm["w_out"], prm["g_ff"], prm["w_ff1"], prm["w_ff2"], g_final)


def _rope_tables():
    t = np.arange(DEC_SEQ)
    rows = (t // GRID_W).astype(np.float32)
    cols = (t % GRID_W).astype(np.float32)
    c = np.arange(GROUP_W) % 32
    freqs = np.float32(ROPE_BASE) ** (-np.arange(8, dtype=np.float32) * np.float32(2.0) / np.float32(16))
    pos = np.where((c < 16)[None, :], rows[:, None], cols[:, None]).astype(np.float32)
    ang = (pos * freqs[(c % 16) % 8][None, :]).astype(np.float32)
    first = ((c % 16) < 8)[None, :]
    cos = np.cos(ang).astype(np.float32)
    sin = np.sin(ang).astype(np.float32)
    return jnp.asarray(cos), jnp.asarray(np.where(first, -sin, sin))


def _qup_gather_index():
    idx = np.full((N_HEADS * GROUP_W,), -1, np.int64)
    for h in range(N_HEADS):
        src = h * (MLA_NOPE + MLA_ROPE)
        dst = h * GROUP_W + (h % 2) * MLA_NOPE
        idx[dst:dst + MLA_NOPE] = np.arange(src, src + MLA_NOPE)
        idx[h * GROUP_W + 128:h * GROUP_W + 128 + MLA_ROPE] = np.arange(src + MLA_NOPE, src + MLA_NOPE + MLA_ROPE)
    return idx


def _kvup_perm():
    k = [h * 128 + d for h in range(N_HEADS) for d in range(MLA_NOPE)]
    v = [h * 128 + MLA_NOPE + d for h in range(N_HEADS) for d in range(HEAD_DIM)]
    return np.asarray(k + v)


def _heads_to_lanes(c):
    b, l, h, s, d = c.shape
    return c.transpose(0, 1, 3, 2, 4).reshape(b, l, s, h * d).astype(BF)


def kernel(x_prompt, x_sample, c, cache_da_k, cache_da_v, cache_mla_ckv, cache_mla_krope, cache_na_k, cache_na_v, c_ctx, w_mod, b_mod, g_norm_mix, g_norm_ff, w_in, da_lambda_q1, da_lambda_k1, da_lambda_q2, da_lambda_k2, g_da_subln, g_mla_q, w_mla_qup, g_mla_kv, w_mla_kvup, na_rpb, conv_dw, conv_b, conv_ln_g, conv_ln_b, w_out, w_ff1, w_ff2, g_final):
    qidx = _qup_gather_index()
    w_qup_e = jnp.where(jnp.asarray(qidx >= 0)[None, None, :],
                        jnp.take(w_mla_qup, jnp.asarray(np.maximum(qidx, 0)), axis=-1), 0.0).astype(BF)
    w_kvup_p = jnp.take(w_mla_kvup, jnp.asarray(_kvup_perm()), axis=-1).astype(BF)
    prm = dict(
        g_mix=g_norm_mix, w_in=jnp.swapaxes(w_in, 1, 2), g_q=g_mla_q, w_qup=w_qup_e, g_kv=g_mla_kv, w_kvup=w_kvup_p,
        lamv=jnp.stack([da_lambda_q1, da_lambda_k1, da_lambda_q2, da_lambda_k2], axis=1),
        g_sub=jnp.tile(g_da_subln, (1, N_HEADS)),
        dw=jnp.concatenate([conv_dw, jnp.zeros((DEPTH, 1, GROUP_W), F32)], axis=1),
        cb=conv_b, ln_g=conv_ln_g, ln_b=conv_ln_b,
        w_out=w_out.astype(BF), g_ff=g_norm_ff, w_ff1=w_ff1.astype(BF), w_ff2=w_ff2.astype(BF))
    cv = jnp.concatenate([c_ctx[None, :], c, jnp.zeros((16 - 1 - DEC_BATCH, D_MODEL), F32)], axis=0)
    cos_t, sin_t = _rope_tables()
    g_final2 = g_final.reshape(1, D_MODEL)

    prm["mod"] = _modulation(cv, w_mod, b_mod)
    c_kr_pad = jnp.pad(cache_mla_krope, ((0, 0), (0, 0), (0, 0), (0, KR_PAD)))
    cka, ckb, cmv = _mla_cache(cache_mla_ckv, c_kr_pad, w_kvup_p)
    caches = (_heads_to_lanes(cache_da_k), _heads_to_lanes(cache_da_v), cka, ckb, cmv,
              _heads_to_lanes(cache_na_k), _heads_to_lanes(cache_na_v))
    nab = _na_bias(na_rpb)

    xp = x_prompt.reshape(BATCH * SEQ, D_MODEL)
    xs = x_sample.reshape(DEC_BATCH * DEC_SEQ, D_MODEL)
    new_ctx = ()
    for l in range(DEPTH):
        mixed_p, *new_ctx = _ctx_layer(l, xp, prm, tuple(new_ctx))
        xp = _out_ffn(f"ctx_ffn{l}", l, BATCH * SEQ // TM_FFN, 0, xp, mixed_p, prm, g_final2)
        proj_outs = _lat_proj(l, xs, prm, cos_t, sin_t)
        mixed_s = _lat_attn(l, proj_outs, caches, nab, prm)
        xs = _out_ffn(f"lat_ffn{l}", l, DEC_SEQ // TM_FFN, 1, xs, mixed_s, prm, g_final2)

    y_prompt = xp.reshape(BATCH, SEQ, D_MODEL)
    y_sample = xs.reshape(DEC_BATCH, DEC_SEQ, D_MODEL)
    new_ctx = [a if k == 2 else jnp.swapaxes(a, -1, -2) for k, a in enumerate(new_ctx)]
    return (y_prompt, y_sample) + tuple(new_ctx)
```

```python
import functools
import math

import numpy as np
import jax
import jax.numpy as jnp
from jax import lax
from jax.experimental import pallas as pl
from jax.experimental.pallas import tpu as pltpu

F32 = jnp.float32
BF = jnp.bfloat16

D_MODEL = 1024
BATCH = 16
SEQ = 256
DEPTH = 2
DEC_BATCH = 2
DEC_SEQ = 2048
PAST_LEN = 256
GRID_W = 64
GROUP_W = 256
HEAD_DIM = 64
N_HEADS = 4
DA_QK = 32
MLA_NOPE = 64
MLA_ROPE = 32
MLA_KV_LORA = 128
NA_KR = 8
NA_KC = 16
CONV_W = 31
D_FF = 4096
ROPE_BASE = 10000.0
EPS = 1e-6
N_MOD = 6
IN_COLS = 2464

P_DAQ, P_DAK, P_DAV, P_QD, P_KVD, P_KR = 0, 256, 512, 768, 1024, 1152
P_NAQ, P_NAK, P_NAV, P_CONV, P_TOT = 1280, 1536, 1792, 2048, 2560
P_SPLIT = P_NAQ
N_CTX_IN = 14
KR_ORIG_END = 1184
KR_PAD = 128 - MLA_ROPE

DA_SCALE = DA_QK ** -0.5
MLA_SCALE = (MLA_NOPE + MLA_ROPE) ** -0.5
NA_SCALE = HEAD_DIM ** -0.5
LOG2E = math.log2(math.e)

TQ = 256
N_QT = DEC_SEQ // TQ
KEYS = DEC_SEQ + PAST_LEN
ROWS_PER_TILE = TQ // GRID_W
N_ROWS = DEC_SEQ // GRID_W
NA_LOCAL = NA_KR * GRID_W
TM_FFN = 512
TM_PROJ = 512
VMEM_LIMIT = 58 * 1024 * 1024

NT_DIMS = (((1,), (1,)), ((), ()))


def _cparams(*sem):
    return pltpu.CompilerParams(dimension_semantics=sem, vmem_limit_bytes=VMEM_LIMIT)


def _const_spec(shape):
    nd = len(shape)
    return pl.BlockSpec(shape, lambda *_: (0,) * nd, pipeline_mode=pl.Buffered(1))


def _layer_spec(layer, shape):
    nd = len(shape)
    return pl.BlockSpec((None,) + tuple(shape), lambda *_: (layer,) + (0,) * nd, pipeline_mode=pl.Buffered(1))


def _rms(x):
    return x * lax.rsqrt(jnp.mean(x * x, axis=-1, keepdims=True) + EPS)


def _row(ref, layer):
    return ref[layer:layer + 1, :]


def _dot(a, b):
    return jnp.dot(a, b, preferred_element_type=F32)


def _qk(q, k):
    return lax.dot_general(q, k, NT_DIMS, preferred_element_type=F32)


def _softmax2_parts(s):
    m = jnp.max(s, axis=-1, keepdims=True)
    e = jnp.exp2(s - m)
    return e, 1.0 / jnp.sum(e, axis=-1, keepdims=True)


def _lane_ids(width):
    return lax.broadcasted_iota(jnp.int32, (1, width), 1)


def _diff_lambda(lamv_ref, lam_init):
    v = lamv_ref[...]
    a = jnp.exp(jnp.sum(v[0:1] * v[1:2], axis=-1, keepdims=True))
    b = jnp.exp(jnp.sum(v[2:3] * v[3:4], axis=-1, keepdims=True))
    return a - b + lam_init


def _group_of(lane, group):
    return lane >> (group.bit_length() - 1)


def _head_rms(x):
    r = _group_of(lax.broadcasted_iota(jnp.int32, (GROUP_W, GROUP_W), 0), HEAD_DIM)
    c = _group_of(lax.broadcasted_iota(jnp.int32, (GROUP_W, GROUP_W), 1), HEAD_DIM)
    ones_bd = jnp.where(r == c, 1.0, 0.0).astype(BF)
    sq = x * x
    hi = sq.astype(BF)
    lo = (sq - hi.astype(F32)).astype(BF)
    ss = _dot(hi, ones_bd) + _dot(lo, ones_bd)
    return x * lax.rsqrt(ss * (1.0 / HEAD_DIM) + EPS)


def _rope(z, cos, sin):
    lane = _lane_ids(z.shape[1])
    swapped = jnp.where((lane & 15) < 8, pltpu.roll(z, 120, 1), pltpu.roll(z, 8, 1))
    return z * cos + swapped * sin


def _stage_w_in(wint_ref, wbt_ref):
    wbt_ref[0:KR_ORIG_END, :] = wint_ref[0:KR_ORIG_END, :].astype(BF)
    wbt_ref[KR_ORIG_END:P_SPLIT, :] = jnp.zeros((P_SPLIT - KR_ORIG_END, D_MODEL), BF)
    wbt_ref[P_SPLIT:P_TOT, :] = wint_ref[KR_ORIG_END:IN_COLS, :].astype(BF)


def _mask_heads(qf, lane, group, idx):
    return jnp.where(_group_of(lane, group) == idx, qf, 0.0).astype(BF)


def _conv_ln_silu(layer, gpad_ref, zsh_ref, n, dw_ref, cb_ref, lng_ref, lnb_ref):
    y = jnp.zeros((n, GROUP_W), F32) + _row(cb_ref, layer)
    for b in range(8):
        z = None
        for a in range(4):
            t = 8 * a + b - 1
            if 0 <= t < CONV_W:
                term = gpad_ref[pl.ds(8 * a, n + 8), :] * dw_ref[t:t + 1, :]
                z = term if z is None else z + term
        if b == 0:
            y = y + z[0:n]
        else:
            zsh_ref[b - 1] = z
            y = y + zsh_ref[b - 1, pl.ds(b, n), :]
    mu = jnp.mean(y, axis=-1, keepdims=True)
    yc = y - mu
    var = jnp.mean(yc * yc, axis=-1, keepdims=True)
    z = yc * lax.rsqrt(var + EPS) * _row(lng_ref, layer) + _row(lnb_ref, layer)
    return z * jax.nn.sigmoid(z)


def _mod_kernel(cv_ref, w_ref, b_ref, o_ref):
    c = cv_ref[...]
    a = (c * jax.nn.sigmoid(c)).astype(BF)
    o_ref[0] = _dot(a, w_ref[0].astype(BF)) + b_ref[0]


MOD_BLOCKS = N_HEADS


def _prep_kernel(cv_ref, wmod_ref, bmod_ref, rpb_ref, ckv_ref, kr_ref, wkv_ref,
                 mod_ref, nab_ref, ka_ref, kb_ref, v_ref, tp_ref):
    _mod_kernel(cv_ref, wmod_ref, bmod_ref, mod_ref)
    _na_bias_kernel(rpb_ref, nab_ref, tp_ref)

    @pl.when(pl.program_id(1) < DEC_BATCH)
    def _():
        _mla_cache_kernel(ckv_ref, kr_ref, wkv_ref, ka_ref, kb_ref, v_ref)


def _prep(cv, w_mod, b_mod, rpb, c_ckv, c_kr_pad, w_kvup_p):
    wblk = N_MOD * D_MODEL // MOD_BLOCKS
    cb = lambda l, j: (jnp.minimum(j, DEC_BATCH - 1), l, 0, 0)
    cache_blk = pl.BlockSpec((1, 1, PAST_LEN, GROUP_W), cb)
    cache_shp = jax.ShapeDtypeStruct((DEC_BATCH, DEPTH, PAST_LEN, GROUP_W), BF)
    return pl.pallas_call(
        _prep_kernel,
        grid=(DEPTH, MOD_BLOCKS),
        in_specs=[pl.BlockSpec((16, D_MODEL), lambda l, j: (0, 0)),
                  pl.BlockSpec((1, D_MODEL, wblk), lambda l, j: (l, 0, j)),
                  pl.BlockSpec((1, 1, wblk), lambda l, j: (l, 0, j)),
                  pl.BlockSpec(memory_space=pltpu.SMEM),
                  pl.BlockSpec((1, 1, PAST_LEN, MLA_KV_LORA), cb),
                  pl.BlockSpec((1, 1, PAST_LEN, 128), cb),
                  pl.BlockSpec((1, MLA_KV_LORA, 2 * GROUP_W), lambda l, j: (l, 0, 0))],
        out_specs=[pl.BlockSpec((1, 16, wblk), lambda l, j: (l, 0, j)),
                   pl.BlockSpec((1, NA_KR, GRID_W, NA_LOCAL), lambda l, j: (l, 0, j, 0)),
                   cache_blk, cache_blk, cache_blk],
        out_shape=[jax.ShapeDtypeStruct((DEPTH, 16, N_MOD * D_MODEL), F32),
                   jax.ShapeDtypeStruct((DEPTH, NA_KR, N_HEADS * GRID_W, NA_LOCAL), F32),
                   cache_shp, cache_shp, cache_shp],
        scratch_shapes=[pltpu.VMEM((N_DR - 1, GRID_W, 128), F32)],
        compiler_params=_cparams("arbitrary", "arbitrary"),
        name="prep",
    )(cv, w_mod, b_mod.reshape(DEPTH, 1, N_MOD * D_MODEL), rpb.reshape(-1), c_ckv, c_kr_pad, w_kvup_p)


def _mla_cache_kernel(ckv_ref, kr_ref, w_ref, ka_ref, kb_ref, v_ref):
    kvm = _dot(ckv_ref[0, 0].astype(BF), w_ref[0])
    kr = kr_ref[0, 0].astype(BF)
    kn = kvm[:, :GROUP_W].astype(BF)
    ka_ref[0, 0] = jnp.concatenate([kn[:, :128], kr], axis=1)
    kb_ref[0, 0] = jnp.concatenate([kn[:, 128:], kr], axis=1)
    v_ref[0, 0] = kvm[:, GROUP_W:].astype(BF)


N_DR = 2 * NA_KR - 1
N_DC = 2 * NA_KC - 1


def _na_bias_kernel(rpb_ref, o_ref, tp_ref):
    base = (pl.program_id(0) * N_HEADS + pl.program_id(1)) * (N_DR * N_DC)
    cq = lax.broadcasted_iota(jnp.int32, (GRID_W, 128), 0)
    lane = lax.broadcasted_iota(jnp.int32, (GRID_W, 128), 1)
    ck = lane & (GRID_W - 1)
    upper = lane >= GRID_W
    cstart = jnp.clip(cq - NA_KC // 2, 0, GRID_W - NA_KC)
    ok = (ck >= cstart) & (ck < cstart + NA_KC)

    j = lax.broadcasted_iota(jnp.int32, (8, 128), 1)
    dclip = jnp.clip(jnp.where(j < GRID_W, j, j - 128), -(NA_KC - 1), NA_KC - 1) + NA_KC - 1

    def toeplitz(dr):
        u = jnp.zeros((8, 128), F32)
        for d in range(N_DC):
            u = jnp.where(dclip == d, rpb_ref[base + dr * N_DC + d], u)
        rows = jnp.broadcast_to(u[0:1], (GRID_W, 128))
        return pltpu.roll(rows, 0, 1, stride=1, stride_axis=0)

    tabs = [toeplitz(dr) for dr in range(N_DR)]
    for dr in range(N_DR - 1):
        val = jnp.where(upper, pltpu.roll(tabs[dr + 1], GRID_W, 1), tabs[dr])
        tp_ref[dr] = jnp.where(ok, val, -jnp.inf)
    for dr0 in range(NA_KR):
        o_ref[0, dr0] = jnp.concatenate([tp_ref[dr0 + 2 * m] for m in range(NA_KR // 2)], axis=1)


def _ctx_kernel(layer, lam_init, n_prev, *refs):
    (x_ref, mod_ref, gmix_ref, win_ref, gq_ref, wqup_ref, gkv_ref, wkv_ref,
     lamv_ref, gsub_ref, dw_ref, cb_ref, lng_ref, lnb_ref) = refs[:N_CTX_IN]
    (mixed_ref, dak_ref, dav_ref, ckv_ref, kr_ref, nak_ref, nav_ref,
     proj_ref, gpad_ref, zsh_ref, wbf_ref) = refs[N_CTX_IN + n_prev:]

    @pl.when(pl.program_id(0) == 0)
    def _():
        _stage_w_in(win_ref, wbf_ref)

    x = x_ref[...]
    sh1 = mod_ref[0:1, 0:D_MODEL]
    sc1 = mod_ref[0:1, D_MODEL:2 * D_MODEL]
    h = ((_rms(x) * _row(gmix_ref, layer)) * (1.0 + sc1) + sh1).astype(BF)
    proj_ref[:, 0:P_SPLIT] = _qk(h, wbf_ref[0:P_SPLIT, :])
    proj_ref[:, P_SPLIT:P_TOT] = _qk(h, wbf_ref[P_SPLIT:P_TOT, :])

    for ref, col in ((dak_ref, P_DAK), (dav_ref, P_DAV), (nak_ref, P_NAK), (nav_ref, P_NAV)):
        slab_t = proj_ref[:, col:col + GROUP_W].T
        for hh in range(N_HEADS):
            ref[0, hh] = slab_t[hh * HEAD_DIM:(hh + 1) * HEAD_DIM, :]

    lane = _lane_ids(GROUP_W)

    lam = _diff_lambda(lamv_ref, lam_init)

    qa = proj_ref[:, P_DAQ:P_DAQ + GROUP_W] * (DA_SCALE * LOG2E)
    ka = proj_ref[:, P_DAK:P_DAK + GROUP_W].astype(BF)
    va = proj_ref[:, P_DAV:P_DAV + GROUP_W].astype(BF)
    qn = proj_ref[:, P_NAQ:P_NAQ + GROUP_W] * (NA_SCALE * LOG2E)
    kn2 = proj_ref[:, P_NAK:P_NAK + GROUP_W].astype(BF)
    vn = proj_ref[:, P_NAV:P_NAV + GROUP_W].astype(BF)
    mla = {}

    def mla_prep():
        qd = _rms(proj_ref[:, P_QD:P_QD + GROUP_W]) * _row(gq_ref, layer)
        ckv = _rms(proj_ref[:, P_KVD:P_KVD + MLA_KV_LORA]) * _row(gkv_ref, layer)
        ckv_ref[0] = ckv
        kr_pad = proj_ref[:, P_KR:P_KR + 128]
        kr_ref[0] = kr_pad.T[0:MLA_ROPE, :]
        mla["q"] = (_dot(qd.astype(BF), wqup_ref[...]) * (MLA_SCALE * LOG2E)).astype(BF)
        kvm = _dot(ckv.astype(BF), wkv_ref[...])
        kn = kvm[:, :GROUP_W].astype(BF)
        krb = kr_pad.astype(BF)
        mla["k"] = (jnp.concatenate([kn[:, :128], krb], axis=1), jnp.concatenate([kn[:, 128:], krb], axis=1))
        mla["v"] = kvm[:, GROUP_W:].astype(BF)

    def conv():
        g = proj_ref[:, P_CONV:P_CONV + GROUP_W] * jax.nn.sigmoid(proj_ref[:, P_CONV + GROUP_W:P_TOT])
        gpad_ref[0:16] = jnp.zeros((16, GROUP_W), F32)
        gpad_ref[16 + SEQ:32 + SEQ] = jnp.zeros((16, GROUP_W), F32)
        gpad_ref[16:16 + SEQ] = g
        return _conv_ln_silu(layer, gpad_ref, zsh_ref, SEQ, dw_ref, cb_ref, lng_ref, lnb_ref)

    def scores(i):
        if i < N_HEADS:
            return (_qk(_mask_heads(qa, lane, DA_QK, 2 * i), ka),
                    _qk(_mask_heads(qa, lane, DA_QK, 2 * i + 1), ka))
        if i < 2 * N_HEADS:
            hh = i - N_HEADS
            return (_qk(mla["q"][:, hh * GROUP_W:(hh + 1) * GROUP_W], mla["k"][hh // 2]),)
        return (_qk(_mask_heads(qn, lane, HEAD_DIM, i - 2 * N_HEADS), kn2),)

    outs = [jnp.zeros((SEQ, GROUP_W), F32)] * 3
    s_next = scores(0)
    o_conv = None
    for i in range(3 * N_HEADS):
        s_cur = s_next
        if i + 1 < 3 * N_HEADS:
            s_next = scores(i + 1)
        if i == 0:
            mla_prep()
        if i == N_HEADS - 1:
            o_conv = conv()
        grp, hh = divmod(i, N_HEADS)
        if grp == 0:
            e1, r1 = _softmax2_parts(s_cur[0])
            e2, r2 = _softmax2_parts(s_cur[1])
            o = _dot((e1 * r1 - e2 * (lam * r2)).astype(BF), va)
        else:
            e, r = _softmax2_parts(s_cur[0])
            o = _dot(e.astype(BF), mla["v"] if grp == 1 else vn) * r
        outs[grp] = jnp.where(_group_of(lane, HEAD_DIM) == hh, o, outs[grp])
    o_da = _head_rms(outs[0]) * _row(gsub_ref, layer) * (1.0 - lam_init)
    o_mla, o_na = outs[1], outs[2]

    mixed_ref[...] = jnp.concatenate([o_da, o_mla, o_na, o_conv], axis=1).astype(BF)


def _ctx_layer(layer, x, prm, prev):
    lam_init = 0.8 - 0.6 * math.exp(-0.3 * layer)
    head_blk = pl.BlockSpec((1, None, N_HEADS, HEAD_DIM, SEQ), lambda b: (b, layer, 0, 0, 0))
    head_shp = jax.ShapeDtypeStruct((BATCH, DEPTH, N_HEADS, HEAD_DIM, SEQ), F32)
    in_specs = [pl.BlockSpec((SEQ, D_MODEL), lambda b: (b, 0)),
                _layer_spec(layer, (16, N_MOD * D_MODEL)),
                _const_spec((DEPTH, D_MODEL)),
                _layer_spec(layer, (IN_COLS, D_MODEL)),
                _const_spec((DEPTH, GROUP_W)),
                _layer_spec(layer, (GROUP_W, N_HEADS * GROUP_W)),
                _const_spec((DEPTH, MLA_KV_LORA)),
                _layer_spec(layer, (MLA_KV_LORA, 2 * GROUP_W)),
                _layer_spec(layer, (4, DA_QK)),
                _const_spec((DEPTH, GROUP_W)),
                _layer_spec(layer, (32, GROUP_W)),
                _const_spec((DEPTH, GROUP_W)),
                _const_spec((DEPTH, GROUP_W)),
                _const_spec((DEPTH, GROUP_W))] + [pl.BlockSpec(memory_space=pl.ANY)] * len(prev)
    out_specs = [pl.BlockSpec((SEQ, D_MODEL), lambda b: (b, 0)),
                 head_blk, head_blk,
                 pl.BlockSpec((1, None, SEQ, MLA_KV_LORA), lambda b: (b, layer, 0, 0)),
                 pl.BlockSpec((1, None, MLA_ROPE, SEQ), lambda b: (b, layer, 0, 0)),
                 head_blk, head_blk]
    out_shape = [jax.ShapeDtypeStruct((BATCH * SEQ, D_MODEL), BF),
                 head_shp, head_shp,
                 jax.ShapeDtypeStruct((BATCH, DEPTH, SEQ, MLA_KV_LORA), F32),
                 jax.ShapeDtypeStruct((BATCH, DEPTH, MLA_ROPE, SEQ), F32),
                 head_shp, head_shp]
    n_in = len(in_specs) - len(prev)
    return pl.pallas_call(
        functools.partial(_ctx_kernel, layer, lam_init, len(prev)),
        grid=(BATCH,),
        in_specs=in_specs,
        out_specs=out_specs,
        out_shape=out_shape,
        input_output_aliases={n_in + k: 1 + k for k in range(len(prev))},
        scratch_shapes=[pltpu.VMEM((SEQ, P_TOT), F32), pltpu.VMEM((SEQ + 32, GROUP_W), F32),
                        pltpu.VMEM((7, SEQ + 8, GROUP_W), F32), pltpu.VMEM((P_TOT, D_MODEL), BF)],
        compiler_params=_cparams("arbitrary"),
        name=f"ctx_layer{layer}",
    )(x, prm["mod"], prm["g_mix"], prm["w_in"], prm["g_q"], prm["w_qup"], prm["g_kv"], prm["w_kvup"],
      prm["lamv"], prm["g_sub"], prm["dw"], prm["cb"], prm["ln_g"], prm["ln_b"], *prev)


def _lat_proj_kernel(layer, x_ref, mod_ref, gmix_ref, win_ref, gq_ref, wqup_ref, gkv_ref, wkv_ref,
                     cos_ref, sin_ref,
                     daq_ref, mq_ref, naq_ref, g_ref,
                     dak_ref, dav_ref, ka_ref, kb_ref, mv_ref, nak_ref, nav_ref, wbf_ref):
    b = pl.program_id(0)

    @pl.when((b == 0) & (pl.program_id(1) == 0))
    def _():
        _stage_w_in(win_ref, wbf_ref)

    x = x_ref[...]
    sh1 = mod_ref[pl.ds(1 + b, 1), 0:D_MODEL]
    sc1 = mod_ref[pl.ds(1 + b, 1), D_MODEL:2 * D_MODEL]
    h = (_rms(x) * _row(gmix_ref, layer)) * (1.0 + sc1) + sh1
    hb = h.astype(BF)
    proj = jnp.concatenate([_qk(hb, wbf_ref[0:P_SPLIT, :]), _qk(hb, wbf_ref[P_SPLIT:P_TOT, :])], axis=1)
    cos = cos_ref[...]
    sin = sin_ref[...]

    def rope2(z):
        return jnp.concatenate([_rope(z[:, :128], cos[:, :128], sin[:, :128]),
                                _rope(z[:, 128:], cos[:, 128:], sin[:, 128:])], axis=1)

    daq_ref[0] = (rope2(proj[:, P_DAQ:P_DAQ + GROUP_W]) * (DA_SCALE * LOG2E)).astype(BF)
    dak_ref[0] = rope2(proj[:, P_DAK:P_DAK + GROUP_W]).astype(BF)
    dav_ref[0] = proj[:, P_DAV:P_DAV + GROUP_W].astype(BF)

    qd = _rms(proj[:, P_QD:P_QD + GROUP_W]) * _row(gq_ref, layer)
    qm = _dot(qd.astype(BF), wqup_ref[...])
    for hh in range(N_HEADS):
        nope = qm[:, hh * GROUP_W:hh * GROUP_W + 128]
        rope = _rope(qm[:, hh * GROUP_W + 128:(hh + 1) * GROUP_W], cos[:, :128], sin[:, :128])
        mq_ref[0, hh] = (jnp.concatenate([nope, rope], axis=1) * (MLA_SCALE * LOG2E)).astype(BF)
    ckv = _rms(proj[:, P_KVD:P_KVD + MLA_KV_LORA]) * _row(gkv_ref, layer)
    kvm = _dot(ckv.astype(BF), wkv_ref[...])
    kn = kvm[:, :GROUP_W].astype(BF)
    krb = _rope(proj[:, P_KR:P_KR + 128], cos[:, :128], sin[:, :128]).astype(BF)
    ka_ref[0] = jnp.concatenate([kn[:, :128], krb], axis=1)
    kb_ref[0] = jnp.concatenate([kn[:, 128:], krb], axis=1)
    mv_ref[0] = kvm[:, GROUP_W:].astype(BF)

    naq_ref[0] = (proj[:, P_NAQ:P_NAQ + GROUP_W] * NA_SCALE).astype(BF)
    nak_ref[0] = proj[:, P_NAK:P_NAK + GROUP_W].astype(BF)
    nav_ref[0] = proj[:, P_NAV:P_NAV + GROUP_W].astype(BF)
    g_ref[0] = proj[:, P_CONV:P_CONV + GROUP_W] * jax.nn.sigmoid(proj[:, P_CONV + GROUP_W:P_TOT])


def _lat_proj(layer, x, prm, cos_t, sin_t):
    n_t = DEC_SEQ // TM_PROJ
    row_spec = pl.BlockSpec((1, TM_PROJ, GROUP_W), lambda b, j: (b, j, 0))
    in_specs = [pl.BlockSpec((TM_PROJ, D_MODEL), lambda b, j: (b * n_t + j, 0)),
                _layer_spec(layer, (16, N_MOD * D_MODEL)),
                _const_spec((DEPTH, D_MODEL)),
                _layer_spec(layer, (IN_COLS, D_MODEL)),
                _const_spec((DEPTH, GROUP_W)),
                _layer_spec(layer, (GROUP_W, N_HEADS * GROUP_W)),
                _const_spec((DEPTH, MLA_KV_LORA)),
                _layer_spec(layer, (MLA_KV_LORA, 2 * GROUP_W)),
                pl.BlockSpec((TM_PROJ, GROUP_W), lambda b, j: (j, 0)),
                pl.BlockSpec((TM_PROJ, GROUP_W), lambda b, j: (j, 0))]
    row_shp = jax.ShapeDtypeStruct((DEC_BATCH, DEC_SEQ, GROUP_W), BF)
    out_specs = [row_spec,
                 pl.BlockSpec((1, N_HEADS, TM_PROJ, GROUP_W), lambda b, j: (b, 0, j, 0)),
                 row_spec, row_spec] + [row_spec] * 7
    out_shape = [row_shp,
                 jax.ShapeDtypeStruct((DEC_BATCH, N_HEADS, DEC_SEQ, GROUP_W), BF),
                 row_shp,
                 jax.ShapeDtypeStruct((DEC_BATCH, DEC_SEQ, GROUP_W), F32)] + [row_shp] * 7
    return pl.pallas_call(
        functools.partial(_lat_proj_kernel, layer),
        grid=(DEC_BATCH, n_t),
        in_specs=in_specs,
        out_specs=out_specs,
        out_shape=out_shape,
        scratch_shapes=[pltpu.VMEM((P_TOT, D_MODEL), BF)],
        compiler_params=_cparams("arbitrary", "arbitrary"),
        name=f"lat_proj{layer}",
    )(x, prm["mod"], prm["g_mix"], prm["w_in"], prm["g_q"], prm["w_qup"], prm["g_kv"], prm["w_kvup"],
      cos_t, sin_t)


def _lat_attn_kernel(layer, lam_init, daq_ref, mq_ref, naq_ref, g_ref,
                     dak_ref, dav_ref, ka_ref, kb_ref, mv_ref, nak_ref, nav_ref,
                     cdak_ref, cdav_ref, cka_ref, ckb_ref, cmv_ref, cnak_ref, cnav_ref,
                     nab_ref, lamv_ref, gsub_ref, dw_ref, cb_ref, lng_ref, lnb_ref,
                     mixed_ref, gpad_ref, zsh_ref, kt_ref, v_ref):
    t = pl.program_id(1)
    lane = _lane_ids(GROUP_W)

    @pl.when(t == 0)
    def _():
        for slot, (lat, cache) in enumerate(((dak_ref, cdak_ref), (ka_ref, cka_ref), (kb_ref, ckb_ref))):
            kt_ref[slot, :, 0:DEC_SEQ] = lat[0].T
            kt_ref[slot, :, DEC_SEQ:KEYS] = cache[0].T
        for slot, (lat, cache) in enumerate(((dav_ref, cdav_ref), (mv_ref, cmv_ref))):
            v_ref[slot, 0:DEC_SEQ, :] = lat[0]
            v_ref[slot, DEC_SEQ:KEYS, :] = cache[0]

    lam = _diff_lambda(lamv_ref, lam_init)
    qa = daq_ref[0].astype(F32)
    va = v_ref[0]
    ka_t = kt_ref[0]
    vm = v_ref[1]

    kc = cnak_ref[0]
    vc = cnav_ref[0]
    n_items = 2 * N_HEADS + ROWS_PER_TILE

    def na_window(j):
        r = t * ROWS_PER_TILE + j
        start = jnp.clip(r - NA_KR // 2, 0, N_ROWS - NA_KR)
        return start - r + NA_KR - 1, pl.multiple_of(start * GRID_W, GRID_W)

    def scores(i):
        if i < N_HEADS:
            return (_dot(_mask_heads(qa, lane, DA_QK, 2 * i), ka_t),
                    _dot(_mask_heads(qa, lane, DA_QK, 2 * i + 1), ka_t))
        if i < 2 * N_HEADS:
            hh = i - N_HEADS
            return (_dot(mq_ref[0, hh], kt_ref[1 + hh // 2]),)
        j = i - 2 * N_HEADS
        dr0, koff = na_window(j)
        qrow = naq_ref[0, j * GRID_W:(j + 1) * GRID_W, :].astype(F32)
        q4 = jnp.concatenate([_mask_heads(qrow, lane, HEAD_DIM, hh) for hh in range(N_HEADS)], axis=0)
        return (_qk(q4, nak_ref[0, pl.ds(koff, NA_LOCAL), :]) + nab_ref[dr0], _qk(q4, kc))

    o_da = jnp.zeros((TQ, GROUP_W), F32)
    o_mla = jnp.zeros((TQ, GROUP_W), F32)
    na_rows = []
    s_next = scores(0)

    base = pl.multiple_of(t * TQ, TQ)
    gpad_ref[16:16 + TQ] = g_ref[0, pl.ds(base, TQ), :]
    lo = g_ref[0, pl.ds(pl.multiple_of(jnp.maximum(base - 16, 0), 16), 16), :]
    hi = g_ref[0, pl.ds(pl.multiple_of(jnp.minimum(base + TQ, DEC_SEQ - 16), 16), 16), :]
    gpad_ref[0:16] = jnp.where(t > 0, lo, 0.0)
    gpad_ref[16 + TQ:32 + TQ] = jnp.where(t < N_QT - 1, hi, 0.0)
    o_conv = _conv_ln_silu(layer, gpad_ref, zsh_ref, TQ, dw_ref, cb_ref, lng_ref, lnb_ref)

    for i in range(n_items):
        s_cur = s_next
        if i + 1 < n_items:
            s_next = scores(i + 1)
        if i < N_HEADS:
            e1, r1 = _softmax2_parts(s_cur[0])
            e2, r2 = _softmax2_parts(s_cur[1])
            p = (e1 - e2 * (lam * r2 / r1)).astype(BF)
            o_da = jnp.where(_group_of(lane, HEAD_DIM) == i, _dot(p, va) * r1, o_da)
        elif i < 2 * N_HEADS:
            e, r = _softmax2_parts(s_cur[0])
            o_mla = jnp.where(_group_of(lane, HEAD_DIM) == i - N_HEADS, _dot(e.astype(BF), vm) * r, o_mla)
        else:
            _, koff = na_window(i - 2 * N_HEADS)
            s_loc, s_ctx = s_cur
            m = jnp.maximum(jnp.max(s_loc, axis=-1, keepdims=True), jnp.max(s_ctx, axis=-1, keepdims=True))
            e_loc = jnp.exp(s_loc - m)
            e_ctx = jnp.exp(s_ctx - m)
            den = jnp.sum(e_loc, axis=-1, keepdims=True) + jnp.sum(e_ctx, axis=-1, keepdims=True)
            o4 = (_dot(e_loc.astype(BF), nav_ref[0, pl.ds(koff, NA_LOCAL), :])
                  + _dot(e_ctx.astype(BF), vc)) * (1.0 / den)
            o_row = o4[0:GRID_W]
            for hh in range(1, N_HEADS):
                o_row = jnp.where(_group_of(lane, HEAD_DIM) == hh, o4[hh * GRID_W:(hh + 1) * GRID_W], o_row)
            na_rows.append(o_row)
    o_da = _head_rms(o_da) * _row(gsub_ref, layer) * (1.0 - lam_init)
    o_na = jnp.concatenate(na_rows, axis=0)

    mixed_ref[...] = jnp.concatenate([o_da, o_mla, o_na, o_conv], axis=1).astype(BF)


def _lat_attn(layer, proj_outs, caches, nab, prm):
    lam_init = 0.8 - 0.6 * math.exp(-0.3 * layer)
    q_spec = pl.BlockSpec((1, TQ, GROUP_W), lambda b, t: (b, t, 0))
    full_k = pl.BlockSpec((1, DEC_SEQ, GROUP_W), lambda b, t: (b, 0, 0))
    cache_spec = pl.BlockSpec((1, None, PAST_LEN, GROUP_W), lambda b, t: (b, layer, 0, 0))
    in_specs = [q_spec,
                pl.BlockSpec((1, N_HEADS, TQ, GROUP_W), lambda b, t: (b, 0, t, 0)),
                q_spec, full_k] + [full_k] * 7 + [cache_spec] * 7 + [
                _layer_spec(layer, (NA_KR, N_HEADS * GRID_W, NA_LOCAL)),
                _layer_spec(layer, (4, DA_QK)),
                _const_spec((DEPTH, GROUP_W)),
                _layer_spec(layer, (32, GROUP_W)),
                _const_spec((DEPTH, GROUP_W)),
                _const_spec((DEPTH, GROUP_W)),
                _const_spec((DEPTH, GROUP_W))]
    return pl.pallas_call(
        functools.partial(_lat_attn_kernel, layer, lam_init),
        grid=(DEC_BATCH, N_QT),
        in_specs=in_specs,
        out_specs=pl.BlockSpec((TQ, D_MODEL), lambda b, t: (b * N_QT + t, 0)),
        out_shape=jax.ShapeDtypeStruct((DEC_BATCH * DEC_SEQ, D_MODEL), BF),
        scratch_shapes=[pltpu.VMEM((TQ + 32, GROUP_W), F32), pltpu.VMEM((7, TQ + 8, GROUP_W), F32),
                        pltpu.VMEM((3, GROUP_W, KEYS), BF), pltpu.VMEM((2, KEYS, GROUP_W), BF)],
        compiler_params=_cparams("arbitrary", "arbitrary"),
        name=f"lat_attn{layer}",
    )(*proj_outs, *caches, nab, prm["lamv"], prm["g_sub"], prm["dw"], prm["cb"], prm["ln_g"], prm["ln_b"])


def _out_ffn_kernel(layer, tiles_per_mod, mod_base, x_ref, mx_ref, mod_ref, wout_ref, gff_ref,
                    w1_ref, w2_ref, gfin_ref, o_ref):
    row = mod_base + pl.program_id(0) // tiles_per_mod
    g1 = mod_ref[pl.ds(row, 1), 2 * D_MODEL:3 * D_MODEL]
    sh2 = mod_ref[pl.ds(row, 1), 3 * D_MODEL:4 * D_MODEL]
    sc2 = mod_ref[pl.ds(row, 1), 4 * D_MODEL:5 * D_MODEL]
    g2 = mod_ref[pl.ds(row, 1), 5 * D_MODEL:6 * D_MODEL]
    x1 = x_ref[...] + g1 * _dot(mx_ref[...], wout_ref[...])
    h2 = ((_rms(x1) * _row(gff_ref, layer)) * (1.0 + sc2) + sh2).astype(BF)
    acc = jnp.zeros((TM_FFN, D_MODEL), F32)
    for c in range(D_FF // D_MODEL):
        a = jnp.maximum(_dot(h2, w1_ref[:, c * D_MODEL:(c + 1) * D_MODEL]), 0.0)
        acc = acc + _dot((a * a).astype(BF), w2_ref[c * D_MODEL:(c + 1) * D_MODEL, :])
    x2 = x1 + g2 * acc
    o_ref[...] = _rms(x2) * gfin_ref[...] if layer == DEPTH - 1 else x2


def _out_ffn(name, layer, tiles_per_mod, mod_base, x, mixed, prm, g_final):
    n = x.shape[0]
    tile = pl.BlockSpec((TM_FFN, D_MODEL), lambda i: (i, 0))
    return pl.pallas_call(
        functools.partial(_out_ffn_kernel, layer, tiles_per_mod, mod_base),
        grid=(n // TM_FFN,),
        in_specs=[tile, tile,
                  _layer_spec(layer, (16, N_MOD * D_MODEL)),
                  _layer_spec(layer, (D_MODEL, D_MODEL)),
                  _const_spec((DEPTH, D_MODEL)),
                  _layer_spec(layer, (D_MODEL, D_FF)),
                  _layer_spec(layer, (D_FF, D_MODEL)),
                  _const_spec((1, D_MODEL))],
        out_specs=tile,
        out_shape=jax.ShapeDtypeStruct((n, D_MODEL), F32),
        compiler_params=_cparams("arbitrary"),
        name=name,
    )(x, mixed, prm["mod"], prm["w_out"], prm["g_ff"], prm["w_ff1"], prm["w_ff2"], g_final)


def _rope_tables():
    t = np.arange(DEC_SEQ)
    rows = (t // GRID_W).astype(np.float32)
    cols = (t % GRID_W).astype(np.float32)
    c = np.arange(GROUP_W) % 32
    freqs = np.float32(ROPE_BASE) ** (-np.arange(8, dtype=np.float32) * np.float32(2.0) / np.float32(16))
    pos = np.where((c < 16)[None, :], rows[:, None], cols[:, None]).astype(np.float32)
    ang = (pos * freqs[(c % 16) % 8][None, :]).astype(np.float32)
    first = ((c % 16) < 8)[None, :]
    cos = np.cos(ang).astype(np.float32)
    sin = np.sin(ang).astype(np.float32)
    return jnp.asarray(cos), jnp.asarray(np.where(first, -sin, sin))


def _qup_gather_index():
    idx = np.full((N_HEADS * GROUP_W,), -1, np.int64)
    for h in range(N_HEADS):
        src = h * (MLA_NOPE + MLA_ROPE)
        dst = h * GROUP_W + (h % 2) * MLA_NOPE
        idx[dst:dst + MLA_NOPE] = np.arange(src, src + MLA_NOPE)
        idx[h * GROUP_W + 128:h * GROUP_W + 128 + MLA_ROPE] = np.arange(src + MLA_NOPE, src + MLA_NOPE + MLA_ROPE)
    return idx


def _kvup_perm():
    k = [h * 128 + d for h in range(N_HEADS) for d in range(MLA_NOPE)]
    v = [h * 128 + MLA_NOPE + d for h in range(N_HEADS) for d in range(HEAD_DIM)]
    return np.asarray(k + v)


def _heads_to_lanes(c):
    b, l, h, s, d = c.shape
    return c.transpose(0, 1, 3, 2, 4).reshape(b, l, s, h * d).astype(BF)


def kernel(x_prompt, x_sample, c, cache_da_k, cache_da_v, cache_mla_ckv, cache_mla_krope, cache_na_k, cache_na_v, c_ctx, w_mod, b_mod, g_norm_mix, g_norm_ff, w_in, da_lambda_q1, da_lambda_k1, da_lambda_q2, da_lambda_k2, g_da_subln, g_mla_q, w_mla_qup, g_mla_kv, w_mla_kvup, na_rpb, conv_dw, conv_b, conv_ln_g, conv_ln_b, w_out, w_ff1, w_ff2, g_final):
    qidx = _qup_gather_index()
    w_qup_e = jnp.where(jnp.asarray(qidx >= 0)[None, None, :],
                        jnp.take(w_mla_qup, jnp.asarray(np.maximum(qidx, 0)), axis=-1), 0.0).astype(BF)
    w_kvup_p = jnp.take(w_mla_kvup, jnp.asarray(_kvup_perm()), axis=-1).astype(BF)
    prm = dict(
        g_mix=g_norm_mix, w_in=jnp.swapaxes(w_in, 1, 2), g_q=g_mla_q, w_qup=w_qup_e, g_kv=g_mla_kv, w_kvup=w_kvup_p,
        lamv=jnp.stack([da_lambda_q1, da_lambda_k1, da_lambda_q2, da_lambda_k2], axis=1),
        g_sub=jnp.tile(g_da_subln, (1, N_HEADS)),
        dw=jnp.concatenate([conv_dw, jnp.zeros((DEPTH, 1, GROUP_W), F32)], axis=1),
        cb=conv_b, ln_g=conv_ln_g, ln_b=conv_ln_b,
        w_out=w_out.astype(BF), g_ff=g_norm_ff, w_ff1=w_ff1.astype(BF), w_ff2=w_ff2.astype(BF))
    cv = jnp.concatenate([c_ctx[None, :], c, jnp.zeros((16 - 1 - DEC_BATCH, D_MODEL), F32)], axis=0)
    cos_t, sin_t = _rope_tables()
    g_final2 = g_final.reshape(1, D_MODEL)

    c_kr_pad = jnp.pad(cache_mla_krope, ((0, 0), (0, 0), (0, 0), (0, KR_PAD)))
    prm["mod"], nab, cka, ckb, cmv = _prep(cv, w_mod, b_mod, na_rpb, cache_mla_ckv, c_kr_pad, w_kvup_p)
    caches = (_heads_to_lanes(cache_da_k), _heads_to_lanes(cache_da_v), cka, ckb, cmv,
              _heads_to_lanes(cache_na_k), _heads_to_lanes(cache_na_v))

    xp = x_prompt.reshape(BATCH * SEQ, D_MODEL)
    xs = x_sample.reshape(DEC_BATCH * DEC_SEQ, D_MODEL)
    new_ctx = ()
    for l in range(DEPTH):
        mixed_p, *new_ctx = _ctx_layer(l, xp, prm, tuple(new_ctx))
        xp = _out_ffn(f"ctx_ffn{l}", l, BATCH * SEQ // TM_FFN, 0, xp, mixed_p, prm, g_final2)
        proj_outs = _lat_proj(l, xs, prm, cos_t, sin_t)
        mixed_s = _lat_attn(l, proj_outs, caches, nab, prm)
        xs = _out_ffn(f"lat_ffn{l}", l, DEC_SEQ // TM_FFN, 1, xs, mixed_s, prm, g_final2)

    y_prompt = xp.reshape(BATCH, SEQ, D_MODEL)
    y_sample = xs.reshape(DEC_BATCH, DEC_SEQ, D_MODEL)
    new_ctx = [a if k == 2 else jnp.swapaxes(a, -1, -2) for k, a in enumerate(new_ctx)]
    return (y_prompt, y_sample) + tuple(new_ctx)
```

```python
import functools
import math

import numpy as np
import jax
import jax.numpy as jnp
from jax import lax
from jax.experimental import pallas as pl
from jax.experimental.pallas import tpu as pltpu

F32 = jnp.float32
BF = jnp.bfloat16

D_MODEL = 1024
BATCH = 16
SEQ = 256
DEPTH = 2
DEC_BATCH = 2
DEC_SEQ = 2048
PAST_LEN = 256
GRID_W = 64
GROUP_W = 256
HEAD_DIM = 64
N_HEADS = 4
DA_QK = 32
MLA_NOPE = 64
MLA_ROPE = 32
MLA_KV_LORA = 128
NA_KR = 8
NA_KC = 16
CONV_W = 31
D_FF = 4096
ROPE_BASE = 10000.0
EPS = 1e-6
N_MOD = 6
IN_COLS = 2464

P_DAQ, P_DAK, P_DAV, P_QD, P_KVD, P_KR = 0, 256, 512, 768, 1024, 1152
P_NAQ, P_NAK, P_NAV, P_CONV, P_TOT = 1280, 1536, 1792, 2048, 2560
P_SPLIT = P_NAQ
N_CTX_IN = 14
N_CTX_OUT = 7
KR_ORIG_END = 1184
KR_PAD = 128 - MLA_ROPE

DA_SCALE = DA_QK ** -0.5
MLA_SCALE = (MLA_NOPE + MLA_ROPE) ** -0.5
NA_SCALE = HEAD_DIM ** -0.5
LOG2E = math.log2(math.e)

TQ = 256
N_QT = DEC_SEQ // TQ
KEYS = DEC_SEQ + PAST_LEN
ROWS_PER_TILE = TQ // GRID_W
N_ROWS = DEC_SEQ // GRID_W
NA_LOCAL = NA_KR * GRID_W
TM_FFN = 512
TM_PROJ = 512
VMEM_LIMIT = 58 * 1024 * 1024

NT_DIMS = (((1,), (1,)), ((), ()))


def _cparams(*sem):
    return pltpu.CompilerParams(dimension_semantics=sem, vmem_limit_bytes=VMEM_LIMIT)


def _const_spec(shape):
    nd = len(shape)
    return pl.BlockSpec(shape, lambda *_: (0,) * nd, pipeline_mode=pl.Buffered(1))


def _layer_spec(layer, shape):
    nd = len(shape)
    return pl.BlockSpec((None,) + tuple(shape), lambda *_: (layer,) + (0,) * nd, pipeline_mode=pl.Buffered(1))


def _rms(x):
    return x * lax.rsqrt(jnp.mean(x * x, axis=-1, keepdims=True) + EPS)


def _row(ref, layer):
    return ref[layer:layer + 1, :]


def _dot(a, b):
    return jnp.dot(a, b, preferred_element_type=F32)


def _qk(q, k):
    return lax.dot_general(q, k, NT_DIMS, preferred_element_type=F32)


def _softmax2_parts(s):
    m = jnp.max(s, axis=-1, keepdims=True)
    e = jnp.exp2(s - m)
    return e, 1.0 / jnp.sum(e, axis=-1, keepdims=True)


def _lane_ids(width):
    return lax.broadcasted_iota(jnp.int32, (1, width), 1)


def _diff_lambda(lamv_ref, lam_init):
    v = lamv_ref[...]
    a = jnp.exp(jnp.sum(v[0:1] * v[1:2], axis=-1, keepdims=True))
    b = jnp.exp(jnp.sum(v[2:3] * v[3:4], axis=-1, keepdims=True))
    return a - b + lam_init


def _group_of(lane, group):
    return lane >> (group.bit_length() - 1)


def _head_rms(x):
    r = _group_of(lax.broadcasted_iota(jnp.int32, (GROUP_W, GROUP_W), 0), HEAD_DIM)
    c = _group_of(lax.broadcasted_iota(jnp.int32, (GROUP_W, GROUP_W), 1), HEAD_DIM)
    ones_bd = jnp.where(r == c, 1.0, 0.0).astype(BF)
    sq = x * x
    hi = sq.astype(BF)
    lo = (sq - hi.astype(F32)).astype(BF)
    ss = _dot(hi, ones_bd) + _dot(lo, ones_bd)
    return x * lax.rsqrt(ss * (1.0 / HEAD_DIM) + EPS)


def _rope(z, cos, sin):
    lane = _lane_ids(z.shape[1])
    swapped = jnp.where((lane & 15) < 8, pltpu.roll(z, 120, 1), pltpu.roll(z, 8, 1))
    return z * cos + swapped * sin


def _stage_w_in(wint_ref, wbt_ref):
    wbt_ref[0:KR_ORIG_END, :] = wint_ref[0:KR_ORIG_END, :].astype(BF)
    wbt_ref[KR_ORIG_END:P_SPLIT, :] = jnp.zeros((P_SPLIT - KR_ORIG_END, D_MODEL), BF)
    wbt_ref[P_SPLIT:P_TOT, :] = wint_ref[KR_ORIG_END:IN_COLS, :].astype(BF)


def _mask_heads(qf, lane, group, idx):
    return jnp.where(_group_of(lane, group) == idx, qf, 0.0).astype(BF)


def _conv_ln_silu(layer, gpad_ref, zsh_ref, n, dw_ref, cb_ref, lng_ref, lnb_ref):
    y = jnp.zeros((n, GROUP_W), F32) + _row(cb_ref, layer)
    for b in range(8):
        z = None
        for a in range(4):
            t = 8 * a + b - 1
            if 0 <= t < CONV_W:
                term = gpad_ref[pl.ds(8 * a, n + 8), :] * dw_ref[t:t + 1, :]
                z = term if z is None else z + term
        if b == 0:
            y = y + z[0:n]
        else:
            zsh_ref[b - 1] = z
            y = y + zsh_ref[b - 1, pl.ds(b, n), :]
    mu = jnp.mean(y, axis=-1, keepdims=True)
    yc = y - mu
    var = jnp.mean(yc * yc, axis=-1, keepdims=True)
    z = yc * lax.rsqrt(var + EPS) * _row(lng_ref, layer) + _row(lnb_ref, layer)
    return z * jax.nn.sigmoid(z)


def _mod_kernel(cv_ref, w_ref, b_ref, o_ref):
    c = cv_ref[...]
    a = (c * jax.nn.sigmoid(c)).astype(BF)
    o_ref[0] = _dot(a, w_ref[0].astype(BF)) + b_ref[0]


MOD_BLOCKS = N_HEADS


def _prep_kernel(cv_ref, wmod_ref, bmod_ref, rpb_ref, ckv_ref, kr_ref, wkv_ref,
                 mod_ref, nab_ref, ka_ref, kb_ref, v_ref, tp_ref):
    _mod_kernel(cv_ref, wmod_ref, bmod_ref, mod_ref)
    _na_bias_kernel(rpb_ref, nab_ref, tp_ref)

    @pl.when(pl.program_id(1) < DEC_BATCH)
    def _():
        _mla_cache_kernel(ckv_ref, kr_ref, wkv_ref, ka_ref, kb_ref, v_ref)


def _prep(cv, w_mod, b_mod, rpb, c_ckv, c_kr_pad, w_kvup_p):
    wblk = N_MOD * D_MODEL // MOD_BLOCKS
    cb = lambda l, j: (jnp.minimum(j, DEC_BATCH - 1), l, 0, 0)
    cache_blk = pl.BlockSpec((1, 1, PAST_LEN, GROUP_W), cb)
    cache_shp = jax.ShapeDtypeStruct((DEC_BATCH, DEPTH, PAST_LEN, GROUP_W), BF)
    return pl.pallas_call(
        _prep_kernel,
        grid=(DEPTH, MOD_BLOCKS),
        in_specs=[pl.BlockSpec((16, D_MODEL), lambda l, j: (0, 0)),
                  pl.BlockSpec((1, D_MODEL, wblk), lambda l, j: (l, 0, j)),
                  pl.BlockSpec((1, 1, wblk), lambda l, j: (l, 0, j)),
                  pl.BlockSpec(memory_space=pltpu.SMEM),
                  pl.BlockSpec((1, 1, PAST_LEN, MLA_KV_LORA), cb),
                  pl.BlockSpec((1, 1, PAST_LEN, 128), cb),
                  pl.BlockSpec((1, MLA_KV_LORA, 2 * GROUP_W), lambda l, j: (l, 0, 0))],
        out_specs=[pl.BlockSpec((1, 16, wblk), lambda l, j: (l, 0, j)),
                   pl.BlockSpec((1, NA_KR, GRID_W, NA_LOCAL), lambda l, j: (l, 0, j, 0)),
                   cache_blk, cache_blk, cache_blk],
        out_shape=[jax.ShapeDtypeStruct((DEPTH, 16, N_MOD * D_MODEL), F32),
                   jax.ShapeDtypeStruct((DEPTH, NA_KR, N_HEADS * GRID_W, NA_LOCAL), F32),
                   cache_shp, cache_shp, cache_shp],
        scratch_shapes=[pltpu.VMEM((N_DR - 1, GRID_W, 128), F32)],
        compiler_params=_cparams("arbitrary", "arbitrary"),
        name="prep",
    )(cv, w_mod, b_mod.reshape(DEPTH, 1, N_MOD * D_MODEL), rpb.reshape(-1), c_ckv, c_kr_pad, w_kvup_p)


def _mla_cache_kernel(ckv_ref, kr_ref, w_ref, ka_ref, kb_ref, v_ref):
    kvm = _dot(ckv_ref[0, 0].astype(BF), w_ref[0])
    kr = kr_ref[0, 0].astype(BF)
    kn = kvm[:, :GROUP_W].astype(BF)
    ka_ref[0, 0] = jnp.concatenate([kn[:, :128], kr], axis=1)
    kb_ref[0, 0] = jnp.concatenate([kn[:, 128:], kr], axis=1)
    v_ref[0, 0] = kvm[:, GROUP_W:].astype(BF)


N_DR = 2 * NA_KR - 1
N_DC = 2 * NA_KC - 1


def _na_bias_kernel(rpb_ref, o_ref, tp_ref):
    base = (pl.program_id(0) * N_HEADS + pl.program_id(1)) * (N_DR * N_DC)
    cq = lax.broadcasted_iota(jnp.int32, (GRID_W, 128), 0)
    lane = lax.broadcasted_iota(jnp.int32, (GRID_W, 128), 1)
    ck = lane & (GRID_W - 1)
    upper = lane >= GRID_W
    cstart = jnp.clip(cq - NA_KC // 2, 0, GRID_W - NA_KC)
    ok = (ck >= cstart) & (ck < cstart + NA_KC)

    j = lax.broadcasted_iota(jnp.int32, (8, 128), 1)
    dclip = jnp.clip(jnp.where(j < GRID_W, j, j - 128), -(NA_KC - 1), NA_KC - 1) + NA_KC - 1

    def toeplitz(dr):
        u = jnp.zeros((8, 128), F32)
        for d in range(N_DC):
            u = jnp.where(dclip == d, rpb_ref[base + dr * N_DC + d], u)
        rows = jnp.broadcast_to(u[0:1], (GRID_W, 128))
        return pltpu.roll(rows, 0, 1, stride=1, stride_axis=0)

    tabs = [toeplitz(dr) for dr in range(N_DR)]
    for dr in range(N_DR - 1):
        val = jnp.where(upper, pltpu.roll(tabs[dr + 1], GRID_W, 1), tabs[dr])
        tp_ref[dr] = jnp.where(ok, val, -jnp.inf)
    for dr0 in range(NA_KR):
        o_ref[0, dr0] = jnp.concatenate([tp_ref[dr0 + 2 * m] for m in range(NA_KR // 2)], axis=1)


def _ctx_kernel(layer, lam_init, n_prev, n_cast, *refs):
    (x_ref, mod_ref, gmix_ref, win_ref, gq_ref, wqup_ref, gkv_ref, wkv_ref,
     lamv_ref, gsub_ref, dw_ref, cb_ref, lng_ref, lnb_ref) = refs[:N_CTX_IN]
    cast_in = refs[N_CTX_IN:N_CTX_IN + n_cast]
    outs = refs[N_CTX_IN + n_cast + n_prev:]
    mixed_ref, dak_ref, dav_ref, ckv_ref, kr_ref, nak_ref, nav_ref = outs[:N_CTX_OUT]
    cast_out = outs[N_CTX_OUT:N_CTX_OUT + n_cast]
    proj_ref, gpad_ref, zsh_ref, wbf_ref = outs[N_CTX_OUT + n_cast:]

    for src, dst in zip(cast_in, cast_out):
        dst[...] = src[...].astype(BF)

    @pl.when(pl.program_id(0) == 0)
    def _():
        _stage_w_in(win_ref, wbf_ref)

    x = x_ref[...]
    sh1 = mod_ref[0:1, 0:D_MODEL]
    sc1 = mod_ref[0:1, D_MODEL:2 * D_MODEL]
    h = ((_rms(x) * _row(gmix_ref, layer)) * (1.0 + sc1) + sh1).astype(BF)
    proj_ref[:, 0:P_SPLIT] = _qk(h, wbf_ref[0:P_SPLIT, :])
    proj_ref[:, P_SPLIT:P_TOT] = _qk(h, wbf_ref[P_SPLIT:P_TOT, :])

    for ref, col in ((dak_ref, P_DAK), (dav_ref, P_DAV), (nak_ref, P_NAK), (nav_ref, P_NAV)):
        slab_t = proj_ref[:, col:col + GROUP_W].T
        for hh in range(N_HEADS):
            ref[0, hh] = slab_t[hh * HEAD_DIM:(hh + 1) * HEAD_DIM, :]

    lane = _lane_ids(GROUP_W)

    lam = _diff_lambda(lamv_ref, lam_init)

    qa = proj_ref[:, P_DAQ:P_DAQ + GROUP_W] * (DA_SCALE * LOG2E)
    ka = proj_ref[:, P_DAK:P_DAK + GROUP_W].astype(BF)
    va = proj_ref[:, P_DAV:P_DAV + GROUP_W].astype(BF)
    qn = proj_ref[:, P_NAQ:P_NAQ + GROUP_W] * (NA_SCALE * LOG2E)
    kn2 = proj_ref[:, P_NAK:P_NAK + GROUP_W].astype(BF)
    vn = proj_ref[:, P_NAV:P_NAV + GROUP_W].astype(BF)
    mla = {}

    def mla_prep():
        qd = _rms(proj_ref[:, P_QD:P_QD + GROUP_W]) * _row(gq_ref, layer)
        ckv = _rms(proj_ref[:, P_KVD:P_KVD + MLA_KV_LORA]) * _row(gkv_ref, layer)
        ckv_ref[0] = ckv
        kr_pad = proj_ref[:, P_KR:P_KR + 128]
        kr_ref[0] = kr_pad.T[0:MLA_ROPE, :]
        mla["q"] = (_dot(qd.astype(BF), wqup_ref[...]) * (MLA_SCALE * LOG2E)).astype(BF)
        kvm = _dot(ckv.astype(BF), wkv_ref[...])
        kn = kvm[:, :GROUP_W].astype(BF)
        krb = kr_pad.astype(BF)
        mla["k"] = (jnp.concatenate([kn[:, :128], krb], axis=1), jnp.concatenate([kn[:, 128:], krb], axis=1))
        mla["v"] = kvm[:, GROUP_W:].astype(BF)

    def conv():
        g = proj_ref[:, P_CONV:P_CONV + GROUP_W] * jax.nn.sigmoid(proj_ref[:, P_CONV + GROUP_W:P_TOT])
        gpad_ref[0:16] = jnp.zeros((16, GROUP_W), F32)
        gpad_ref[16 + SEQ:32 + SEQ] = jnp.zeros((16, GROUP_W), F32)
        gpad_ref[16:16 + SEQ] = g
        return _conv_ln_silu(layer, gpad_ref, zsh_ref, SEQ, dw_ref, cb_ref, lng_ref, lnb_ref)

    def scores(i):
        if i < N_HEADS:
            return (_qk(_mask_heads(qa, lane, DA_QK, 2 * i), ka),
                    _qk(_mask_heads(qa, lane, DA_QK, 2 * i + 1), ka))
        if i < 2 * N_HEADS:
            hh = i - N_HEADS
            return (_qk(mla["q"][:, hh * GROUP_W:(hh + 1) * GROUP_W], mla["k"][hh // 2]),)
        return (_qk(_mask_heads(qn, lane, HEAD_DIM, i - 2 * N_HEADS), kn2),)

    outs = [jnp.zeros((SEQ, GROUP_W), F32)] * 3
    s_next = scores(0)
    o_conv = None
    for i in range(3 * N_HEADS):
        s_cur = s_next
        if i + 1 < 3 * N_HEADS:
            s_next = scores(i + 1)
        if i == 0:
            mla_prep()
        if i == N_HEADS - 1:
            o_conv = conv()
        grp, hh = divmod(i, N_HEADS)
        if grp == 0:
            e1, r1 = _softmax2_parts(s_cur[0])
            e2, r2 = _softmax2_parts(s_cur[1])
            o = _dot((e1 * r1 - e2 * (lam * r2)).astype(BF), va)
        else:
            e, r = _softmax2_parts(s_cur[0])
            o = _dot(e.astype(BF), mla["v"] if grp == 1 else vn) * r
        outs[grp] = jnp.where(_group_of(lane, HEAD_DIM) == hh, o, outs[grp])
    o_da = _head_rms(outs[0]) * _row(gsub_ref, layer) * (1.0 - lam_init)
    o_mla, o_na = outs[1], outs[2]

    mixed_ref[...] = jnp.concatenate([o_da, o_mla, o_na, o_conv], axis=1).astype(BF)


def _ctx_layer(layer, x, prm, prev, cast):
    lam_init = 0.8 - 0.6 * math.exp(-0.3 * layer)
    head_blk = pl.BlockSpec((1, None, N_HEADS, HEAD_DIM, SEQ), lambda b: (b, layer, 0, 0, 0))
    head_shp = jax.ShapeDtypeStruct((BATCH, DEPTH, N_HEADS, HEAD_DIM, SEQ), F32)
    in_specs = [pl.BlockSpec((SEQ, D_MODEL), lambda b: (b, 0)),
                _layer_spec(layer, (16, N_MOD * D_MODEL)),
                _const_spec((DEPTH, D_MODEL)),
                _layer_spec(layer, (IN_COLS, D_MODEL)),
                _const_spec((DEPTH, GROUP_W)),
                _layer_spec(layer, (GROUP_W, N_HEADS * GROUP_W)),
                _const_spec((DEPTH, MLA_KV_LORA)),
                _layer_spec(layer, (MLA_KV_LORA, 2 * GROUP_W)),
                _layer_spec(layer, (4, DA_QK)),
                _const_spec((DEPTH, GROUP_W)),
                _layer_spec(layer, (32, GROUP_W)),
                _const_spec((DEPTH, GROUP_W)),
                _const_spec((DEPTH, GROUP_W)),
                _const_spec((DEPTH, GROUP_W))]
    cast_specs = [pl.BlockSpec((DEPTH, w.shape[1] // BATCH, w.shape[2]), lambda b: (0, b, 0)) for w in cast]
    in_specs += cast_specs + [pl.BlockSpec(memory_space=pl.ANY)] * len(prev)
    out_specs = [pl.BlockSpec((SEQ, D_MODEL), lambda b: (b, 0)),
                 head_blk, head_blk,
                 pl.BlockSpec((1, None, SEQ, MLA_KV_LORA), lambda b: (b, layer, 0, 0)),
                 pl.BlockSpec((1, None, MLA_ROPE, SEQ), lambda b: (b, layer, 0, 0)),
                 head_blk, head_blk]
    out_shape = [jax.ShapeDtypeStruct((BATCH * SEQ, D_MODEL), BF),
                 head_shp, head_shp,
                 jax.ShapeDtypeStruct((BATCH, DEPTH, SEQ, MLA_KV_LORA), F32),
                 jax.ShapeDtypeStruct((BATCH, DEPTH, MLA_ROPE, SEQ), F32),
                 head_shp, head_shp] + [jax.ShapeDtypeStruct(w.shape, BF) for w in cast]
    out_specs += cast_specs
    n_in = len(in_specs) - len(prev)
    return pl.pallas_call(
        functools.partial(_ctx_kernel, layer, lam_init, len(prev), len(cast)),
        grid=(BATCH,),
        in_specs=in_specs,
        out_specs=out_specs,
        out_shape=out_shape,
        input_output_aliases={n_in + k: 1 + k for k in range(len(prev))},
        scratch_shapes=[pltpu.VMEM((SEQ, P_TOT), F32), pltpu.VMEM((SEQ + 32, GROUP_W), F32),
                        pltpu.VMEM((7, SEQ + 8, GROUP_W), F32), pltpu.VMEM((P_TOT, D_MODEL), BF)],
        compiler_params=_cparams("arbitrary"),
        name=f"ctx_layer{layer}",
    )(x, prm["mod"], prm["g_mix"], prm["w_in"], prm["g_q"], prm["w_qup"], prm["g_kv"], prm["w_kvup"],
      prm["lamv"], prm["g_sub"], prm["dw"], prm["cb"], prm["ln_g"], prm["ln_b"], *cast, *prev)


def _lat_proj_kernel(layer, x_ref, mod_ref, gmix_ref, win_ref, gq_ref, wqup_ref, gkv_ref, wkv_ref,
                     cos_ref, sin_ref,
                     daq_ref, mq_ref, naq_ref, g_ref,
                     dak_ref, dav_ref, ka_ref, kb_ref, mv_ref, nak_ref, nav_ref, wbf_ref):
    b = pl.program_id(0)

    @pl.when((b == 0) & (pl.program_id(1) == 0))
    def _():
        _stage_w_in(win_ref, wbf_ref)

    x = x_ref[...]
    sh1 = mod_ref[pl.ds(1 + b, 1), 0:D_MODEL]
    sc1 = mod_ref[pl.ds(1 + b, 1), D_MODEL:2 * D_MODEL]
    h = (_rms(x) * _row(gmix_ref, layer)) * (1.0 + sc1) + sh1
    hb = h.astype(BF)
    proj = jnp.concatenate([_qk(hb, wbf_ref[0:P_SPLIT, :]), _qk(hb, wbf_ref[P_SPLIT:P_TOT, :])], axis=1)
    cos = cos_ref[...]
    sin = sin_ref[...]

    def rope2(z):
        return jnp.concatenate([_rope(z[:, :128], cos[:, :128], sin[:, :128]),
                                _rope(z[:, 128:], cos[:, 128:], sin[:, 128:])], axis=1)

    daq_ref[0] = (rope2(proj[:, P_DAQ:P_DAQ + GROUP_W]) * (DA_SCALE * LOG2E)).astype(BF)
    dak_ref[0] = rope2(proj[:, P_DAK:P_DAK + GROUP_W]).astype(BF)
    dav_ref[0] = proj[:, P_DAV:P_DAV + GROUP_W].astype(BF)

    qd = _rms(proj[:, P_QD:P_QD + GROUP_W]) * _row(gq_ref, layer)
    qm = _dot(qd.astype(BF), wqup_ref[...])
    for hh in range(N_HEADS):
        nope = qm[:, hh * GROUP_W:hh * GROUP_W + 128]
        rope = _rope(qm[:, hh * GROUP_W + 128:(hh + 1) * GROUP_W], cos[:, :128], sin[:, :128])
        mq_ref[0, hh] = (jnp.concatenate([nope, rope], axis=1) * (MLA_SCALE * LOG2E)).astype(BF)
    ckv = _rms(proj[:, P_KVD:P_KVD + MLA_KV_LORA]) * _row(gkv_ref, layer)
    kvm = _dot(ckv.astype(BF), wkv_ref[...])
    kn = kvm[:, :GROUP_W].astype(BF)
    krb = _rope(proj[:, P_KR:P_KR + 128], cos[:, :128], sin[:, :128]).astype(BF)
    ka_ref[0] = jnp.concatenate([kn[:, :128], krb], axis=1)
    kb_ref[0] = jnp.concatenate([kn[:, 128:], krb], axis=1)
    mv_ref[0] = kvm[:, GROUP_W:].astype(BF)

    naq_ref[0] = (proj[:, P_NAQ:P_NAQ + GROUP_W] * NA_SCALE).astype(BF)
    nak_ref[0] = proj[:, P_NAK:P_NAK + GROUP_W].astype(BF)
    nav_ref[0] = proj[:, P_NAV:P_NAV + GROUP_W].astype(BF)
    g_ref[0] = proj[:, P_CONV:P_CONV + GROUP_W] * jax.nn.sigmoid(proj[:, P_CONV + GROUP_W:P_TOT])


def _lat_proj(layer, x, prm, cos_t, sin_t):
    n_t = DEC_SEQ // TM_PROJ
    row_spec = pl.BlockSpec((1, TM_PROJ, GROUP_W), lambda b, j: (b, j, 0))
    in_specs = [pl.BlockSpec((TM_PROJ, D_MODEL), lambda b, j: (b * n_t + j, 0)),
                _layer_spec(layer, (16, N_MOD * D_MODEL)),
                _const_spec((DEPTH, D_MODEL)),
                _layer_spec(layer, (IN_COLS, D_MODEL)),
                _const_spec((DEPTH, GROUP_W)),
                _layer_spec(layer, (GROUP_W, N_HEADS * GROUP_W)),
                _const_spec((DEPTH, MLA_KV_LORA)),
                _layer_spec(layer, (MLA_KV_LORA, 2 * GROUP_W)),
                pl.BlockSpec((TM_PROJ, GROUP_W), lambda b, j: (j, 0)),
                pl.BlockSpec((TM_PROJ, GROUP_W), lambda b, j: (j, 0))]
    row_shp = jax.ShapeDtypeStruct((DEC_BATCH, DEC_SEQ, GROUP_W), BF)
    out_specs = [row_spec,
                 pl.BlockSpec((1, N_HEADS, TM_PROJ, GROUP_W), lambda b, j: (b, 0, j, 0)),
                 row_spec, row_spec] + [row_spec] * 7
    out_shape = [row_shp,
                 jax.ShapeDtypeStruct((DEC_BATCH, N_HEADS, DEC_SEQ, GROUP_W), BF),
                 row_shp,
                 jax.ShapeDtypeStruct((DEC_BATCH, DEC_SEQ, GROUP_W), F32)] + [row_shp] * 7
    return pl.pallas_call(
        functools.partial(_lat_proj_kernel, layer),
        grid=(DEC_BATCH, n_t),
        in_specs=in_specs,
        out_specs=out_specs,
        out_shape=out_shape,
        scratch_shapes=[pltpu.VMEM((P_TOT, D_MODEL), BF)],
        compiler_params=_cparams("arbitrary", "arbitrary"),
        name=f"lat_proj{layer}",
    )(x, prm["mod"], prm["g_mix"], prm["w_in"], prm["g_q"], prm["w_qup"], prm["g_kv"], prm["w_kvup"],
      cos_t, sin_t)


def _lat_attn_kernel(layer, lam_init, daq_ref, mq_ref, naq_ref, g_ref,
                     dak_ref, dav_ref, ka_ref, kb_ref, mv_ref, nak_ref, nav_ref,
                     cdak_ref, cdav_ref, cka_ref, ckb_ref, cmv_ref, cnak_ref, cnav_ref,
                     nab_ref, lamv_ref, gsub_ref, dw_ref, cb_ref, lng_ref, lnb_ref,
                     mixed_ref, gpad_ref, zsh_ref, kt_ref, v_ref):
    t = pl.program_id(1)
    lane = _lane_ids(GROUP_W)

    @pl.when(t == 0)
    def _():
        for slot, (lat, cache) in enumerate(((dak_ref, cdak_ref), (ka_ref, cka_ref), (kb_ref, ckb_ref))):
            kt_ref[slot, :, 0:DEC_SEQ] = lat[0].T
            kt_ref[slot, :, DEC_SEQ:KEYS] = cache[0].T
        for slot, (lat, cache) in enumerate(((dav_ref, cdav_ref), (mv_ref, cmv_ref))):
            v_ref[slot, 0:DEC_SEQ, :] = lat[0]
            v_ref[slot, DEC_SEQ:KEYS, :] = cache[0]

    lam = _diff_lambda(lamv_ref, lam_init)
    qa = daq_ref[0].astype(F32)
    va = v_ref[0]
    ka_t = kt_ref[0]
    vm = v_ref[1]

    kc = cnak_ref[0]
    vc = cnav_ref[0]
    n_items = 2 * N_HEADS + ROWS_PER_TILE

    def na_window(j):
        r = t * ROWS_PER_TILE + j
        start = jnp.clip(r - NA_KR // 2, 0, N_ROWS - NA_KR)
        return start - r + NA_KR - 1, pl.multiple_of(start * GRID_W, GRID_W)

    def scores(i):
        if i < N_HEADS:
            return (_dot(_mask_heads(qa, lane, DA_QK, 2 * i), ka_t),
                    _dot(_mask_heads(qa, lane, DA_QK, 2 * i + 1), ka_t))
        if i < 2 * N_HEADS:
            hh = i - N_HEADS
            return (_dot(mq_ref[0, hh], kt_ref[1 + hh // 2]),)
        j = i - 2 * N_HEADS
        dr0, koff = na_window(j)
        qrow = naq_ref[0, j * GRID_W:(j + 1) * GRID_W, :].astype(F32)
        q4 = jnp.concatenate([_mask_heads(qrow, lane, HEAD_DIM, hh) for hh in range(N_HEADS)], axis=0)
        return (_qk(q4, nak_ref[0, pl.ds(koff, NA_LOCAL), :]) + nab_ref[dr0], _qk(q4, kc))

    o_da = jnp.zeros((TQ, GROUP_W), F32)
    o_mla = jnp.zeros((TQ, GROUP_W), F32)
    na_rows = []
    s_next = scores(0)

    base = pl.multiple_of(t * TQ, TQ)
    gpad_ref[16:16 + TQ] = g_ref[0, pl.ds(base, TQ), :]
    lo = g_ref[0, pl.ds(pl.multiple_of(jnp.maximum(base - 16, 0), 16), 16), :]
    hi = g_ref[0, pl.ds(pl.multiple_of(jnp.minimum(base + TQ, DEC_SEQ - 16), 16), 16), :]
    gpad_ref[0:16] = jnp.where(t > 0, lo, 0.0)
    gpad_ref[16 + TQ:32 + TQ] = jnp.where(t < N_QT - 1, hi, 0.0)
    o_conv = _conv_ln_silu(layer, gpad_ref, zsh_ref, TQ, dw_ref, cb_ref, lng_ref, lnb_ref)

    for i in range(n_items):
        s_cur = s_next
        if i + 1 < n_items:
            s_next = scores(i + 1)
        if i < N_HEADS:
            e1, r1 = _softmax2_parts(s_cur[0])
            e2, r2 = _softmax2_parts(s_cur[1])
            p = (e1 - e2 * (lam * r2 / r1)).astype(BF)
            o_da = jnp.where(_group_of(lane, HEAD_DIM) == i, _dot(p, va) * r1, o_da)
        elif i < 2 * N_HEADS:
            e, r = _softmax2_parts(s_cur[0])
            o_mla = jnp.where(_group_of(lane, HEAD_DIM) == i - N_HEADS, _dot(e.astype(BF), vm) * r, o_mla)
        else:
            _, koff = na_window(i - 2 * N_HEADS)
            s_loc, s_ctx = s_cur
            m = jnp.maximum(jnp.max(s_loc, axis=-1, keepdims=True), jnp.max(s_ctx, axis=-1, keepdims=True))
            e_loc = jnp.exp(s_loc - m)
            e_ctx = jnp.exp(s_ctx - m)
            den = jnp.sum(e_loc, axis=-1, keepdims=True) + jnp.sum(e_ctx, axis=-1, keepdims=True)
            o4 = (_dot(e_loc.astype(BF), nav_ref[0, pl.ds(koff, NA_LOCAL), :])
                  + _dot(e_ctx.astype(BF), vc)) * (1.0 / den)
            o_row = o4[0:GRID_W]
            for hh in range(1, N_HEADS):
                o_row = jnp.where(_group_of(lane, HEAD_DIM) == hh, o4[hh * GRID_W:(hh + 1) * GRID_W], o_row)
            na_rows.append(o_row)
    o_da = _head_rms(o_da) * _row(gsub_ref, layer) * (1.0 - lam_init)
    o_na = jnp.concatenate(na_rows, axis=0)

    mixed_ref[...] = jnp.concatenate([o_da, o_mla, o_na, o_conv], axis=1).astype(BF)


def _lat_attn(layer, proj_outs, caches, nab, prm):
    lam_init = 0.8 - 0.6 * math.exp(-0.3 * layer)
    q_spec = pl.BlockSpec((1, TQ, GROUP_W), lambda b, t: (b, t, 0))
    full_k = pl.BlockSpec((1, DEC_SEQ, GROUP_W), lambda b, t: (b, 0, 0))
    cache_spec = pl.BlockSpec((1, None, PAST_LEN, GROUP_W), lambda b, t: (b, layer, 0, 0))
    in_specs = [q_spec,
                pl.BlockSpec((1, N_HEADS, TQ, GROUP_W), lambda b, t: (b, 0, t, 0)),
                q_spec, full_k] + [full_k] * 7 + [cache_spec] * 7 + [
                _layer_spec(layer, (NA_KR, N_HEADS * GRID_W, NA_LOCAL)),
                _layer_spec(layer, (4, DA_QK)),
                _const_spec((DEPTH, GROUP_W)),
                _layer_spec(layer, (32, GROUP_W)),
                _const_spec((DEPTH, GROUP_W)),
                _const_spec((DEPTH, GROUP_W)),
                _const_spec((DEPTH, GROUP_W))]
    return pl.pallas_call(
        functools.partial(_lat_attn_kernel, layer, lam_init),
        grid=(DEC_BATCH, N_QT),
        in_specs=in_specs,
        out_specs=pl.BlockSpec((TQ, D_MODEL), lambda b, t: (b * N_QT + t, 0)),
        out_shape=jax.ShapeDtypeStruct((DEC_BATCH * DEC_SEQ, D_MODEL), BF),
        scratch_shapes=[pltpu.VMEM((TQ + 32, GROUP_W), F32), pltpu.VMEM((7, TQ + 8, GROUP_W), F32),
                        pltpu.VMEM((3, GROUP_W, KEYS), BF), pltpu.VMEM((2, KEYS, GROUP_W), BF)],
        compiler_params=_cparams("arbitrary", "arbitrary"),
        name=f"lat_attn{layer}",
    )(*proj_outs, *caches, nab, prm["lamv"], prm["g_sub"], prm["dw"], prm["cb"], prm["ln_g"], prm["ln_b"])


def _out_ffn_kernel(layer, tiles_per_mod, mod_base, x_ref, mx_ref, mod_ref, wout_ref, gff_ref,
                    w1_ref, w2_ref, gfin_ref, o_ref):
    row = mod_base + pl.program_id(0) // tiles_per_mod
    g1 = mod_ref[pl.ds(row, 1), 2 * D_MODEL:3 * D_MODEL]
    sh2 = mod_ref[pl.ds(row, 1), 3 * D_MODEL:4 * D_MODEL]
    sc2 = mod_ref[pl.ds(row, 1), 4 * D_MODEL:5 * D_MODEL]
    g2 = mod_ref[pl.ds(row, 1), 5 * D_MODEL:6 * D_MODEL]
    x1 = x_ref[...] + g1 * _dot(mx_ref[...], wout_ref[...])
    h2 = ((_rms(x1) * _row(gff_ref, layer)) * (1.0 + sc2) + sh2).astype(BF)
    acc = jnp.zeros((TM_FFN, D_MODEL), F32)
    for c in range(D_FF // D_MODEL):
        a = jnp.maximum(_dot(h2, w1_ref[:, c * D_MODEL:(c + 1) * D_MODEL]), 0.0)
        acc = acc + _dot((a * a).astype(BF), w2_ref[c * D_MODEL:(c + 1) * D_MODEL, :])
    x2 = x1 + g2 * acc
    o_ref[...] = _rms(x2) * gfin_ref[...] if layer == DEPTH - 1 else x2


def _out_ffn(name, layer, tiles_per_mod, mod_base, x, mixed, prm, g_final):
    n = x.shape[0]
    tile = pl.BlockSpec((TM_FFN, D_MODEL), lambda i: (i, 0))
    return pl.pallas_call(
        functools.partial(_out_ffn_kernel, layer, tiles_per_mod, mod_base),
        grid=(n // TM_FFN,),
        in_specs=[tile, tile,
                  _layer_spec(layer, (16, N_MOD * D_MODEL)),
                  _layer_spec(layer, (D_MODEL, D_MODEL)),
                  _const_spec((DEPTH, D_MODEL)),
                  _layer_spec(layer, (D_MODEL, D_FF)),
                  _layer_spec(layer, (D_FF, D_MODEL)),
                  _const_spec((1, D_MODEL))],
        out_specs=tile,
        out_shape=jax.ShapeDtypeStruct((n, D_MODEL), F32),
        compiler_params=_cparams("arbitrary"),
        name=name,
    )(x, mixed, prm["mod"], prm["w_out"], prm["g_ff"], prm["w_ff1"], prm["w_ff2"], g_final)


def _rope_tables():
    t = np.arange(DEC_SEQ)
    rows = (t // GRID_W).astype(np.float32)
    cols = (t % GRID_W).astype(np.float32)
    c = np.arange(GROUP_W) % 32
    freqs = np.float32(ROPE_BASE) ** (-np.arange(8, dtype=np.float32) * np.float32(2.0) / np.float32(16))
    pos = np.where((c < 16)[None, :], rows[:, None], cols[:, None]).astype(np.float32)
    ang = (pos * freqs[(c % 16) % 8][None, :]).astype(np.float32)
    first = ((c % 16) < 8)[None, :]
    cos = np.cos(ang).astype(np.float32)
    sin = np.sin(ang).astype(np.float32)
    return jnp.asarray(cos), jnp.asarray(np.where(first, -sin, sin))


def _qup_gather_index():
    idx = np.full((N_HEADS * GROUP_W,), -1, np.int64)
    for h in range(N_HEADS):
        src = h * (MLA_NOPE + MLA_ROPE)
        dst = h * GROUP_W + (h % 2) * MLA_NOPE
        idx[dst:dst + MLA_NOPE] = np.arange(src, src + MLA_NOPE)
        idx[h * GROUP_W + 128:h * GROUP_W + 128 + MLA_ROPE] = np.arange(src + MLA_NOPE, src + MLA_NOPE + MLA_ROPE)
    return idx


def _kvup_perm():
    k = [h * 128 + d for h in range(N_HEADS) for d in range(MLA_NOPE)]
    v = [h * 128 + MLA_NOPE + d for h in range(N_HEADS) for d in range(HEAD_DIM)]
    return np.asarray(k + v)


def _heads_to_lanes(c):
    b, l, h, s, d = c.shape
    return c.transpose(0, 1, 3, 2, 4).reshape(b, l, s, h * d).astype(BF)


def kernel(x_prompt, x_sample, c, cache_da_k, cache_da_v, cache_mla_ckv, cache_mla_krope, cache_na_k, cache_na_v, c_ctx, w_mod, b_mod, g_norm_mix, g_norm_ff, w_in, da_lambda_q1, da_lambda_k1, da_lambda_q2, da_lambda_k2, g_da_subln, g_mla_q, w_mla_qup, g_mla_kv, w_mla_kvup, na_rpb, conv_dw, conv_b, conv_ln_g, conv_ln_b, w_out, w_ff1, w_ff2, g_final):
    qidx = _qup_gather_index()
    w_qup_e = jnp.where(jnp.asarray(qidx >= 0)[None, None, :],
                        jnp.take(w_mla_qup, jnp.asarray(np.maximum(qidx, 0)), axis=-1), 0.0).astype(BF)
    w_kvup_p = jnp.take(w_mla_kvup, jnp.asarray(_kvup_perm()), axis=-1).astype(BF)
    prm = dict(
        g_mix=g_norm_mix, w_in=jnp.swapaxes(w_in, 1, 2), g_q=g_mla_q, w_qup=w_qup_e, g_kv=g_mla_kv, w_kvup=w_kvup_p,
        lamv=jnp.stack([da_lambda_q1, da_lambda_k1, da_lambda_q2, da_lambda_k2], axis=1),
        g_sub=jnp.tile(g_da_subln, (1, N_HEADS)),
        dw=jnp.concatenate([conv_dw, jnp.zeros((DEPTH, 1, GROUP_W), F32)], axis=1),
        cb=conv_b, ln_g=conv_ln_g, ln_b=conv_ln_b,
        g_ff=g_norm_ff)
    cv = jnp.concatenate([c_ctx[None, :], c, jnp.zeros((16 - 1 - DEC_BATCH, D_MODEL), F32)], axis=0)
    cos_t, sin_t = _rope_tables()
    g_final2 = g_final.reshape(1, D_MODEL)

    c_kr_pad = jnp.pad(cache_mla_krope, ((0, 0), (0, 0), (0, 0), (0, KR_PAD)))
    prm["mod"], nab, cka, ckb, cmv = _prep(cv, w_mod, b_mod, na_rpb, cache_mla_ckv, c_kr_pad, w_kvup_p)
    caches = (_heads_to_lanes(cache_da_k), _heads_to_lanes(cache_da_v), cka, ckb, cmv,
              _heads_to_lanes(cache_na_k), _heads_to_lanes(cache_na_v))

    xp = x_prompt.reshape(BATCH * SEQ, D_MODEL)
    xs = x_sample.reshape(DEC_BATCH * DEC_SEQ, D_MODEL)
    new_ctx = ()
    for l in range(DEPTH):
        cast = (w_out, w_ff1, w_ff2) if l == 0 else ()
        mixed_p, *rest = _ctx_layer(l, xp, prm, tuple(new_ctx), cast)
        new_ctx = rest[:len(rest) - len(cast)]
        if cast:
            prm["w_out"], prm["w_ff1"], prm["w_ff2"] = rest[len(new_ctx):]
        xp = _out_ffn(f"ctx_ffn{l}", l, BATCH * SEQ // TM_FFN, 0, xp, mixed_p, prm, g_final2)
        proj_outs = _lat_proj(l, xs, prm, cos_t, sin_t)
        mixed_s = _lat_attn(l, proj_outs, caches, nab, prm)
        xs = _out_ffn(f"lat_ffn{l}", l, DEC_SEQ // TM_FFN, 1, xs, mixed_s, prm, g_final2)

    y_prompt = xp.reshape(BATCH, SEQ, D_MODEL)
    y_sample = xs.reshape(DEC_BATCH, DEC_SEQ, D_MODEL)
    new_ctx = [a if k == 2 else jnp.swapaxes(a, -1, -2) for k, a in enumerate(new_ctx)]
    return (y_prompt, y_sample) + tuple(new_ctx)
```

```python
import functools
import math

import numpy as np
import jax
import jax.numpy as jnp
from jax import lax
from jax.experimental import pallas as pl
from jax.experimental.pallas import tpu as pltpu

F32 = jnp.float32
BF = jnp.bfloat16

D_MODEL = 1024
BATCH = 16
SEQ = 256
DEPTH = 2
DEC_BATCH = 2
DEC_SEQ = 2048
PAST_LEN = 256
GRID_W = 64
GROUP_W = 256
HEAD_DIM = 64
N_HEADS = 4
DA_QK = 32
MLA_NOPE = 64
MLA_ROPE = 32
MLA_KV_LORA = 128
NA_KR = 8
NA_KC = 16
CONV_W = 31
D_FF = 4096
ROPE_BASE = 10000.0
EPS = 1e-6
N_MOD = 6
IN_COLS = 2464

P_DAQ, P_DAK, P_DAV, P_QD, P_KVD, P_KR = 0, 256, 512, 768, 1024, 1152
P_NAQ, P_NAK, P_NAV, P_CONV, P_TOT = 1280, 1536, 1792, 2048, 2560
P_SPLIT = P_NAQ
N_CTX_IN = 14
N_CTX_OUT = 7
KR_ORIG_END = 1184
KR_PAD = 128 - MLA_ROPE

DA_SCALE = DA_QK ** -0.5
MLA_SCALE = (MLA_NOPE + MLA_ROPE) ** -0.5
NA_SCALE = HEAD_DIM ** -0.5
LOG2E = math.log2(math.e)

TQ = 256
N_QT = DEC_SEQ // TQ
KEYS = DEC_SEQ + PAST_LEN
ROWS_PER_TILE = TQ // GRID_W
N_ROWS = DEC_SEQ // GRID_W
NA_LOCAL = NA_KR * GRID_W
TM_FFN = 512
TM_PROJ = 512
VMEM_LIMIT = 58 * 1024 * 1024

NT_DIMS = (((1,), (1,)), ((), ()))


def _cparams(*sem):
    return pltpu.CompilerParams(dimension_semantics=sem, vmem_limit_bytes=VMEM_LIMIT)


def _const_spec(shape):
    nd = len(shape)
    return pl.BlockSpec(shape, lambda *_: (0,) * nd, pipeline_mode=pl.Buffered(1))


def _layer_spec(layer, shape):
    nd = len(shape)
    return pl.BlockSpec((None,) + tuple(shape), lambda *_: (layer,) + (0,) * nd, pipeline_mode=pl.Buffered(1))


def _rms(x):
    return x * lax.rsqrt(jnp.mean(x * x, axis=-1, keepdims=True) + EPS)


def _row(ref, layer):
    return ref[layer:layer + 1, :]


def _dot(a, b):
    return jnp.dot(a, b, preferred_element_type=F32)


def _qk(q, k):
    return lax.dot_general(q, k, NT_DIMS, preferred_element_type=F32)


def _softmax2_parts(s):
    m = jnp.max(s, axis=-1, keepdims=True)
    e = jnp.exp2(s - m)
    return e, 1.0 / jnp.sum(e, axis=-1, keepdims=True)


def _lane_ids(width):
    return lax.broadcasted_iota(jnp.int32, (1, width), 1)


def _diff_lambda(lamv_ref, lam_init):
    v = lamv_ref[...]
    a = jnp.exp(jnp.sum(v[0:1] * v[1:2], axis=-1, keepdims=True))
    b = jnp.exp(jnp.sum(v[2:3] * v[3:4], axis=-1, keepdims=True))
    return a - b + lam_init


def _group_of(lane, group):
    return lane >> (group.bit_length() - 1)


def _head_rms(x):
    r = _group_of(lax.broadcasted_iota(jnp.int32, (GROUP_W, GROUP_W), 0), HEAD_DIM)
    c = _group_of(lax.broadcasted_iota(jnp.int32, (GROUP_W, GROUP_W), 1), HEAD_DIM)
    ones_bd = jnp.where(r == c, 1.0, 0.0).astype(BF)
    sq = x * x
    hi = sq.astype(BF)
    lo = (sq - hi.astype(F32)).astype(BF)
    ss = _dot(hi, ones_bd) + _dot(lo, ones_bd)
    return x * lax.rsqrt(ss * (1.0 / HEAD_DIM) + EPS)


def _rope(z, cos, sin):
    lane = _lane_ids(z.shape[1])
    swapped = jnp.where((lane & 15) < 8, pltpu.roll(z, 120, 1), pltpu.roll(z, 8, 1))
    return z * cos + swapped * sin


def _stage_w_in(wint_ref, wbt_ref):
    wbt_ref[0:KR_ORIG_END, :] = wint_ref[0:KR_ORIG_END, :].astype(BF)
    wbt_ref[KR_ORIG_END:P_SPLIT, :] = jnp.zeros((P_SPLIT - KR_ORIG_END, D_MODEL), BF)
    wbt_ref[P_SPLIT:P_TOT, :] = wint_ref[KR_ORIG_END:IN_COLS, :].astype(BF)


def _mask_heads(qf, lane, group, idx):
    return jnp.where(_group_of(lane, group) == idx, qf, 0.0).astype(BF)


def _conv_ln_silu(layer, gpad_ref, zsh_ref, n, dw_ref, cb_ref, lng_ref, lnb_ref):
    y = jnp.zeros((n, GROUP_W), F32) + _row(cb_ref, layer)
    for b in range(8):
        z = None
        for a in range(4):
            t = 8 * a + b - 1
            if 0 <= t < CONV_W:
                term = gpad_ref[pl.ds(8 * a, n + 8), :] * dw_ref[t:t + 1, :]
                z = term if z is None else z + term
        if b == 0:
            y = y + z[0:n]
        else:
            zsh_ref[b - 1] = z
            y = y + zsh_ref[b - 1, pl.ds(b, n), :]
    mu = jnp.mean(y, axis=-1, keepdims=True)
    yc = y - mu
    var = jnp.mean(yc * yc, axis=-1, keepdims=True)
    z = yc * lax.rsqrt(var + EPS) * _row(lng_ref, layer) + _row(lnb_ref, layer)
    return z * jax.nn.sigmoid(z)


def _mod_kernel(cv_ref, w_ref, b_ref, o_ref):
    c = cv_ref[...]
    a = (c * jax.nn.sigmoid(c)).astype(BF)
    o_ref[0] = _dot(a, w_ref[0].astype(BF)) + b_ref[pl.ds(pl.program_id(0), 1), :]


MOD_BLOCKS = N_HEADS


def _prep_kernel(cv_ref, wmod_ref, bmod_ref, rpb_ref, ckv_ref, kr_ref, wq_ref, wkv_ref, selq_ref, selkv_ref,
                 mod_ref, nab_ref, ka_ref, kb_ref, v_ref, wqe_ref, wkvp_ref, tp_ref):
    _mod_kernel(cv_ref, wmod_ref, bmod_ref, mod_ref)
    _na_bias_kernel(rpb_ref, nab_ref, tp_ref)

    @pl.when(pl.program_id(1) < DEC_BATCH)
    def _():
        wqe_ref[0] = _dot(wq_ref[0].astype(BF), selq_ref[...].astype(BF)).astype(BF)
        w_kvup = _dot(wkv_ref[0].astype(BF), selkv_ref[...].astype(BF)).astype(BF)
        wkvp_ref[0] = w_kvup
        _mla_cache_kernel(ckv_ref, kr_ref, w_kvup, ka_ref, kb_ref, v_ref)


def _prep(cv, w_mod, b_mod, rpb, c_ckv, c_kr_t, w_qup, w_kvup):
    wblk = N_MOD * D_MODEL // MOD_BLOCKS
    cb = lambda l, j: (jnp.minimum(j, DEC_BATCH - 1), l, 0, 0)
    cache_blk = pl.BlockSpec((1, 1, PAST_LEN, GROUP_W), cb)
    cache_shp = jax.ShapeDtypeStruct((DEC_BATCH, DEPTH, PAST_LEN, GROUP_W), BF)
    per_layer = lambda shape: pl.BlockSpec((1,) + tuple(shape), lambda l, j: (l,) + (0,) * len(shape))
    selq, selkv = _qup_selection(), _kvup_selection()
    return pl.pallas_call(
        _prep_kernel,
        grid=(DEPTH, MOD_BLOCKS),
        in_specs=[pl.BlockSpec((16, D_MODEL), lambda l, j: (0, 0)),
                  pl.BlockSpec((1, D_MODEL, wblk), lambda l, j: (l, 0, j)),
                  pl.BlockSpec((DEPTH, wblk), lambda l, j: (0, j)),
                  pl.BlockSpec(memory_space=pltpu.SMEM),
                  pl.BlockSpec((1, 1, PAST_LEN, MLA_KV_LORA), cb),
                  pl.BlockSpec((1, 1, MLA_ROPE, PAST_LEN), cb),
                  per_layer(w_qup.shape[1:]), per_layer(w_kvup.shape[1:]),
                  _const_spec(selq.shape), _const_spec(selkv.shape)],
        out_specs=[pl.BlockSpec((1, 16, wblk), lambda l, j: (l, 0, j)),
                   pl.BlockSpec((1, NA_KR, GRID_W, NA_LOCAL), lambda l, j: (l, 0, j, 0)),
                   cache_blk, cache_blk, cache_blk,
                   per_layer((GROUP_W, N_HEADS * GROUP_W)), per_layer((MLA_KV_LORA, 2 * GROUP_W))],
        out_shape=[jax.ShapeDtypeStruct((DEPTH, 16, N_MOD * D_MODEL), F32),
                   jax.ShapeDtypeStruct((DEPTH, NA_KR, N_HEADS * GRID_W, NA_LOCAL), F32),
                   cache_shp, cache_shp, cache_shp,
                   jax.ShapeDtypeStruct((DEPTH, GROUP_W, N_HEADS * GROUP_W), BF),
                   jax.ShapeDtypeStruct((DEPTH, MLA_KV_LORA, 2 * GROUP_W), BF)],
        scratch_shapes=[pltpu.VMEM((N_DR - 1, GRID_W, 128), F32)],
        compiler_params=_cparams("arbitrary", "arbitrary"),
        name="prep",
    )(cv, w_mod, b_mod, rpb.reshape(-1), c_ckv, c_kr_t, w_qup, w_kvup, selq, selkv)


def _mla_cache_kernel(ckv_ref, kr_ref, w_kvup, ka_ref, kb_ref, v_ref):
    kvm = _dot(ckv_ref[0, 0].astype(BF), w_kvup)
    kr_t = jnp.concatenate([kr_ref[0, 0], jnp.zeros((KR_PAD, PAST_LEN), F32)], axis=0)
    kr = kr_t.T.astype(BF)
    kn = kvm[:, :GROUP_W].astype(BF)
    ka_ref[0, 0] = jnp.concatenate([kn[:, :128], kr], axis=1)
    kb_ref[0, 0] = jnp.concatenate([kn[:, 128:], kr], axis=1)
    v_ref[0, 0] = kvm[:, GROUP_W:].astype(BF)


N_DR = 2 * NA_KR - 1
N_DC = 2 * NA_KC - 1


def _na_bias_kernel(rpb_ref, o_ref, tp_ref):
    base = (pl.program_id(0) * N_HEADS + pl.program_id(1)) * (N_DR * N_DC)
    cq = lax.broadcasted_iota(jnp.int32, (GRID_W, 128), 0)
    lane = lax.broadcasted_iota(jnp.int32, (GRID_W, 128), 1)
    ck = lane & (GRID_W - 1)
    upper = lane >= GRID_W
    cstart = jnp.clip(cq - NA_KC // 2, 0, GRID_W - NA_KC)
    ok = (ck >= cstart) & (ck < cstart + NA_KC)

    j = lax.broadcasted_iota(jnp.int32, (8, 128), 1)
    dclip = jnp.clip(jnp.where(j < GRID_W, j, j - 128), -(NA_KC - 1), NA_KC - 1) + NA_KC - 1

    def toeplitz(dr):
        u = jnp.zeros((8, 128), F32)
        for d in range(N_DC):
            u = jnp.where(dclip == d, rpb_ref[base + dr * N_DC + d], u)
        rows = jnp.broadcast_to(u[0:1], (GRID_W, 128))
        return pltpu.roll(rows, 0, 1, stride=1, stride_axis=0)

    tabs = [toeplitz(dr) for dr in range(N_DR)]
    for dr in range(N_DR - 1):
        val = jnp.where(upper, pltpu.roll(tabs[dr + 1], GRID_W, 1), tabs[dr])
        tp_ref[dr] = jnp.where(ok, val, -jnp.inf)
    for dr0 in range(NA_KR):
        o_ref[0, dr0] = jnp.concatenate([tp_ref[dr0 + 2 * m] for m in range(NA_KR // 2)], axis=1)


def _ctx_kernel(layer, lam_init, n_prev, n_cast, *refs):
    (x_ref, mod_ref, gmix_ref, win_ref, gq_ref, wqup_ref, gkv_ref, wkv_ref,
     lamv_ref, gsub_ref, dw_ref, cb_ref, lng_ref, lnb_ref) = refs[:N_CTX_IN]
    cast_in = refs[N_CTX_IN:N_CTX_IN + n_cast]
    outs = refs[N_CTX_IN + n_cast + n_prev:]
    mixed_ref, dak_ref, dav_ref, ckv_ref, kr_ref, nak_ref, nav_ref = outs[:N_CTX_OUT]
    cast_out = outs[N_CTX_OUT:N_CTX_OUT + n_cast]
    proj_ref, gpad_ref, zsh_ref, wbf_ref = outs[N_CTX_OUT + n_cast:]

    for src, dst in zip(cast_in, cast_out):
        dst[...] = src[...].astype(BF)

    @pl.when(pl.program_id(0) == 0)
    def _():
        _stage_w_in(win_ref, wbf_ref)

    x = x_ref[...]
    sh1 = mod_ref[0:1, 0:D_MODEL]
    sc1 = mod_ref[0:1, D_MODEL:2 * D_MODEL]
    h = ((_rms(x) * _row(gmix_ref, layer)) * (1.0 + sc1) + sh1).astype(BF)
    proj_ref[:, 0:P_SPLIT] = _qk(h, wbf_ref[0:P_SPLIT, :])
    proj_ref[:, P_SPLIT:P_TOT] = _qk(h, wbf_ref[P_SPLIT:P_TOT, :])

    for ref, col in ((dak_ref, P_DAK), (dav_ref, P_DAV), (nak_ref, P_NAK), (nav_ref, P_NAV)):
        slab_t = proj_ref[:, col:col + GROUP_W].T
        for hh in range(N_HEADS):
            ref[0, hh] = slab_t[hh * HEAD_DIM:(hh + 1) * HEAD_DIM, :]

    lane = _lane_ids(GROUP_W)

    lam = _diff_lambda(lamv_ref, lam_init)

    qa = proj_ref[:, P_DAQ:P_DAQ + GROUP_W] * (DA_SCALE * LOG2E)
    ka = proj_ref[:, P_DAK:P_DAK + GROUP_W].astype(BF)
    va = proj_ref[:, P_DAV:P_DAV + GROUP_W].astype(BF)
    qn = proj_ref[:, P_NAQ:P_NAQ + GROUP_W] * (NA_SCALE * LOG2E)
    kn2 = proj_ref[:, P_NAK:P_NAK + GROUP_W].astype(BF)
    vn = proj_ref[:, P_NAV:P_NAV + GROUP_W].astype(BF)
    mla = {}

    def mla_prep():
        qd = _rms(proj_ref[:, P_QD:P_QD + GROUP_W]) * _row(gq_ref, layer)
        ckv = _rms(proj_ref[:, P_KVD:P_KVD + MLA_KV_LORA]) * _row(gkv_ref, layer)
        ckv_ref[0] = ckv
        kr_pad = proj_ref[:, P_KR:P_KR + 128]
        kr_ref[0] = kr_pad.T[0:MLA_ROPE, :]
        mla["q"] = (_dot(qd.astype(BF), wqup_ref[...]) * (MLA_SCALE * LOG2E)).astype(BF)
        kvm = _dot(ckv.astype(BF), wkv_ref[...])
        kn = kvm[:, :GROUP_W].astype(BF)
        krb = kr_pad.astype(BF)
        mla["k"] = (jnp.concatenate([kn[:, :128], krb], axis=1), jnp.concatenate([kn[:, 128:], krb], axis=1))
        mla["v"] = kvm[:, GROUP_W:].astype(BF)

    def conv():
        g = proj_ref[:, P_CONV:P_CONV + GROUP_W] * jax.nn.sigmoid(proj_ref[:, P_CONV + GROUP_W:P_TOT])
        gpad_ref[0:16] = jnp.zeros((16, GROUP_W), F32)
        gpad_ref[16 + SEQ:32 + SEQ] = jnp.zeros((16, GROUP_W), F32)
        gpad_ref[16:16 + SEQ] = g
        return _conv_ln_silu(layer, gpad_ref, zsh_ref, SEQ, dw_ref, cb_ref, lng_ref, lnb_ref)

    def scores(i):
        if i < N_HEADS:
            return (_qk(_mask_heads(qa, lane, DA_QK, 2 * i), ka),
                    _qk(_mask_heads(qa, lane, DA_QK, 2 * i + 1), ka))
        if i < 2 * N_HEADS:
            hh = i - N_HEADS
            return (_qk(mla["q"][:, hh * GROUP_W:(hh + 1) * GROUP_W], mla["k"][hh // 2]),)
        return (_qk(_mask_heads(qn, lane, HEAD_DIM, i - 2 * N_HEADS), kn2),)

    outs = [jnp.zeros((SEQ, GROUP_W), F32)] * 3
    s_next = scores(0)
    o_conv = None
    for i in range(3 * N_HEADS):
        s_cur = s_next
        if i + 1 < 3 * N_HEADS:
            s_next = scores(i + 1)
        if i == 0:
            mla_prep()
        if i == N_HEADS - 1:
            o_conv = conv()
        grp, hh = divmod(i, N_HEADS)
        if grp == 0:
            e1, r1 = _softmax2_parts(s_cur[0])
            e2, r2 = _softmax2_parts(s_cur[1])
            o = _dot((e1 * r1 - e2 * (lam * r2)).astype(BF), va)
        else:
            e, r = _softmax2_parts(s_cur[0])
            o = _dot(e.astype(BF), mla["v"] if grp == 1 else vn) * r
        outs[grp] = jnp.where(_group_of(lane, HEAD_DIM) == hh, o, outs[grp])
    o_da = _head_rms(outs[0]) * _row(gsub_ref, layer) * (1.0 - lam_init)
    o_mla, o_na = outs[1], outs[2]

    mixed_ref[...] = jnp.concatenate([o_da, o_mla, o_na, o_conv], axis=1).astype(BF)


def _ctx_layer(layer, x, prm, prev, cast):
    lam_init = 0.8 - 0.6 * math.exp(-0.3 * layer)
    head_blk = pl.BlockSpec((1, None, N_HEADS, HEAD_DIM, SEQ), lambda b: (b, layer, 0, 0, 0))
    head_shp = jax.ShapeDtypeStruct((BATCH, DEPTH, N_HEADS, HEAD_DIM, SEQ), F32)
    in_specs = [pl.BlockSpec((SEQ, D_MODEL), lambda b: (b, 0)),
                _layer_spec(layer, (16, N_MOD * D_MODEL)),
                _const_spec((DEPTH, D_MODEL)),
                _layer_spec(layer, (IN_COLS, D_MODEL)),
                _const_spec((DEPTH, GROUP_W)),
                _layer_spec(layer, (GROUP_W, N_HEADS * GROUP_W)),
                _const_spec((DEPTH, MLA_KV_LORA)),
                _layer_spec(layer, (MLA_KV_LORA, 2 * GROUP_W)),
                _layer_spec(layer, (4, DA_QK)),
                _const_spec((DEPTH, GROUP_W)),
                _layer_spec(layer, (32, GROUP_W)),
                _const_spec((DEPTH, GROUP_W)),
                _const_spec((DEPTH, GROUP_W)),
                _const_spec((DEPTH, GROUP_W))]
    cast_specs = [pl.BlockSpec((DEPTH, w.shape[1] // BATCH, w.shape[2]), lambda b: (0, b, 0)) for w in cast]
    in_specs += cast_specs + [pl.BlockSpec(memory_space=pl.ANY)] * len(prev)
    out_specs = [pl.BlockSpec((SEQ, D_MODEL), lambda b: (b, 0)),
                 head_blk, head_blk,
                 pl.BlockSpec((1, None, SEQ, MLA_KV_LORA), lambda b: (b, layer, 0, 0)),
                 pl.BlockSpec((1, None, MLA_ROPE, SEQ), lambda b: (b, layer, 0, 0)),
                 head_blk, head_blk]
    out_shape = [jax.ShapeDtypeStruct((BATCH * SEQ, D_MODEL), BF),
                 head_shp, head_shp,
                 jax.ShapeDtypeStruct((BATCH, DEPTH, SEQ, MLA_KV_LORA), F32),
                 jax.ShapeDtypeStruct((BATCH, DEPTH, MLA_ROPE, SEQ), F32),
                 head_shp, head_shp] + [jax.ShapeDtypeStruct(w.shape, BF) for w in cast]
    out_specs += cast_specs
    n_in = len(in_specs) - len(prev)
    return pl.pallas_call(
        functools.partial(_ctx_kernel, layer, lam_init, len(prev), len(cast)),
        grid=(BATCH,),
        in_specs=in_specs,
        out_specs=out_specs,
        out_shape=out_shape,
        input_output_aliases={n_in + k: 1 + k for k in range(len(prev))},
        scratch_shapes=[pltpu.VMEM((SEQ, P_TOT), F32), pltpu.VMEM((SEQ + 32, GROUP_W), F32),
                        pltpu.VMEM((7, SEQ + 8, GROUP_W), F32), pltpu.VMEM((P_TOT, D_MODEL), BF)],
        compiler_params=_cparams("arbitrary"),
        name=f"ctx_layer{layer}",
    )(x, prm["mod"], prm["g_mix"], prm["w_in"], prm["g_q"], prm["w_qup"], prm["g_kv"], prm["w_kvup"],
      prm["lamv"], prm["g_sub"], prm["dw"], prm["cb"], prm["ln_g"], prm["ln_b"], *cast, *prev)


def _lat_proj_kernel(layer, x_ref, mod_ref, gmix_ref, win_ref, gq_ref, wqup_ref, gkv_ref, wkv_ref,
                     cos_ref, sin_ref,
                     daq_ref, mq_ref, naq_ref, g_ref,
                     dak_ref, dav_ref, ka_ref, kb_ref, mv_ref, nak_ref, nav_ref, wbf_ref):
    b = pl.program_id(0)

    @pl.when((b == 0) & (pl.program_id(1) == 0))
    def _():
        _stage_w_in(win_ref, wbf_ref)

    x = x_ref[...]
    sh1 = mod_ref[pl.ds(1 + b, 1), 0:D_MODEL]
    sc1 = mod_ref[pl.ds(1 + b, 1), D_MODEL:2 * D_MODEL]
    h = (_rms(x) * _row(gmix_ref, layer)) * (1.0 + sc1) + sh1
    hb = h.astype(BF)
    proj = jnp.concatenate([_qk(hb, wbf_ref[0:P_SPLIT, :]), _qk(hb, wbf_ref[P_SPLIT:P_TOT, :])], axis=1)
    cos = cos_ref[...]
    sin = sin_ref[...]

    def rope2(z):
        return jnp.concatenate([_rope(z[:, :128], cos[:, :128], sin[:, :128]),
                                _rope(z[:, 128:], cos[:, 128:], sin[:, 128:])], axis=1)

    daq_ref[0] = (rope2(proj[:, P_DAQ:P_DAQ + GROUP_W]) * (DA_SCALE * LOG2E)).astype(BF)
    dak_ref[0] = rope2(proj[:, P_DAK:P_DAK + GROUP_W]).astype(BF)
    dav_ref[0] = proj[:, P_DAV:P_DAV + GROUP_W].astype(BF)

    qd = _rms(proj[:, P_QD:P_QD + GROUP_W]) * _row(gq_ref, layer)
    qm = _dot(qd.astype(BF), wqup_ref[...])
    for hh in range(N_HEADS):
        nope = qm[:, hh * GROUP_W:hh * GROUP_W + 128]
        rope = _rope(qm[:, hh * GROUP_W + 128:(hh + 1) * GROUP_W], cos[:, :128], sin[:, :128])
        mq_ref[0, hh] = (jnp.concatenate([nope, rope], axis=1) * (MLA_SCALE * LOG2E)).astype(BF)
    ckv = _rms(proj[:, P_KVD:P_KVD + MLA_KV_LORA]) * _row(gkv_ref, layer)
    kvm = _dot(ckv.astype(BF), wkv_ref[...])
    kn = kvm[:, :GROUP_W].astype(BF)
    krb = _rope(proj[:, P_KR:P_KR + 128], cos[:, :128], sin[:, :128]).astype(BF)
    ka_ref[0] = jnp.concatenate([kn[:, :128], krb], axis=1)
    kb_ref[0] = jnp.concatenate([kn[:, 128:], krb], axis=1)
    mv_ref[0] = kvm[:, GROUP_W:].astype(BF)

    naq_ref[0] = (proj[:, P_NAQ:P_NAQ + GROUP_W] * NA_SCALE).astype(BF)
    nak_ref[0] = proj[:, P_NAK:P_NAK + GROUP_W].astype(BF)
    nav_ref[0] = proj[:, P_NAV:P_NAV + GROUP_W].astype(BF)
    g_ref[0] = proj[:, P_CONV:P_CONV + GROUP_W] * jax.nn.sigmoid(proj[:, P_CONV + GROUP_W:P_TOT])


def _lat_proj(layer, x, prm, cos_t, sin_t):
    n_t = DEC_SEQ // TM_PROJ
    row_spec = pl.BlockSpec((1, TM_PROJ, GROUP_W), lambda b, j: (b, j, 0))
    in_specs = [pl.BlockSpec((TM_PROJ, D_MODEL), lambda b, j: (b * n_t + j, 0)),
                _layer_spec(layer, (16, N_MOD * D_MODEL)),
                _const_spec((DEPTH, D_MODEL)),
                _layer_spec(layer, (IN_COLS, D_MODEL)),
                _const_spec((DEPTH, GROUP_W)),
                _layer_spec(layer, (GROUP_W, N_HEADS * GROUP_W)),
                _const_spec((DEPTH, MLA_KV_LORA)),
                _layer_spec(layer, (MLA_KV_LORA, 2 * GROUP_W)),
                pl.BlockSpec((TM_PROJ, GROUP_W), lambda b, j: (j, 0)),
                pl.BlockSpec((TM_PROJ, GROUP_W), lambda b, j: (j, 0))]
    row_shp = jax.ShapeDtypeStruct((DEC_BATCH, DEC_SEQ, GROUP_W), BF)
    out_specs = [row_spec,
                 pl.BlockSpec((1, N_HEADS, TM_PROJ, GROUP_W), lambda b, j: (b, 0, j, 0)),
                 row_spec, row_spec] + [row_spec] * 7
    out_shape = [row_shp,
                 jax.ShapeDtypeStruct((DEC_BATCH, N_HEADS, DEC_SEQ, GROUP_W), BF),
                 row_shp,
                 jax.ShapeDtypeStruct((DEC_BATCH, DEC_SEQ, GROUP_W), F32)] + [row_shp] * 7
    return pl.pallas_call(
        functools.partial(_lat_proj_kernel, layer),
        grid=(DEC_BATCH, n_t),
        in_specs=in_specs,
        out_specs=out_specs,
        out_shape=out_shape,
        scratch_shapes=[pltpu.VMEM((P_TOT, D_MODEL), BF)],
        compiler_params=_cparams("arbitrary", "arbitrary"),
        name=f"lat_proj{layer}",
    )(x, prm["mod"], prm["g_mix"], prm["w_in"], prm["g_q"], prm["w_qup"], prm["g_kv"], prm["w_kvup"],
      cos_t, sin_t)


def _lat_attn_kernel(layer, lam_init, daq_ref, mq_ref, naq_ref, g_ref,
                     dak_ref, dav_ref, ka_ref, kb_ref, mv_ref, nak_ref, nav_ref,
                     cdak_ref, cdav_ref, cka_ref, ckb_ref, cmv_ref, cnak_ref, cnav_ref,
                     nab_ref, lamv_ref, gsub_ref, dw_ref, cb_ref, lng_ref, lnb_ref,
                     mixed_ref, gpad_ref, zsh_ref, kt_ref, v_ref, nac_ref):
    t = pl.program_id(1)
    lane = _lane_ids(GROUP_W)

    @pl.when(t == 0)
    def _():
        for slot, lat in enumerate((dak_ref, ka_ref, kb_ref)):
            kt_ref[slot, :, 0:DEC_SEQ] = lat[0].T
        kt_ref[0, :, DEC_SEQ:KEYS] = cdak_ref[0].astype(BF)
        kt_ref[1, :, DEC_SEQ:KEYS] = cka_ref[0].T
        kt_ref[2, :, DEC_SEQ:KEYS] = ckb_ref[0].T
        v_ref[0, 0:DEC_SEQ, :] = dav_ref[0]
        v_ref[0, DEC_SEQ:KEYS, :] = cdav_ref[0].T.astype(BF)
        v_ref[1, 0:DEC_SEQ, :] = mv_ref[0]
        v_ref[1, DEC_SEQ:KEYS, :] = cmv_ref[0]
        nac_ref[0] = cnak_ref[0].astype(BF)
        nac_ref[1] = cnav_ref[0].T.astype(BF)

    lam = _diff_lambda(lamv_ref, lam_init)
    qa = daq_ref[0].astype(F32)
    va = v_ref[0]
    ka_t = kt_ref[0]
    vm = v_ref[1]

    kc_t = nac_ref[0]
    vc = nac_ref[1]
    n_items = 2 * N_HEADS + ROWS_PER_TILE

    def na_window(j):
        r = t * ROWS_PER_TILE + j
        start = jnp.clip(r - NA_KR // 2, 0, N_ROWS - NA_KR)
        return start - r + NA_KR - 1, pl.multiple_of(start * GRID_W, GRID_W)

    def scores(i):
        if i < N_HEADS:
            return (_dot(_mask_heads(qa, lane, DA_QK, 2 * i), ka_t),
                    _dot(_mask_heads(qa, lane, DA_QK, 2 * i + 1), ka_t))
        if i < 2 * N_HEADS:
            hh = i - N_HEADS
            return (_dot(mq_ref[0, hh], kt_ref[1 + hh // 2]),)
        j = i - 2 * N_HEADS
        dr0, koff = na_window(j)
        qrow = naq_ref[0, j * GRID_W:(j + 1) * GRID_W, :].astype(F32)
        q4 = jnp.concatenate([_mask_heads(qrow, lane, HEAD_DIM, hh) for hh in range(N_HEADS)], axis=0)
        return (_qk(q4, nak_ref[0, pl.ds(koff, NA_LOCAL), :]) + nab_ref[dr0], _dot(q4, kc_t))

    o_da = jnp.zeros((TQ, GROUP_W), F32)
    o_mla = jnp.zeros((TQ, GROUP_W), F32)
    na_rows = []
    s_next = scores(0)

    base = pl.multiple_of(t * TQ, TQ)
    gpad_ref[16:16 + TQ] = g_ref[0, pl.ds(base, TQ), :]
    lo = g_ref[0, pl.ds(pl.multiple_of(jnp.maximum(base - 16, 0), 16), 16), :]
    hi = g_ref[0, pl.ds(pl.multiple_of(jnp.minimum(base + TQ, DEC_SEQ - 16), 16), 16), :]
    gpad_ref[0:16] = jnp.where(t > 0, lo, 0.0)
    gpad_ref[16 + TQ:32 + TQ] = jnp.where(t < N_QT - 1, hi, 0.0)
    o_conv = _conv_ln_silu(layer, gpad_ref, zsh_ref, TQ, dw_ref, cb_ref, lng_ref, lnb_ref)

    for i in range(n_items):
        s_cur = s_next
        if i + 1 < n_items:
            s_next = scores(i + 1)
        if i < N_HEADS:
            e1, r1 = _softmax2_parts(s_cur[0])
            e2, r2 = _softmax2_parts(s_cur[1])
            p = (e1 - e2 * (lam * r2 / r1)).astype(BF)
            o_da = jnp.where(_group_of(lane, HEAD_DIM) == i, _dot(p, va) * r1, o_da)
        elif i < 2 * N_HEADS:
            e, r = _softmax2_parts(s_cur[0])
            o_mla = jnp.where(_group_of(lane, HEAD_DIM) == i - N_HEADS, _dot(e.astype(BF), vm) * r, o_mla)
        else:
            _, koff = na_window(i - 2 * N_HEADS)
            s_loc, s_ctx = s_cur
            m = jnp.maximum(jnp.max(s_loc, axis=-1, keepdims=True), jnp.max(s_ctx, axis=-1, keepdims=True))
            e_loc = jnp.exp(s_loc - m)
            e_ctx = jnp.exp(s_ctx - m)
            den = jnp.sum(e_loc, axis=-1, keepdims=True) + jnp.sum(e_ctx, axis=-1, keepdims=True)
            o4 = (_dot(e_loc.astype(BF), nav_ref[0, pl.ds(koff, NA_LOCAL), :])
                  + _dot(e_ctx.astype(BF), vc)) * (1.0 / den)
            o_row = o4[0:GRID_W]
            for hh in range(1, N_HEADS):
                o_row = jnp.where(_group_of(lane, HEAD_DIM) == hh, o4[hh * GRID_W:(hh + 1) * GRID_W], o_row)
            na_rows.append(o_row)
    o_da = _head_rms(o_da) * _row(gsub_ref, layer) * (1.0 - lam_init)
    o_na = jnp.concatenate(na_rows, axis=0)

    mixed_ref[...] = jnp.concatenate([o_da, o_mla, o_na, o_conv], axis=1).astype(BF)


def _lat_attn(layer, proj_outs, caches, nab, prm):
    lam_init = 0.8 - 0.6 * math.exp(-0.3 * layer)
    q_spec = pl.BlockSpec((1, TQ, GROUP_W), lambda b, t: (b, t, 0))
    full_k = pl.BlockSpec((1, DEC_SEQ, GROUP_W), lambda b, t: (b, 0, 0))
    cache_spec = pl.BlockSpec((1, None, PAST_LEN, GROUP_W), lambda b, t: (b, layer, 0, 0))
    assert PAST_LEN == GROUP_W
    in_specs = [q_spec,
                pl.BlockSpec((1, N_HEADS, TQ, GROUP_W), lambda b, t: (b, 0, t, 0)),
                q_spec, full_k] + [full_k] * 7 + [cache_spec] * 7 + [
                _layer_spec(layer, (NA_KR, N_HEADS * GRID_W, NA_LOCAL)),
                _layer_spec(layer, (4, DA_QK)),
                _const_spec((DEPTH, GROUP_W)),
                _layer_spec(layer, (32, GROUP_W)),
                _const_spec((DEPTH, GROUP_W)),
                _const_spec((DEPTH, GROUP_W)),
                _const_spec((DEPTH, GROUP_W))]
    return pl.pallas_call(
        functools.partial(_lat_attn_kernel, layer, lam_init),
        grid=(DEC_BATCH, N_QT),
        in_specs=in_specs,
        out_specs=pl.BlockSpec((TQ, D_MODEL), lambda b, t: (b * N_QT + t, 0)),
        out_shape=jax.ShapeDtypeStruct((DEC_BATCH * DEC_SEQ, D_MODEL), BF),
        scratch_shapes=[pltpu.VMEM((TQ + 32, GROUP_W), F32), pltpu.VMEM((7, TQ + 8, GROUP_W), F32),
                        pltpu.VMEM((3, GROUP_W, KEYS), BF), pltpu.VMEM((2, KEYS, GROUP_W), BF),
                        pltpu.VMEM((2, PAST_LEN, GROUP_W), BF)],
        compiler_params=_cparams("arbitrary", "arbitrary"),
        name=f"lat_attn{layer}",
    )(*proj_outs, *caches, nab, prm["lamv"], prm["g_sub"], prm["dw"], prm["cb"], prm["ln_g"], prm["ln_b"])


N_FF_CHUNKS = D_FF // D_MODEL


def _ffn_weight_copies(layer, wout_hbm, w1_hbm, w2_hbm, wout_ref, w1_ref, w2_ref, sem):
    copies = [pltpu.make_async_copy(wout_hbm.at[layer], wout_ref, sem.at[0])]
    for c in range(N_FF_CHUNKS):
        lo, hi = c * D_MODEL, (c + 1) * D_MODEL
        copies.append(pltpu.make_async_copy(w1_hbm.at[layer, :, lo:hi], w1_ref.at[:, lo:hi], sem.at[1 + 2 * c]))
        copies.append(pltpu.make_async_copy(w2_hbm.at[layer, lo:hi, :], w2_ref.at[lo:hi, :], sem.at[2 + 2 * c]))
    return copies


def _out_ffn_kernel(layer, tiles_per_mod, mod_base, x_ref, mx_ref, mod_ref, wout_hbm, gff_ref,
                    w1_hbm, w2_hbm, gfin_ref, o_ref, wout_ref, w1_ref, w2_ref, sem):
    copies = _ffn_weight_copies(layer, wout_hbm, w1_hbm, w2_hbm, wout_ref, w1_ref, w2_ref, sem)

    def body(first):
        def arrive(k):
            if first:
                copies[k].wait()

        if first:
            for cp in copies:
                cp.start()
        row = mod_base + pl.program_id(0) // tiles_per_mod
        g1 = mod_ref[pl.ds(row, 1), 2 * D_MODEL:3 * D_MODEL]
        sh2 = mod_ref[pl.ds(row, 1), 3 * D_MODEL:4 * D_MODEL]
        sc2 = mod_ref[pl.ds(row, 1), 4 * D_MODEL:5 * D_MODEL]
        g2 = mod_ref[pl.ds(row, 1), 5 * D_MODEL:6 * D_MODEL]
        arrive(0)
        x1 = x_ref[...] + g1 * _dot(mx_ref[...], wout_ref[...])
        h2 = ((_rms(x1) * _row(gff_ref, layer)) * (1.0 + sc2) + sh2).astype(BF)
        acc = jnp.zeros((TM_FFN, D_MODEL), F32)
        for c in range(N_FF_CHUNKS):
            arrive(1 + 2 * c)
            a = jnp.maximum(_dot(h2, w1_ref[:, c * D_MODEL:(c + 1) * D_MODEL]), 0.0)
            arrive(2 + 2 * c)
            acc = acc + _dot((a * a).astype(BF), w2_ref[c * D_MODEL:(c + 1) * D_MODEL, :])
        x2 = x1 + g2 * acc
        o_ref[...] = _rms(x2) * gfin_ref[...] if layer == DEPTH - 1 else x2

    @pl.when(pl.program_id(0) == 0)
    def _():
        body(True)

    @pl.when(pl.program_id(0) > 0)
    def _():
        body(False)


def _out_ffn(name, layer, tiles_per_mod, mod_base, x, mixed, prm, g_final):
    n = x.shape[0]
    tile = pl.BlockSpec((TM_FFN, D_MODEL), lambda i: (i, 0))
    return pl.pallas_call(
        functools.partial(_out_ffn_kernel, layer, tiles_per_mod, mod_base),
        grid=(n // TM_FFN,),
        in_specs=[tile, tile,
                  _layer_spec(layer, (16, N_MOD * D_MODEL)),
                  pl.BlockSpec(memory_space=pl.ANY),
                  _const_spec((DEPTH, D_MODEL)),
                  pl.BlockSpec(memory_space=pl.ANY),
                  pl.BlockSpec(memory_space=pl.ANY),
                  _const_spec((1, D_MODEL))],
        out_specs=tile,
        out_shape=jax.ShapeDtypeStruct((n, D_MODEL), F32),
        scratch_shapes=[pltpu.VMEM((D_MODEL, D_MODEL), BF), pltpu.VMEM((D_MODEL, D_FF), BF),
                        pltpu.VMEM((D_FF, D_MODEL), BF), pltpu.SemaphoreType.DMA((1 + 2 * N_FF_CHUNKS,))],
        compiler_params=_cparams("arbitrary"),
        name=name,
    )(x, mixed, prm["mod"], prm["w_out"], prm["g_ff"], prm["w_ff1"], prm["w_ff2"], g_final)


def _rope_tables():
    t = np.arange(DEC_SEQ)
    rows = (t // GRID_W).astype(np.float32)
    cols = (t % GRID_W).astype(np.float32)
    c = np.arange(GROUP_W) % 32
    freqs = np.float32(ROPE_BASE) ** (-np.arange(8, dtype=np.float32) * np.float32(2.0) / np.float32(16))
    pos = np.where((c < 16)[None, :], rows[:, None], cols[:, None]).astype(np.float32)
    ang = (pos * freqs[(c % 16) % 8][None, :]).astype(np.float32)
    first = ((c % 16) < 8)[None, :]
    cos = np.cos(ang).astype(np.float32)
    sin = np.sin(ang).astype(np.float32)
    return jnp.asarray(cos), jnp.asarray(np.where(first, -sin, sin))


def _qup_selection():
    sel = np.zeros((N_HEADS * (MLA_NOPE + MLA_ROPE), N_HEADS * GROUP_W), np.float32)
    for h in range(N_HEADS):
        src = h * (MLA_NOPE + MLA_ROPE)
        dst = h * GROUP_W + (h % 2) * MLA_NOPE
        sel[np.arange(src, src + MLA_NOPE), np.arange(dst, dst + MLA_NOPE)] = 1.0
        sel[np.arange(src + MLA_NOPE, src + MLA_NOPE + MLA_ROPE), h * GROUP_W + 128 + np.arange(MLA_ROPE)] = 1.0
    return jnp.asarray(sel)


def _kvup_selection():
    per_head = MLA_NOPE + HEAD_DIM
    k = [h * per_head + d for h in range(N_HEADS) for d in range(MLA_NOPE)]
    v = [h * per_head + MLA_NOPE + d for h in range(N_HEADS) for d in range(HEAD_DIM)]
    src = np.asarray(k + v)
    sel = np.zeros((src.size, src.size), np.float32)
    sel[src, np.arange(src.size)] = 1.0
    return jnp.asarray(sel)


def _heads_to_rows(c):
    b, l, h, s, d = c.shape
    return jnp.swapaxes(c, -1, -2).reshape(b, l, h * d, s)


def kernel(x_prompt, x_sample, c, cache_da_k, cache_da_v, cache_mla_ckv, cache_mla_krope, cache_na_k, cache_na_v, c_ctx, w_mod, b_mod, g_norm_mix, g_norm_ff, w_in, da_lambda_q1, da_lambda_k1, da_lambda_q2, da_lambda_k2, g_da_subln, g_mla_q, w_mla_qup, g_mla_kv, w_mla_kvup, na_rpb, conv_dw, conv_b, conv_ln_g, conv_ln_b, w_out, w_ff1, w_ff2, g_final):
    prm = dict(
        g_mix=g_norm_mix, w_in=jnp.swapaxes(w_in, 1, 2), g_q=g_mla_q, g_kv=g_mla_kv,
        lamv=jnp.stack([da_lambda_q1, da_lambda_k1, da_lambda_q2, da_lambda_k2], axis=1),
        g_sub=jnp.tile(g_da_subln, (1, N_HEADS)),
        dw=jnp.concatenate([conv_dw, jnp.zeros((DEPTH, 1, GROUP_W), F32)], axis=1),
        cb=conv_b, ln_g=conv_ln_g, ln_b=conv_ln_b,
        g_ff=g_norm_ff)
    cv = jnp.concatenate([c_ctx[None, :], c, jnp.zeros((16 - 1 - DEC_BATCH, D_MODEL), F32)], axis=0)
    cos_t, sin_t = _rope_tables()
    g_final2 = g_final.reshape(1, D_MODEL)

    c_kr_t = jnp.swapaxes(cache_mla_krope, -1, -2)
    prm["mod"], nab, cka, ckb, cmv, prm["w_qup"], prm["w_kvup"] = _prep(
        cv, w_mod, b_mod, na_rpb, cache_mla_ckv, c_kr_t, w_mla_qup, w_mla_kvup)
    caches = (_heads_to_rows(cache_da_k), _heads_to_rows(cache_da_v), cka, ckb, cmv,
              _heads_to_rows(cache_na_k), _heads_to_rows(cache_na_v))

    xp = x_prompt.reshape(BATCH * SEQ, D_MODEL)
    xs = x_sample.reshape(DEC_BATCH * DEC_SEQ, D_MODEL)
    new_ctx = ()
    for l in range(DEPTH):
        cast = (w_out, w_ff1, w_ff2) if l == 0 else ()
        mixed_p, *rest = _ctx_layer(l, xp, prm, tuple(new_ctx), cast)
        new_ctx = rest[:len(rest) - len(cast)]
        if cast:
            prm["w_out"], prm["w_ff1"], prm["w_ff2"] = rest[len(new_ctx):]
        xp = _out_ffn(f"ctx_ffn{l}", l, BATCH * SEQ // TM_FFN, 0, xp, mixed_p, prm, g_final2)
        proj_outs = _lat_proj(l, xs, prm, cos_t, sin_t)
        mixed_s = _lat_attn(l, proj_outs, caches, nab, prm)
        xs = _out_ffn(f"lat_ffn{l}", l, DEC_SEQ // TM_FFN, 1, xs, mixed_s, prm, g_final2)

    y_prompt = xp.reshape(BATCH, SEQ, D_MODEL)
    y_sample = xs.reshape(DEC_BATCH, DEC_SEQ, D_MODEL)
    new_ctx = [a if k == 2 else jnp.swapaxes(a, -1, -2) for k, a in enumerate(new_ctx)]
    return (y_prompt, y_sample) + tuple(new_ctx)
```

```python
import functools
import math

import numpy as np
import jax
import jax.numpy as jnp
from jax import lax
from jax.experimental import pallas as pl
from jax.experimental.pallas import tpu as pltpu

F32 = jnp.float32
BF = jnp.bfloat16

D_MODEL = 1024
BATCH = 16
SEQ = 256
DEPTH = 2
DEC_BATCH = 2
DEC_SEQ = 2048
PAST_LEN = 256
GRID_W = 64
GROUP_W = 256
HEAD_DIM = 64
N_HEADS = 4
DA_QK = 32
MLA_NOPE = 64
MLA_ROPE = 32
MLA_KV_LORA = 128
NA_KR = 8
NA_KC = 16
CONV_W = 31
D_FF = 4096
ROPE_BASE = 10000.0
EPS = 1e-6
N_MOD = 6
IN_COLS = 2464

P_DAQ, P_DAK, P_DAV, P_QD, P_KVD, P_KR = 0, 256, 512, 768, 1024, 1152
P_NAQ, P_NAK, P_NAV, P_CONV, P_TOT = 1280, 1536, 1792, 2048, 2560
P_SPLIT = P_NAQ
N_CTX_IN = 14
N_CTX_OUT = 7
KR_ORIG_END = 1184
KR_PAD = 128 - MLA_ROPE

DA_SCALE = DA_QK ** -0.5
MLA_SCALE = (MLA_NOPE + MLA_ROPE) ** -0.5
NA_SCALE = HEAD_DIM ** -0.5
LOG2E = math.log2(math.e)

TQ = 256
N_QT = DEC_SEQ // TQ
KEYS = DEC_SEQ + PAST_LEN
ROWS_PER_TILE = TQ // GRID_W
N_ROWS = DEC_SEQ // GRID_W
NA_LOCAL = NA_KR * GRID_W
TM_FFN = 512
TM_PROJ = 512
VMEM_LIMIT = 58 * 1024 * 1024

NT_DIMS = (((1,), (1,)), ((), ()))


def _cparams(*sem):
    return pltpu.CompilerParams(dimension_semantics=sem, vmem_limit_bytes=VMEM_LIMIT)


def _const_spec(shape):
    nd = len(shape)
    return pl.BlockSpec(shape, lambda *_: (0,) * nd, pipeline_mode=pl.Buffered(1))


def _layer_spec(layer, shape):
    nd = len(shape)
    return pl.BlockSpec((None,) + tuple(shape), lambda *_: (layer,) + (0,) * nd, pipeline_mode=pl.Buffered(1))


def _rms(x):
    return x * lax.rsqrt(jnp.mean(x * x, axis=-1, keepdims=True) + EPS)


def _row(ref, layer):
    return ref[layer:layer + 1, :]


def _dot(a, b):
    return jnp.dot(a, b, preferred_element_type=F32)


def _qk(q, k):
    return lax.dot_general(q, k, NT_DIMS, preferred_element_type=F32)


def _softmax2_parts(s):
    m = jnp.max(s, axis=-1, keepdims=True)
    e = jnp.exp2(s - m)
    return e, 1.0 / jnp.sum(e, axis=-1, keepdims=True)


def _lane_ids(width):
    return lax.broadcasted_iota(jnp.int32, (1, width), 1)


def _diff_lambda(lamv_ref, lam_init):
    v = lamv_ref[...]
    a = jnp.exp(jnp.sum(v[0:1] * v[1:2], axis=-1, keepdims=True))
    b = jnp.exp(jnp.sum(v[2:3] * v[3:4], axis=-1, keepdims=True))
    return a - b + lam_init


def _group_of(lane, group):
    return lane >> (group.bit_length() - 1)


def _head_rms(x):
    r = _group_of(lax.broadcasted_iota(jnp.int32, (GROUP_W, GROUP_W), 0), HEAD_DIM)
    c = _group_of(lax.broadcasted_iota(jnp.int32, (GROUP_W, GROUP_W), 1), HEAD_DIM)
    ones_bd = jnp.where(r == c, 1.0, 0.0).astype(BF)
    sq = x * x
    hi = sq.astype(BF)
    lo = (sq - hi.astype(F32)).astype(BF)
    ss = _dot(hi, ones_bd) + _dot(lo, ones_bd)
    return x * lax.rsqrt(ss * (1.0 / HEAD_DIM) + EPS)


def _rope(z, cos, sin):
    lane = _lane_ids(z.shape[1])
    swapped = jnp.where((lane & 15) < 8, pltpu.roll(z, 120, 1), pltpu.roll(z, 8, 1))
    return z * cos + swapped * sin


def _stage_w_in(wint_ref, wbt_ref):
    wbt_ref[0:KR_ORIG_END, :] = wint_ref[0:KR_ORIG_END, :].astype(BF)
    wbt_ref[KR_ORIG_END:P_SPLIT, :] = jnp.zeros((P_SPLIT - KR_ORIG_END, D_MODEL), BF)
    wbt_ref[P_SPLIT:P_TOT, :] = wint_ref[KR_ORIG_END:IN_COLS, :].astype(BF)


def _mask_heads(qf, lane, group, idx):
    return jnp.where(_group_of(lane, group) == idx, qf, 0.0).astype(BF)


def _conv_ln_silu(layer, gpad_ref, zsh_ref, n, dw_ref, cb_ref, lng_ref, lnb_ref):
    y = jnp.zeros((n, GROUP_W), F32) + _row(cb_ref, layer)
    for b in range(8):
        z = None
        for a in range(4):
            t = 8 * a + b - 1
            if 0 <= t < CONV_W:
                term = gpad_ref[pl.ds(8 * a, n + 8), :] * dw_ref[t:t + 1, :]
                z = term if z is None else z + term
        if b == 0:
            y = y + z[0:n]
        else:
            zsh_ref[b - 1] = z
            y = y + zsh_ref[b - 1, pl.ds(b, n), :]
    mu = jnp.mean(y, axis=-1, keepdims=True)
    yc = y - mu
    var = jnp.mean(yc * yc, axis=-1, keepdims=True)
    z = yc * lax.rsqrt(var + EPS) * _row(lng_ref, layer) + _row(lnb_ref, layer)
    return z * jax.nn.sigmoid(z)


def _mod_block(cv_ref, w, b_row, o_ref):
    c = cv_ref[...]
    a = (c * jax.nn.sigmoid(c)).astype(BF)
    o_ref[...] = _dot(a, w.astype(BF)) + b_row


MOD_BLOCKS = N_HEADS


def _prep_kernel(cv_ref, wmod_ref, bmod_ref, rpb_ref, ckv_ref, kr_ref, wq_ref, wkv_ref, selq_ref, selkv_ref,
                 mod_ref, nab_ref, ka_ref, kb_ref, v_ref, wqe_ref, wkvp_ref, tp_ref):
    @pl.when(pl.program_id(0) == 0)
    def _():
        _mod_block(cv_ref, wmod_ref[0], bmod_ref[0:1, :], mod_ref)

    _na_bias_kernel(rpb_ref, nab_ref, tp_ref)

    @pl.when(pl.program_id(1) < DEC_BATCH)
    def _():
        wqe_ref[0] = _dot(wq_ref[0].astype(BF), selq_ref[...].astype(BF)).astype(BF)
        w_kvup = _dot(wkv_ref[0].astype(BF), selkv_ref[...].astype(BF)).astype(BF)
        wkvp_ref[0] = w_kvup
        _mla_cache_kernel(ckv_ref, kr_ref, w_kvup, ka_ref, kb_ref, v_ref)


def _prep(cv, w_mod, b_mod, rpb, c_ckv, c_kr_t, w_qup, w_kvup):
    wblk = N_MOD * D_MODEL // MOD_BLOCKS
    mblk = lambda l, j: jnp.where(l == 0, j, MOD_BLOCKS - 1)
    cb = lambda l, j: (jnp.minimum(j, DEC_BATCH - 1), l, 0, 0)
    cache_blk = pl.BlockSpec((1, 1, PAST_LEN, GROUP_W), cb)
    cache_shp = jax.ShapeDtypeStruct((DEC_BATCH, DEPTH, PAST_LEN, GROUP_W), BF)
    per_layer = lambda shape: pl.BlockSpec((1,) + tuple(shape), lambda l, j: (l,) + (0,) * len(shape))
    selq, selkv = _qup_selection(), _kvup_selection()
    return pl.pallas_call(
        _prep_kernel,
        grid=(DEPTH, MOD_BLOCKS),
        in_specs=[pl.BlockSpec((16, D_MODEL), lambda l, j: (0, 0)),
                  pl.BlockSpec((1, D_MODEL, wblk), lambda l, j: (0, 0, mblk(l, j))),
                  pl.BlockSpec((DEPTH, wblk), lambda l, j: (0, mblk(l, j))),
                  pl.BlockSpec(memory_space=pltpu.SMEM),
                  pl.BlockSpec((1, 1, PAST_LEN, MLA_KV_LORA), cb),
                  pl.BlockSpec((1, 1, MLA_ROPE, PAST_LEN), cb),
                  per_layer(w_qup.shape[1:]), per_layer(w_kvup.shape[1:]),
                  _const_spec(selq.shape), _const_spec(selkv.shape)],
        out_specs=[pl.BlockSpec((16, wblk), lambda l, j: (0, mblk(l, j))),
                   pl.BlockSpec((1, NA_KR, GRID_W, NA_LOCAL), lambda l, j: (l, 0, j, 0)),
                   cache_blk, cache_blk, cache_blk,
                   per_layer((GROUP_W, N_HEADS * GROUP_W)), per_layer((MLA_KV_LORA, 2 * GROUP_W))],
        out_shape=[jax.ShapeDtypeStruct((16, N_MOD * D_MODEL), F32),
                   jax.ShapeDtypeStruct((DEPTH, NA_KR, N_HEADS * GRID_W, NA_LOCAL), F32),
                   cache_shp, cache_shp, cache_shp,
                   jax.ShapeDtypeStruct((DEPTH, GROUP_W, N_HEADS * GROUP_W), BF),
                   jax.ShapeDtypeStruct((DEPTH, MLA_KV_LORA, 2 * GROUP_W), BF)],
        scratch_shapes=[pltpu.VMEM((N_DR - 1, GRID_W, 128), F32)],
        compiler_params=_cparams("arbitrary", "arbitrary"),
        name="prep",
    )(cv, w_mod, b_mod, rpb.reshape(-1), c_ckv, c_kr_t, w_qup, w_kvup, selq, selkv)


def _mla_cache_kernel(ckv_ref, kr_ref, w_kvup, ka_ref, kb_ref, v_ref):
    kvm = _dot(ckv_ref[0, 0].astype(BF), w_kvup)
    kr_t = jnp.concatenate([kr_ref[0, 0], jnp.zeros((KR_PAD, PAST_LEN), F32)], axis=0)
    kr = kr_t.T.astype(BF)
    kn = kvm[:, :GROUP_W].astype(BF)
    ka_ref[0, 0] = jnp.concatenate([kn[:, :128], kr], axis=1)
    kb_ref[0, 0] = jnp.concatenate([kn[:, 128:], kr], axis=1)
    v_ref[0, 0] = kvm[:, GROUP_W:].astype(BF)


N_DR = 2 * NA_KR - 1
N_DC = 2 * NA_KC - 1


def _na_bias_kernel(rpb_ref, o_ref, tp_ref):
    base = (pl.program_id(0) * N_HEADS + pl.program_id(1)) * (N_DR * N_DC)
    cq = lax.broadcasted_iota(jnp.int32, (GRID_W, 128), 0)
    lane = lax.broadcasted_iota(jnp.int32, (GRID_W, 128), 1)
    ck = lane & (GRID_W - 1)
    upper = lane >= GRID_W
    cstart = jnp.clip(cq - NA_KC // 2, 0, GRID_W - NA_KC)
    ok = (ck >= cstart) & (ck < cstart + NA_KC)

    j = lax.broadcasted_iota(jnp.int32, (8, 128), 1)
    dclip = jnp.clip(jnp.where(j < GRID_W, j, j - 128), -(NA_KC - 1), NA_KC - 1) + NA_KC - 1

    def toeplitz(dr):
        u = jnp.zeros((8, 128), F32)
        for d in range(N_DC):
            u = jnp.where(dclip == d, rpb_ref[base + dr * N_DC + d], u)
        rows = jnp.broadcast_to(u[0:1], (GRID_W, 128))
        return pltpu.roll(rows, 0, 1, stride=1, stride_axis=0)

    tabs = [toeplitz(dr) for dr in range(N_DR)]
    for dr in range(N_DR - 1):
        val = jnp.where(upper, pltpu.roll(tabs[dr + 1], GRID_W, 1), tabs[dr])
        tp_ref[dr] = jnp.where(ok, val, -jnp.inf)
    for dr0 in range(NA_KR):
        o_ref[0, dr0] = jnp.concatenate([tp_ref[dr0 + 2 * m] for m in range(NA_KR // 2)], axis=1)


def _ctx_kernel(layer, lam_init, n_prev, n_cast, *refs):
    (x_ref, mod_ref, gmix_ref, win_ref, gq_ref, wqup_ref, gkv_ref, wkv_ref,
     lamv_ref, gsub_ref, dw_ref, cb_ref, lng_ref, lnb_ref) = refs[:N_CTX_IN]
    cast_in = refs[N_CTX_IN:N_CTX_IN + n_cast]
    outs = refs[N_CTX_IN + n_cast + n_prev:]
    mixed_ref, dak_ref, dav_ref, ckv_ref, kr_ref, nak_ref, nav_ref = outs[:N_CTX_OUT]
    cast_out = outs[N_CTX_OUT:N_CTX_OUT + n_cast]
    proj_ref, gpad_ref, zsh_ref, wbf_ref = outs[N_CTX_OUT + n_cast:]

    for src, dst in zip(cast_in, cast_out):
        dst[...] = src[...].astype(BF)

    @pl.when(pl.program_id(0) == 0)
    def _():
        _stage_w_in(win_ref, wbf_ref)

    x = x_ref[...]
    sh1 = mod_ref[0:1, 0:D_MODEL]
    sc1 = mod_ref[0:1, D_MODEL:2 * D_MODEL]
    h = ((_rms(x) * _row(gmix_ref, layer)) * (1.0 + sc1) + sh1).astype(BF)
    proj_ref[:, 0:P_SPLIT] = _qk(h, wbf_ref[0:P_SPLIT, :])
    proj_ref[:, P_SPLIT:P_TOT] = _qk(h, wbf_ref[P_SPLIT:P_TOT, :])

    for ref, col in ((dak_ref, P_DAK), (dav_ref, P_DAV), (nak_ref, P_NAK), (nav_ref, P_NAV)):
        slab_t = proj_ref[:, col:col + GROUP_W].T
        for hh in range(N_HEADS):
            ref[0, hh] = slab_t[hh * HEAD_DIM:(hh + 1) * HEAD_DIM, :]

    lane = _lane_ids(GROUP_W)

    lam = _diff_lambda(lamv_ref, lam_init)

    qa = proj_ref[:, P_DAQ:P_DAQ + GROUP_W] * (DA_SCALE * LOG2E)
    ka = proj_ref[:, P_DAK:P_DAK + GROUP_W].astype(BF)
    va = proj_ref[:, P_DAV:P_DAV + GROUP_W].astype(BF)
    qn = proj_ref[:, P_NAQ:P_NAQ + GROUP_W] * (NA_SCALE * LOG2E)
    kn2 = proj_ref[:, P_NAK:P_NAK + GROUP_W].astype(BF)
    vn = proj_ref[:, P_NAV:P_NAV + GROUP_W].astype(BF)
    mla = {}

    def mla_prep():
        qd = _rms(proj_ref[:, P_QD:P_QD + GROUP_W]) * _row(gq_ref, layer)
        ckv = _rms(proj_ref[:, P_KVD:P_KVD + MLA_KV_LORA]) * _row(gkv_ref, layer)
        ckv_ref[0] = ckv
        kr_pad = proj_ref[:, P_KR:P_KR + 128]
        kr_ref[0] = kr_pad.T[0:MLA_ROPE, :]
        mla["q"] = (_dot(qd.astype(BF), wqup_ref[...]) * (MLA_SCALE * LOG2E)).astype(BF)
        kvm = _dot(ckv.astype(BF), wkv_ref[...])
        kn = kvm[:, :GROUP_W].astype(BF)
        krb = kr_pad.astype(BF)
        mla["k"] = (jnp.concatenate([kn[:, :128], krb], axis=1), jnp.concatenate([kn[:, 128:], krb], axis=1))
        mla["v"] = kvm[:, GROUP_W:].astype(BF)

    def conv():
        g = proj_ref[:, P_CONV:P_CONV + GROUP_W] * jax.nn.sigmoid(proj_ref[:, P_CONV + GROUP_W:P_TOT])
        gpad_ref[0:16] = jnp.zeros((16, GROUP_W), F32)
        gpad_ref[16 + SEQ:32 + SEQ] = jnp.zeros((16, GROUP_W), F32)
        gpad_ref[16:16 + SEQ] = g
        return _conv_ln_silu(layer, gpad_ref, zsh_ref, SEQ, dw_ref, cb_ref, lng_ref, lnb_ref)

    def scores(i):
        if i < N_HEADS:
            return (_qk(_mask_heads(qa, lane, DA_QK, 2 * i), ka),
                    _qk(_mask_heads(qa, lane, DA_QK, 2 * i + 1), ka))
        if i < 2 * N_HEADS:
            hh = i - N_HEADS
            return (_qk(mla["q"][:, hh * GROUP_W:(hh + 1) * GROUP_W], mla["k"][hh // 2]),)
        return (_qk(_mask_heads(qn, lane, HEAD_DIM, i - 2 * N_HEADS), kn2),)

    outs = [jnp.zeros((SEQ, GROUP_W), F32)] * 3
    s_next = scores(0)
    o_conv = None
    for i in range(3 * N_HEADS):
        s_cur = s_next
        if i + 1 < 3 * N_HEADS:
            s_next = scores(i + 1)
        if i == 0:
            mla_prep()
        if i == N_HEADS - 1:
            o_conv = conv()
        grp, hh = divmod(i, N_HEADS)
        if grp == 0:
            e1, r1 = _softmax2_parts(s_cur[0])
            e2, r2 = _softmax2_parts(s_cur[1])
            o = _dot((e1 * r1 - e2 * (lam * r2)).astype(BF), va)
        else:
            e, r = _softmax2_parts(s_cur[0])
            o = _dot(e.astype(BF), mla["v"] if grp == 1 else vn) * r
        outs[grp] = jnp.where(_group_of(lane, HEAD_DIM) == hh, o, outs[grp])
    o_da = _head_rms(outs[0]) * _row(gsub_ref, layer) * (1.0 - lam_init)
    o_mla, o_na = outs[1], outs[2]

    mixed_ref[...] = jnp.concatenate([o_da, o_mla, o_na, o_conv], axis=1).astype(BF)


def _ctx_layer(layer, x, prm, prev, cast):
    lam_init = 0.8 - 0.6 * math.exp(-0.3 * layer)
    head_blk = pl.BlockSpec((1, None, N_HEADS, HEAD_DIM, SEQ), lambda b: (b, layer, 0, 0, 0))
    head_shp = jax.ShapeDtypeStruct((BATCH, DEPTH, N_HEADS, HEAD_DIM, SEQ), F32)
    in_specs = [pl.BlockSpec((SEQ, D_MODEL), lambda b: (b, 0)),
                _const_spec((16, N_MOD * D_MODEL)),
                _const_spec((DEPTH, D_MODEL)),
                _layer_spec(layer, (IN_COLS, D_MODEL)),
                _const_spec((DEPTH, GROUP_W)),
                _layer_spec(layer, (GROUP_W, N_HEADS * GROUP_W)),
                _const_spec((DEPTH, MLA_KV_LORA)),
                _layer_spec(layer, (MLA_KV_LORA, 2 * GROUP_W)),
                _layer_spec(layer, (4, DA_QK)),
                _const_spec((DEPTH, GROUP_W)),
                _layer_spec(layer, (32, GROUP_W)),
                _const_spec((DEPTH, GROUP_W)),
                _const_spec((DEPTH, GROUP_W)),
                _const_spec((DEPTH, GROUP_W))]
    cast_specs = [pl.BlockSpec((DEPTH, w.shape[1] // BATCH, w.shape[2]), lambda b: (0, b, 0)) for w in cast]
    in_specs += cast_specs + [pl.BlockSpec(memory_space=pl.ANY)] * len(prev)
    out_specs = [pl.BlockSpec((SEQ, D_MODEL), lambda b: (b, 0)),
                 head_blk, head_blk,
                 pl.BlockSpec((1, None, SEQ, MLA_KV_LORA), lambda b: (b, layer, 0, 0)),
                 pl.BlockSpec((1, None, MLA_ROPE, SEQ), lambda b: (b, layer, 0, 0)),
                 head_blk, head_blk]
    out_shape = [jax.ShapeDtypeStruct((BATCH * SEQ, D_MODEL), BF),
                 head_shp, head_shp,
                 jax.ShapeDtypeStruct((BATCH, DEPTH, SEQ, MLA_KV_LORA), F32),
                 jax.ShapeDtypeStruct((BATCH, DEPTH, MLA_ROPE, SEQ), F32),
                 head_shp, head_shp] + [jax.ShapeDtypeStruct(w.shape, BF) for w in cast]
    out_specs += cast_specs
    n_in = len(in_specs) - len(prev)
    return pl.pallas_call(
        functools.partial(_ctx_kernel, layer, lam_init, len(prev), len(cast)),
        grid=(BATCH,),
        in_specs=in_specs,
        out_specs=out_specs,
        out_shape=out_shape,
        input_output_aliases={n_in + k: 1 + k for k in range(len(prev))},
        scratch_shapes=[pltpu.VMEM((SEQ, P_TOT), F32), pltpu.VMEM((SEQ + 32, GROUP_W), F32),
                        pltpu.VMEM((7, SEQ + 8, GROUP_W), F32), pltpu.VMEM((P_TOT, D_MODEL), BF)],
        compiler_params=_cparams("arbitrary"),
        name=f"ctx_layer{layer}",
    )(x, prm["mod"], prm["g_mix"], prm["w_in"], prm["g_q"], prm["w_qup"], prm["g_kv"], prm["w_kvup"],
      prm["lamv"], prm["g_sub"], prm["dw"], prm["cb"], prm["ln_g"], prm["ln_b"], *cast, *prev)


def _lat_proj_kernel(layer, x_ref, mod_ref, gmix_ref, win_ref, gq_ref, wqup_ref, gkv_ref, wkv_ref,
                     cos_ref, sin_ref,
                     daq_ref, mq_ref, naq_ref, g_ref,
                     dak_ref, dav_ref, ka_ref, kb_ref, mv_ref, nak_ref, nav_ref, wbf_ref):
    b = pl.program_id(0)

    @pl.when((b == 0) & (pl.program_id(1) == 0))
    def _():
        _stage_w_in(win_ref, wbf_ref)

    x = x_ref[...]
    sh1 = mod_ref[pl.ds(1 + b, 1), 0:D_MODEL]
    sc1 = mod_ref[pl.ds(1 + b, 1), D_MODEL:2 * D_MODEL]
    h = (_rms(x) * _row(gmix_ref, layer)) * (1.0 + sc1) + sh1
    hb = h.astype(BF)
    proj = jnp.concatenate([_qk(hb, wbf_ref[0:P_SPLIT, :]), _qk(hb, wbf_ref[P_SPLIT:P_TOT, :])], axis=1)
    cos = cos_ref[...]
    sin = sin_ref[...]

    def rope2(z):
        return jnp.concatenate([_rope(z[:, :128], cos[:, :128], sin[:, :128]),
                                _rope(z[:, 128:], cos[:, 128:], sin[:, 128:])], axis=1)

    daq_ref[0] = (rope2(proj[:, P_DAQ:P_DAQ + GROUP_W]) * (DA_SCALE * LOG2E)).astype(BF)
    dak_ref[0] = rope2(proj[:, P_DAK:P_DAK + GROUP_W]).astype(BF)
    dav_ref[0] = proj[:, P_DAV:P_DAV + GROUP_W].astype(BF)

    qd = _rms(proj[:, P_QD:P_QD + GROUP_W]) * _row(gq_ref, layer)
    qm = _dot(qd.astype(BF), wqup_ref[...])
    for hh in range(N_HEADS):
        nope = qm[:, hh * GROUP_W:hh * GROUP_W + 128]
        rope = _rope(qm[:, hh * GROUP_W + 128:(hh + 1) * GROUP_W], cos[:, :128], sin[:, :128])
        mq_ref[0, hh] = (jnp.concatenate([nope, rope], axis=1) * (MLA_SCALE * LOG2E)).astype(BF)
    ckv = _rms(proj[:, P_KVD:P_KVD + MLA_KV_LORA]) * _row(gkv_ref, layer)
    kvm = _dot(ckv.astype(BF), wkv_ref[...])
    kn = kvm[:, :GROUP_W].astype(BF)
    krb = _rope(proj[:, P_KR:P_KR + 128], cos[:, :128], sin[:, :128]).astype(BF)
    ka_ref[0] = jnp.concatenate([kn[:, :128], krb], axis=1)
    kb_ref[0] = jnp.concatenate([kn[:, 128:], krb], axis=1)
    mv_ref[0] = kvm[:, GROUP_W:].astype(BF)

    naq_ref[0] = (proj[:, P_NAQ:P_NAQ + GROUP_W] * NA_SCALE).astype(BF)
    nak_ref[0] = proj[:, P_NAK:P_NAK + GROUP_W].astype(BF)
    nav_ref[0] = proj[:, P_NAV:P_NAV + GROUP_W].astype(BF)
    g_ref[0] = proj[:, P_CONV:P_CONV + GROUP_W] * jax.nn.sigmoid(proj[:, P_CONV + GROUP_W:P_TOT])


def _lat_proj(layer, x, prm, cos_t, sin_t):
    n_t = DEC_SEQ // TM_PROJ
    row_spec = pl.BlockSpec((1, TM_PROJ, GROUP_W), lambda b, j: (b, j, 0))
    in_specs = [pl.BlockSpec((TM_PROJ, D_MODEL), lambda b, j: (b * n_t + j, 0)),
                _const_spec((16, N_MOD * D_MODEL)),
                _const_spec((DEPTH, D_MODEL)),
                _layer_spec(layer, (IN_COLS, D_MODEL)),
                _const_spec((DEPTH, GROUP_W)),
                _layer_spec(layer, (GROUP_W, N_HEADS * GROUP_W)),
                _const_spec((DEPTH, MLA_KV_LORA)),
                _layer_spec(layer, (MLA_KV_LORA, 2 * GROUP_W)),
                pl.BlockSpec((TM_PROJ, GROUP_W), lambda b, j: (j, 0)),
                pl.BlockSpec((TM_PROJ, GROUP_W), lambda b, j: (j, 0))]
    row_shp = jax.ShapeDtypeStruct((DEC_BATCH, DEC_SEQ, GROUP_W), BF)
    out_specs = [row_spec,
                 pl.BlockSpec((1, N_HEADS, TM_PROJ, GROUP_W), lambda b, j: (b, 0, j, 0)),
                 row_spec, row_spec] + [row_spec] * 7
    out_shape = [row_shp,
                 jax.ShapeDtypeStruct((DEC_BATCH, N_HEADS, DEC_SEQ, GROUP_W), BF),
                 row_shp,
                 jax.ShapeDtypeStruct((DEC_BATCH, DEC_SEQ, GROUP_W), F32)] + [row_shp] * 7
    return pl.pallas_call(
        functools.partial(_lat_proj_kernel, layer),
        grid=(DEC_BATCH, n_t),
        in_specs=in_specs,
        out_specs=out_specs,
        out_shape=out_shape,
        scratch_shapes=[pltpu.VMEM((P_TOT, D_MODEL), BF)],
        compiler_params=_cparams("arbitrary", "arbitrary"),
        name=f"lat_proj{layer}",
    )(x, prm["mod"], prm["g_mix"], prm["w_in"], prm["g_q"], prm["w_qup"], prm["g_kv"], prm["w_kvup"],
      cos_t, sin_t)


N_LAT_ATTN_IN = 25


def _lat_attn_kernel(layer, lam_init, emit_mod, *refs):
    (daq_ref, mq_ref, naq_ref, g_ref,
     dak_ref, dav_ref, ka_ref, kb_ref, mv_ref, nak_ref, nav_ref,
     cdak_ref, cdav_ref, cka_ref, ckb_ref, cmv_ref, cnak_ref, cnav_ref,
     nab_ref, lamv_ref, gsub_ref, dw_ref, cb_ref, lng_ref, lnb_ref) = refs[:N_LAT_ATTN_IN]
    refs = refs[N_LAT_ATTN_IN:]
    if emit_mod:
        cv_ref, wmod_ref, bmod_ref, mixed_ref, modn_ref = refs[:5]
        _mod_block(cv_ref, wmod_ref[...], bmod_ref[layer + 1:layer + 2, :], modn_ref)
    else:
        mixed_ref = refs[0]
    gpad_ref, zsh_ref, kt_ref, v_ref, nac_ref = refs[-5:]
    t = pl.program_id(1)
    lane = _lane_ids(GROUP_W)

    @pl.when(t == 0)
    def _():
        for slot, lat in enumerate((dak_ref, ka_ref, kb_ref)):
            kt_ref[slot, :, 0:DEC_SEQ] = lat[0].T
        kt_ref[0, :, DEC_SEQ:KEYS] = cdak_ref[0].astype(BF)
        kt_ref[1, :, DEC_SEQ:KEYS] = cka_ref[0].T
        kt_ref[2, :, DEC_SEQ:KEYS] = ckb_ref[0].T
        v_ref[0, 0:DEC_SEQ, :] = dav_ref[0]
        v_ref[0, DEC_SEQ:KEYS, :] = cdav_ref[0].T.astype(BF)
        v_ref[1, 0:DEC_SEQ, :] = mv_ref[0]
        v_ref[1, DEC_SEQ:KEYS, :] = cmv_ref[0]
        nac_ref[0] = cnak_ref[0].astype(BF)
        nac_ref[1] = cnav_ref[0].T.astype(BF)

    lam = _diff_lambda(lamv_ref, lam_init)
    qa = daq_ref[0].astype(F32)
    va = v_ref[0]
    ka_t = kt_ref[0]
    vm = v_ref[1]

    kc_t = nac_ref[0]
    vc = nac_ref[1]
    n_items = 2 * N_HEADS + ROWS_PER_TILE

    def na_window(j):
        r = t * ROWS_PER_TILE + j
        start = jnp.clip(r - NA_KR // 2, 0, N_ROWS - NA_KR)
        return start - r + NA_KR - 1, pl.multiple_of(start * GRID_W, GRID_W)

    def scores(i):
        if i < N_HEADS:
            return (_dot(_mask_heads(qa, lane, DA_QK, 2 * i), ka_t),
                    _dot(_mask_heads(qa, lane, DA_QK, 2 * i + 1), ka_t))
        if i < 2 * N_HEADS:
            hh = i - N_HEADS
            return (_dot(mq_ref[0, hh], kt_ref[1 + hh // 2]),)
        j = i - 2 * N_HEADS
        dr0, koff = na_window(j)
        qrow = naq_ref[0, j * GRID_W:(j + 1) * GRID_W, :].astype(F32)
        q4 = jnp.concatenate([_mask_heads(qrow, lane, HEAD_DIM, hh) for hh in range(N_HEADS)], axis=0)
        return (_qk(q4, nak_ref[0, pl.ds(koff, NA_LOCAL), :]) + nab_ref[dr0], _dot(q4, kc_t))

    o_da = jnp.zeros((TQ, GROUP_W), F32)
    o_mla = jnp.zeros((TQ, GROUP_W), F32)
    na_rows = []
    s_next = scores(0)

    base = pl.multiple_of(t * TQ, TQ)
    gpad_ref[16:16 + TQ] = g_ref[0, pl.ds(base, TQ), :]
    lo = g_ref[0, pl.ds(pl.multiple_of(jnp.maximum(base - 16, 0), 16), 16), :]
    hi = g_ref[0, pl.ds(pl.multiple_of(jnp.minimum(base + TQ, DEC_SEQ - 16), 16), 16), :]
    gpad_ref[0:16] = jnp.where(t > 0, lo, 0.0)
    gpad_ref[16 + TQ:32 + TQ] = jnp.where(t < N_QT - 1, hi, 0.0)
    o_conv = _conv_ln_silu(layer, gpad_ref, zsh_ref, TQ, dw_ref, cb_ref, lng_ref, lnb_ref)

    for i in range(n_items):
        s_cur = s_next
        if i + 1 < n_items:
            s_next = scores(i + 1)
        if i < N_HEADS:
            e1, r1 = _softmax2_parts(s_cur[0])
            e2, r2 = _softmax2_parts(s_cur[1])
            p = (e1 - e2 * (lam * r2 / r1)).astype(BF)
            o_da = jnp.where(_group_of(lane, HEAD_DIM) == i, _dot(p, va) * r1, o_da)
        elif i < 2 * N_HEADS:
            e, r = _softmax2_parts(s_cur[0])
            o_mla = jnp.where(_group_of(lane, HEAD_DIM) == i - N_HEADS, _dot(e.astype(BF), vm) * r, o_mla)
        else:
            _, koff = na_window(i - 2 * N_HEADS)
            s_loc, s_ctx = s_cur
            m = jnp.maximum(jnp.max(s_loc, axis=-1, keepdims=True), jnp.max(s_ctx, axis=-1, keepdims=True))
            e_loc = jnp.exp(s_loc - m)
            e_ctx = jnp.exp(s_ctx - m)
            den = jnp.sum(e_loc, axis=-1, keepdims=True) + jnp.sum(e_ctx, axis=-1, keepdims=True)
            o4 = (_dot(e_loc.astype(BF), nav_ref[0, pl.ds(koff, NA_LOCAL), :])
                  + _dot(e_ctx.astype(BF), vc)) * (1.0 / den)
            o_row = o4[0:GRID_W]
            for hh in range(1, N_HEADS):
                o_row = jnp.where(_group_of(lane, HEAD_DIM) == hh, o4[hh * GRID_W:(hh + 1) * GRID_W], o_row)
            na_rows.append(o_row)
    o_da = _head_rms(o_da) * _row(gsub_ref, layer) * (1.0 - lam_init)
    o_na = jnp.concatenate(na_rows, axis=0)

    mixed_ref[...] = jnp.concatenate([o_da, o_mla, o_na, o_conv], axis=1).astype(BF)


def _lat_attn(layer, proj_outs, caches, nab, prm, next_mod):
    lam_init = 0.8 - 0.6 * math.exp(-0.3 * layer)
    n_steps = DEC_BATCH * N_QT
    mblk = N_MOD * D_MODEL // n_steps
    step = lambda b, t: b * N_QT + t
    mixed_spec = pl.BlockSpec((TQ, D_MODEL), lambda b, t: (step(b, t), 0))
    mixed_shape = jax.ShapeDtypeStruct((DEC_BATCH * DEC_SEQ, D_MODEL), BF)
    mod_in, out_specs, out_shape = [], mixed_spec, mixed_shape
    if next_mod:
        mod_in = [pl.BlockSpec((16, D_MODEL), lambda b, t: (0, 0)),
                  pl.BlockSpec((None, D_MODEL, mblk), lambda b, t: (layer + 1, 0, step(b, t))),
                  pl.BlockSpec((DEPTH, mblk), lambda b, t: (0, step(b, t)))]
        out_specs = [mixed_spec, pl.BlockSpec((16, mblk), lambda b, t: (0, step(b, t)))]
        out_shape = [mixed_shape, jax.ShapeDtypeStruct((16, N_MOD * D_MODEL), F32)]
    q_spec = pl.BlockSpec((1, TQ, GROUP_W), lambda b, t: (b, t, 0))
    full_k = pl.BlockSpec((1, DEC_SEQ, GROUP_W), lambda b, t: (b, 0, 0))
    cache_spec = pl.BlockSpec((1, None, PAST_LEN, GROUP_W), lambda b, t: (b, layer, 0, 0))
    assert PAST_LEN == GROUP_W
    in_specs = [q_spec,
                pl.BlockSpec((1, N_HEADS, TQ, GROUP_W), lambda b, t: (b, 0, t, 0)),
                q_spec, full_k] + [full_k] * 7 + [cache_spec] * 7 + [
                _layer_spec(layer, (NA_KR, N_HEADS * GRID_W, NA_LOCAL)),
                _layer_spec(layer, (4, DA_QK)),
                _const_spec((DEPTH, GROUP_W)),
                _layer_spec(layer, (32, GROUP_W)),
                _const_spec((DEPTH, GROUP_W)),
                _const_spec((DEPTH, GROUP_W)),
                _const_spec((DEPTH, GROUP_W))] + mod_in
    assert len(in_specs) == N_LAT_ATTN_IN + len(mod_in)
    return pl.pallas_call(
        functools.partial(_lat_attn_kernel, layer, lam_init, bool(next_mod)),
        grid=(DEC_BATCH, N_QT),
        in_specs=in_specs,
        out_specs=out_specs,
        out_shape=out_shape,
        scratch_shapes=[pltpu.VMEM((TQ + 32, GROUP_W), F32), pltpu.VMEM((7, TQ + 8, GROUP_W), F32),
                        pltpu.VMEM((3, GROUP_W, KEYS), BF), pltpu.VMEM((2, KEYS, GROUP_W), BF),
                        pltpu.VMEM((2, PAST_LEN, GROUP_W), BF)],
        compiler_params=_cparams("arbitrary", "arbitrary"),
        name=f"lat_attn{layer}",
    )(*proj_outs, *caches, nab, prm["lamv"], prm["g_sub"], prm["dw"], prm["cb"], prm["ln_g"], prm["ln_b"], *next_mod)


N_FF_CHUNKS = D_FF // D_MODEL


def _ffn_weight_copies(layer, wout_hbm, w1_hbm, w2_hbm, wout_ref, w1_ref, w2_ref, sem):
    copies = [pltpu.make_async_copy(wout_hbm.at[layer], wout_ref, sem.at[0])]
    for c in range(N_FF_CHUNKS):
        lo, hi = c * D_MODEL, (c + 1) * D_MODEL
        copies.append(pltpu.make_async_copy(w1_hbm.at[layer, :, lo:hi], w1_ref.at[:, lo:hi], sem.at[1 + 2 * c]))
        copies.append(pltpu.make_async_copy(w2_hbm.at[layer, lo:hi, :], w2_ref.at[lo:hi, :], sem.at[2 + 2 * c]))
    return copies


def _out_ffn_kernel(layer, tiles_per_mod, mod_base, x_ref, mx_ref, mod_ref, wout_hbm, gff_ref,
                    w1_hbm, w2_hbm, gfin_ref, o_ref, wout_ref, w1_ref, w2_ref, sem):
    copies = _ffn_weight_copies(layer, wout_hbm, w1_hbm, w2_hbm, wout_ref, w1_ref, w2_ref, sem)

    def body(first):
        def arrive(k):
            if first:
                copies[k].wait()

        if first:
            for cp in copies:
                cp.start()
        row = mod_base + pl.program_id(0) // tiles_per_mod
        g1 = mod_ref[pl.ds(row, 1), 2 * D_MODEL:3 * D_MODEL]
        sh2 = mod_ref[pl.ds(row, 1), 3 * D_MODEL:4 * D_MODEL]
        sc2 = mod_ref[pl.ds(row, 1), 4 * D_MODEL:5 * D_MODEL]
        g2 = mod_ref[pl.ds(row, 1), 5 * D_MODEL:6 * D_MODEL]
        arrive(0)
        x1 = x_ref[...] + g1 * _dot(mx_ref[...], wout_ref[...])
        h2 = ((_rms(x1) * _row(gff_ref, layer)) * (1.0 + sc2) + sh2).astype(BF)
        acc = jnp.zeros((TM_FFN, D_MODEL), F32)
        for c in range(N_FF_CHUNKS):
            arrive(1 + 2 * c)
            a = jnp.maximum(_dot(h2, w1_ref[:, c * D_MODEL:(c + 1) * D_MODEL]), 0.0)
            arrive(2 + 2 * c)
            acc = acc + _dot((a * a).astype(BF), w2_ref[c * D_MODEL:(c + 1) * D_MODEL, :])
        x2 = x1 + g2 * acc
        o_ref[...] = _rms(x2) * gfin_ref[...] if layer == DEPTH - 1 else x2

    @pl.when(pl.program_id(0) == 0)
    def _():
        body(True)

    @pl.when(pl.program_id(0) > 0)
    def _():
        body(False)


def _out_ffn(name, layer, tiles_per_mod, mod_base, x, mixed, prm, g_final):
    n = x.shape[0]
    tile = pl.BlockSpec((TM_FFN, D_MODEL), lambda i: (i, 0))
    return pl.pallas_call(
        functools.partial(_out_ffn_kernel, layer, tiles_per_mod, mod_base),
        grid=(n // TM_FFN,),
        in_specs=[tile, tile,
                  _const_spec((16, N_MOD * D_MODEL)),
                  pl.BlockSpec(memory_space=pl.ANY),
                  _const_spec((DEPTH, D_MODEL)),
                  pl.BlockSpec(memory_space=pl.ANY),
                  pl.BlockSpec(memory_space=pl.ANY),
                  _const_spec((1, D_MODEL))],
        out_specs=tile,
        out_shape=jax.ShapeDtypeStruct((n, D_MODEL), F32),
        scratch_shapes=[pltpu.VMEM((D_MODEL, D_MODEL), BF), pltpu.VMEM((D_MODEL, D_FF), BF),
                        pltpu.VMEM((D_FF, D_MODEL), BF), pltpu.SemaphoreType.DMA((1 + 2 * N_FF_CHUNKS,))],
        compiler_params=_cparams("arbitrary"),
        name=name,
    )(x, mixed, prm["mod"], prm["w_out"], prm["g_ff"], prm["w_ff1"], prm["w_ff2"], g_final)


def _rope_tables():
    t = np.arange(DEC_SEQ)
    rows = (t // GRID_W).astype(np.float32)
    cols = (t % GRID_W).astype(np.float32)
    c = np.arange(GROUP_W) % 32
    freqs = np.float32(ROPE_BASE) ** (-np.arange(8, dtype=np.float32) * np.float32(2.0) / np.float32(16))
    pos = np.where((c < 16)[None, :], rows[:, None], cols[:, None]).astype(np.float32)
    ang = (pos * freqs[(c % 16) % 8][None, :]).astype(np.float32)
    first = ((c % 16) < 8)[None, :]
    cos = np.cos(ang).astype(np.float32)
    sin = np.sin(ang).astype(np.float32)
    return jnp.asarray(cos), jnp.asarray(np.where(first, -sin, sin))


def _qup_selection():
    sel = np.zeros((N_HEADS * (MLA_NOPE + MLA_ROPE), N_HEADS * GROUP_W), np.float32)
    for h in range(N_HEADS):
        src = h * (MLA_NOPE + MLA_ROPE)
        dst = h * GROUP_W + (h % 2) * MLA_NOPE
        sel[np.arange(src, src + MLA_NOPE), np.arange(dst, dst + MLA_NOPE)] = 1.0
        sel[np.arange(src + MLA_NOPE, src + MLA_NOPE + MLA_ROPE), h * GROUP_W + 128 + np.arange(MLA_ROPE)] = 1.0
    return jnp.asarray(sel)


def _kvup_selection():
    per_head = MLA_NOPE + HEAD_DIM
    k = [h * per_head + d for h in range(N_HEADS) for d in range(MLA_NOPE)]
    v = [h * per_head + MLA_NOPE + d for h in range(N_HEADS) for d in range(HEAD_DIM)]
    src = np.asarray(k + v)
    sel = np.zeros((src.size, src.size), np.float32)
    sel[src, np.arange(src.size)] = 1.0
    return jnp.asarray(sel)


def _heads_to_rows(c):
    b, l, h, s, d = c.shape
    return jnp.swapaxes(c, -1, -2).reshape(b, l, h * d, s)


def kernel(x_prompt, x_sample, c, cache_da_k, cache_da_v, cache_mla_ckv, cache_mla_krope, cache_na_k, cache_na_v, c_ctx, w_mod, b_mod, g_norm_mix, g_norm_ff, w_in, da_lambda_q1, da_lambda_k1, da_lambda_q2, da_lambda_k2, g_da_subln, g_mla_q, w_mla_qup, g_mla_kv, w_mla_kvup, na_rpb, conv_dw, conv_b, conv_ln_g, conv_ln_b, w_out, w_ff1, w_ff2, g_final):
    prm = dict(
        g_mix=g_norm_mix, w_in=jnp.swapaxes(w_in, 1, 2), g_q=g_mla_q, g_kv=g_mla_kv,
        lamv=jnp.stack([da_lambda_q1, da_lambda_k1, da_lambda_q2, da_lambda_k2], axis=1),
        g_sub=jnp.tile(g_da_subln, (1, N_HEADS)),
        dw=jnp.concatenate([conv_dw, jnp.zeros((DEPTH, 1, GROUP_W), F32)], axis=1),
        cb=conv_b, ln_g=conv_ln_g, ln_b=conv_ln_b,
        g_ff=g_norm_ff)
    cv = jnp.concatenate([c_ctx[None, :], c, jnp.zeros((16 - 1 - DEC_BATCH, D_MODEL), F32)], axis=0)
    cos_t, sin_t = _rope_tables()
    g_final2 = g_final.reshape(1, D_MODEL)

    c_kr_t = jnp.swapaxes(cache_mla_krope, -1, -2)
    prm["mod"], nab, cka, ckb, cmv, prm["w_qup"], prm["w_kvup"] = _prep(
        cv, w_mod, b_mod, na_rpb, cache_mla_ckv, c_kr_t, w_mla_qup, w_mla_kvup)
    caches = (_heads_to_rows(cache_da_k), _heads_to_rows(cache_da_v), cka, ckb, cmv,
              _heads_to_rows(cache_na_k), _heads_to_rows(cache_na_v))

    xp = x_prompt.reshape(BATCH * SEQ, D_MODEL)
    xs = x_sample.reshape(DEC_BATCH * DEC_SEQ, D_MODEL)
    new_ctx = ()
    for l in range(DEPTH):
        cast = (w_out, w_ff1, w_ff2) if l == 0 else ()
        mixed_p, *rest = _ctx_layer(l, xp, prm, tuple(new_ctx), cast)
        new_ctx = rest[:len(rest) - len(cast)]
        if cast:
            prm["w_out"], prm["w_ff1"], prm["w_ff2"] = rest[len(new_ctx):]
        xp = _out_ffn(f"ctx_ffn{l}", l, BATCH * SEQ // TM_FFN, 0, xp, mixed_p, prm, g_final2)
        proj_outs = _lat_proj(l, xs, prm, cos_t, sin_t)
        if l + 1 < DEPTH:
            mixed_s, mod_next = _lat_attn(l, proj_outs, caches, nab, prm, (cv, w_mod, b_mod))
        else:
            mixed_s, mod_next = _lat_attn(l, proj_outs, caches, nab, prm, ()), None
        xs = _out_ffn(f"lat_ffn{l}", l, DEC_SEQ // TM_FFN, 1, xs, mixed_s, prm, g_final2)
        prm["mod"] = mod_next

    y_prompt = xp.reshape(BATCH, SEQ, D_MODEL)
    y_sample = xs.reshape(DEC_BATCH, DEC_SEQ, D_MODEL)
    new_ctx = [a if k == 2 else jnp.swapaxes(a, -1, -2) for k, a in enumerate(new_ctx)]
    return (y_prompt, y_sample) + tuple(new_ctx)
```

```python
import functools
import math

import numpy as np
import jax
import jax.numpy as jnp
from jax import lax
from jax.experimental import pallas as pl
from jax.experimental.pallas import tpu as pltpu

F32 = jnp.float32
BF = jnp.bfloat16

D_MODEL = 1024
BATCH = 16
SEQ = 256
DEPTH = 2
DEC_BATCH = 2
DEC_SEQ = 2048
PAST_LEN = 256
GRID_W = 64
GROUP_W = 256
HEAD_DIM = 64
N_HEADS = 4
DA_QK = 32
MLA_NOPE = 64
MLA_ROPE = 32
MLA_KV_LORA = 128
NA_KR = 8
NA_KC = 16
CONV_W = 31
D_FF = 4096
ROPE_BASE = 10000.0
EPS = 1e-6
N_MOD = 6
IN_COLS = 2464

P_DAQ, P_DAK, P_DAV, P_QD, P_KVD, P_KR = 0, 256, 512, 768, 1024, 1152
P_NAQ, P_NAK, P_NAV, P_CONV, P_TOT = 1280, 1536, 1792, 2048, 2560
P_SPLIT = P_NAQ
N_CTX_IN = 14
N_CTX_OUT = 7
KR_ORIG_END = 1184
KR_PAD = 128 - MLA_ROPE

DA_SCALE = DA_QK ** -0.5
MLA_SCALE = (MLA_NOPE + MLA_ROPE) ** -0.5
NA_SCALE = HEAD_DIM ** -0.5
LOG2E = math.log2(math.e)

TQ = 256
N_QT = DEC_SEQ // TQ
KEYS = DEC_SEQ + PAST_LEN
ROWS_PER_TILE = TQ // GRID_W
N_ROWS = DEC_SEQ // GRID_W
NA_LOCAL = NA_KR * GRID_W
TM_FFN = 512
TM_PROJ = 512
VMEM_LIMIT = 58 * 1024 * 1024

NT_DIMS = (((1,), (1,)), ((), ()))


def _cparams(*sem):
    return pltpu.CompilerParams(dimension_semantics=sem, vmem_limit_bytes=VMEM_LIMIT)


def _const_spec(shape):
    nd = len(shape)
    return pl.BlockSpec(shape, lambda *_: (0,) * nd, pipeline_mode=pl.Buffered(1))


def _layer_spec(layer, shape):
    nd = len(shape)
    return pl.BlockSpec((None,) + tuple(shape), lambda *_: (layer,) + (0,) * nd, pipeline_mode=pl.Buffered(1))


def _rms(x):
    return x * lax.rsqrt(jnp.mean(x * x, axis=-1, keepdims=True) + EPS)


def _row(ref, layer):
    return ref[layer:layer + 1, :]


def _dot(a, b):
    return jnp.dot(a, b, preferred_element_type=F32)


def _qk(q, k):
    return lax.dot_general(q, k, NT_DIMS, preferred_element_type=F32)


def _softmax2_parts(s):
    m = jnp.max(s, axis=-1, keepdims=True)
    e = jnp.exp2(s - m)
    return e, 1.0 / jnp.sum(e, axis=-1, keepdims=True)


def _lane_ids(width):
    return lax.broadcasted_iota(jnp.int32, (1, width), 1)


def _diff_lambda(lamv_ref, lam_init):
    v = lamv_ref[...]
    a = jnp.exp(jnp.sum(v[0:1] * v[1:2], axis=-1, keepdims=True))
    b = jnp.exp(jnp.sum(v[2:3] * v[3:4], axis=-1, keepdims=True))
    return a - b + lam_init


def _group_of(lane, group):
    return lane >> (group.bit_length() - 1)


def _head_rms(x):
    r = _group_of(lax.broadcasted_iota(jnp.int32, (GROUP_W, GROUP_W), 0), HEAD_DIM)
    c = _group_of(lax.broadcasted_iota(jnp.int32, (GROUP_W, GROUP_W), 1), HEAD_DIM)
    ones_bd = jnp.where(r == c, 1.0, 0.0).astype(BF)
    sq = x * x
    hi = sq.astype(BF)
    lo = (sq - hi.astype(F32)).astype(BF)
    ss = _dot(hi, ones_bd) + _dot(lo, ones_bd)
    return x * lax.rsqrt(ss * (1.0 / HEAD_DIM) + EPS)


def _rope(z, cos, sin):
    lane = _lane_ids(z.shape[1])
    swapped = jnp.where((lane & 15) < 8, pltpu.roll(z, 120, 1), pltpu.roll(z, 8, 1))
    return z * cos + swapped * sin


def _stage_w_in(wint_ref, wbt_ref):
    wbt_ref[0:KR_ORIG_END, :] = wint_ref[0:KR_ORIG_END, :].astype(BF)
    wbt_ref[KR_ORIG_END:P_SPLIT, :] = jnp.zeros((P_SPLIT - KR_ORIG_END, D_MODEL), BF)
    wbt_ref[P_SPLIT:P_TOT, :] = wint_ref[KR_ORIG_END:IN_COLS, :].astype(BF)


def _mask_heads(qf, lane, group, idx):
    return jnp.where(_group_of(lane, group) == idx, qf, 0.0).astype(BF)


def _conv_ln_silu(layer, gpad_ref, zsh_ref, n, dw_ref, cb_ref, lng_ref, lnb_ref):
    y = jnp.zeros((n, GROUP_W), F32) + _row(cb_ref, layer)
    for b in range(8):
        z = None
        for a in range(4):
            t = 8 * a + b - 1
            if 0 <= t < CONV_W:
                term = gpad_ref[pl.ds(8 * a, n + 8), :] * dw_ref[t:t + 1, :]
                z = term if z is None else z + term
        if b == 0:
            y = y + z[0:n]
        else:
            zsh_ref[b - 1] = z
            y = y + zsh_ref[b - 1, pl.ds(b, n), :]
    mu = jnp.mean(y, axis=-1, keepdims=True)
    yc = y - mu
    var = jnp.mean(yc * yc, axis=-1, keepdims=True)
    z = yc * lax.rsqrt(var + EPS) * _row(lng_ref, layer) + _row(lnb_ref, layer)
    return z * jax.nn.sigmoid(z)


def _mod_block(cv_ref, w, b_row, o_ref):
    c = cv_ref[...]
    a = (c * jax.nn.sigmoid(c)).astype(BF)
    o_ref[...] = _dot(a, w.astype(BF)) + b_row


MOD_BLOCKS = N_HEADS


def _prep_kernel(cv_ref, wmod_ref, bmod_ref, rpb_ref, ckv_ref, kr_ref, wq_ref, wkv_ref, selq_ref, selkv_ref,
                 mod_ref, nab_ref, ka_ref, kb_ref, v_ref, wqe_ref, wkvp_ref, tp_ref):
    @pl.when(pl.program_id(0) == 0)
    def _():
        _mod_block(cv_ref, wmod_ref[0], bmod_ref[0:1, :], mod_ref)

    _na_bias_kernel(rpb_ref, nab_ref, tp_ref)

    @pl.when(pl.program_id(1) < DEC_BATCH)
    def _():
        wqe_ref[0] = _dot(wq_ref[0].astype(BF), selq_ref[...].astype(BF)).astype(BF)
        w_kvup = _dot(wkv_ref[0].astype(BF), selkv_ref[...].astype(BF)).astype(BF)
        wkvp_ref[0] = w_kvup
        _mla_cache_kernel(ckv_ref, kr_ref, w_kvup, ka_ref, kb_ref, v_ref)


def _prep(cv, w_mod, b_mod, rpb, c_ckv, c_kr_t, w_qup, w_kvup):
    wblk = N_MOD * D_MODEL // MOD_BLOCKS
    mblk = lambda l, j: jnp.where(l == 0, j, MOD_BLOCKS - 1)
    cb = lambda l, j: (jnp.minimum(j, DEC_BATCH - 1), l, 0, 0)
    cache_blk = pl.BlockSpec((1, 1, PAST_LEN, GROUP_W), cb)
    cache_shp = jax.ShapeDtypeStruct((DEC_BATCH, DEPTH, PAST_LEN, GROUP_W), BF)
    per_layer = lambda shape: pl.BlockSpec((1,) + tuple(shape), lambda l, j: (l,) + (0,) * len(shape))
    selq, selkv = _qup_selection(), _kvup_selection()
    return pl.pallas_call(
        _prep_kernel,
        grid=(DEPTH, MOD_BLOCKS),
        in_specs=[pl.BlockSpec((16, D_MODEL), lambda l, j: (0, 0)),
                  pl.BlockSpec((1, D_MODEL, wblk), lambda l, j: (0, 0, mblk(l, j))),
                  pl.BlockSpec((DEPTH, wblk), lambda l, j: (0, mblk(l, j))),
                  pl.BlockSpec(memory_space=pltpu.SMEM),
                  pl.BlockSpec((1, 1, PAST_LEN, MLA_KV_LORA), cb),
                  pl.BlockSpec((1, 1, MLA_ROPE, PAST_LEN), cb),
                  per_layer(w_qup.shape[1:]), per_layer(w_kvup.shape[1:]),
                  _const_spec(selq.shape), _const_spec(selkv.shape)],
        out_specs=[pl.BlockSpec((16, wblk), lambda l, j: (0, mblk(l, j))),
                   pl.BlockSpec((1, NA_KR, GRID_W, NA_LOCAL), lambda l, j: (l, 0, j, 0)),
                   cache_blk, cache_blk, cache_blk,
                   per_layer((GROUP_W, N_HEADS * GROUP_W)), per_layer((MLA_KV_LORA, 2 * GROUP_W))],
        out_shape=[jax.ShapeDtypeStruct((16, N_MOD * D_MODEL), F32),
                   jax.ShapeDtypeStruct((DEPTH, NA_KR, N_HEADS * GRID_W, NA_LOCAL), F32),
                   cache_shp, cache_shp, cache_shp,
                   jax.ShapeDtypeStruct((DEPTH, GROUP_W, N_HEADS * GROUP_W), BF),
                   jax.ShapeDtypeStruct((DEPTH, MLA_KV_LORA, 2 * GROUP_W), BF)],
        scratch_shapes=[pltpu.VMEM((N_DR - 1, GRID_W, 128), F32)],
        compiler_params=_cparams("arbitrary", "arbitrary"),
        name="prep",
    )(cv, w_mod, b_mod, rpb.reshape(-1), c_ckv, c_kr_t, w_qup, w_kvup, selq, selkv)


def _mla_cache_kernel(ckv_ref, kr_ref, w_kvup, ka_ref, kb_ref, v_ref):
    kvm = _dot(ckv_ref[0, 0].astype(BF), w_kvup)
    kr_t = jnp.concatenate([kr_ref[0, 0], jnp.zeros((KR_PAD, PAST_LEN), F32)], axis=0)
    kr = kr_t.T.astype(BF)
    kn = kvm[:, :GROUP_W].astype(BF)
    ka_ref[0, 0] = jnp.concatenate([kn[:, :128], kr], axis=1)
    kb_ref[0, 0] = jnp.concatenate([kn[:, 128:], kr], axis=1)
    v_ref[0, 0] = kvm[:, GROUP_W:].astype(BF)


N_DR = 2 * NA_KR - 1
N_DC = 2 * NA_KC - 1


def _na_bias_kernel(rpb_ref, o_ref, tp_ref):
    base = (pl.program_id(0) * N_HEADS + pl.program_id(1)) * (N_DR * N_DC)
    cq = lax.broadcasted_iota(jnp.int32, (GRID_W, 128), 0)
    lane = lax.broadcasted_iota(jnp.int32, (GRID_W, 128), 1)
    ck = lane & (GRID_W - 1)
    upper = lane >= GRID_W
    cstart = jnp.clip(cq - NA_KC // 2, 0, GRID_W - NA_KC)
    ok = (ck >= cstart) & (ck < cstart + NA_KC)

    j = lax.broadcasted_iota(jnp.int32, (8, 128), 1)
    dclip = jnp.clip(jnp.where(j < GRID_W, j, j - 128), -(NA_KC - 1), NA_KC - 1) + NA_KC - 1

    def toeplitz(dr):
        u = jnp.zeros((8, 128), F32)
        for d in range(N_DC):
            u = jnp.where(dclip == d, rpb_ref[base + dr * N_DC + d], u)
        rows = jnp.broadcast_to(u[0:1], (GRID_W, 128))
        return pltpu.roll(rows, 0, 1, stride=1, stride_axis=0)

    tabs = [toeplitz(dr) for dr in range(N_DR)]
    for dr in range(N_DR - 1):
        val = jnp.where(upper, pltpu.roll(tabs[dr + 1], GRID_W, 1), tabs[dr])
        tp_ref[dr] = jnp.where(ok, val, -jnp.inf)
    for dr0 in range(NA_KR):
        o_ref[0, dr0] = jnp.concatenate([tp_ref[dr0 + 2 * m] for m in range(NA_KR // 2)], axis=1)


def _ctx_kernel(layer, lam_init, n_prev, n_cast, *refs):
    (x_ref, mod_ref, gmix_ref, win_ref, gq_ref, wqup_ref, gkv_ref, wkv_ref,
     lamv_ref, gsub_ref, dw_ref, cb_ref, lng_ref, lnb_ref) = refs[:N_CTX_IN]
    cast_in = refs[N_CTX_IN:N_CTX_IN + n_cast]
    outs = refs[N_CTX_IN + n_cast + n_prev:]
    mixed_ref, dak_ref, dav_ref, ckv_ref, kr_ref, nak_ref, nav_ref = outs[:N_CTX_OUT]
    cast_out = outs[N_CTX_OUT:N_CTX_OUT + n_cast]
    proj_ref, gpad_ref, zsh_ref = outs[N_CTX_OUT + n_cast:]
    wbf_ref = win_ref

    for src, dst in zip(cast_in, cast_out):
        dst[...] = src[...].astype(BF)

    x = x_ref[...]
    sh1 = mod_ref[0:1, 0:D_MODEL]
    sc1 = mod_ref[0:1, D_MODEL:2 * D_MODEL]
    h = ((_rms(x) * _row(gmix_ref, layer)) * (1.0 + sc1) + sh1).astype(BF)
    proj_ref[:, 0:P_SPLIT] = _qk(h, wbf_ref[0:P_SPLIT, :])
    proj_ref[:, P_SPLIT:P_TOT] = _qk(h, wbf_ref[P_SPLIT:P_TOT, :])

    for ref, col in ((dak_ref, P_DAK), (dav_ref, P_DAV), (nak_ref, P_NAK), (nav_ref, P_NAV)):
        slab_t = proj_ref[:, col:col + GROUP_W].T
        for hh in range(N_HEADS):
            ref[0, hh] = slab_t[hh * HEAD_DIM:(hh + 1) * HEAD_DIM, :]

    lane = _lane_ids(GROUP_W)

    lam = _diff_lambda(lamv_ref, lam_init)

    qa = proj_ref[:, P_DAQ:P_DAQ + GROUP_W] * (DA_SCALE * LOG2E)
    ka = proj_ref[:, P_DAK:P_DAK + GROUP_W].astype(BF)
    va = proj_ref[:, P_DAV:P_DAV + GROUP_W].astype(BF)
    qn = proj_ref[:, P_NAQ:P_NAQ + GROUP_W] * (NA_SCALE * LOG2E)
    kn2 = proj_ref[:, P_NAK:P_NAK + GROUP_W].astype(BF)
    vn = proj_ref[:, P_NAV:P_NAV + GROUP_W].astype(BF)
    mla = {}

    def mla_prep():
        qd = _rms(proj_ref[:, P_QD:P_QD + GROUP_W]) * _row(gq_ref, layer)
        ckv = _rms(proj_ref[:, P_KVD:P_KVD + MLA_KV_LORA]) * _row(gkv_ref, layer)
        ckv_ref[0] = ckv
        kr_pad = proj_ref[:, P_KR:P_KR + 128]
        kr_ref[0] = kr_pad.T[0:MLA_ROPE, :]
        mla["q"] = (_dot(qd.astype(BF), wqup_ref[...]) * (MLA_SCALE * LOG2E)).astype(BF)
        kvm = _dot(ckv.astype(BF), wkv_ref[...])
        kn = kvm[:, :GROUP_W].astype(BF)
        krb = kr_pad.astype(BF)
        mla["k"] = (jnp.concatenate([kn[:, :128], krb], axis=1), jnp.concatenate([kn[:, 128:], krb], axis=1))
        mla["v"] = kvm[:, GROUP_W:].astype(BF)

    def conv():
        g = proj_ref[:, P_CONV:P_CONV + GROUP_W] * jax.nn.sigmoid(proj_ref[:, P_CONV + GROUP_W:P_TOT])
        gpad_ref[0:16] = jnp.zeros((16, GROUP_W), F32)
        gpad_ref[16 + SEQ:32 + SEQ] = jnp.zeros((16, GROUP_W), F32)
        gpad_ref[16:16 + SEQ] = g
        return _conv_ln_silu(layer, gpad_ref, zsh_ref, SEQ, dw_ref, cb_ref, lng_ref, lnb_ref)

    def scores(i):
        if i < N_HEADS:
            return (_qk(_mask_heads(qa, lane, DA_QK, 2 * i), ka),
                    _qk(_mask_heads(qa, lane, DA_QK, 2 * i + 1), ka))
        if i < 2 * N_HEADS:
            hh = i - N_HEADS
            return (_qk(mla["q"][:, hh * GROUP_W:(hh + 1) * GROUP_W], mla["k"][hh // 2]),)
        return (_qk(_mask_heads(qn, lane, HEAD_DIM, i - 2 * N_HEADS), kn2),)

    outs = [jnp.zeros((SEQ, GROUP_W), F32)] * 3
    s_next = scores(0)
    o_conv = None
    for i in range(3 * N_HEADS):
        s_cur = s_next
        if i + 1 < 3 * N_HEADS:
            s_next = scores(i + 1)
        if i == 0:
            mla_prep()
        if i == N_HEADS - 1:
            o_conv = conv()
        grp, hh = divmod(i, N_HEADS)
        if grp == 0:
            e1, r1 = _softmax2_parts(s_cur[0])
            e2, r2 = _softmax2_parts(s_cur[1])
            o = _dot((e1 * r1 - e2 * (lam * r2)).astype(BF), va)
        else:
            e, r = _softmax2_parts(s_cur[0])
            o = _dot(e.astype(BF), mla["v"] if grp == 1 else vn) * r
        outs[grp] = jnp.where(_group_of(lane, HEAD_DIM) == hh, o, outs[grp])
    o_da = _head_rms(outs[0]) * _row(gsub_ref, layer) * (1.0 - lam_init)
    o_mla, o_na = outs[1], outs[2]

    mixed_ref[...] = jnp.concatenate([o_da, o_mla, o_na, o_conv], axis=1).astype(BF)


def _ctx_layer(layer, x, prm, w_in_staged, prev, cast):
    lam_init = 0.8 - 0.6 * math.exp(-0.3 * layer)
    head_blk = pl.BlockSpec((1, None, N_HEADS, HEAD_DIM, SEQ), lambda b: (b, layer, 0, 0, 0))
    head_shp = jax.ShapeDtypeStruct((BATCH, DEPTH, N_HEADS, HEAD_DIM, SEQ), F32)
    in_specs = [pl.BlockSpec((SEQ, D_MODEL), lambda b: (b, 0)),
                _const_spec((16, N_MOD * D_MODEL)),
                _const_spec((DEPTH, D_MODEL)),
                _const_spec((P_TOT, D_MODEL)),
                _const_spec((DEPTH, GROUP_W)),
                _layer_spec(layer, (GROUP_W, N_HEADS * GROUP_W)),
                _const_spec((DEPTH, MLA_KV_LORA)),
                _layer_spec(layer, (MLA_KV_LORA, 2 * GROUP_W)),
                _layer_spec(layer, (4, DA_QK)),
                _const_spec((DEPTH, GROUP_W)),
                _layer_spec(layer, (32, GROUP_W)),
                _const_spec((DEPTH, GROUP_W)),
                _const_spec((DEPTH, GROUP_W)),
                _const_spec((DEPTH, GROUP_W))]
    cast_specs = [pl.BlockSpec((DEPTH, w.shape[1] // BATCH, w.shape[2]), lambda b: (0, b, 0)) for w in cast]
    in_specs += cast_specs + [pl.BlockSpec(memory_space=pl.ANY)] * len(prev)
    out_specs = [pl.BlockSpec((SEQ, D_MODEL), lambda b: (b, 0)),
                 head_blk, head_blk,
                 pl.BlockSpec((1, None, SEQ, MLA_KV_LORA), lambda b: (b, layer, 0, 0)),
                 pl.BlockSpec((1, None, MLA_ROPE, SEQ), lambda b: (b, layer, 0, 0)),
                 head_blk, head_blk]
    out_shape = [jax.ShapeDtypeStruct((BATCH * SEQ, D_MODEL), BF),
                 head_shp, head_shp,
                 jax.ShapeDtypeStruct((BATCH, DEPTH, SEQ, MLA_KV_LORA), F32),
                 jax.ShapeDtypeStruct((BATCH, DEPTH, MLA_ROPE, SEQ), F32),
                 head_shp, head_shp] + [jax.ShapeDtypeStruct(w.shape, BF) for w in cast]
    out_specs += cast_specs
    n_in = len(in_specs) - len(prev)
    return pl.pallas_call(
        functools.partial(_ctx_kernel, layer, lam_init, len(prev), len(cast)),
        grid=(BATCH,),
        in_specs=in_specs,
        out_specs=out_specs,
        out_shape=out_shape,
        input_output_aliases={n_in + k: 1 + k for k in range(len(prev))},
        scratch_shapes=[pltpu.VMEM((SEQ, P_TOT), F32), pltpu.VMEM((SEQ + 32, GROUP_W), F32),
                        pltpu.VMEM((7, SEQ + 8, GROUP_W), F32)],
        compiler_params=_cparams("arbitrary"),
        name=f"ctx_layer{layer}",
    )(x, prm["mod"], prm["g_mix"], w_in_staged, prm["g_q"], prm["w_qup"], prm["g_kv"], prm["w_kvup"],
      prm["lamv"], prm["g_sub"], prm["dw"], prm["cb"], prm["ln_g"], prm["ln_b"], *cast, *prev)


def _lat_proj_kernel(layer, x_ref, mod_ref, gmix_ref, win_ref, gq_ref, wqup_ref, gkv_ref, wkv_ref,
                     cos_ref, sin_ref,
                     daq_ref, mq_ref, naq_ref, g_ref,
                     dak_ref, dav_ref, ka_ref, kb_ref, mv_ref, nak_ref, nav_ref, wbf_ref):
    b = pl.program_id(0)

    @pl.when((b == 0) & (pl.program_id(1) == 0))
    def _():
        _stage_w_in(win_ref, wbf_ref)

    x = x_ref[...]
    sh1 = mod_ref[pl.ds(1 + b, 1), 0:D_MODEL]
    sc1 = mod_ref[pl.ds(1 + b, 1), D_MODEL:2 * D_MODEL]
    h = (_rms(x) * _row(gmix_ref, layer)) * (1.0 + sc1) + sh1
    hb = h.astype(BF)
    proj = jnp.concatenate([_qk(hb, wbf_ref[0:P_SPLIT, :]), _qk(hb, wbf_ref[P_SPLIT:P_TOT, :])], axis=1)
    cos = cos_ref[...]
    sin = sin_ref[...]

    def rope2(z):
        return jnp.concatenate([_rope(z[:, :128], cos[:, :128], sin[:, :128]),
                                _rope(z[:, 128:], cos[:, 128:], sin[:, 128:])], axis=1)

    daq_ref[0] = (rope2(proj[:, P_DAQ:P_DAQ + GROUP_W]) * (DA_SCALE * LOG2E)).astype(BF)
    dak_ref[0] = rope2(proj[:, P_DAK:P_DAK + GROUP_W]).astype(BF)
    dav_ref[0] = proj[:, P_DAV:P_DAV + GROUP_W].astype(BF)

    qd = _rms(proj[:, P_QD:P_QD + GROUP_W]) * _row(gq_ref, layer)
    qm = _dot(qd.astype(BF), wqup_ref[...])
    for hh in range(N_HEADS):
        nope = qm[:, hh * GROUP_W:hh * GROUP_W + 128]
        rope = _rope(qm[:, hh * GROUP_W + 128:(hh + 1) * GROUP_W], cos[:, :128], sin[:, :128])
        mq_ref[0, hh] = (jnp.concatenate([nope, rope], axis=1) * (MLA_SCALE * LOG2E)).astype(BF)
    ckv = _rms(proj[:, P_KVD:P_KVD + MLA_KV_LORA]) * _row(gkv_ref, layer)
    kvm = _dot(ckv.astype(BF), wkv_ref[...])
    kn = kvm[:, :GROUP_W].astype(BF)
    krb = _rope(proj[:, P_KR:P_KR + 128], cos[:, :128], sin[:, :128]).astype(BF)
    ka_ref[0] = jnp.concatenate([kn[:, :128], krb], axis=1)
    kb_ref[0] = jnp.concatenate([kn[:, 128:], krb], axis=1)
    mv_ref[0] = kvm[:, GROUP_W:].astype(BF)

    naq_ref[0] = (proj[:, P_NAQ:P_NAQ + GROUP_W] * NA_SCALE).astype(BF)
    nak_ref[0] = proj[:, P_NAK:P_NAK + GROUP_W].astype(BF)
    nav_ref[0] = proj[:, P_NAV:P_NAV + GROUP_W].astype(BF)
    g_ref[0] = proj[:, P_CONV:P_CONV + GROUP_W] * jax.nn.sigmoid(proj[:, P_CONV + GROUP_W:P_TOT])


def _lat_proj(layer, x, prm, cos_t, sin_t):
    n_t = DEC_SEQ // TM_PROJ
    row_spec = pl.BlockSpec((1, TM_PROJ, GROUP_W), lambda b, j: (b, j, 0))
    in_specs = [pl.BlockSpec((TM_PROJ, D_MODEL), lambda b, j: (b * n_t + j, 0)),
                _const_spec((16, N_MOD * D_MODEL)),
                _const_spec((DEPTH, D_MODEL)),
                _layer_spec(layer, (IN_COLS, D_MODEL)),
                _const_spec((DEPTH, GROUP_W)),
                _layer_spec(layer, (GROUP_W, N_HEADS * GROUP_W)),
                _const_spec((DEPTH, MLA_KV_LORA)),
                _layer_spec(layer, (MLA_KV_LORA, 2 * GROUP_W)),
                pl.BlockSpec((TM_PROJ, GROUP_W), lambda b, j: (j, 0)),
                pl.BlockSpec((TM_PROJ, GROUP_W), lambda b, j: (j, 0))]
    row_shp = jax.ShapeDtypeStruct((DEC_BATCH, DEC_SEQ, GROUP_W), BF)
    out_specs = [row_spec,
                 pl.BlockSpec((1, N_HEADS, TM_PROJ, GROUP_W), lambda b, j: (b, 0, j, 0)),
                 row_spec, row_spec] + [row_spec] * 7 + [pl.BlockSpec((P_TOT, D_MODEL), lambda b, j: (0, 0))]
    out_shape = [row_shp,
                 jax.ShapeDtypeStruct((DEC_BATCH, N_HEADS, DEC_SEQ, GROUP_W), BF),
                 row_shp,
                 jax.ShapeDtypeStruct((DEC_BATCH, DEC_SEQ, GROUP_W), F32)] + [row_shp] * 7 + [
                 jax.ShapeDtypeStruct((P_TOT, D_MODEL), BF)]
    return pl.pallas_call(
        functools.partial(_lat_proj_kernel, layer),
        grid=(DEC_BATCH, n_t),
        in_specs=in_specs,
        out_specs=out_specs,
        out_shape=out_shape,
        compiler_params=_cparams("arbitrary", "arbitrary"),
        name=f"lat_proj{layer}",
    )(x, prm["mod"], prm["g_mix"], prm["w_in"], prm["g_q"], prm["w_qup"], prm["g_kv"], prm["w_kvup"],
      cos_t, sin_t)


N_LAT_ATTN_IN = 25


def _lat_attn_kernel(layer, lam_init, emit_mod, *refs):
    (daq_ref, mq_ref, naq_ref, g_ref,
     dak_ref, dav_ref, ka_ref, kb_ref, mv_ref, nak_ref, nav_ref,
     cdak_ref, cdav_ref, cka_ref, ckb_ref, cmv_ref, cnak_ref, cnav_ref,
     nab_ref, lamv_ref, gsub_ref, dw_ref, cb_ref, lng_ref, lnb_ref) = refs[:N_LAT_ATTN_IN]
    refs = refs[N_LAT_ATTN_IN:]
    if emit_mod:
        cv_ref, wmod_ref, bmod_ref, mixed_ref, modn_ref = refs[:5]
        _mod_block(cv_ref, wmod_ref[...], bmod_ref[layer + 1:layer + 2, :], modn_ref)
    else:
        mixed_ref = refs[0]
    gpad_ref, zsh_ref, kt_ref, v_ref, nac_ref = refs[-5:]
    t = pl.program_id(1)
    lane = _lane_ids(GROUP_W)

    @pl.when(t == 0)
    def _():
        for slot, lat in enumerate((dak_ref, ka_ref, kb_ref)):
            kt_ref[slot, :, 0:DEC_SEQ] = lat[0].T
        kt_ref[0, :, DEC_SEQ:KEYS] = cdak_ref[0].astype(BF)
        kt_ref[1, :, DEC_SEQ:KEYS] = cka_ref[0].T
        kt_ref[2, :, DEC_SEQ:KEYS] = ckb_ref[0].T
        v_ref[0, 0:DEC_SEQ, :] = dav_ref[0]
        v_ref[0, DEC_SEQ:KEYS, :] = cdav_ref[0].T.astype(BF)
        v_ref[1, 0:DEC_SEQ, :] = mv_ref[0]
        v_ref[1, DEC_SEQ:KEYS, :] = cmv_ref[0]
        nac_ref[0] = cnak_ref[0].astype(BF)
        nac_ref[1] = cnav_ref[0].T.astype(BF)

    lam = _diff_lambda(lamv_ref, lam_init)
    qa = daq_ref[0].astype(F32)
    va = v_ref[0]
    ka_t = kt_ref[0]
    vm = v_ref[1]

    kc_t = nac_ref[0]
    vc = nac_ref[1]
    n_items = 2 * N_HEADS + ROWS_PER_TILE

    def na_window(j):
        r = t * ROWS_PER_TILE + j
        start = jnp.clip(r - NA_KR // 2, 0, N_ROWS - NA_KR)
        return start - r + NA_KR - 1, pl.multiple_of(start * GRID_W, GRID_W)

    def scores(i):
        if i < N_HEADS:
            return (_dot(_mask_heads(qa, lane, DA_QK, 2 * i), ka_t),
                    _dot(_mask_heads(qa, lane, DA_QK, 2 * i + 1), ka_t))
        if i < 2 * N_HEADS:
            hh = i - N_HEADS
            return (_dot(mq_ref[0, hh], kt_ref[1 + hh // 2]),)
        j = i - 2 * N_HEADS
        dr0, koff = na_window(j)
        qrow = naq_ref[0, j * GRID_W:(j + 1) * GRID_W, :].astype(F32)
        q4 = jnp.concatenate([_mask_heads(qrow, lane, HEAD_DIM, hh) for hh in range(N_HEADS)], axis=0)
        return (_qk(q4, nak_ref[0, pl.ds(koff, NA_LOCAL), :]) + nab_ref[dr0], _dot(q4, kc_t))

    o_da = jnp.zeros((TQ, GROUP_W), F32)
    o_mla = jnp.zeros((TQ, GROUP_W), F32)
    na_rows = []
    s_next = scores(0)

    base = pl.multiple_of(t * TQ, TQ)
    gpad_ref[16:16 + TQ] = g_ref[0, pl.ds(base, TQ), :]
    lo = g_ref[0, pl.ds(pl.multiple_of(jnp.maximum(base - 16, 0), 16), 16), :]
    hi = g_ref[0, pl.ds(pl.multiple_of(jnp.minimum(base + TQ, DEC_SEQ - 16), 16), 16), :]
    gpad_ref[0:16] = jnp.where(t > 0, lo, 0.0)
    gpad_ref[16 + TQ:32 + TQ] = jnp.where(t < N_QT - 1, hi, 0.0)
    o_conv = _conv_ln_silu(layer, gpad_ref, zsh_ref, TQ, dw_ref, cb_ref, lng_ref, lnb_ref)

    for i in range(n_items):
        s_cur = s_next
        if i + 1 < n_items:
            s_next = scores(i + 1)
        if i < N_HEADS:
            e1, r1 = _softmax2_parts(s_cur[0])
            e2, r2 = _softmax2_parts(s_cur[1])
            p = (e1 - e2 * (lam * r2 / r1)).astype(BF)
            o_da = jnp.where(_group_of(lane, HEAD_DIM) == i, _dot(p, va) * r1, o_da)
        elif i < 2 * N_HEADS:
            e, r = _softmax2_parts(s_cur[0])
            o_mla = jnp.where(_group_of(lane, HEAD_DIM) == i - N_HEADS, _dot(e.astype(BF), vm) * r, o_mla)
        else:
            _, koff = na_window(i - 2 * N_HEADS)
            s_loc, s_ctx = s_cur
            m = jnp.maximum(jnp.max(s_loc, axis=-1, keepdims=True), jnp.max(s_ctx, axis=-1, keepdims=True))
            e_loc = jnp.exp(s_loc - m)
            e_ctx = jnp.exp(s_ctx - m)
            den = jnp.sum(e_loc, axis=-1, keepdims=True) + jnp.sum(e_ctx, axis=-1, keepdims=True)
            o4 = (_dot(e_loc.astype(BF), nav_ref[0, pl.ds(koff, NA_LOCAL), :])
                  + _dot(e_ctx.astype(BF), vc)) * (1.0 / den)
            o_row = o4[0:GRID_W]
            for hh in range(1, N_HEADS):
                o_row = jnp.where(_group_of(lane, HEAD_DIM) == hh, o4[hh * GRID_W:(hh + 1) * GRID_W], o_row)
            na_rows.append(o_row)
    o_da = _head_rms(o_da) * _row(gsub_ref, layer) * (1.0 - lam_init)
    o_na = jnp.concatenate(na_rows, axis=0)

    mixed_ref[...] = jnp.concatenate([o_da, o_mla, o_na, o_conv], axis=1).astype(BF)


def _lat_attn(layer, proj_outs, caches, nab, prm, next_mod):
    lam_init = 0.8 - 0.6 * math.exp(-0.3 * layer)
    n_steps = DEC_BATCH * N_QT
    mblk = N_MOD * D_MODEL // n_steps
    step = lambda b, t: b * N_QT + t
    mixed_spec = pl.BlockSpec((TQ, D_MODEL), lambda b, t: (step(b, t), 0))
    mixed_shape = jax.ShapeDtypeStruct((DEC_BATCH * DEC_SEQ, D_MODEL), BF)
    mod_in, out_specs, out_shape = [], mixed_spec, mixed_shape
    if next_mod:
        mod_in = [pl.BlockSpec((16, D_MODEL), lambda b, t: (0, 0)),
                  pl.BlockSpec((None, D_MODEL, mblk), lambda b, t: (layer + 1, 0, step(b, t))),
                  pl.BlockSpec((DEPTH, mblk), lambda b, t: (0, step(b, t)))]
        out_specs = [mixed_spec, pl.BlockSpec((16, mblk), lambda b, t: (0, step(b, t)))]
        out_shape = [mixed_shape, jax.ShapeDtypeStruct((16, N_MOD * D_MODEL), F32)]
    q_spec = pl.BlockSpec((1, TQ, GROUP_W), lambda b, t: (b, t, 0))
    full_k = pl.BlockSpec((1, DEC_SEQ, GROUP_W), lambda b, t: (b, 0, 0))
    cache_spec = pl.BlockSpec((1, None, PAST_LEN, GROUP_W), lambda b, t: (b, layer, 0, 0))
    assert PAST_LEN == GROUP_W
    in_specs = [q_spec,
                pl.BlockSpec((1, N_HEADS, TQ, GROUP_W), lambda b, t: (b, 0, t, 0)),
                q_spec, full_k] + [full_k] * 7 + [cache_spec] * 7 + [
                _layer_spec(layer, (NA_KR, N_HEADS * GRID_W, NA_LOCAL)),
                _layer_spec(layer, (4, DA_QK)),
                _const_spec((DEPTH, GROUP_W)),
                _layer_spec(layer, (32, GROUP_W)),
                _const_spec((DEPTH, GROUP_W)),
                _const_spec((DEPTH, GROUP_W)),
                _const_spec((DEPTH, GROUP_W))] + mod_in
    assert len(in_specs) == N_LAT_ATTN_IN + len(mod_in)
    return pl.pallas_call(
        functools.partial(_lat_attn_kernel, layer, lam_init, bool(next_mod)),
        grid=(DEC_BATCH, N_QT),
        in_specs=in_specs,
        out_specs=out_specs,
        out_shape=out_shape,
        scratch_shapes=[pltpu.VMEM((TQ + 32, GROUP_W), F32), pltpu.VMEM((7, TQ + 8, GROUP_W), F32),
                        pltpu.VMEM((3, GROUP_W, KEYS), BF), pltpu.VMEM((2, KEYS, GROUP_W), BF),
                        pltpu.VMEM((2, PAST_LEN, GROUP_W), BF)],
        compiler_params=_cparams("arbitrary", "arbitrary"),
        name=f"lat_attn{layer}",
    )(*proj_outs, *caches, nab, prm["lamv"], prm["g_sub"], prm["dw"], prm["cb"], prm["ln_g"], prm["ln_b"], *next_mod)


N_FF_CHUNKS = D_FF // D_MODEL


def _ffn_weight_copies(layer, wout_hbm, w1_hbm, w2_hbm, wout_ref, w1_ref, w2_ref, sem):
    copies = [pltpu.make_async_copy(wout_hbm.at[layer], wout_ref, sem.at[0])]
    for c in range(N_FF_CHUNKS):
        lo, hi = c * D_MODEL, (c + 1) * D_MODEL
        copies.append(pltpu.make_async_copy(w1_hbm.at[layer, :, lo:hi], w1_ref.at[:, lo:hi], sem.at[1 + 2 * c]))
        copies.append(pltpu.make_async_copy(w2_hbm.at[layer, lo:hi, :], w2_ref.at[lo:hi, :], sem.at[2 + 2 * c]))
    return copies


def _out_ffn_kernel(layer, tiles_per_mod, mod_base, x_ref, mx_ref, mod_ref, wout_hbm, gff_ref,
                    w1_hbm, w2_hbm, gfin_ref, o_ref, wout_ref, w1_ref, w2_ref, sem):
    copies = _ffn_weight_copies(layer, wout_hbm, w1_hbm, w2_hbm, wout_ref, w1_ref, w2_ref, sem)

    def body(first):
        def arrive(k):
            if first:
                copies[k].wait()

        if first:
            for cp in copies:
                cp.start()
        row = mod_base + pl.program_id(0) // tiles_per_mod
        g1 = mod_ref[pl.ds(row, 1), 2 * D_MODEL:3 * D_MODEL]
        sh2 = mod_ref[pl.ds(row, 1), 3 * D_MODEL:4 * D_MODEL]
        sc2 = mod_ref[pl.ds(row, 1), 4 * D_MODEL:5 * D_MODEL]
        g2 = mod_ref[pl.ds(row, 1), 5 * D_MODEL:6 * D_MODEL]
        arrive(0)
        x1 = x_ref[...] + g1 * _dot(mx_ref[...], wout_ref[...])
        h2 = ((_rms(x1) * _row(gff_ref, layer)) * (1.0 + sc2) + sh2).astype(BF)
        acc = jnp.zeros((TM_FFN, D_MODEL), F32)
        for c in range(N_FF_CHUNKS):
            arrive(1 + 2 * c)
            a = jnp.maximum(_dot(h2, w1_ref[:, c * D_MODEL:(c + 1) * D_MODEL]), 0.0)
            arrive(2 + 2 * c)
            acc = acc + _dot((a * a).astype(BF), w2_ref[c * D_MODEL:(c + 1) * D_MODEL, :])
        x2 = x1 + g2 * acc
        o_ref[...] = _rms(x2) * gfin_ref[...] if layer == DEPTH - 1 else x2

    @pl.when(pl.program_id(0) == 0)
    def _():
        body(True)

    @pl.when(pl.program_id(0) > 0)
    def _():
        body(False)


def _out_ffn(name, layer, tiles_per_mod, mod_base, x, mixed, prm, g_final):
    n = x.shape[0]
    tile = pl.BlockSpec((TM_FFN, D_MODEL), lambda i: (i, 0))
    return pl.pallas_call(
        functools.partial(_out_ffn_kernel, layer, tiles_per_mod, mod_base),
        grid=(n // TM_FFN,),
        in_specs=[tile, tile,
                  _const_spec((16, N_MOD * D_MODEL)),
                  pl.BlockSpec(memory_space=pl.ANY),
                  _const_spec((DEPTH, D_MODEL)),
                  pl.BlockSpec(memory_space=pl.ANY),
                  pl.BlockSpec(memory_space=pl.ANY),
                  _const_spec((1, D_MODEL))],
        out_specs=tile,
        out_shape=jax.ShapeDtypeStruct((n, D_MODEL), F32),
        scratch_shapes=[pltpu.VMEM((D_MODEL, D_MODEL), BF), pltpu.VMEM((D_MODEL, D_FF), BF),
                        pltpu.VMEM((D_FF, D_MODEL), BF), pltpu.SemaphoreType.DMA((1 + 2 * N_FF_CHUNKS,))],
        compiler_params=_cparams("arbitrary"),
        name=name,
    )(x, mixed, prm["mod"], prm["w_out"], prm["g_ff"], prm["w_ff1"], prm["w_ff2"], g_final)


def _rope_tables():
    t = np.arange(DEC_SEQ)
    rows = (t // GRID_W).astype(np.float32)
    cols = (t % GRID_W).astype(np.float32)
    c = np.arange(GROUP_W) % 32
    freqs = np.float32(ROPE_BASE) ** (-np.arange(8, dtype=np.float32) * np.float32(2.0) / np.float32(16))
    pos = np.where((c < 16)[None, :], rows[:, None], cols[:, None]).astype(np.float32)
    ang = (pos * freqs[(c % 16) % 8][None, :]).astype(np.float32)
    first = ((c % 16) < 8)[None, :]
    cos = np.cos(ang).astype(np.float32)
    sin = np.sin(ang).astype(np.float32)
    return jnp.asarray(cos), jnp.asarray(np.where(first, -sin, sin))


def _qup_selection():
    sel = np.zeros((N_HEADS * (MLA_NOPE + MLA_ROPE), N_HEADS * GROUP_W), np.float32)
    for h in range(N_HEADS):
        src = h * (MLA_NOPE + MLA_ROPE)
        dst = h * GROUP_W + (h % 2) * MLA_NOPE
        sel[np.arange(src, src + MLA_NOPE), np.arange(dst, dst + MLA_NOPE)] = 1.0
        sel[np.arange(src + MLA_NOPE, src + MLA_NOPE + MLA_ROPE), h * GROUP_W + 128 + np.arange(MLA_ROPE)] = 1.0
    return jnp.asarray(sel)


def _kvup_selection():
    per_head = MLA_NOPE + HEAD_DIM
    k = [h * per_head + d for h in range(N_HEADS) for d in range(MLA_NOPE)]
    v = [h * per_head + MLA_NOPE + d for h in range(N_HEADS) for d in range(HEAD_DIM)]
    src = np.asarray(k + v)
    sel = np.zeros((src.size, src.size), np.float32)
    sel[src, np.arange(src.size)] = 1.0
    return jnp.asarray(sel)


def _heads_to_rows(c):
    b, l, h, s, d = c.shape
    return jnp.swapaxes(c, -1, -2).reshape(b, l, h * d, s)


def kernel(x_prompt, x_sample, c, cache_da_k, cache_da_v, cache_mla_ckv, cache_mla_krope, cache_na_k, cache_na_v, c_ctx, w_mod, b_mod, g_norm_mix, g_norm_ff, w_in, da_lambda_q1, da_lambda_k1, da_lambda_q2, da_lambda_k2, g_da_subln, g_mla_q, w_mla_qup, g_mla_kv, w_mla_kvup, na_rpb, conv_dw, conv_b, conv_ln_g, conv_ln_b, w_out, w_ff1, w_ff2, g_final):
    prm = dict(
        g_mix=g_norm_mix, w_in=jnp.swapaxes(w_in, 1, 2), g_q=g_mla_q, g_kv=g_mla_kv,
        lamv=jnp.stack([da_lambda_q1, da_lambda_k1, da_lambda_q2, da_lambda_k2], axis=1),
        g_sub=jnp.tile(g_da_subln, (1, N_HEADS)),
        dw=jnp.concatenate([conv_dw, jnp.zeros((DEPTH, 1, GROUP_W), F32)], axis=1),
        cb=conv_b, ln_g=conv_ln_g, ln_b=conv_ln_b,
        g_ff=g_norm_ff)
    cv = jnp.concatenate([c_ctx[None, :], c, jnp.zeros((16 - 1 - DEC_BATCH, D_MODEL), F32)], axis=0)
    cos_t, sin_t = _rope_tables()
    g_final2 = g_final.reshape(1, D_MODEL)

    c_kr_t = jnp.swapaxes(cache_mla_krope, -1, -2)
    prm["mod"], nab, cka, ckb, cmv, prm["w_qup"], prm["w_kvup"] = _prep(
        cv, w_mod, b_mod, na_rpb, cache_mla_ckv, c_kr_t, w_mla_qup, w_mla_kvup)
    caches = (_heads_to_rows(cache_da_k), _heads_to_rows(cache_da_v), cka, ckb, cmv,
              _heads_to_rows(cache_na_k), _heads_to_rows(cache_na_v))

    xp = x_prompt.reshape(BATCH * SEQ, D_MODEL)
    xs = x_sample.reshape(DEC_BATCH * DEC_SEQ, D_MODEL)
    new_ctx = ()
    for l in range(DEPTH):
        cast = (w_out, w_ff1, w_ff2) if l == 0 else ()
        *proj_outs, w_in_staged = _lat_proj(l, xs, prm, cos_t, sin_t)
        mixed_p, *rest = _ctx_layer(l, xp, prm, w_in_staged, tuple(new_ctx), cast)
        new_ctx = rest[:len(rest) - len(cast)]
        if cast:
            prm["w_out"], prm["w_ff1"], prm["w_ff2"] = rest[len(new_ctx):]
        xp = _out_ffn(f"ctx_ffn{l}", l, BATCH * SEQ // TM_FFN, 0, xp, mixed_p, prm, g_final2)
        if l + 1 < DEPTH:
            mixed_s, mod_next = _lat_attn(l, proj_outs, caches, nab, prm, (cv, w_mod, b_mod))
        else:
            mixed_s, mod_next = _lat_attn(l, proj_outs, caches, nab, prm, ()), None
        xs = _out_ffn(f"lat_ffn{l}", l, DEC_SEQ // TM_FFN, 1, xs, mixed_s, prm, g_final2)
        prm["mod"] = mod_next

    y_prompt = xp.reshape(BATCH, SEQ, D_MODEL)
    y_sample = xs.reshape(DEC_BATCH, DEC_SEQ, D_MODEL)
    new_ctx = [a if k == 2 else jnp.swapaxes(a, -1, -2) for k, a in enumerate(new_ctx)]
    return (y_prompt, y_sample) + tuple(new_ctx)
```

```python
import functools
import math

import numpy as np
import jax
import jax.numpy as jnp
from jax import lax
from jax.experimental import pallas as pl
from jax.experimental.pallas import tpu as pltpu

F32 = jnp.float32
BF = jnp.bfloat16

D_MODEL = 1024
BATCH = 16
SEQ = 256
DEPTH = 2
DEC_BATCH = 2
DEC_SEQ = 2048
PAST_LEN = 256
GRID_W = 64
GROUP_W = 256
HEAD_DIM = 64
N_HEADS = 4
DA_QK = 32
MLA_NOPE = 64
MLA_ROPE = 32
MLA_KV_LORA = 128
NA_KR = 8
NA_KC = 16
CONV_W = 31
D_FF = 4096
ROPE_BASE = 10000.0
EPS = 1e-6
N_MOD = 6
IN_COLS = 2464

P_DAQ, P_DAK, P_DAV, P_QD, P_KVD, P_KR = 0, 256, 512, 768, 1024, 1152
P_NAQ, P_NAK, P_NAV, P_CONV, P_TOT = 1280, 1536, 1792, 2048, 2560
P_SPLIT = P_NAQ
N_CTX_IN = 14
N_CTX_OUT = 7
KR_ORIG_END = 1184
KR_PAD = 128 - MLA_ROPE

DA_SCALE = DA_QK ** -0.5
MLA_SCALE = (MLA_NOPE + MLA_ROPE) ** -0.5
NA_SCALE = HEAD_DIM ** -0.5
LOG2E = math.log2(math.e)

TQ = 256
N_QT = DEC_SEQ // TQ
KEYS = DEC_SEQ + PAST_LEN
ROWS_PER_TILE = TQ // GRID_W
N_ROWS = DEC_SEQ // GRID_W
NA_LOCAL = NA_KR * GRID_W
TM_FFN = 512
TM_PROJ = 512
VMEM_LIMIT = 58 * 1024 * 1024

NT_DIMS = (((1,), (1,)), ((), ()))


def _cparams(*sem):
    return pltpu.CompilerParams(dimension_semantics=sem, vmem_limit_bytes=VMEM_LIMIT)


def _const_spec(shape):
    nd = len(shape)
    return pl.BlockSpec(shape, lambda *_: (0,) * nd, pipeline_mode=pl.Buffered(1))


def _layer_spec(layer, shape):
    nd = len(shape)
    return pl.BlockSpec((None,) + tuple(shape), lambda *_: (layer,) + (0,) * nd, pipeline_mode=pl.Buffered(1))


def _rms(x):
    return x * lax.rsqrt(jnp.mean(x * x, axis=-1, keepdims=True) + EPS)


def _row(ref, layer):
    return ref[layer:layer + 1, :]


def _dot(a, b):
    return jnp.dot(a, b, preferred_element_type=F32)


def _qk(q, k):
    return lax.dot_general(q, k, NT_DIMS, preferred_element_type=F32)


def _softmax2_parts(s):
    m = jnp.max(s, axis=-1, keepdims=True)
    e = jnp.exp2(s - m)
    return e, 1.0 / jnp.sum(e, axis=-1, keepdims=True)


def _lane_ids(width):
    return lax.broadcasted_iota(jnp.int32, (1, width), 1)


def _diff_lambda(lamv_ref, lam_init):
    v = lamv_ref[...]
    a = jnp.exp(jnp.sum(v[0:1] * v[1:2], axis=-1, keepdims=True))
    b = jnp.exp(jnp.sum(v[2:3] * v[3:4], axis=-1, keepdims=True))
    return a - b + lam_init


def _group_of(lane, group):
    return lane >> (group.bit_length() - 1)


def _head_rms(x):
    r = _group_of(lax.broadcasted_iota(jnp.int32, (GROUP_W, GROUP_W), 0), HEAD_DIM)
    c = _group_of(lax.broadcasted_iota(jnp.int32, (GROUP_W, GROUP_W), 1), HEAD_DIM)
    ones_bd = jnp.where(r == c, 1.0, 0.0).astype(BF)
    sq = x * x
    hi = sq.astype(BF)
    lo = (sq - hi.astype(F32)).astype(BF)
    ss = _dot(hi, ones_bd) + _dot(lo, ones_bd)
    return x * lax.rsqrt(ss * (1.0 / HEAD_DIM) + EPS)


def _rope(z, cos, sin):
    lane = _lane_ids(z.shape[1])
    swapped = jnp.where((lane & 15) < 8, pltpu.roll(z, 120, 1), pltpu.roll(z, 8, 1))
    return z * cos + swapped * sin


def _stage_w_in(wint_ref, wbt_ref):
    wbt_ref[0:KR_ORIG_END, :] = wint_ref[0:KR_ORIG_END, :].astype(BF)
    wbt_ref[KR_ORIG_END:P_SPLIT, :] = jnp.zeros((P_SPLIT - KR_ORIG_END, D_MODEL), BF)
    wbt_ref[P_SPLIT:P_TOT, :] = wint_ref[KR_ORIG_END:IN_COLS, :].astype(BF)


def _mask_heads(qf, lane, group, idx):
    return jnp.where(_group_of(lane, group) == idx, qf, 0.0).astype(BF)


def _conv_ln_silu(layer, gpad_ref, zsh_ref, n, dw_ref, cb_ref, lng_ref, lnb_ref):
    y = jnp.zeros((n, GROUP_W), F32) + _row(cb_ref, layer)
    for b in range(8):
        z = None
        for a in range(4):
            t = 8 * a + b - 1
            if 0 <= t < CONV_W:
                term = gpad_ref[pl.ds(8 * a, n + 8), :] * dw_ref[t:t + 1, :]
                z = term if z is None else z + term
        if b == 0:
            y = y + z[0:n]
        else:
            zsh_ref[b - 1] = z
            y = y + zsh_ref[b - 1, pl.ds(b, n), :]
    mu = jnp.mean(y, axis=-1, keepdims=True)
    yc = y - mu
    var = jnp.mean(yc * yc, axis=-1, keepdims=True)
    z = yc * lax.rsqrt(var + EPS) * _row(lng_ref, layer) + _row(lnb_ref, layer)
    return z * jax.nn.sigmoid(z)


def _mod_block(cv_ref, w, b_row, o_ref):
    c = cv_ref[...]
    a = (c * jax.nn.sigmoid(c)).astype(BF)
    o_ref[...] = _dot(a, w.astype(BF)) + b_row


MOD_BLOCKS = N_HEADS


def _prep_kernel(cv_ref, wmod_ref, bmod_ref, rpb_ref, ckv_ref, kr_ref, wq_ref, wkv_ref, selq_ref, selkv_ref,
                 mod_ref, nab_ref, ka_ref, kb_ref, v_ref, wqe_ref, wkvp_ref, tp_ref):
    @pl.when(pl.program_id(0) == 0)
    def _():
        _mod_block(cv_ref, wmod_ref[0], bmod_ref[0:1, :], mod_ref)

    _na_bias_kernel(rpb_ref, nab_ref, tp_ref)

    @pl.when(pl.program_id(1) < DEC_BATCH)
    def _():
        wqe_ref[0] = _dot(wq_ref[0].astype(BF), selq_ref[...].astype(BF)).astype(BF)
        w_kvup = _dot(wkv_ref[0].astype(BF), selkv_ref[...].astype(BF)).astype(BF)
        wkvp_ref[0] = w_kvup
        _mla_cache_kernel(ckv_ref, kr_ref, w_kvup, ka_ref, kb_ref, v_ref)


def _prep(cv, w_mod, b_mod, rpb, c_ckv, c_kr_t, w_qup, w_kvup):
    wblk = N_MOD * D_MODEL // MOD_BLOCKS
    mblk = lambda l, j: jnp.where(l == 0, j, MOD_BLOCKS - 1)
    cb = lambda l, j: (jnp.minimum(j, DEC_BATCH - 1), l, 0, 0)
    cache_blk = pl.BlockSpec((1, 1, PAST_LEN, GROUP_W), cb)
    cache_shp = jax.ShapeDtypeStruct((DEC_BATCH, DEPTH, PAST_LEN, GROUP_W), BF)
    per_layer = lambda shape: pl.BlockSpec((1,) + tuple(shape), lambda l, j: (l,) + (0,) * len(shape))
    selq, selkv = _qup_selection(), _kvup_selection()
    return pl.pallas_call(
        _prep_kernel,
        grid=(DEPTH, MOD_BLOCKS),
        in_specs=[pl.BlockSpec((16, D_MODEL), lambda l, j: (0, 0)),
                  pl.BlockSpec((1, D_MODEL, wblk), lambda l, j: (0, 0, mblk(l, j))),
                  pl.BlockSpec((DEPTH, wblk), lambda l, j: (0, mblk(l, j))),
                  pl.BlockSpec(memory_space=pltpu.SMEM),
                  pl.BlockSpec((1, 1, PAST_LEN, MLA_KV_LORA), cb),
                  pl.BlockSpec((1, 1, MLA_ROPE, PAST_LEN), cb),
                  per_layer(w_qup.shape[1:]), per_layer(w_kvup.shape[1:]),
                  _const_spec(selq.shape), _const_spec(selkv.shape)],
        out_specs=[pl.BlockSpec((16, wblk), lambda l, j: (0, mblk(l, j))),
                   pl.BlockSpec((1, NA_KR, GRID_W, NA_LOCAL), lambda l, j: (l, 0, j, 0)),
                   cache_blk, cache_blk, cache_blk,
                   per_layer((GROUP_W, N_HEADS * GROUP_W)), per_layer((MLA_KV_LORA, 2 * GROUP_W))],
        out_shape=[jax.ShapeDtypeStruct((16, N_MOD * D_MODEL), F32),
                   jax.ShapeDtypeStruct((DEPTH, NA_KR, N_HEADS * GRID_W, NA_LOCAL), F32),
                   cache_shp, cache_shp, cache_shp,
                   jax.ShapeDtypeStruct((DEPTH, GROUP_W, N_HEADS * GROUP_W), BF),
                   jax.ShapeDtypeStruct((DEPTH, MLA_KV_LORA, 2 * GROUP_W), BF)],
        scratch_shapes=[pltpu.VMEM((N_DR - 1, GRID_W, 128), F32)],
        compiler_params=_cparams("arbitrary", "arbitrary"),
        name="prep",
    )(cv, w_mod, b_mod, rpb.reshape(-1), c_ckv, c_kr_t, w_qup, w_kvup, selq, selkv)


def _mla_cache_kernel(ckv_ref, kr_ref, w_kvup, ka_ref, kb_ref, v_ref):
    kvm = _dot(ckv_ref[0, 0].astype(BF), w_kvup)
    kr_t = jnp.concatenate([kr_ref[0, 0], jnp.zeros((KR_PAD, PAST_LEN), F32)], axis=0)
    kr = kr_t.T.astype(BF)
    kn = kvm[:, :GROUP_W].astype(BF)
    ka_ref[0, 0] = jnp.concatenate([kn[:, :128], kr], axis=1)
    kb_ref[0, 0] = jnp.concatenate([kn[:, 128:], kr], axis=1)
    v_ref[0, 0] = kvm[:, GROUP_W:].astype(BF)


N_DR = 2 * NA_KR - 1
N_DC = 2 * NA_KC - 1


def _na_bias_kernel(rpb_ref, o_ref, tp_ref):
    base = (pl.program_id(0) * N_HEADS + pl.program_id(1)) * (N_DR * N_DC)
    cq = lax.broadcasted_iota(jnp.int32, (GRID_W, 128), 0)
    lane = lax.broadcasted_iota(jnp.int32, (GRID_W, 128), 1)
    ck = lane & (GRID_W - 1)
    upper = lane >= GRID_W
    cstart = jnp.clip(cq - NA_KC // 2, 0, GRID_W - NA_KC)
    ok = (ck >= cstart) & (ck < cstart + NA_KC)

    j = lax.broadcasted_iota(jnp.int32, (8, 128), 1)
    dclip = jnp.clip(jnp.where(j < GRID_W, j, j - 128), -(NA_KC - 1), NA_KC - 1) + NA_KC - 1

    def toeplitz(dr):
        u = jnp.zeros((8, 128), F32)
        for d in range(N_DC):
            u = jnp.where(dclip == d, rpb_ref[base + dr * N_DC + d], u)
        rows = jnp.broadcast_to(u[0:1], (GRID_W, 128))
        return pltpu.roll(rows, 0, 1, stride=1, stride_axis=0)

    tabs = [toeplitz(dr) for dr in range(N_DR)]
    for dr in range(N_DR - 1):
        val = jnp.where(upper, pltpu.roll(tabs[dr + 1], GRID_W, 1), tabs[dr])
        tp_ref[dr] = jnp.where(ok, val, -jnp.inf)
    for dr0 in range(NA_KR):
        o_ref[0, dr0] = jnp.concatenate([tp_ref[dr0 + 2 * m] for m in range(NA_KR // 2)], axis=1)


def _ctx_kernel(layer, lam_init, n_prev, n_cast, *refs):
    (x_ref, mod_ref, gmix_ref, win_ref, gq_ref, wqup_ref, gkv_ref, wkv_ref,
     lamv_ref, gsub_ref, dw_ref, cb_ref, lng_ref, lnb_ref) = refs[:N_CTX_IN]
    cast_in = refs[N_CTX_IN:N_CTX_IN + n_cast]
    outs = refs[N_CTX_IN + n_cast + n_prev:]
    mixed_ref, dak_ref, dav_ref, ckv_ref, kr_ref, nak_ref, nav_ref = outs[:N_CTX_OUT]
    cast_out = outs[N_CTX_OUT:N_CTX_OUT + n_cast]
    proj_ref, gpad_ref, zsh_ref, wbf_ref = outs[N_CTX_OUT + n_cast:]

    for src, dst in zip(cast_in, cast_out):
        dst[...] = src[...].astype(BF)

    @pl.when(pl.program_id(0) == 0)
    def _():
        _stage_w_in(win_ref, wbf_ref)

    x = x_ref[...]
    sh1 = mod_ref[0:1, 0:D_MODEL]
    sc1 = mod_ref[0:1, D_MODEL:2 * D_MODEL]
    h = ((_rms(x) * _row(gmix_ref, layer)) * (1.0 + sc1) + sh1).astype(BF)
    proj_ref[:, 0:P_SPLIT] = _qk(h, wbf_ref[0:P_SPLIT, :])
    proj_ref[:, P_SPLIT:P_TOT] = _qk(h, wbf_ref[P_SPLIT:P_TOT, :])

    for ref, col in ((dak_ref, P_DAK), (dav_ref, P_DAV), (nak_ref, P_NAK), (nav_ref, P_NAV)):
        slab_t = proj_ref[:, col:col + GROUP_W].T
        for hh in range(N_HEADS):
            ref[0, hh] = slab_t[hh * HEAD_DIM:(hh + 1) * HEAD_DIM, :]

    lane = _lane_ids(GROUP_W)

    lam = _diff_lambda(lamv_ref, lam_init)

    qa = proj_ref[:, P_DAQ:P_DAQ + GROUP_W] * (DA_SCALE * LOG2E)
    ka = proj_ref[:, P_DAK:P_DAK + GROUP_W].astype(BF)
    va = proj_ref[:, P_DAV:P_DAV + GROUP_W].astype(BF)
    qn = proj_ref[:, P_NAQ:P_NAQ + GROUP_W] * (NA_SCALE * LOG2E)
    kn2 = proj_ref[:, P_NAK:P_NAK + GROUP_W].astype(BF)
    vn = proj_ref[:, P_NAV:P_NAV + GROUP_W].astype(BF)
    mla = {}

    def mla_prep():
        qd = _rms(proj_ref[:, P_QD:P_QD + GROUP_W]) * _row(gq_ref, layer)
        ckv = _rms(proj_ref[:, P_KVD:P_KVD + MLA_KV_LORA]) * _row(gkv_ref, layer)
        ckv_ref[0] = ckv
        kr_pad = proj_ref[:, P_KR:P_KR + 128]
        kr_ref[0] = kr_pad.T[0:MLA_ROPE, :]
        mla["q"] = (_dot(qd.astype(BF), wqup_ref[...]) * (MLA_SCALE * LOG2E)).astype(BF)
        kvm = _dot(ckv.astype(BF), wkv_ref[...])
        kn = kvm[:, :GROUP_W].astype(BF)
        krb = kr_pad.astype(BF)
        mla["k"] = (jnp.concatenate([kn[:, :128], krb], axis=1), jnp.concatenate([kn[:, 128:], krb], axis=1))
        mla["v"] = kvm[:, GROUP_W:].astype(BF)

    def conv():
        g = proj_ref[:, P_CONV:P_CONV + GROUP_W] * jax.nn.sigmoid(proj_ref[:, P_CONV + GROUP_W:P_TOT])
        gpad_ref[0:16] = jnp.zeros((16, GROUP_W), F32)
        gpad_ref[16 + SEQ:32 + SEQ] = jnp.zeros((16, GROUP_W), F32)
        gpad_ref[16:16 + SEQ] = g
        return _conv_ln_silu(layer, gpad_ref, zsh_ref, SEQ, dw_ref, cb_ref, lng_ref, lnb_ref)

    def scores(i):
        if i < N_HEADS:
            return (_qk(_mask_heads(qa, lane, DA_QK, 2 * i), ka),
                    _qk(_mask_heads(qa, lane, DA_QK, 2 * i + 1), ka))
        if i < 2 * N_HEADS:
            hh = i - N_HEADS
            return (_qk(mla["q"][:, hh * GROUP_W:(hh + 1) * GROUP_W], mla["k"][hh // 2]),)
        return (_qk(_mask_heads(qn, lane, HEAD_DIM, i - 2 * N_HEADS), kn2),)

    outs = [jnp.zeros((SEQ, GROUP_W), F32)] * 3
    s_next = scores(0)
    o_conv = None
    for i in range(3 * N_HEADS):
        s_cur = s_next
        if i + 1 < 3 * N_HEADS:
            s_next = scores(i + 1)
        if i == 0:
            mla_prep()
        if i == N_HEADS - 1:
            o_conv = conv()
        grp, hh = divmod(i, N_HEADS)
        if grp == 0:
            e1, r1 = _softmax2_parts(s_cur[0])
            e2, r2 = _softmax2_parts(s_cur[1])
            o = _dot((e1 * r1 - e2 * (lam * r2)).astype(BF), va)
        else:
            e, r = _softmax2_parts(s_cur[0])
            o = _dot(e.astype(BF), mla["v"] if grp == 1 else vn) * r
        outs[grp] = jnp.where(_group_of(lane, HEAD_DIM) == hh, o, outs[grp])
    o_da = _head_rms(outs[0]) * _row(gsub_ref, layer) * (1.0 - lam_init)
    o_mla, o_na = outs[1], outs[2]

    mixed_ref[...] = jnp.concatenate([o_da, o_mla, o_na, o_conv], axis=1).astype(BF)


def _ctx_layer(layer, x, prm, prev, cast):
    lam_init = 0.8 - 0.6 * math.exp(-0.3 * layer)
    head_blk = pl.BlockSpec((1, None, N_HEADS, HEAD_DIM, SEQ), lambda b: (b, layer, 0, 0, 0))
    head_shp = jax.ShapeDtypeStruct((BATCH, DEPTH, N_HEADS, HEAD_DIM, SEQ), F32)
    in_specs = [pl.BlockSpec((SEQ, D_MODEL), lambda b: (b, 0)),
                _const_spec((16, N_MOD * D_MODEL)),
                _const_spec((DEPTH, D_MODEL)),
                _layer_spec(layer, (IN_COLS, D_MODEL)),
                _const_spec((DEPTH, GROUP_W)),
                _layer_spec(layer, (GROUP_W, N_HEADS * GROUP_W)),
                _const_spec((DEPTH, MLA_KV_LORA)),
                _layer_spec(layer, (MLA_KV_LORA, 2 * GROUP_W)),
                _layer_spec(layer, (4, DA_QK)),
                _const_spec((DEPTH, GROUP_W)),
                _layer_spec(layer, (32, GROUP_W)),
                _const_spec((DEPTH, GROUP_W)),
                _const_spec((DEPTH, GROUP_W)),
                _const_spec((DEPTH, GROUP_W))]
    cast_specs = [pl.BlockSpec((DEPTH, w.shape[1] // BATCH, w.shape[2]), lambda b: (0, b, 0)) for w in cast]
    in_specs += cast_specs + [pl.BlockSpec(memory_space=pl.ANY)] * len(prev)
    out_specs = [pl.BlockSpec((SEQ, D_MODEL), lambda b: (b, 0)),
                 head_blk, head_blk,
                 pl.BlockSpec((1, None, SEQ, MLA_KV_LORA), lambda b: (b, layer, 0, 0)),
                 pl.BlockSpec((1, None, MLA_ROPE, SEQ), lambda b: (b, layer, 0, 0)),
                 head_blk, head_blk]
    out_shape = [jax.ShapeDtypeStruct((BATCH * SEQ, D_MODEL), BF),
                 head_shp, head_shp,
                 jax.ShapeDtypeStruct((BATCH, DEPTH, SEQ, MLA_KV_LORA), F32),
                 jax.ShapeDtypeStruct((BATCH, DEPTH, MLA_ROPE, SEQ), F32),
                 head_shp, head_shp] + [jax.ShapeDtypeStruct(w.shape, BF) for w in cast]
    out_specs += cast_specs
    n_in = len(in_specs) - len(prev)
    return pl.pallas_call(
        functools.partial(_ctx_kernel, layer, lam_init, len(prev), len(cast)),
        grid=(BATCH,),
        in_specs=in_specs,
        out_specs=out_specs,
        out_shape=out_shape,
        input_output_aliases={n_in + k: 1 + k for k in range(len(prev))},
        scratch_shapes=[pltpu.VMEM((SEQ, P_TOT), F32), pltpu.VMEM((SEQ + 32, GROUP_W), F32),
                        pltpu.VMEM((7, SEQ + 8, GROUP_W), F32), pltpu.VMEM((P_TOT, D_MODEL), BF)],
        compiler_params=_cparams("arbitrary"),
        name=f"ctx_layer{layer}",
    )(x, prm["mod"], prm["g_mix"], prm["w_in"], prm["g_q"], prm["w_qup"], prm["g_kv"], prm["w_kvup"],
      prm["lamv"], prm["g_sub"], prm["dw"], prm["cb"], prm["ln_g"], prm["ln_b"], *cast, *prev)


def _lat_proj_kernel(layer, x_ref, mod_ref, gmix_ref, win_ref, gq_ref, wqup_ref, gkv_ref, wkv_ref,
                     cos_ref, sin_ref,
                     daq_ref, mq_ref, naq_ref, g_ref,
                     dak_ref, dav_ref, ka_ref, kb_ref, mv_ref, nak_ref, nav_ref, wbf_ref):
    b = pl.program_id(0)

    @pl.when((b == 0) & (pl.program_id(1) == 0))
    def _():
        _stage_w_in(win_ref, wbf_ref)

    x = x_ref[...]
    sh1 = mod_ref[pl.ds(1 + b, 1), 0:D_MODEL]
    sc1 = mod_ref[pl.ds(1 + b, 1), D_MODEL:2 * D_MODEL]
    h = (_rms(x) * _row(gmix_ref, layer)) * (1.0 + sc1) + sh1
    hb = h.astype(BF)
    proj = jnp.concatenate([_qk(hb, wbf_ref[0:P_SPLIT, :]), _qk(hb, wbf_ref[P_SPLIT:P_TOT, :])], axis=1)
    cos = cos_ref[...]
    sin = sin_ref[...]

    def rope2(z):
        return jnp.concatenate([_rope(z[:, :128], cos[:, :128], sin[:, :128]),
                                _rope(z[:, 128:], cos[:, 128:], sin[:, 128:])], axis=1)

    daq_ref[0] = (rope2(proj[:, P_DAQ:P_DAQ + GROUP_W]) * (DA_SCALE * LOG2E)).astype(BF)
    dak_ref[0] = rope2(proj[:, P_DAK:P_DAK + GROUP_W]).astype(BF)
    dav_ref[0] = proj[:, P_DAV:P_DAV + GROUP_W].astype(BF)

    qd = _rms(proj[:, P_QD:P_QD + GROUP_W]) * _row(gq_ref, layer)
    qm = _dot(qd.astype(BF), wqup_ref[...])
    for hh in range(N_HEADS):
        nope = qm[:, hh * GROUP_W:hh * GROUP_W + 128]
        rope = _rope(qm[:, hh * GROUP_W + 128:(hh + 1) * GROUP_W], cos[:, :128], sin[:, :128])
        mq_ref[0, hh] = (jnp.concatenate([nope, rope], axis=1) * (MLA_SCALE * LOG2E)).astype(BF)
    ckv = _rms(proj[:, P_KVD:P_KVD + MLA_KV_LORA]) * _row(gkv_ref, layer)
    kvm = _dot(ckv.astype(BF), wkv_ref[...])
    kn = kvm[:, :GROUP_W].astype(BF)
    krb = _rope(proj[:, P_KR:P_KR + 128], cos[:, :128], sin[:, :128]).astype(BF)
    ka_ref[0] = jnp.concatenate([kn[:, :128], krb], axis=1)
    kb_ref[0] = jnp.concatenate([kn[:, 128:], krb], axis=1)
    mv_ref[0] = kvm[:, GROUP_W:].astype(BF)

    naq_ref[0] = (proj[:, P_NAQ:P_NAQ + GROUP_W] * NA_SCALE).astype(BF)
    nak_ref[0] = proj[:, P_NAK:P_NAK + GROUP_W].astype(BF)
    nav_ref[0] = proj[:, P_NAV:P_NAV + GROUP_W].astype(BF)
    g_ref[0] = proj[:, P_CONV:P_CONV + GROUP_W] * jax.nn.sigmoid(proj[:, P_CONV + GROUP_W:P_TOT])


def _lat_proj(layer, x, prm, cos_t, sin_t):
    n_t = DEC_SEQ // TM_PROJ
    row_spec = pl.BlockSpec((1, TM_PROJ, GROUP_W), lambda b, j: (b, j, 0))
    in_specs = [pl.BlockSpec((TM_PROJ, D_MODEL), lambda b, j: (b * n_t + j, 0)),
                _const_spec((16, N_MOD * D_MODEL)),
                _const_spec((DEPTH, D_MODEL)),
                _layer_spec(layer, (IN_COLS, D_MODEL)),
                _const_spec((DEPTH, GROUP_W)),
                _layer_spec(layer, (GROUP_W, N_HEADS * GROUP_W)),
                _const_spec((DEPTH, MLA_KV_LORA)),
                _layer_spec(layer, (MLA_KV_LORA, 2 * GROUP_W)),
                pl.BlockSpec((TM_PROJ, GROUP_W), lambda b, j: (j, 0)),
                pl.BlockSpec((TM_PROJ, GROUP_W), lambda b, j: (j, 0))]
    row_shp = jax.ShapeDtypeStruct((DEC_BATCH, DEC_SEQ, GROUP_W), BF)
    out_specs = [row_spec,
                 pl.BlockSpec((1, N_HEADS, TM_PROJ, GROUP_W), lambda b, j: (b, 0, j, 0)),
                 row_spec, row_spec] + [row_spec] * 7
    out_shape = [row_shp,
                 jax.ShapeDtypeStruct((DEC_BATCH, N_HEADS, DEC_SEQ, GROUP_W), BF),
                 row_shp,
                 jax.ShapeDtypeStruct((DEC_BATCH, DEC_SEQ, GROUP_W), F32)] + [row_shp] * 7
    return pl.pallas_call(
        functools.partial(_lat_proj_kernel, layer),
        grid=(DEC_BATCH, n_t),
        in_specs=in_specs,
        out_specs=out_specs,
        out_shape=out_shape,
        scratch_shapes=[pltpu.VMEM((P_TOT, D_MODEL), BF)],
        compiler_params=_cparams("arbitrary", "arbitrary"),
        name=f"lat_proj{layer}",
    )(x, prm["mod"], prm["g_mix"], prm["w_in"], prm["g_q"], prm["w_qup"], prm["g_kv"], prm["w_kvup"],
      cos_t, sin_t)


N_LAT_ATTN_IN = 25


def _lat_attn_kernel(layer, lam_init, emit_mod, *refs):
    (daq_ref, mq_ref, naq_ref, g_ref,
     dak_ref, dav_ref, ka_ref, kb_ref, mv_ref, nak_ref, nav_ref,
     cdak_ref, cdav_ref, cka_ref, ckb_ref, cmv_ref, cnak_ref, cnav_ref,
     nab_ref, lamv_ref, gsub_ref, dw_ref, cb_ref, lng_ref, lnb_ref) = refs[:N_LAT_ATTN_IN]
    refs = refs[N_LAT_ATTN_IN:]
    if emit_mod:
        cv_ref, wmod_ref, bmod_ref, mixed_ref, modn_ref = refs[:5]
    else:
        mixed_ref = refs[0]
    gpad_ref, zsh_ref, kt_ref, v_ref, nac_ref = refs[-5:]
    t = pl.program_id(1)
    lane = _lane_ids(GROUP_W)

    @pl.when(t == 0)
    def _():
        for slot, lat in enumerate((dak_ref, ka_ref, kb_ref)):
            kt_ref[slot, :, 0:DEC_SEQ] = lat[0].T
        kt_ref[0, :, DEC_SEQ:KEYS] = cdak_ref[0].astype(BF)
        kt_ref[1, :, DEC_SEQ:KEYS] = cka_ref[0].T
        kt_ref[2, :, DEC_SEQ:KEYS] = ckb_ref[0].T
        v_ref[0, 0:DEC_SEQ, :] = dav_ref[0]
        v_ref[0, DEC_SEQ:KEYS, :] = cdav_ref[0].T.astype(BF)
        v_ref[1, 0:DEC_SEQ, :] = mv_ref[0]
        v_ref[1, DEC_SEQ:KEYS, :] = cmv_ref[0]
        nac_ref[0] = cnak_ref[0].astype(BF)
        nac_ref[1] = cnav_ref[0].T.astype(BF)

    lam = _diff_lambda(lamv_ref, lam_init)
    qa = daq_ref[0].astype(F32)
    va = v_ref[0]
    ka_t = kt_ref[0]
    vm = v_ref[1]

    kc_t = nac_ref[0]
    vc = nac_ref[1]
    n_items = 2 * N_HEADS + ROWS_PER_TILE

    def na_window(j):
        r = t * ROWS_PER_TILE + j
        start = jnp.clip(r - NA_KR // 2, 0, N_ROWS - NA_KR)
        return start - r + NA_KR - 1, pl.multiple_of(start * GRID_W, GRID_W)

    def scores(i):
        if i < N_HEADS:
            return (_dot(_mask_heads(qa, lane, DA_QK, 2 * i), ka_t),
                    _dot(_mask_heads(qa, lane, DA_QK, 2 * i + 1), ka_t))
        if i < 2 * N_HEADS:
            hh = i - N_HEADS
            return (_dot(mq_ref[0, hh], kt_ref[1 + hh // 2]),)
        j = i - 2 * N_HEADS
        dr0, koff = na_window(j)
        qrow = naq_ref[0, j * GRID_W:(j + 1) * GRID_W, :].astype(F32)
        q4 = jnp.concatenate([_mask_heads(qrow, lane, HEAD_DIM, hh) for hh in range(N_HEADS)], axis=0)
        return (_qk(q4, nak_ref[0, pl.ds(koff, NA_LOCAL), :]) + nab_ref[dr0], _dot(q4, kc_t))

    o_da = jnp.zeros((TQ, GROUP_W), F32)
    o_mla = jnp.zeros((TQ, GROUP_W), F32)
    na_rows = []
    s_next = scores(0)

    base = pl.multiple_of(t * TQ, TQ)
    gpad_ref[16:16 + TQ] = g_ref[0, pl.ds(base, TQ), :]
    lo = g_ref[0, pl.ds(pl.multiple_of(jnp.maximum(base - 16, 0), 16), 16), :]
    hi = g_ref[0, pl.ds(pl.multiple_of(jnp.minimum(base + TQ, DEC_SEQ - 16), 16), 16), :]
    gpad_ref[0:16] = jnp.where(t > 0, lo, 0.0)
    gpad_ref[16 + TQ:32 + TQ] = jnp.where(t < N_QT - 1, hi, 0.0)
    o_conv = _conv_ln_silu(layer, gpad_ref, zsh_ref, TQ, dw_ref, cb_ref, lng_ref, lnb_ref)

    for i in range(n_items):
        s_cur = s_next
        if i + 1 < n_items:
            s_next = scores(i + 1)
        if i < N_HEADS:
            e1, r1 = _softmax2_parts(s_cur[0])
            e2, r2 = _softmax2_parts(s_cur[1])
            p = (e1 - e2 * (lam * r2 / r1)).astype(BF)
            o_da = jnp.where(_group_of(lane, HEAD_DIM) == i, _dot(p, va) * r1, o_da)
        elif i < 2 * N_HEADS:
            e, r = _softmax2_parts(s_cur[0])
            o_mla = jnp.where(_group_of(lane, HEAD_DIM) == i - N_HEADS, _dot(e.astype(BF), vm) * r, o_mla)
        else:
            _, koff = na_window(i - 2 * N_HEADS)
            s_loc, s_ctx = s_cur
            m = jnp.maximum(jnp.max(s_loc, axis=-1, keepdims=True), jnp.max(s_ctx, axis=-1, keepdims=True))
            e_loc = jnp.exp(s_loc - m)
            e_ctx = jnp.exp(s_ctx - m)
            den = jnp.sum(e_loc, axis=-1, keepdims=True) + jnp.sum(e_ctx, axis=-1, keepdims=True)
            o4 = (_dot(e_loc.astype(BF), nav_ref[0, pl.ds(koff, NA_LOCAL), :])
                  + _dot(e_ctx.astype(BF), vc)) * (1.0 / den)
            o_row = o4[0:GRID_W]
            for hh in range(1, N_HEADS):
                o_row = jnp.where(_group_of(lane, HEAD_DIM) == hh, o4[hh * GRID_W:(hh + 1) * GRID_W], o_row)
            na_rows.append(o_row)
    if emit_mod:
        _mod_block(cv_ref, wmod_ref[...], bmod_ref[layer + 1:layer + 2, :], modn_ref)
    o_da = _head_rms(o_da) * _row(gsub_ref, layer) * (1.0 - lam_init)
    o_na = jnp.concatenate(na_rows, axis=0)

    mixed_ref[...] = jnp.concatenate([o_da, o_mla, o_na, o_conv], axis=1).astype(BF)


def _lat_attn(layer, proj_outs, caches, nab, prm, next_mod):
    lam_init = 0.8 - 0.6 * math.exp(-0.3 * layer)
    n_steps = DEC_BATCH * N_QT
    mblk = N_MOD * D_MODEL // n_steps
    step = lambda b, t: b * N_QT + t
    mixed_spec = pl.BlockSpec((TQ, D_MODEL), lambda b, t: (step(b, t), 0))
    mixed_shape = jax.ShapeDtypeStruct((DEC_BATCH * DEC_SEQ, D_MODEL), BF)
    mod_in, out_specs, out_shape = [], mixed_spec, mixed_shape
    if next_mod:
        mod_in = [pl.BlockSpec((16, D_MODEL), lambda b, t: (0, 0)),
                  pl.BlockSpec((None, D_MODEL, mblk), lambda b, t: (layer + 1, 0, step(b, t))),
                  pl.BlockSpec((DEPTH, mblk), lambda b, t: (0, step(b, t)))]
        out_specs = [mixed_spec, pl.BlockSpec((16, mblk), lambda b, t: (0, step(b, t)))]
        out_shape = [mixed_shape, jax.ShapeDtypeStruct((16, N_MOD * D_MODEL), F32)]
    q_spec = pl.BlockSpec((1, TQ, GROUP_W), lambda b, t: (b, t, 0))
    full_k = pl.BlockSpec((1, DEC_SEQ, GROUP_W), lambda b, t: (b, 0, 0))
    cache_spec = pl.BlockSpec((1, None, PAST_LEN, GROUP_W), lambda b, t: (b, layer, 0, 0))
    assert PAST_LEN == GROUP_W
    in_specs = [q_spec,
                pl.BlockSpec((1, N_HEADS, TQ, GROUP_W), lambda b, t: (b, 0, t, 0)),
                q_spec, full_k] + [full_k] * 7 + [cache_spec] * 7 + [
                _layer_spec(layer, (NA_KR, N_HEADS * GRID_W, NA_LOCAL)),
                _layer_spec(layer, (4, DA_QK)),
                _const_spec((DEPTH, GROUP_W)),
                _layer_spec(layer, (32, GROUP_W)),
                _const_spec((DEPTH, GROUP_W)),
                _const_spec((DEPTH, GROUP_W)),
                _const_spec((DEPTH, GROUP_W))] + mod_in
    assert len(in_specs) == N_LAT_ATTN_IN + len(mod_in)
    return pl.pallas_call(
        functools.partial(_lat_attn_kernel, layer, lam_init, bool(next_mod)),
        grid=(DEC_BATCH, N_QT),
        in_specs=in_specs,
        out_specs=out_specs,
        out_shape=out_shape,
        scratch_shapes=[pltpu.VMEM((TQ + 32, GROUP_W), F32), pltpu.VMEM((7, TQ + 8, GROUP_W), F32),
                        pltpu.VMEM((3, GROUP_W, KEYS), BF), pltpu.VMEM((2, KEYS, GROUP_W), BF),
                        pltpu.VMEM((2, PAST_LEN, GROUP_W), BF)],
        compiler_params=_cparams("arbitrary", "arbitrary"),
        name=f"lat_attn{layer}",
    )(*proj_outs, *caches, nab, prm["lamv"], prm["g_sub"], prm["dw"], prm["cb"], prm["ln_g"], prm["ln_b"], *next_mod)


N_FF_CHUNKS = D_FF // D_MODEL


def _ffn_weight_copies(layer, wout_hbm, w1_hbm, w2_hbm, wout_ref, w1_ref, w2_ref, sem):
    copies = [pltpu.make_async_copy(wout_hbm.at[layer], wout_ref, sem.at[0])]
    for c in range(N_FF_CHUNKS):
        lo, hi = c * D_MODEL, (c + 1) * D_MODEL
        copies.append(pltpu.make_async_copy(w1_hbm.at[layer, :, lo:hi], w1_ref.at[:, lo:hi], sem.at[1 + 2 * c]))
        copies.append(pltpu.make_async_copy(w2_hbm.at[layer, lo:hi, :], w2_ref.at[lo:hi, :], sem.at[2 + 2 * c]))
    return copies


def _out_ffn_kernel(layer, tiles_per_mod, mod_base, x_ref, mx_ref, mod_ref, wout_hbm, gff_ref,
                    w1_hbm, w2_hbm, gfin_ref, o_ref, wout_ref, w1_ref, w2_ref, sem):
    copies = _ffn_weight_copies(layer, wout_hbm, w1_hbm, w2_hbm, wout_ref, w1_ref, w2_ref, sem)

    def body(first):
        def arrive(k):
            if first:
                copies[k].wait()

        if first:
            for cp in copies:
                cp.start()
        row = mod_base + pl.program_id(0) // tiles_per_mod
        g1 = mod_ref[pl.ds(row, 1), 2 * D_MODEL:3 * D_MODEL]
        sh2 = mod_ref[pl.ds(row, 1), 3 * D_MODEL:4 * D_MODEL]
        sc2 = mod_ref[pl.ds(row, 1), 4 * D_MODEL:5 * D_MODEL]
        g2 = mod_ref[pl.ds(row, 1), 5 * D_MODEL:6 * D_MODEL]
        arrive(0)
        x1 = x_ref[...] + g1 * _dot(mx_ref[...], wout_ref[...])
        h2 = ((_rms(x1) * _row(gff_ref, layer)) * (1.0 + sc2) + sh2).astype(BF)
        acc = jnp.zeros((TM_FFN, D_MODEL), F32)
        for c in range(N_FF_CHUNKS):
            arrive(1 + 2 * c)
            a = jnp.maximum(_dot(h2, w1_ref[:, c * D_MODEL:(c + 1) * D_MODEL]), 0.0)
            arrive(2 + 2 * c)
            acc = acc + _dot((a * a).astype(BF), w2_ref[c * D_MODEL:(c + 1) * D_MODEL, :])
        x2 = x1 + g2 * acc
        o_ref[...] = _rms(x2) * gfin_ref[...] if layer == DEPTH - 1 else x2

    @pl.when(pl.program_id(0) == 0)
    def _():
        body(True)

    @pl.when(pl.program_id(0) > 0)
    def _():
        body(False)


def _out_ffn(name, layer, tiles_per_mod, mod_base, x, mixed, prm, g_final):
    n = x.shape[0]
    tile = pl.BlockSpec((TM_FFN, D_MODEL), lambda i: (i, 0))
    return pl.pallas_call(
        functools.partial(_out_ffn_kernel, layer, tiles_per_mod, mod_base),
        grid=(n // TM_FFN,),
        in_specs=[tile, tile,
                  _const_spec((16, N_MOD * D_MODEL)),
                  pl.BlockSpec(memory_space=pl.ANY),
                  _const_spec((DEPTH, D_MODEL)),
                  pl.BlockSpec(memory_space=pl.ANY),
                  pl.BlockSpec(memory_space=pl.ANY),
                  _const_spec((1, D_MODEL))],
        out_specs=tile,
        out_shape=jax.ShapeDtypeStruct((n, D_MODEL), F32),
        scratch_shapes=[pltpu.VMEM((D_MODEL, D_MODEL), BF), pltpu.VMEM((D_MODEL, D_FF), BF),
                        pltpu.VMEM((D_FF, D_MODEL), BF), pltpu.SemaphoreType.DMA((1 + 2 * N_FF_CHUNKS,))],
        compiler_params=_cparams("arbitrary"),
        name=name,
    )(x, mixed, prm["mod"], prm["w_out"], prm["g_ff"], prm["w_ff1"], prm["w_ff2"], g_final)


def _rope_tables():
    t = np.arange(DEC_SEQ)
    rows = (t // GRID_W).astype(np.float32)
    cols = (t % GRID_W).astype(np.float32)
    c = np.arange(GROUP_W) % 32
    freqs = np.float32(ROPE_BASE) ** (-np.arange(8, dtype=np.float32) * np.float32(2.0) / np.float32(16))
    pos = np.where((c < 16)[None, :], rows[:, None], cols[:, None]).astype(np.float32)
    ang = (pos * freqs[(c % 16) % 8][None, :]).astype(np.float32)
    first = ((c % 16) < 8)[None, :]
    cos = np.cos(ang).astype(np.float32)
    sin = np.sin(ang).astype(np.float32)
    return jnp.asarray(cos), jnp.asarray(np.where(first, -sin, sin))


def _qup_selection():
    sel = np.zeros((N_HEADS * (MLA_NOPE + MLA_ROPE), N_HEADS * GROUP_W), np.float32)
    for h in range(N_HEADS):
        src = h * (MLA_NOPE + MLA_ROPE)
        dst = h * GROUP_W + (h % 2) * MLA_NOPE
        sel[np.arange(src, src + MLA_NOPE), np.arange(dst, dst + MLA_NOPE)] = 1.0
        sel[np.arange(src + MLA_NOPE, src + MLA_NOPE + MLA_ROPE), h * GROUP_W + 128 + np.arange(MLA_ROPE)] = 1.0
    return jnp.asarray(sel)


def _kvup_selection():
    per_head = MLA_NOPE + HEAD_DIM
    k = [h * per_head + d for h in range(N_HEADS) for d in range(MLA_NOPE)]
    v = [h * per_head + MLA_NOPE + d for h in range(N_HEADS) for d in range(HEAD_DIM)]
    src = np.asarray(k + v)
    sel = np.zeros((src.size, src.size), np.float32)
    sel[src, np.arange(src.size)] = 1.0
    return jnp.asarray(sel)


def _heads_to_rows(c):
    b, l, h, s, d = c.shape
    return jnp.swapaxes(c, -1, -2).reshape(b, l, h * d, s)


def kernel(x_prompt, x_sample, c, cache_da_k, cache_da_v, cache_mla_ckv, cache_mla_krope, cache_na_k, cache_na_v, c_ctx, w_mod, b_mod, g_norm_mix, g_norm_ff, w_in, da_lambda_q1, da_lambda_k1, da_lambda_q2, da_lambda_k2, g_da_subln, g_mla_q, w_mla_qup, g_mla_kv, w_mla_kvup, na_rpb, conv_dw, conv_b, conv_ln_g, conv_ln_b, w_out, w_ff1, w_ff2, g_final):
    prm = dict(
        g_mix=g_norm_mix, w_in=jnp.swapaxes(w_in, 1, 2), g_q=g_mla_q, g_kv=g_mla_kv,
        lamv=jnp.stack([da_lambda_q1, da_lambda_k1, da_lambda_q2, da_lambda_k2], axis=1),
        g_sub=jnp.tile(g_da_subln, (1, N_HEADS)),
        dw=jnp.concatenate([conv_dw, jnp.zeros((DEPTH, 1, GROUP_W), F32)], axis=1),
        cb=conv_b, ln_g=conv_ln_g, ln_b=conv_ln_b,
        g_ff=g_norm_ff)
    cv = jnp.concatenate([c_ctx[None, :], c, jnp.zeros((16 - 1 - DEC_BATCH, D_MODEL), F32)], axis=0)
    cos_t, sin_t = _rope_tables()
    g_final2 = g_final.reshape(1, D_MODEL)

    c_kr_t = jnp.swapaxes(cache_mla_krope, -1, -2)
    prm["mod"], nab, cka, ckb, cmv, prm["w_qup"], prm["w_kvup"] = _prep(
        cv, w_mod, b_mod, na_rpb, cache_mla_ckv, c_kr_t, w_mla_qup, w_mla_kvup)
    caches = (_heads_to_rows(cache_da_k), _heads_to_rows(cache_da_v), cka, ckb, cmv,
              _heads_to_rows(cache_na_k), _heads_to_rows(cache_na_v))

    xp = x_prompt.reshape(BATCH * SEQ, D_MODEL)
    xs = x_sample.reshape(DEC_BATCH * DEC_SEQ, D_MODEL)
    new_ctx = ()
    for l in range(DEPTH):
        cast = (w_out, w_ff1, w_ff2) if l == 0 else ()
        mixed_p, *rest = _ctx_layer(l, xp, prm, tuple(new_ctx), cast)
        new_ctx = rest[:len(rest) - len(cast)]
        if cast:
            prm["w_out"], prm["w_ff1"], prm["w_ff2"] = rest[len(new_ctx):]
        xp = _out_ffn(f"ctx_ffn{l}", l, BATCH * SEQ // TM_FFN, 0, xp, mixed_p, prm, g_final2)
        proj_outs = _lat_proj(l, xs, prm, cos_t, sin_t)
        if l + 1 < DEPTH:
            mixed_s, mod_next = _lat_attn(l, proj_outs, caches, nab, prm, (cv, w_mod, b_mod))
        else:
            mixed_s, mod_next = _lat_attn(l, proj_outs, caches, nab, prm, ()), None
        xs = _out_ffn(f"lat_ffn{l}", l, DEC_SEQ // TM_FFN, 1, xs, mixed_s, prm, g_final2)
        prm["mod"] = mod_next

    y_prompt = xp.reshape(BATCH, SEQ, D_MODEL)
    y_sample = xs.reshape(DEC_BATCH, DEC_SEQ, D_MODEL)
    new_ctx = [a if k == 2 else jnp.swapaxes(a, -1, -2) for k, a in enumerate(new_ctx)]
    return (y_prompt, y_sample) + tuple(new_ctx)
```

```python
import functools
import math

import numpy as np
import jax
import jax.numpy as jnp
from jax import lax
from jax.experimental import pallas as pl
from jax.experimental.pallas import tpu as pltpu

F32 = jnp.float32
BF = jnp.bfloat16

D_MODEL = 1024
BATCH = 16
SEQ = 256
DEPTH = 2
DEC_BATCH = 2
DEC_SEQ = 2048
PAST_LEN = 256
GRID_W = 64
GROUP_W = 256
HEAD_DIM = 64
N_HEADS = 4
DA_QK = 32
MLA_NOPE = 64
MLA_ROPE = 32
MLA_KV_LORA = 128
NA_KR = 8
NA_KC = 16
CONV_W = 31
D_FF = 4096
ROPE_BASE = 10000.0
EPS = 1e-6
N_MOD = 6
IN_COLS = 2464

P_DAQ, P_DAK, P_DAV, P_QD, P_KVD, P_KR = 0, 256, 512, 768, 1024, 1152
P_NAQ, P_NAK, P_NAV, P_CONV, P_TOT = 1280, 1536, 1792, 2048, 2560
P_SPLIT = P_NAQ
N_CTX_IN = 14
N_CTX_OUT = 7
KR_ORIG_END = 1184
KR_PAD = 128 - MLA_ROPE

DA_SCALE = DA_QK ** -0.5
MLA_SCALE = (MLA_NOPE + MLA_ROPE) ** -0.5
NA_SCALE = HEAD_DIM ** -0.5
LOG2E = math.log2(math.e)

TQ = 256
N_QT = DEC_SEQ // TQ
KEYS = DEC_SEQ + PAST_LEN
ROWS_PER_TILE = TQ // GRID_W
N_ROWS = DEC_SEQ // GRID_W
NA_LOCAL = NA_KR * GRID_W
TM_FFN = 512
TM_PROJ = 512
VMEM_LIMIT = 58 * 1024 * 1024

NT_DIMS = (((1,), (1,)), ((), ()))


def _cparams(*sem):
    return pltpu.CompilerParams(dimension_semantics=sem, vmem_limit_bytes=VMEM_LIMIT)


def _const_spec(shape):
    nd = len(shape)
    return pl.BlockSpec(shape, lambda *_: (0,) * nd, pipeline_mode=pl.Buffered(1))


def _layer_spec(layer, shape):
    nd = len(shape)
    return pl.BlockSpec((None,) + tuple(shape), lambda *_: (layer,) + (0,) * nd, pipeline_mode=pl.Buffered(1))


def _rms(x):
    return x * lax.rsqrt(jnp.mean(x * x, axis=-1, keepdims=True) + EPS)


def _row(ref, layer):
    return ref[layer:layer + 1, :]


def _dot(a, b):
    return jnp.dot(a, b, preferred_element_type=F32)


def _qk(q, k):
    return lax.dot_general(q, k, NT_DIMS, preferred_element_type=F32)


def _softmax2_parts(s):
    m = jnp.max(s, axis=-1, keepdims=True)
    e = jnp.exp2(s - m)
    return e, 1.0 / jnp.sum(e, axis=-1, keepdims=True)


def _lane_ids(width):
    return lax.broadcasted_iota(jnp.int32, (1, width), 1)


def _diff_lambda(lamv_ref, lam_init):
    v = lamv_ref[...]
    a = jnp.exp(jnp.sum(v[0:1] * v[1:2], axis=-1, keepdims=True))
    b = jnp.exp(jnp.sum(v[2:3] * v[3:4], axis=-1, keepdims=True))
    return a - b + lam_init


def _group_of(lane, group):
    return lane >> (group.bit_length() - 1)


def _head_rms(x):
    r = _group_of(lax.broadcasted_iota(jnp.int32, (GROUP_W, GROUP_W), 0), HEAD_DIM)
    c = _group_of(lax.broadcasted_iota(jnp.int32, (GROUP_W, GROUP_W), 1), HEAD_DIM)
    ones_bd = jnp.where(r == c, 1.0, 0.0).astype(BF)
    sq = x * x
    hi = sq.astype(BF)
    lo = (sq - hi.astype(F32)).astype(BF)
    ss = _dot(hi, ones_bd) + _dot(lo, ones_bd)
    return x * lax.rsqrt(ss * (1.0 / HEAD_DIM) + EPS)


def _rope(z, cos, sin):
    lane = _lane_ids(z.shape[1])
    swapped = jnp.where((lane & 15) < 8, pltpu.roll(z, 120, 1), pltpu.roll(z, 8, 1))
    return z * cos + swapped * sin


def _stage_w_in(wint_ref, wbt_ref):
    wbt_ref[0:KR_ORIG_END, :] = wint_ref[0:KR_ORIG_END, :].astype(BF)
    wbt_ref[KR_ORIG_END:P_SPLIT, :] = jnp.zeros((P_SPLIT - KR_ORIG_END, D_MODEL), BF)
    wbt_ref[P_SPLIT:P_TOT, :] = wint_ref[KR_ORIG_END:IN_COLS, :].astype(BF)


def _mask_heads(qf, lane, group, idx):
    return jnp.where(_group_of(lane, group) == idx, qf, 0.0).astype(BF)


def _conv_ln_silu(layer, gpad_ref, zsh_ref, n, dw_ref, cb_ref, lng_ref, lnb_ref):
    y = jnp.zeros((n, GROUP_W), F32) + _row(cb_ref, layer)
    for b in range(8):
        z = None
        for a in range(4):
            t = 8 * a + b - 1
            if 0 <= t < CONV_W:
                term = gpad_ref[pl.ds(8 * a, n + 8), :] * dw_ref[t:t + 1, :]
                z = term if z is None else z + term
        if b == 0:
            y = y + z[0:n]
        else:
            zsh_ref[b - 1] = z
            y = y + zsh_ref[b - 1, pl.ds(b, n), :]
    mu = jnp.mean(y, axis=-1, keepdims=True)
    yc = y - mu
    var = jnp.mean(yc * yc, axis=-1, keepdims=True)
    z = yc * lax.rsqrt(var + EPS) * _row(lng_ref, layer) + _row(lnb_ref, layer)
    return z * jax.nn.sigmoid(z)


def _mod_block(cv_ref, w, b_row, o_ref):
    c = cv_ref[...]
    a = (c * jax.nn.sigmoid(c)).astype(BF)
    o_ref[...] = _dot(a, w.astype(BF)) + b_row


MOD_BLOCKS = N_HEADS


def _prep_kernel(cv_ref, wmod_ref, bmod_ref, rpb_ref, ckv_ref, kr_ref, wq_ref, wkv_ref, selq_ref, selkv_ref,
                 mod_ref, nab_ref, ka_ref, kb_ref, v_ref, wqe_ref, wkvp_ref, tp_ref):
    @pl.when(pl.program_id(0) == 0)
    def _():
        _mod_block(cv_ref, wmod_ref[0], bmod_ref[0:1, :], mod_ref)

    _na_bias_kernel(rpb_ref, nab_ref, tp_ref)

    @pl.when(pl.program_id(1) < DEC_BATCH)
    def _():
        wqe_ref[0] = _dot(wq_ref[0].astype(BF), selq_ref[...].astype(BF)).astype(BF)
        w_kvup = _dot(wkv_ref[0].astype(BF), selkv_ref[...].astype(BF)).astype(BF)
        wkvp_ref[0] = w_kvup
        _mla_cache_kernel(ckv_ref, kr_ref, w_kvup, ka_ref, kb_ref, v_ref)


def _prep(cv, w_mod, b_mod, rpb, c_ckv, c_kr_t, w_qup, w_kvup):
    wblk = N_MOD * D_MODEL // MOD_BLOCKS
    mblk = lambda l, j: jnp.where(l == 0, j, MOD_BLOCKS - 1)
    cb = lambda l, j: (jnp.minimum(j, DEC_BATCH - 1), l, 0, 0)
    cache_blk = pl.BlockSpec((1, 1, PAST_LEN, GROUP_W), cb)
    cache_shp = jax.ShapeDtypeStruct((DEC_BATCH, DEPTH, PAST_LEN, GROUP_W), BF)
    per_layer = lambda shape: pl.BlockSpec((1,) + tuple(shape), lambda l, j: (l,) + (0,) * len(shape))
    selq, selkv = _qup_selection(), _kvup_selection()
    return pl.pallas_call(
        _prep_kernel,
        grid=(DEPTH, MOD_BLOCKS),
        in_specs=[pl.BlockSpec((16, D_MODEL), lambda l, j: (0, 0)),
                  pl.BlockSpec((1, D_MODEL, wblk), lambda l, j: (0, 0, mblk(l, j))),
                  pl.BlockSpec((DEPTH, wblk), lambda l, j: (0, mblk(l, j))),
                  pl.BlockSpec(memory_space=pltpu.SMEM),
                  pl.BlockSpec((1, 1, PAST_LEN, MLA_KV_LORA), cb),
                  pl.BlockSpec((1, 1, MLA_ROPE, PAST_LEN), cb),
                  per_layer(w_qup.shape[1:]), per_layer(w_kvup.shape[1:]),
                  _const_spec(selq.shape), _const_spec(selkv.shape)],
        out_specs=[pl.BlockSpec((16, wblk), lambda l, j: (0, mblk(l, j))),
                   pl.BlockSpec((1, NA_KR, GRID_W, NA_LOCAL), lambda l, j: (l, 0, j, 0)),
                   cache_blk, cache_blk, cache_blk,
                   per_layer((GROUP_W, N_HEADS * GROUP_W)), per_layer((MLA_KV_LORA, 2 * GROUP_W))],
        out_shape=[jax.ShapeDtypeStruct((16, N_MOD * D_MODEL), F32),
                   jax.ShapeDtypeStruct((DEPTH, NA_KR, N_HEADS * GRID_W, NA_LOCAL), F32),
                   cache_shp, cache_shp, cache_shp,
                   jax.ShapeDtypeStruct((DEPTH, GROUP_W, N_HEADS * GROUP_W), BF),
                   jax.ShapeDtypeStruct((DEPTH, MLA_KV_LORA, 2 * GROUP_W), BF)],
        scratch_shapes=[pltpu.VMEM((N_DR - 1, GRID_W, 128), F32)],
        compiler_params=_cparams("arbitrary", "arbitrary"),
        name="prep",
    )(cv, w_mod, b_mod, rpb.reshape(-1), c_ckv, c_kr_t, w_qup, w_kvup, selq, selkv)


def _mla_cache_kernel(ckv_ref, kr_ref, w_kvup, ka_ref, kb_ref, v_ref):
    kvm = _dot(ckv_ref[0, 0].astype(BF), w_kvup)
    kr_t = jnp.concatenate([kr_ref[0, 0], jnp.zeros((KR_PAD, PAST_LEN), F32)], axis=0)
    kr = kr_t.T.astype(BF)
    kn = kvm[:, :GROUP_W].astype(BF)
    ka_ref[0, 0] = jnp.concatenate([kn[:, :128], kr], axis=1)
    kb_ref[0, 0] = jnp.concatenate([kn[:, 128:], kr], axis=1)
    v_ref[0, 0] = kvm[:, GROUP_W:].astype(BF)


N_DR = 2 * NA_KR - 1
N_DC = 2 * NA_KC - 1


def _na_bias_kernel(rpb_ref, o_ref, tp_ref):
    base = (pl.program_id(0) * N_HEADS + pl.program_id(1)) * (N_DR * N_DC)
    cq = lax.broadcasted_iota(jnp.int32, (GRID_W, 128), 0)
    lane = lax.broadcasted_iota(jnp.int32, (GRID_W, 128), 1)
    ck = lane & (GRID_W - 1)
    upper = lane >= GRID_W
    cstart = jnp.clip(cq - NA_KC // 2, 0, GRID_W - NA_KC)
    ok = (ck >= cstart) & (ck < cstart + NA_KC)

    j = lax.broadcasted_iota(jnp.int32, (8, 128), 1)
    dclip = jnp.clip(jnp.where(j < GRID_W, j, j - 128), -(NA_KC - 1), NA_KC - 1) + NA_KC - 1

    def toeplitz(dr):
        u = jnp.zeros((8, 128), F32)
        for d in range(N_DC):
            u = jnp.where(dclip == d, rpb_ref[base + dr * N_DC + d], u)
        rows = jnp.broadcast_to(u[0:1], (GRID_W, 128))
        return pltpu.roll(rows, 0, 1, stride=1, stride_axis=0)

    tabs = [toeplitz(dr) for dr in range(N_DR)]
    for dr in range(N_DR - 1):
        val = jnp.where(upper, pltpu.roll(tabs[dr + 1], GRID_W, 1), tabs[dr])
        tp_ref[dr] = jnp.where(ok, val, -jnp.inf)
    for dr0 in range(NA_KR):
        o_ref[0, dr0] = jnp.concatenate([tp_ref[dr0 + 2 * m] for m in range(NA_KR // 2)], axis=1)


def _ctx_kernel(layer, lam_init, n_prev, n_cast, *refs):
    (x_ref, mod_ref, gmix_ref, win_ref, gq_ref, wqup_ref, gkv_ref, wkv_ref,
     lamv_ref, gsub_ref, dw_ref, cb_ref, lng_ref, lnb_ref) = refs[:N_CTX_IN]
    cast_in = refs[N_CTX_IN:N_CTX_IN + n_cast]
    outs = refs[N_CTX_IN + n_cast + n_prev:]
    mixed_ref, dak_ref, dav_ref, ckv_ref, kr_ref, nak_ref, nav_ref = outs[:N_CTX_OUT]
    cast_out = outs[N_CTX_OUT:N_CTX_OUT + n_cast]
    proj_ref, gpad_ref, zsh_ref, wbf_ref = outs[N_CTX_OUT + n_cast:]

    @pl.when(pl.program_id(0) == 0)
    def _():
        _stage_w_in(win_ref, wbf_ref)

    x = x_ref[...]
    sh1 = mod_ref[0:1, 0:D_MODEL]
    sc1 = mod_ref[0:1, D_MODEL:2 * D_MODEL]
    h = ((_rms(x) * _row(gmix_ref, layer)) * (1.0 + sc1) + sh1).astype(BF)
    proj_ref[:, 0:P_SPLIT] = _qk(h, wbf_ref[0:P_SPLIT, :])
    proj_ref[:, P_SPLIT:P_TOT] = _qk(h, wbf_ref[P_SPLIT:P_TOT, :])

    for ref, col in ((dak_ref, P_DAK), (dav_ref, P_DAV), (nak_ref, P_NAK), (nav_ref, P_NAV)):
        slab_t = proj_ref[:, col:col + GROUP_W].T
        for hh in range(N_HEADS):
            ref[0, hh] = slab_t[hh * HEAD_DIM:(hh + 1) * HEAD_DIM, :]

    lane = _lane_ids(GROUP_W)

    lam = _diff_lambda(lamv_ref, lam_init)

    qa = proj_ref[:, P_DAQ:P_DAQ + GROUP_W] * (DA_SCALE * LOG2E)
    ka = proj_ref[:, P_DAK:P_DAK + GROUP_W].astype(BF)
    va = proj_ref[:, P_DAV:P_DAV + GROUP_W].astype(BF)
    qn = proj_ref[:, P_NAQ:P_NAQ + GROUP_W] * (NA_SCALE * LOG2E)
    kn2 = proj_ref[:, P_NAK:P_NAK + GROUP_W].astype(BF)
    vn = proj_ref[:, P_NAV:P_NAV + GROUP_W].astype(BF)
    mla = {}

    def mla_prep():
        qd = _rms(proj_ref[:, P_QD:P_QD + GROUP_W]) * _row(gq_ref, layer)
        ckv = _rms(proj_ref[:, P_KVD:P_KVD + MLA_KV_LORA]) * _row(gkv_ref, layer)
        ckv_ref[0] = ckv
        kr_pad = proj_ref[:, P_KR:P_KR + 128]
        kr_ref[0] = kr_pad.T[0:MLA_ROPE, :]
        mla["q"] = (_dot(qd.astype(BF), wqup_ref[...]) * (MLA_SCALE * LOG2E)).astype(BF)
        kvm = _dot(ckv.astype(BF), wkv_ref[...])
        kn = kvm[:, :GROUP_W].astype(BF)
        krb = kr_pad.astype(BF)
        mla["k"] = (jnp.concatenate([kn[:, :128], krb], axis=1), jnp.concatenate([kn[:, 128:], krb], axis=1))
        mla["v"] = kvm[:, GROUP_W:].astype(BF)

    def conv():
        g = proj_ref[:, P_CONV:P_CONV + GROUP_W] * jax.nn.sigmoid(proj_ref[:, P_CONV + GROUP_W:P_TOT])
        gpad_ref[0:16] = jnp.zeros((16, GROUP_W), F32)
        gpad_ref[16 + SEQ:32 + SEQ] = jnp.zeros((16, GROUP_W), F32)
        gpad_ref[16:16 + SEQ] = g
        return _conv_ln_silu(layer, gpad_ref, zsh_ref, SEQ, dw_ref, cb_ref, lng_ref, lnb_ref)

    def scores(i):
        if i < N_HEADS:
            return (_qk(_mask_heads(qa, lane, DA_QK, 2 * i), ka),
                    _qk(_mask_heads(qa, lane, DA_QK, 2 * i + 1), ka))
        if i < 2 * N_HEADS:
            hh = i - N_HEADS
            return (_qk(mla["q"][:, hh * GROUP_W:(hh + 1) * GROUP_W], mla["k"][hh // 2]),)
        return (_qk(_mask_heads(qn, lane, HEAD_DIM, i - 2 * N_HEADS), kn2),)

    outs = [jnp.zeros((SEQ, GROUP_W), F32)] * 3
    s_next = scores(0)
    o_conv = None
    for i in range(3 * N_HEADS):
        s_cur = s_next
        if i + 1 < 3 * N_HEADS:
            s_next = scores(i + 1)
        if i == 0:
            mla_prep()
        if i == N_HEADS - 1:
            o_conv = conv()
        grp, hh = divmod(i, N_HEADS)
        if grp == 0:
            e1, r1 = _softmax2_parts(s_cur[0])
            e2, r2 = _softmax2_parts(s_cur[1])
            o = _dot((e1 * r1 - e2 * (lam * r2)).astype(BF), va)
        else:
            e, r = _softmax2_parts(s_cur[0])
            o = _dot(e.astype(BF), mla["v"] if grp == 1 else vn) * r
        outs[grp] = jnp.where(_group_of(lane, HEAD_DIM) == hh, o, outs[grp])
    o_da = _head_rms(outs[0]) * _row(gsub_ref, layer) * (1.0 - lam_init)
    o_mla, o_na = outs[1], outs[2]

    mixed_ref[...] = jnp.concatenate([o_da, o_mla, o_na, o_conv], axis=1).astype(BF)

    for src, dst in zip(cast_in, cast_out):
        dst[...] = src[...].astype(BF)


def _ctx_layer(layer, x, prm, prev, cast):
    lam_init = 0.8 - 0.6 * math.exp(-0.3 * layer)
    head_blk = pl.BlockSpec((1, None, N_HEADS, HEAD_DIM, SEQ), lambda b: (b, layer, 0, 0, 0))
    head_shp = jax.ShapeDtypeStruct((BATCH, DEPTH, N_HEADS, HEAD_DIM, SEQ), F32)
    in_specs = [pl.BlockSpec((SEQ, D_MODEL), lambda b: (b, 0)),
                _const_spec((16, N_MOD * D_MODEL)),
                _const_spec((DEPTH, D_MODEL)),
                _layer_spec(layer, (IN_COLS, D_MODEL)),
                _const_spec((DEPTH, GROUP_W)),
                _layer_spec(layer, (GROUP_W, N_HEADS * GROUP_W)),
                _const_spec((DEPTH, MLA_KV_LORA)),
                _layer_spec(layer, (MLA_KV_LORA, 2 * GROUP_W)),
                _layer_spec(layer, (4, DA_QK)),
                _const_spec((DEPTH, GROUP_W)),
                _layer_spec(layer, (32, GROUP_W)),
                _const_spec((DEPTH, GROUP_W)),
                _const_spec((DEPTH, GROUP_W)),
                _const_spec((DEPTH, GROUP_W))]
    cast_specs = [pl.BlockSpec((DEPTH, w.shape[1] // BATCH, w.shape[2]), lambda b: (0, b, 0)) for w in cast]
    in_specs += cast_specs + [pl.BlockSpec(memory_space=pl.ANY)] * len(prev)
    out_specs = [pl.BlockSpec((SEQ, D_MODEL), lambda b: (b, 0)),
                 head_blk, head_blk,
                 pl.BlockSpec((1, None, SEQ, MLA_KV_LORA), lambda b: (b, layer, 0, 0)),
                 pl.BlockSpec((1, None, MLA_ROPE, SEQ), lambda b: (b, layer, 0, 0)),
                 head_blk, head_blk]
    out_shape = [jax.ShapeDtypeStruct((BATCH * SEQ, D_MODEL), BF),
                 head_shp, head_shp,
                 jax.ShapeDtypeStruct((BATCH, DEPTH, SEQ, MLA_KV_LORA), F32),
                 jax.ShapeDtypeStruct((BATCH, DEPTH, MLA_ROPE, SEQ), F32),
                 head_shp, head_shp] + [jax.ShapeDtypeStruct(w.shape, BF) for w in cast]
    out_specs += cast_specs
    n_in = len(in_specs) - len(prev)
    return pl.pallas_call(
        functools.partial(_ctx_kernel, layer, lam_init, len(prev), len(cast)),
        grid=(BATCH,),
        in_specs=in_specs,
        out_specs=out_specs,
        out_shape=out_shape,
        input_output_aliases={n_in + k: 1 + k for k in range(len(prev))},
        scratch_shapes=[pltpu.VMEM((SEQ, P_TOT), F32), pltpu.VMEM((SEQ + 32, GROUP_W), F32),
                        pltpu.VMEM((7, SEQ + 8, GROUP_W), F32), pltpu.VMEM((P_TOT, D_MODEL), BF)],
        compiler_params=_cparams("arbitrary"),
        name=f"ctx_layer{layer}",
    )(x, prm["mod"], prm["g_mix"], prm["w_in"], prm["g_q"], prm["w_qup"], prm["g_kv"], prm["w_kvup"],
      prm["lamv"], prm["g_sub"], prm["dw"], prm["cb"], prm["ln_g"], prm["ln_b"], *cast, *prev)


def _lat_proj_kernel(layer, x_ref, mod_ref, gmix_ref, win_ref, gq_ref, wqup_ref, gkv_ref, wkv_ref,
                     cos_ref, sin_ref,
                     daq_ref, mq_ref, naq_ref, g_ref,
                     dak_ref, dav_ref, ka_ref, kb_ref, mv_ref, nak_ref, nav_ref, wbf_ref):
    b = pl.program_id(0)

    @pl.when((b == 0) & (pl.program_id(1) == 0))
    def _():
        _stage_w_in(win_ref, wbf_ref)

    x = x_ref[...]
    sh1 = mod_ref[pl.ds(1 + b, 1), 0:D_MODEL]
    sc1 = mod_ref[pl.ds(1 + b, 1), D_MODEL:2 * D_MODEL]
    h = (_rms(x) * _row(gmix_ref, layer)) * (1.0 + sc1) + sh1
    hb = h.astype(BF)
    proj = jnp.concatenate([_qk(hb, wbf_ref[0:P_SPLIT, :]), _qk(hb, wbf_ref[P_SPLIT:P_TOT, :])], axis=1)
    cos = cos_ref[...]
    sin = sin_ref[...]

    def rope2(z):
        return jnp.concatenate([_rope(z[:, :128], cos[:, :128], sin[:, :128]),
                                _rope(z[:, 128:], cos[:, 128:], sin[:, 128:])], axis=1)

    daq_ref[0] = (rope2(proj[:, P_DAQ:P_DAQ + GROUP_W]) * (DA_SCALE * LOG2E)).astype(BF)
    dak_ref[0] = rope2(proj[:, P_DAK:P_DAK + GROUP_W]).astype(BF)
    dav_ref[0] = proj[:, P_DAV:P_DAV + GROUP_W].astype(BF)

    qd = _rms(proj[:, P_QD:P_QD + GROUP_W]) * _row(gq_ref, layer)
    qm = _dot(qd.astype(BF), wqup_ref[...])
    for hh in range(N_HEADS):
        nope = qm[:, hh * GROUP_W:hh * GROUP_W + 128]
        rope = _rope(qm[:, hh * GROUP_W + 128:(hh + 1) * GROUP_W], cos[:, :128], sin[:, :128])
        mq_ref[0, hh] = (jnp.concatenate([nope, rope], axis=1) * (MLA_SCALE * LOG2E)).astype(BF)
    ckv = _rms(proj[:, P_KVD:P_KVD + MLA_KV_LORA]) * _row(gkv_ref, layer)
    kvm = _dot(ckv.astype(BF), wkv_ref[...])
    kn = kvm[:, :GROUP_W].astype(BF)
    krb = _rope(proj[:, P_KR:P_KR + 128], cos[:, :128], sin[:, :128]).astype(BF)
    ka_ref[0] = jnp.concatenate([kn[:, :128], krb], axis=1)
    kb_ref[0] = jnp.concatenate([kn[:, 128:], krb], axis=1)
    mv_ref[0] = kvm[:, GROUP_W:].astype(BF)

    naq_ref[0] = (proj[:, P_NAQ:P_NAQ + GROUP_W] * NA_SCALE).astype(BF)
    nak_ref[0] = proj[:, P_NAK:P_NAK + GROUP_W].astype(BF)
    nav_ref[0] = proj[:, P_NAV:P_NAV + GROUP_W].astype(BF)
    g_ref[0] = proj[:, P_CONV:P_CONV + GROUP_W] * jax.nn.sigmoid(proj[:, P_CONV + GROUP_W:P_TOT])


def _lat_proj(layer, x, prm, cos_t, sin_t):
    n_t = DEC_SEQ // TM_PROJ
    row_spec = pl.BlockSpec((1, TM_PROJ, GROUP_W), lambda b, j: (b, j, 0))
    in_specs = [pl.BlockSpec((TM_PROJ, D_MODEL), lambda b, j: (b * n_t + j, 0)),
                _const_spec((16, N_MOD * D_MODEL)),
                _const_spec((DEPTH, D_MODEL)),
                _layer_spec(layer, (IN_COLS, D_MODEL)),
                _const_spec((DEPTH, GROUP_W)),
                _layer_spec(layer, (GROUP_W, N_HEADS * GROUP_W)),
                _const_spec((DEPTH, MLA_KV_LORA)),
                _layer_spec(layer, (MLA_KV_LORA, 2 * GROUP_W)),
                pl.BlockSpec((TM_PROJ, GROUP_W), lambda b, j: (j, 0)),
                pl.BlockSpec((TM_PROJ, GROUP_W), lambda b, j: (j, 0))]
    row_shp = jax.ShapeDtypeStruct((DEC_BATCH, DEC_SEQ, GROUP_W), BF)
    out_specs = [row_spec,
                 pl.BlockSpec((1, N_HEADS, TM_PROJ, GROUP_W), lambda b, j: (b, 0, j, 0)),
                 row_spec, row_spec] + [row_spec] * 7
    out_shape = [row_shp,
                 jax.ShapeDtypeStruct((DEC_BATCH, N_HEADS, DEC_SEQ, GROUP_W), BF),
                 row_shp,
                 jax.ShapeDtypeStruct((DEC_BATCH, DEC_SEQ, GROUP_W), F32)] + [row_shp] * 7
    return pl.pallas_call(
        functools.partial(_lat_proj_kernel, layer),
        grid=(DEC_BATCH, n_t),
        in_specs=in_specs,
        out_specs=out_specs,
        out_shape=out_shape,
        scratch_shapes=[pltpu.VMEM((P_TOT, D_MODEL), BF)],
        compiler_params=_cparams("arbitrary", "arbitrary"),
        name=f"lat_proj{layer}",
    )(x, prm["mod"], prm["g_mix"], prm["w_in"], prm["g_q"], prm["w_qup"], prm["g_kv"], prm["w_kvup"],
      cos_t, sin_t)


N_LAT_ATTN_IN = 25


def _lat_attn_kernel(layer, lam_init, emit_mod, *refs):
    (daq_ref, mq_ref, naq_ref, g_ref,
     dak_ref, dav_ref, ka_ref, kb_ref, mv_ref, nak_ref, nav_ref,
     cdak_ref, cdav_ref, cka_ref, ckb_ref, cmv_ref, cnak_ref, cnav_ref,
     nab_ref, lamv_ref, gsub_ref, dw_ref, cb_ref, lng_ref, lnb_ref) = refs[:N_LAT_ATTN_IN]
    refs = refs[N_LAT_ATTN_IN:]
    if emit_mod:
        cv_ref, wmod_ref, bmod_ref, mixed_ref, modn_ref = refs[:5]
    else:
        mixed_ref = refs[0]
    gpad_ref, zsh_ref, kt_ref, v_ref, nac_ref = refs[-5:]
    t = pl.program_id(1)
    lane = _lane_ids(GROUP_W)

    @pl.when(t == 0)
    def _():
        for slot, lat in enumerate((dak_ref, ka_ref, kb_ref)):
            kt_ref[slot, :, 0:DEC_SEQ] = lat[0].T
        kt_ref[0, :, DEC_SEQ:KEYS] = cdak_ref[0].astype(BF)
        kt_ref[1, :, DEC_SEQ:KEYS] = cka_ref[0].T
        kt_ref[2, :, DEC_SEQ:KEYS] = ckb_ref[0].T
        v_ref[0, 0:DEC_SEQ, :] = dav_ref[0]
        v_ref[0, DEC_SEQ:KEYS, :] = cdav_ref[0].T.astype(BF)
        v_ref[1, 0:DEC_SEQ, :] = mv_ref[0]
        v_ref[1, DEC_SEQ:KEYS, :] = cmv_ref[0]
        nac_ref[0] = cnak_ref[0].astype(BF)
        nac_ref[1] = cnav_ref[0].T.astype(BF)

    lam = _diff_lambda(lamv_ref, lam_init)
    qa = daq_ref[0].astype(F32)
    va = v_ref[0]
    ka_t = kt_ref[0]
    vm = v_ref[1]

    kc_t = nac_ref[0]
    vc = nac_ref[1]
    n_items = 2 * N_HEADS + ROWS_PER_TILE

    def na_window(j):
        r = t * ROWS_PER_TILE + j
        start = jnp.clip(r - NA_KR // 2, 0, N_ROWS - NA_KR)
        return start - r + NA_KR - 1, pl.multiple_of(start * GRID_W, GRID_W)

    def scores(i):
        if i < N_HEADS:
            return (_dot(_mask_heads(qa, lane, DA_QK, 2 * i), ka_t),
                    _dot(_mask_heads(qa, lane, DA_QK, 2 * i + 1), ka_t))
        if i < 2 * N_HEADS:
            hh = i - N_HEADS
            return (_dot(mq_ref[0, hh], kt_ref[1 + hh // 2]),)
        j = i - 2 * N_HEADS
        dr0, koff = na_window(j)
        qrow = naq_ref[0, j * GRID_W:(j + 1) * GRID_W, :].astype(F32)
        q4 = jnp.concatenate([_mask_heads(qrow, lane, HEAD_DIM, hh) for hh in range(N_HEADS)], axis=0)
        return (_qk(q4, nak_ref[0, pl.ds(koff, NA_LOCAL), :]) + nab_ref[dr0], _dot(q4, kc_t))

    o_da = jnp.zeros((TQ, GROUP_W), F32)
    o_mla = jnp.zeros((TQ, GROUP_W), F32)
    na_rows = []
    s_next = scores(0)

    base = pl.multiple_of(t * TQ, TQ)
    gpad_ref[16:16 + TQ] = g_ref[0, pl.ds(base, TQ), :]
    lo = g_ref[0, pl.ds(pl.multiple_of(jnp.maximum(base - 16, 0), 16), 16), :]
    hi = g_ref[0, pl.ds(pl.multiple_of(jnp.minimum(base + TQ, DEC_SEQ - 16), 16), 16), :]
    gpad_ref[0:16] = jnp.where(t > 0, lo, 0.0)
    gpad_ref[16 + TQ:32 + TQ] = jnp.where(t < N_QT - 1, hi, 0.0)
    o_conv = _conv_ln_silu(layer, gpad_ref, zsh_ref, TQ, dw_ref, cb_ref, lng_ref, lnb_ref)

    for i in range(n_items):
        s_cur = s_next
        if i + 1 < n_items:
            s_next = scores(i + 1)
        if i < N_HEADS:
            e1, r1 = _softmax2_parts(s_cur[0])
            e2, r2 = _softmax2_parts(s_cur[1])
            p = (e1 - e2 * (lam * r2 / r1)).astype(BF)
            o_da = jnp.where(_group_of(lane, HEAD_DIM) == i, _dot(p, va) * r1, o_da)
        elif i < 2 * N_HEADS:
            e, r = _softmax2_parts(s_cur[0])
            o_mla = jnp.where(_group_of(lane, HEAD_DIM) == i - N_HEADS, _dot(e.astype(BF), vm) * r, o_mla)
        else:
            _, koff = na_window(i - 2 * N_HEADS)
            s_loc, s_ctx = s_cur
            m = jnp.maximum(jnp.max(s_loc, axis=-1, keepdims=True), jnp.max(s_ctx, axis=-1, keepdims=True))
            e_loc = jnp.exp(s_loc - m)
            e_ctx = jnp.exp(s_ctx - m)
            den = jnp.sum(e_loc, axis=-1, keepdims=True) + jnp.sum(e_ctx, axis=-1, keepdims=True)
            o4 = (_dot(e_loc.astype(BF), nav_ref[0, pl.ds(koff, NA_LOCAL), :])
                  + _dot(e_ctx.astype(BF), vc)) * (1.0 / den)
            o_row = o4[0:GRID_W]
            for hh in range(1, N_HEADS):
                o_row = jnp.where(_group_of(lane, HEAD_DIM) == hh, o4[hh * GRID_W:(hh + 1) * GRID_W], o_row)
            na_rows.append(o_row)
    if emit_mod:
        _mod_block(cv_ref, wmod_ref[...], bmod_ref[layer + 1:layer + 2, :], modn_ref)
    o_da = _head_rms(o_da) * _row(gsub_ref, layer) * (1.0 - lam_init)
    o_na = jnp.concatenate(na_rows, axis=0)

    mixed_ref[...] = jnp.concatenate([o_da, o_mla, o_na, o_conv], axis=1).astype(BF)


def _lat_attn(layer, proj_outs, caches, nab, prm, next_mod):
    lam_init = 0.8 - 0.6 * math.exp(-0.3 * layer)
    n_steps = DEC_BATCH * N_QT
    mblk = N_MOD * D_MODEL // n_steps
    step = lambda b, t: b * N_QT + t
    mixed_spec = pl.BlockSpec((TQ, D_MODEL), lambda b, t: (step(b, t), 0))
    mixed_shape = jax.ShapeDtypeStruct((DEC_BATCH * DEC_SEQ, D_MODEL), BF)
    mod_in, out_specs, out_shape = [], mixed_spec, mixed_shape
    if next_mod:
        mod_in = [pl.BlockSpec((16, D_MODEL), lambda b, t: (0, 0)),
                  pl.BlockSpec((None, D_MODEL, mblk), lambda b, t: (layer + 1, 0, step(b, t))),
                  pl.BlockSpec((DEPTH, mblk), lambda b, t: (0, step(b, t)))]
        out_specs = [mixed_spec, pl.BlockSpec((16, mblk), lambda b, t: (0, step(b, t)))]
        out_shape = [mixed_shape, jax.ShapeDtypeStruct((16, N_MOD * D_MODEL), F32)]
    q_spec = pl.BlockSpec((1, TQ, GROUP_W), lambda b, t: (b, t, 0))
    full_k = pl.BlockSpec((1, DEC_SEQ, GROUP_W), lambda b, t: (b, 0, 0))
    cache_spec = pl.BlockSpec((1, None, PAST_LEN, GROUP_W), lambda b, t: (b, layer, 0, 0))
    assert PAST_LEN == GROUP_W
    in_specs = [q_spec,
                pl.BlockSpec((1, N_HEADS, TQ, GROUP_W), lambda b, t: (b, 0, t, 0)),
                q_spec, full_k] + [full_k] * 7 + [cache_spec] * 7 + [
                _layer_spec(layer, (NA_KR, N_HEADS * GRID_W, NA_LOCAL)),
                _layer_spec(layer, (4, DA_QK)),
                _const_spec((DEPTH, GROUP_W)),
                _layer_spec(layer, (32, GROUP_W)),
                _const_spec((DEPTH, GROUP_W)),
                _const_spec((DEPTH, GROUP_W)),
                _const_spec((DEPTH, GROUP_W))] + mod_in
    assert len(in_specs) == N_LAT_ATTN_IN + len(mod_in)
    return pl.pallas_call(
        functools.partial(_lat_attn_kernel, layer, lam_init, bool(next_mod)),
        grid=(DEC_BATCH, N_QT),
        in_specs=in_specs,
        out_specs=out_specs,
        out_shape=out_shape,
        scratch_shapes=[pltpu.VMEM((TQ + 32, GROUP_W), F32), pltpu.VMEM((7, TQ + 8, GROUP_W), F32),
                        pltpu.VMEM((3, GROUP_W, KEYS), BF), pltpu.VMEM((2, KEYS, GROUP_W), BF),
                        pltpu.VMEM((2, PAST_LEN, GROUP_W), BF)],
        compiler_params=_cparams("arbitrary", "arbitrary"),
        name=f"lat_attn{layer}",
    )(*proj_outs, *caches, nab, prm["lamv"], prm["g_sub"], prm["dw"], prm["cb"], prm["ln_g"], prm["ln_b"], *next_mod)


N_FF_CHUNKS = D_FF // D_MODEL


def _ffn_weight_copies(layer, wout_hbm, w1_hbm, w2_hbm, wout_ref, w1_ref, w2_ref, sem):
    copies = [pltpu.make_async_copy(wout_hbm.at[layer], wout_ref, sem.at[0])]
    for c in range(N_FF_CHUNKS):
        lo, hi = c * D_MODEL, (c + 1) * D_MODEL
        copies.append(pltpu.make_async_copy(w1_hbm.at[layer, :, lo:hi], w1_ref.at[:, lo:hi], sem.at[1 + 2 * c]))
        copies.append(pltpu.make_async_copy(w2_hbm.at[layer, lo:hi, :], w2_ref.at[lo:hi, :], sem.at[2 + 2 * c]))
    return copies


def _out_ffn_kernel(layer, tiles_per_mod, mod_base, x_ref, mx_ref, mod_ref, wout_hbm, gff_ref,
                    w1_hbm, w2_hbm, gfin_ref, o_ref, wout_ref, w1_ref, w2_ref, sem):
    copies = _ffn_weight_copies(layer, wout_hbm, w1_hbm, w2_hbm, wout_ref, w1_ref, w2_ref, sem)

    def body(first):
        def arrive(k):
            if first:
                copies[k].wait()

        if first:
            for cp in copies:
                cp.start()
        row = mod_base + pl.program_id(0) // tiles_per_mod
        g1 = mod_ref[pl.ds(row, 1), 2 * D_MODEL:3 * D_MODEL]
        sh2 = mod_ref[pl.ds(row, 1), 3 * D_MODEL:4 * D_MODEL]
        sc2 = mod_ref[pl.ds(row, 1), 4 * D_MODEL:5 * D_MODEL]
        g2 = mod_ref[pl.ds(row, 1), 5 * D_MODEL:6 * D_MODEL]
        arrive(0)
        x1 = x_ref[...] + g1 * _dot(mx_ref[...], wout_ref[...])
        h2 = ((_rms(x1) * _row(gff_ref, layer)) * (1.0 + sc2) + sh2).astype(BF)
        acc = jnp.zeros((TM_FFN, D_MODEL), F32)
        for c in range(N_FF_CHUNKS):
            arrive(1 + 2 * c)
            a = jnp.maximum(_dot(h2, w1_ref[:, c * D_MODEL:(c + 1) * D_MODEL]), 0.0)
            arrive(2 + 2 * c)
            acc = acc + _dot((a * a).astype(BF), w2_ref[c * D_MODEL:(c + 1) * D_MODEL, :])
        x2 = x1 + g2 * acc
        o_ref[...] = _rms(x2) * gfin_ref[...] if layer == DEPTH - 1 else x2

    @pl.when(pl.program_id(0) == 0)
    def _():
        body(True)

    @pl.when(pl.program_id(0) > 0)
    def _():
        body(False)


def _out_ffn(name, layer, tiles_per_mod, mod_base, x, mixed, prm, g_final):
    n = x.shape[0]
    tile = pl.BlockSpec((TM_FFN, D_MODEL), lambda i: (i, 0))
    return pl.pallas_call(
        functools.partial(_out_ffn_kernel, layer, tiles_per_mod, mod_base),
        grid=(n // TM_FFN,),
        in_specs=[tile, tile,
                  _const_spec((16, N_MOD * D_MODEL)),
                  pl.BlockSpec(memory_space=pl.ANY),
                  _const_spec((DEPTH, D_MODEL)),
                  pl.BlockSpec(memory_space=pl.ANY),
                  pl.BlockSpec(memory_space=pl.ANY),
                  _const_spec((1, D_MODEL))],
        out_specs=tile,
        out_shape=jax.ShapeDtypeStruct((n, D_MODEL), F32),
        scratch_shapes=[pltpu.VMEM((D_MODEL, D_MODEL), BF), pltpu.VMEM((D_MODEL, D_FF), BF),
                        pltpu.VMEM((D_FF, D_MODEL), BF), pltpu.SemaphoreType.DMA((1 + 2 * N_FF_CHUNKS,))],
        compiler_params=_cparams("arbitrary"),
        name=name,
    )(x, mixed, prm["mod"], prm["w_out"], prm["g_ff"], prm["w_ff1"], prm["w_ff2"], g_final)


def _rope_tables():
    t = np.arange(DEC_SEQ)
    rows = (t // GRID_W).astype(np.float32)
    cols = (t % GRID_W).astype(np.float32)
    c = np.arange(GROUP_W) % 32
    freqs = np.float32(ROPE_BASE) ** (-np.arange(8, dtype=np.float32) * np.float32(2.0) / np.float32(16))
    pos = np.where((c < 16)[None, :], rows[:, None], cols[:, None]).astype(np.float32)
    ang = (pos * freqs[(c % 16) % 8][None, :]).astype(np.float32)
    first = ((c % 16) < 8)[None, :]
    cos = np.cos(ang).astype(np.float32)
    sin = np.sin(ang).astype(np.float32)
    return jnp.asarray(cos), jnp.asarray(np.where(first, -sin, sin))


def _qup_selection():
    sel = np.zeros((N_HEADS * (MLA_NOPE + MLA_ROPE), N_HEADS * GROUP_W), np.float32)
    for h in range(N_HEADS):
        src = h * (MLA_NOPE + MLA_ROPE)
        dst = h * GROUP_W + (h % 2) * MLA_NOPE
        sel[np.arange(src, src + MLA_NOPE), np.arange(dst, dst + MLA_NOPE)] = 1.0
        sel[np.arange(src + MLA_NOPE, src + MLA_NOPE + MLA_ROPE), h * GROUP_W + 128 + np.arange(MLA_ROPE)] = 1.0
    return jnp.asarray(sel)


def _kvup_selection():
    per_head = MLA_NOPE + HEAD_DIM
    k = [h * per_head + d for h in range(N_HEADS) for d in range(MLA_NOPE)]
    v = [h * per_head + MLA_NOPE + d for h in range(N_HEADS) for d in range(HEAD_DIM)]
    src = np.asarray(k + v)
    sel = np.zeros((src.size, src.size), np.float32)
    sel[src, np.arange(src.size)] = 1.0
    return jnp.asarray(sel)


def _heads_to_rows(c):
    b, l, h, s, d = c.shape
    return jnp.swapaxes(c, -1, -2).reshape(b, l, h * d, s)


def kernel(x_prompt, x_sample, c, cache_da_k, cache_da_v, cache_mla_ckv, cache_mla_krope, cache_na_k, cache_na_v, c_ctx, w_mod, b_mod, g_norm_mix, g_norm_ff, w_in, da_lambda_q1, da_lambda_k1, da_lambda_q2, da_lambda_k2, g_da_subln, g_mla_q, w_mla_qup, g_mla_kv, w_mla_kvup, na_rpb, conv_dw, conv_b, conv_ln_g, conv_ln_b, w_out, w_ff1, w_ff2, g_final):
    prm = dict(
        g_mix=g_norm_mix, w_in=jnp.swapaxes(w_in, 1, 2), g_q=g_mla_q, g_kv=g_mla_kv,
        lamv=jnp.stack([da_lambda_q1, da_lambda_k1, da_lambda_q2, da_lambda_k2], axis=1),
        g_sub=jnp.tile(g_da_subln, (1, N_HEADS)),
        dw=jnp.concatenate([conv_dw, jnp.zeros((DEPTH, 1, GROUP_W), F32)], axis=1),
        cb=conv_b, ln_g=conv_ln_g, ln_b=conv_ln_b,
        g_ff=g_norm_ff)
    cv = jnp.concatenate([c_ctx[None, :], c, jnp.zeros((16 - 1 - DEC_BATCH, D_MODEL), F32)], axis=0)
    cos_t, sin_t = _rope_tables()
    g_final2 = g_final.reshape(1, D_MODEL)

    c_kr_t = jnp.swapaxes(cache_mla_krope, -1, -2)
    prm["mod"], nab, cka, ckb, cmv, prm["w_qup"], prm["w_kvup"] = _prep(
        cv, w_mod, b_mod, na_rpb, cache_mla_ckv, c_kr_t, w_mla_qup, w_mla_kvup)
    caches = (_heads_to_rows(cache_da_k), _heads_to_rows(cache_da_v), cka, ckb, cmv,
              _heads_to_rows(cache_na_k), _heads_to_rows(cache_na_v))

    xp = x_prompt.reshape(BATCH * SEQ, D_MODEL)
    xs = x_sample.reshape(DEC_BATCH * DEC_SEQ, D_MODEL)
    new_ctx = ()
    for l in range(DEPTH):
        cast = (w_out, w_ff1, w_ff2) if l == 0 else ()
        mixed_p, *rest = _ctx_layer(l, xp, prm, tuple(new_ctx), cast)
        new_ctx = rest[:len(rest) - len(cast)]
        if cast:
            prm["w_out"], prm["w_ff1"], prm["w_ff2"] = rest[len(new_ctx):]
        xp = _out_ffn(f"ctx_ffn{l}", l, BATCH * SEQ // TM_FFN, 0, xp, mixed_p, prm, g_final2)
        proj_outs = _lat_proj(l, xs, prm, cos_t, sin_t)
        if l + 1 < DEPTH:
            mixed_s, mod_next = _lat_attn(l, proj_outs, caches, nab, prm, (cv, w_mod, b_mod))
        else:
            mixed_s, mod_next = _lat_attn(l, proj_outs, caches, nab, prm, ()), None
        xs = _out_ffn(f"lat_ffn{l}", l, DEC_SEQ // TM_FFN, 1, xs, mixed_s, prm, g_final2)
        prm["mod"] = mod_next

    y_prompt = xp.reshape(BATCH, SEQ, D_MODEL)
    y_sample = xs.reshape(DEC_BATCH, DEC_SEQ, D_MODEL)
    new_ctx = [a if k == 2 else jnp.swapaxes(a, -1, -2) for k, a in enumerate(new_ctx)]
    return (y_prompt, y_sample) + tuple(new_ctx)
```
